```python
import math
import jax, jax.numpy as jnp
from jax import lax
import numpy as np

D_MODEL = 1024
BATCH = 32
SEQ = 2048
DEPTH = 2

HEAD_DIM = 64
GRID_W = 64
Q_BLOCK = 128
ROPE_THETA = 10000.0
EPS = 1e-6
FORGET_FLOOR = 1e-6

A_HEADS = 6
A_KV_HEADS = 2
B_HEADS = 6
B_QK_DIM = HEAD_DIM // 2
C_HEADS = 4
C_KEY_DIM = 64
C_VAL_DIM = 64
C_CHUNK = 64

A_WIDTH = A_HEADS * HEAD_DIM
B_WIDTH = B_HEADS * HEAD_DIM
C_WIDTH = C_HEADS * C_VAL_DIM
MIX_WIDTH = A_WIDTH + B_WIDTH + C_WIDTH

IN_SIZES = [A_HEADS * HEAD_DIM, A_KV_HEADS * HEAD_DIM, A_KV_HEADS * HEAD_DIM,
            B_HEADS * 2 * B_QK_DIM, B_HEADS * 2 * B_QK_DIM, B_HEADS * HEAD_DIM,
            C_HEADS * C_KEY_DIM, C_HEADS * C_KEY_DIM, C_HEADS * C_KEY_DIM,
            C_HEADS * C_VAL_DIM, C_HEADS * C_VAL_DIM]
IN_TOTAL = sum(IN_SIZES)
D_FF = 4 * D_MODEL

kernel_name = "hymba_style_bidir_hybrid_encoder"


def rms_norm(x, gain=None):
    xf = x.astype(jnp.float32)
    y = xf * lax.rsqrt(jnp.mean(xf * xf, axis=-1, keepdims=True) + EPS)
    if gain is not None:
        y = y * gain.astype(jnp.float32)
    return y.astype(x.dtype)


def rope_angles(pos, dim):
    inv = ROPE_THETA ** (-jnp.arange(0, dim, 2, dtype=jnp.float32) / dim)
    return pos.astype(jnp.float32)[:, None] * inv[None, :]


def apply_rope(x, ang):
    shape = (1, ang.shape[0]) + (1,) * (x.ndim - 3) + (ang.shape[1],)
    cos = jnp.cos(ang).reshape(shape)
    sin = jnp.sin(ang).reshape(shape)
    xf = x.astype(jnp.float32)
    x1, x2 = jnp.split(xf, 2, axis=-1)
    out = jnp.concatenate([x1 * cos - x2 * sin, x1 * sin + x2 * cos], axis=-1)
    return out.astype(x.dtype)


def axial_rope(x, ang_row, ang_col):
    half = x.shape[-1] // 2
    return jnp.concatenate([apply_rope(x[..., :half], ang_row), apply_rope(x[..., half:], ang_col)], axis=-1)


def blocked_attention(q, k, v):
    B, S, H, dq = q.shape
    Hkv = k.shape[2]
    G = H // Hkv
    dv = v.shape[-1]
    nb = S // Q_BLOCK
    scale = dq ** -0.5
    qb = q.reshape(B, nb, Q_BLOCK, Hkv, G, dq).transpose(1, 0, 2, 3, 4, 5)

    def one_block(qblk):
        s = jnp.einsum('bqkgd,bskd->bkgqs', qblk, k).astype(jnp.float32) * scale
        p = jax.nn.softmax(s, axis=-1).astype(v.dtype)
        return jnp.einsum('bkgqs,bskd->bqkgd', p, v)

    o = lax.map(one_block, qb)
    return o.transpose(1, 0, 2, 3, 4, 5).reshape(B, S, H, dv)


def hgrn2_chunk_scan(q, logf, k, v):
    B, S, H, dk = q.shape
    dv = v.shape[-1]
    n = S // C_CHUNK

    def to_chunks(a):
        return a.reshape(B, n, C_CHUNK, H, a.shape[-1]).transpose(1, 0, 3, 2, 4)

    mask = jnp.tril(jnp.ones((C_CHUNK, C_CHUNK), dtype=bool))[None, None, :, :, None]

    def step(state, inp):
        qi, gi, ki, vi = inp
        bcum = jnp.cumsum(gi, axis=2)
        diff = bcum[:, :, :, None, :] - bcum[:, :, None, :, :]
        decay = jnp.where(mask, jnp.exp(jnp.where(mask, diff, 0.0)), 0.0)
        scores = jnp.einsum('bhtd,bhtsd,bhsd->bhts', qi, decay, ki)
        o = jnp.einsum('bhts,bhse->bhte', scores, vi) + jnp.einsum('bhtd,bhde->bhte', qi * jnp.exp(bcum), state)
        blast = bcum[:, :, -1:, :]
        new_state = jnp.exp(blast[:, :, 0, :])[..., None] * state + jnp.einsum('bhsd,bhse->bhde', ki * jnp.exp(blast - bcum), vi)
        return new_state, o

    init = jnp.zeros((B, H, dk, dv), jnp.float32)
    _, o = lax.scan(step, init, (to_chunks(q), to_chunks(logf), to_chunks(k), to_chunks(v)))
    return o.transpose(1, 0, 3, 2, 4).reshape(B, S, H, dv)


def mixer_axial_gqa(zq, zk, zv, qk_gains, ang_row, ang_col):
    B, S, _ = zq.shape
    q = rms_norm(zq.reshape(B, S, A_HEADS, HEAD_DIM), qk_gains[0])
    k = rms_norm(zk.reshape(B, S, A_KV_HEADS, HEAD_DIM), qk_gains[1])
    v = zv.reshape(B, S, A_KV_HEADS, HEAD_DIM)
    q = axial_rope(q, ang_row, ang_col)
    k = axial_rope(k, ang_row, ang_col)
    return blocked_attention(q, k, v).reshape(B, S, A_WIDTH)


def mixer_diff_attention(zq, zk, zv, lam_params, subln_gain, lam_init, ang):
    B, S, _ = zq.shape
    q = apply_rope(zq.reshape(B, S, B_HEADS, 2, B_QK_DIM), ang)
    k = apply_rope(zk.reshape(B, S, B_HEADS, 2, B_QK_DIM), ang)
    v = zv.reshape(B, S, B_HEADS, HEAD_DIM)
    lp = lam_params.astype(jnp.float32)
    lam = jnp.exp(jnp.sum(lp[0] * lp[1])) - jnp.exp(jnp.sum(lp[2] * lp[3])) + lam_init
    o1 = blocked_attention(q[:, :, :, 0], k[:, :, :, 0], v)
    o2 = blocked_attention(q[:, :, :, 1], k[:, :, :, 1], v)
    o = o1 - lam.astype(o1.dtype) * o2
    o = rms_norm(o, subln_gain) * (1.0 - lam_init)
    return o.reshape(B, S, B_WIDTH)


def mixer_hgrn2(zq, zf_fwd, zf_bwd, zi, zg, lb, norm_gain):
    B, S, _ = zq.shape

    def heads(a, d):
        return a.reshape(B, S, C_HEADS, d).astype(jnp.float32)

    q = jax.nn.silu(heads(zq, C_KEY_DIM)) * (C_KEY_DIM ** -0.5)
    v = heads(zi, C_VAL_DIM)
    lb = lb.reshape(C_HEADS, C_KEY_DIM)

    def gates(zf):
        zf = heads(zf, C_KEY_DIM)
        f = lb + (1.0 - lb) * jax.nn.sigmoid(zf)
        logf = jnp.log(jnp.maximum(f, FORGET_FLOOR))
        key = (1.0 - lb) * jax.nn.sigmoid(-zf)
        return logf, key

    logf_f, k_f = gates(zf_fwd)
    logf_b, k_b = gates(zf_bwd)
    flip = lambda a: jnp.flip(a, axis=1)
    o_f = hgrn2_chunk_scan(q, logf_f, k_f, v)
    o_b = flip(hgrn2_chunk_scan(flip(q), flip(logf_b), flip(k_b), flip(v)))
    o = rms_norm(o_f + o_b, norm_gain) * jax.nn.silu(heads(zg, C_VAL_DIM))
    return o.reshape(B, S, C_WIDTH).astype(zq.dtype)


def setup_inputs(seed: int = 0) -> dict:
    key = jax.random.key(seed)
    ks = jax.random.split(key, 14)
    nrm = jax.random.normal
    f32 = jnp.float32
    return {
        "x": nrm(ks[0], (BATCH, SEQ, D_MODEL), f32),
        "c": nrm(ks[1], (BATCH, D_MODEL), f32),
        "w_mod": nrm(ks[2], (DEPTH, D_MODEL, 6 * D_MODEL), f32) * (0.5 * D_MODEL ** -0.5),
        "b_mod": nrm(ks[3], (DEPTH, 6 * D_MODEL), f32) * 0.02,
        "w_in": nrm(ks[4], (DEPTH, D_MODEL, IN_TOTAL), f32) * (D_MODEL ** -0.5),
        "a_qk_norm": 1.0 + 0.02 * nrm(ks[5], (DEPTH, 2, HEAD_DIM), f32),
        "diff_lambda": 0.1 * nrm(ks[6], (DEPTH, 4, B_QK_DIM), f32),
        "diff_subln": 1.0 + 0.02 * nrm(ks[7], (DEPTH, HEAD_DIM), f32),
        "hgrn_lower_bounds": 0.1 * nrm(ks[8], (DEPTH, C_HEADS * C_KEY_DIM), f32),
        "hgrn_norm": 1.0 + 0.02 * nrm(ks[9], (DEPTH, C_VAL_DIM), f32),
        "w_out": nrm(ks[10], (DEPTH, MIX_WIDTH, D_MODEL), f32) * (MIX_WIDTH ** -0.5),
        "w_ff1": nrm(ks[11], (DEPTH, D_MODEL, D_FF), f32) * (D_MODEL ** -0.5),
        "w_ff2": nrm(ks[12], (DEPTH, D_FF, D_MODEL), f32) * (D_FF ** -0.5),
        "final_norm": 1.0 + 0.02 * nrm(ks[13], (D_MODEL,), f32),
    }


def reference(x, c, w_mod, b_mod, w_in, a_qk_norm, diff_lambda, diff_subln, hgrn_lower_bounds, hgrn_norm, w_out, w_ff1, w_ff2, final_norm):
    S = x.shape[1]
    rows = S // GRID_W
    t = jnp.arange(S)
    row = jnp.repeat(jnp.arange(rows), GRID_W)
    col = jnp.tile(jnp.arange(GRID_W), rows)
    ang_row = rope_angles(row, HEAD_DIM // 2)
    ang_col = rope_angles(col, HEAD_DIM // 2)
    ang_1d = rope_angles(t, B_QK_DIM)

    lbp = jax.nn.softmax(hgrn_lower_bounds.astype(jnp.float32), axis=0)
    lbs = jnp.clip(jnp.cumsum(lbp, axis=0) - lbp[0:1], 0.0, 1.0)

    split_idx = [int(i) for i in np.cumsum(IN_SIZES)[:-1]]
    cond = jax.nn.silu(c)
    for l in range(DEPTH):
        mod = cond @ w_mod[l] + b_mod[l]
        sh1, sc1, g1, sh2, sc2, g2 = [m[:, None, :] for m in jnp.split(mod, 6, axis=-1)]

        h = rms_norm(x) * (1.0 + sc1) + sh1
        z = h @ w_in[l]
        aq, ak, av, bq, bk, bv, cq, cff, cfb, ci, cg = jnp.split(z, split_idx, axis=-1)
        o_a = mixer_axial_gqa(aq, ak, av, a_qk_norm[l], ang_row, ang_col)
        lam_init = 0.8 - 0.6 * math.exp(-0.3 * l)
        o_b = mixer_diff_attention(bq, bk, bv, diff_lambda[l], diff_subln[l], lam_init, ang_1d)
        o_c = mixer_hgrn2(cq, cff, cfb, ci, cg, lbs[l], hgrn_norm[l])
        x = x + g1 * (jnp.concatenate([o_a, o_b, o_c], axis=-1) @ w_out[l])

        h = rms_norm(x) * (1.0 + sc2) + sh2
        x = x + g2 * (jnp.square(jax.nn.relu(h @ w_ff1[l])) @ w_ff2[l])
    return rms_norm(x, final_norm)
```

```python
import functools
import math

import numpy as np
import jax
import jax.numpy as jnp
from jax import lax
from jax.experimental import pallas as pl
from jax.experimental.pallas import tpu as pltpu

F32 = jnp.float32
BF16 = jnp.bfloat16

HEAD_DIM = 64
GRID_W = 64
ROPE_THETA = 10000.0
EPS = 1e-6
FORGET_FLOOR = 1e-6
A_HEADS, A_KV_HEADS = 6, 2
B_HEADS, B_QK_DIM = 6, 32
C_HEADS, C_DIM = 4, 64
A_WIDTH = A_HEADS * HEAD_DIM
A_KV_WIDTH = A_KV_HEADS * HEAD_DIM
B_WIDTH = B_HEADS * HEAD_DIM
C_WIDTH = C_HEADS * C_DIM
LANES = 128
HGRN_CHUNK = 64
HGRN_LEAF = 8
VMEM_LIMIT = 56 * 1024 * 1024

OFF_AQ, OFF_AK, OFF_AV = 0, 384, 512
OFF_BQ, OFF_BK, OFF_BV = 640, 1024, 1408
OFF_C = 1792
IN_TOTAL = 3072


def _dot(a, b):
    return jnp.dot(a, b, preferred_element_type=F32)


def _dot_nt(a, b):
    return lax.dot_general(a, b, (((1,), (1,)), ((), ())), preferred_element_type=F32)


def _dot_tn(a, b):
    return lax.dot_general(a, b, (((0,), (0,)), ((), ())), preferred_element_type=F32)


def _split_dot(x, w_bf16):
    hi = x.astype(BF16)
    lo = (x - hi.astype(F32)).astype(BF16)
    return _dot(hi, w_bf16) + _dot(lo, w_bf16)


def _group_ones(n, group):
    r = lax.broadcasted_iota(jnp.int32, (n, n), 0) // group
    c = lax.broadcasted_iota(jnp.int32, (n, n), 1) // group
    return jnp.where(r == c, 1.0 / group, 0.0).astype(BF16)


def _rope(xb, cos, sin_signed):
    lane = lax.broadcasted_iota(jnp.int32, xb.shape, 1)
    low = (lane % 32) < 16
    partner = jnp.where(low, pltpu.roll(xb, LANES - 16, 1), pltpu.roll(xb, 16, 1))
    return xb * cos + partner * sin_signed


def _rms_rows(x):
    return x * lax.rsqrt(jnp.mean(x * x, axis=-1, keepdims=True) + EPS)


def _mod_kernel(c_ref, w_ref, b_ref, o_ref):
    c = c_ref[...]
    cond = (c * jax.nn.sigmoid(c)).astype(BF16)
    o_ref[0] = _dot(cond, w_ref[0]) + b_ref[0]


def _modulation(c, w_mod, b_mod):
    depth, d, n = w_mod.shape
    bsz = c.shape[0]
    tn = 1024
    return pl.pallas_call(
        _mod_kernel,
        grid=(depth, n // tn),
        in_specs=[
            pl.BlockSpec((bsz, d), lambda l, j: (0, 0)),
            pl.BlockSpec((1, d, tn), lambda l, j: (l, 0, j)),
            pl.BlockSpec((1, 1, tn), lambda l, j: (l, 0, j)),
        ],
        out_specs=pl.BlockSpec((1, bsz, tn), lambda l, j: (l, 0, j)),
        out_shape=jax.ShapeDtypeStruct((depth, bsz, n), F32),
        compiler_params=pltpu.CompilerParams(
            dimension_semantics=("arbitrary", "arbitrary"), vmem_limit_bytes=VMEM_LIMIT),
        name="adaln_mod",
    )(c, w_mod.astype(BF16), b_mod.reshape(depth, 1, n))


def _inproj_kernel(x_ref, mod_ref, w_ref, cosa_ref, sina_ref, cosb_ref, sinb_ref, gain_ref,
                   qa_ref, kat_ref, va_ref, qb_ref, kbt_ref, vb_ref, zc_ref, h_scr, z_scr):
    x = x_ref[0]
    h = _rms_rows(x) * (1.0 + mod_ref[0, 1:2, :]) + mod_ref[0, 0:1, :]
    h_scr[...] = h.astype(BF16)
    for j in range(IN_TOTAL // 1024):
        z_scr[:, j * 1024:(j + 1) * 1024] = _dot(h_scr[...], w_ref[0, :, j * 1024:(j + 1) * 1024])

    ones = _group_ones(LANES, HEAD_DIM)
    cosa, sina = cosa_ref[...], sina_ref[...]
    cosb, sinb = cosb_ref[...], sinb_ref[...]

    def a_norm_rope(zb, gain):
        ms = _split_dot(zb * zb, ones)
        return _rope(zb * lax.rsqrt(ms + EPS) * gain, cosa, sina)

    for cblk in range(A_WIDTH // LANES):
        zb = z_scr[:, OFF_AQ + cblk * LANES:OFF_AQ + (cblk + 1) * LANES]
        y = a_norm_rope(zb, gain_ref[0:1, :]) * (HEAD_DIM ** -0.5)
        qa_ref[0, :, cblk * LANES:(cblk + 1) * LANES] = y.astype(BF16)
    y = a_norm_rope(z_scr[:, OFF_AK:OFF_AK + LANES], gain_ref[1:2, :])
    kat_ref[0] = y.T.astype(BF16)
    va_ref[0] = z_scr[:, OFF_AV:OFF_AV + LANES].astype(BF16)

    for cblk in range(B_WIDTH // LANES):
        lo = cblk * LANES
        zq = z_scr[:, OFF_BQ + lo:OFF_BQ + lo + LANES]
        qb_ref[0, :, lo:lo + LANES] = _rope(zq, cosb, sinb).astype(BF16)
        zk = z_scr[:, OFF_BK + lo:OFF_BK + lo + LANES]
        kbt_ref[0, lo:lo + LANES, :] = _rope(zk, cosb, sinb).T.astype(BF16)
    vb_ref[0] = z_scr[:, OFF_BV:OFF_BV + B_WIDTH].astype(BF16)
    zc_ref[0] = z_scr[:, OFF_C:IN_TOTAL]


def _in_projection(x, mod_l, w_in_bf16, layer, tables, qk_gain, tm):
    bsz, s, d = x.shape
    cosa, sina, cosb, sinb = tables
    row_spec = lambda width: pl.BlockSpec((1, tm, width), lambda b, i: (b, i, 0))
    tab_spec = pl.BlockSpec((tm, LANES), lambda b, i: (i, 0))
    col_spec = lambda height: pl.BlockSpec((1, height, tm), lambda b, i: (b, 0, i))
    return pl.pallas_call(
        _inproj_kernel,
        grid=(bsz, s // tm),
        in_specs=[
            row_spec(d),
            pl.BlockSpec((1, 6, d), lambda b, i: (b, 0, 0)),
            pl.BlockSpec((1, d, IN_TOTAL), lambda b, i: (layer, 0, 0)),
            tab_spec, tab_spec, tab_spec, tab_spec,
            pl.BlockSpec((2, LANES), lambda b, i: (0, 0)),
        ],
        out_specs=[
            row_spec(A_WIDTH), col_spec(A_KV_WIDTH), row_spec(A_KV_WIDTH),
            row_spec(B_WIDTH), col_spec(B_WIDTH), row_spec(B_WIDTH),
            row_spec(IN_TOTAL - OFF_C),
        ],
        out_shape=[
            jax.ShapeDtypeStruct((bsz, s, A_WIDTH), BF16),
            jax.ShapeDtypeStruct((bsz, A_KV_WIDTH, s), BF16),
            jax.ShapeDtypeStruct((bsz, s, A_KV_WIDTH), BF16),
            jax.ShapeDtypeStruct((bsz, s, B_WIDTH), BF16),
            jax.ShapeDtypeStruct((bsz, B_WIDTH, s), BF16),
            jax.ShapeDtypeStruct((bsz, s, B_WIDTH), BF16),
            jax.ShapeDtypeStruct((bsz, s, IN_TOTAL - OFF_C), F32),
        ],
        scratch_shapes=[pltpu.VMEM((tm, d), BF16), pltpu.VMEM((tm, IN_TOTAL), F32)],
        compiler_params=pltpu.CompilerParams(
            dimension_semantics=("arbitrary", "arbitrary"), vmem_limit_bytes=VMEM_LIMIT),
        name="in_projection",
    )(x, mod_l, w_in_bf16, cosa, sina, cosb, sinb, qk_gain)


def _attn_a_kernel(q_ref, kt_ref, v_ref, o_ref):
    q = q_ref[0]
    v = v_ref[0]
    group = A_HEADS // A_KV_HEADS
    outs = []
    for h in range(A_HEADS):
        g = h // group
        s = _dot(q[:, h * HEAD_DIM:(h + 1) * HEAD_DIM], kt_ref[0, g * HEAD_DIM:(g + 1) * HEAD_DIM, :])
        p = jnp.exp(s - jnp.max(s, axis=-1, keepdims=True))
        inv = 1.0 / jnp.sum(p, axis=-1, keepdims=True)
        o = _dot(p.astype(BF16), v)
        outs.append(o[:, g * HEAD_DIM:(g + 1) * HEAD_DIM] * inv)
    o_ref[0] = jnp.concatenate(outs, axis=-1).astype(BF16)


def _attention_a(qa, kat, va, tq):
    bsz, s, _ = qa.shape
    return pl.pallas_call(
        _attn_a_kernel,
        grid=(bsz, s // tq),
        in_specs=[
            pl.BlockSpec((1, tq, A_WIDTH), lambda b, i: (b, i, 0)),
            pl.BlockSpec((1, A_KV_WIDTH, s), lambda b, i: (b, 0, 0)),
            pl.BlockSpec((1, s, A_KV_WIDTH), lambda b, i: (b, 0, 0)),
        ],
        out_specs=pl.BlockSpec((1, tq, A_WIDTH), lambda b, i: (b, i, 0)),
        out_shape=jax.ShapeDtypeStruct((bsz, s, A_WIDTH), BF16),
        compiler_params=pltpu.CompilerParams(
            dimension_semantics=("arbitrary", "arbitrary"), vmem_limit_bytes=VMEM_LIMIT),
        name="attn_axial_gqa",
    )(qa, kat, va)


def _attn_b_kernel(lam_ref, gain_ref, q_ref, kt_ref, v_ref, o_ref, *, lam_init):
    lp = lam_ref[...]
    lam = (jnp.exp(jnp.sum(lp[0:1] * lp[1:2], axis=-1, keepdims=True))
           - jnp.exp(jnp.sum(lp[2:3] * lp[3:4], axis=-1, keepdims=True)) + lam_init)
    scale = B_QK_DIM ** -0.5
    q = q_ref[0]

    def softmax_parts(qc, kc):
        s = _dot(qc, kc)
        p = jnp.exp((s - jnp.max(s, axis=-1, keepdims=True)) * scale)
        return p, 1.0 / jnp.sum(p, axis=-1, keepdims=True)

    outs = []
    for h in range(B_HEADS):
        lo = h * HEAD_DIM
        p1, inv1 = softmax_parts(q[:, lo:lo + B_QK_DIM], kt_ref[0, lo:lo + B_QK_DIM, :])
        p2, inv2 = softmax_parts(q[:, lo + B_QK_DIM:lo + HEAD_DIM], kt_ref[0, lo + B_QK_DIM:lo + HEAD_DIM, :])
        p = p1 * inv1 - p2 * (inv2 * lam)
        pair = (h // 2) * LANES
        o = _dot(p.astype(BF16), v_ref[0, :, pair:pair + LANES])
        half = (h % 2) * HEAD_DIM
        outs.append(o[:, half:half + HEAD_DIM])
    o_all = jnp.concatenate(outs, axis=-1)
    ones = _group_ones(LANES, HEAD_DIM)
    for cblk in range(B_WIDTH // LANES):
        blk = o_all[:, cblk * LANES:(cblk + 1) * LANES]
        ms = _split_dot(blk * blk, ones)
        y = blk * lax.rsqrt(ms + EPS) * gain_ref[...] * (1.0 - lam_init)
        o_ref[0, :, cblk * LANES:(cblk + 1) * LANES] = y.astype(BF16)


def _attention_b(qb, kbt, vb, lam_params, subln_gain, lam_init, tq):
    bsz, s, _ = qb.shape
    return pl.pallas_call(
        functools.partial(_attn_b_kernel, lam_init=lam_init),
        grid=(bsz, s // tq),
        in_specs=[
            pl.BlockSpec((4, B_QK_DIM), lambda b, i: (0, 0)),
            pl.BlockSpec((1, LANES), lambda b, i: (0, 0)),
            pl.BlockSpec((1, tq, B_WIDTH), lambda b, i: (b, i, 0)),
            pl.BlockSpec((1, B_WIDTH, s), lambda b, i: (b, 0, 0)),
            pl.BlockSpec((1, s, B_WIDTH), lambda b, i: (b, 0, 0)),
        ],
        out_specs=pl.BlockSpec((1, tq, B_WIDTH), lambda b, i: (b, i, 0)),
        out_shape=jax.ShapeDtypeStruct((bsz, s, B_WIDTH), BF16),
        compiler_params=pltpu.CompilerParams(
            dimension_semantics=("arbitrary", "arbitrary"), vmem_limit_bytes=VMEM_LIMIT),
        name="attn_differential",
    )(lam_params, subln_gain, qb, kbt, vb)


def _hgrn_levels():
    sizes = []
    bs = HGRN_CHUNK
    while bs > HGRN_LEAF:
        sizes.append(bs)
        bs //= 2
    return sizes


def _hgrn_constants():
    c, w = HGRN_CHUNK, C_WIDTH
    t = np.arange(c)
    tri = np.stack([(t[:, None] >= t[None, :]), (t[:, None] <= t[None, :])]).astype(np.float32)
    s_of_lane = np.arange(w) % C_DIM
    sizes = _hgrn_levels()
    qmask = np.zeros((2, len(sizes), c, w), np.float32)
    amask = np.zeros((2, len(sizes) + 1, c, w), np.float32)
    for d in range(2):
        for li, bs in enumerate(sizes):
            upper = (t % bs) >= bs // 2
            qrow = upper if d == 0 else ~upper
            qmask[d, li] = np.broadcast_to(qrow[:, None], (c, w))
            amask[d, li] = (t[:, None] // bs) == (s_of_lane[None, :] // bs)
        same_leaf = (t[:, None] // HGRN_LEAF) == (s_of_lane[None, :] // HGRN_LEAF)
        order = (s_of_lane[None, :] <= t[:, None]) if d == 0 else (s_of_lane[None, :] >= t[:, None])
        amask[d, len(sizes)] = same_leaf & order
    head_r = np.arange(w)[:, None] // C_DIM
    head_c = np.arange(w)[None, :] // C_DIM
    bdmask = (head_r == head_c).astype(np.float32)
    return tri, qmask, amask, bdmask


def _block_diag(rows_f32, bdmask):
    c = rows_f32.shape[0]
    reps = C_WIDTH // c
    tiled = jnp.concatenate([rows_f32] * reps, axis=0)
    return tiled * bdmask


def _ref_rows(b, bs, offset):
    parts = []
    for lo in range(0, b.shape[0], bs):
        parts.append(jnp.broadcast_to(b[lo + offset:lo + offset + 1, :], (bs, b.shape[1])))
    return jnp.concatenate(parts, axis=0)


def _hgrn_kernel(z_ref, lb_ref, gain_ref, tri_ref, qmask_ref, amask_ref, bd_ref, o_ref,
                 of_scr, ob_scr, st_scr, *, layer, depth, n_chunks):
    c, w = HGRN_CHUNK, C_WIDTH
    sizes = _hgrn_levels()

    rows = [lb_ref[i:i + 1, :] for i in range(depth)]
    mx = functools.reduce(jnp.maximum, rows)
    es = [jnp.exp(r - mx) for r in rows]
    tot = functools.reduce(lambda a, b_: a + b_, es)
    ps = [e / tot for e in es]
    cum = functools.reduce(lambda a, b_: a + b_, ps[:layer + 1])
    lb = jnp.clip(cum - ps[0], 0.0, 1.0)

    bdmask = bd_ref[...]
    st_scr[...] = jnp.zeros_like(st_scr)

    def chunk(ci, d):
        r0 = pl.multiple_of(ci * c, c)
        zq = z_ref[0, pl.ds(r0, c), 0:w]
        zf = z_ref[0, pl.ds(r0, c), (1 + d) * w:(2 + d) * w]
        v = z_ref[0, pl.ds(r0, c), 3 * w:4 * w]
        q = zq * jax.nn.sigmoid(zq) * (C_DIM ** -0.5)
        e = jnp.exp(-jnp.abs(zf))
        r = 1.0 / (1.0 + e)
        pos = zf >= 0.0
        sig_p = jnp.where(pos, r, e * r)
        sig_n = jnp.where(pos, e * r, r)
        g = jnp.log(jnp.maximum(lb + (1.0 - lb) * sig_p, FORGET_FLOOR))
        k = (1.0 - lb) * sig_n

        g_hi = g.astype(BF16)
        g_r = g - g_hi.astype(F32)
        g_mid = g_r.astype(BF16)
        g_lo = (g_r - g_mid.astype(F32)).astype(BF16)
        tri = tri_ref[d]
        b = _dot(tri, g_hi) + _dot(tri, g_mid) + _dot(tri, g_lo)
        total = b[c - 1:c, :] if d == 0 else b[0:1, :]

        kbf = None
        a = jnp.zeros((c, w), F32)
        for li, bs in enumerate(sizes):
            ref = _ref_rows(b, bs, bs // 2 - 1 if d == 0 else bs // 2)
            qm = qmask_ref[d, li]
            arg = jnp.where(qm > 0.5, b - ref, ref - b)
            ex = jnp.exp(arg)
            qp = (q * ex * qm).astype(BF16)
            kp = _block_diag(k * ex * (1.0 - qm), bdmask).astype(BF16)
            a = a + _dot_nt(qp, kp) * amask_ref[d, li]
        ref = _ref_rows(b, HGRN_LEAF, HGRN_LEAF // 2 - 1 if d == 0 else HGRN_LEAF // 2)
        arg = b - ref
        qp = (q * jnp.exp(arg)).astype(BF16)
        kp = _block_diag(k * jnp.exp(-arg), bdmask).astype(BF16)
        a = a + _dot_nt(qp, kp) * amask_ref[d, len(sizes)]

        st = st_scr[d]
        o = _dot(a.astype(BF16), _block_diag(v, bdmask).astype(BF16))
        o = o + _dot_nt((q * jnp.exp(b)).astype(BF16), st.astype(BF16))
        kb = (k * jnp.exp(total - b)).astype(BF16)
        st_scr[d] = st * jnp.exp(total) + _dot_tn(v.astype(BF16), kb) * bdmask
        if d == 0:
            of_scr[pl.ds(r0, c), :] = o
        else:
            ob_scr[pl.ds(r0, c), :] = o

    def body(i, carry):
        chunk(i, 0)
        chunk(n_chunks - 1 - i, 1)
        return carry

    lax.fori_loop(0, n_chunks, body, 0)

    ones = _group_ones(w, C_DIM)
    tile = 256
    for r0 in range(0, n_chunks * c, tile):
        o = of_scr[r0:r0 + tile, :] + ob_scr[r0:r0 + tile, :]
        ms = _split_dot(o * o, ones)
        zg = z_ref[0, r0:r0 + tile, 4 * w:5 * w]
        y = o * lax.rsqrt(ms + EPS) * gain_ref[...] * (zg * jax.nn.sigmoid(zg))
        o_ref[0, r0:r0 + tile, :] = y.astype(BF16)


def _hgrn(zc, lower_bounds, norm_gain, layer):
    bsz, s, zw = zc.shape
    depth = lower_bounds.shape[0]
    n_chunks = s // HGRN_CHUNK
    tri, qmask, amask, bdmask = _hgrn_constants()
    full = lambda a: pl.BlockSpec(a.shape, lambda b: (0,) * a.ndim)
    consts = (jnp.asarray(tri, BF16), jnp.asarray(qmask), jnp.asarray(amask), jnp.asarray(bdmask))
    return pl.pallas_call(
        functools.partial(_hgrn_kernel, layer=layer, depth=depth, n_chunks=n_chunks),
        grid=(bsz,),
        in_specs=[
            pl.BlockSpec((1, s, zw), lambda b: (b, 0, 0)),
            full(lower_bounds), full(norm_gain),
        ] + [full(a) for a in consts],
        out_specs=pl.BlockSpec((1, s, C_WIDTH), lambda b: (b, 0, 0)),
        out_shape=jax.ShapeDtypeStruct((bsz, s, C_WIDTH), BF16),
        scratch_shapes=[
            pltpu.VMEM((s, C_WIDTH), F32), pltpu.VMEM((s, C_WIDTH), F32),
            pltpu.VMEM((2, C_WIDTH, C_WIDTH), F32),
        ],
        compiler_params=pltpu.CompilerParams(
            dimension_semantics=("arbitrary",), vmem_limit_bytes=VMEM_LIMIT),
        name="hgrn2_bidir",
    )(zc, lower_bounds, norm_gain, *consts)


def _out_ffn_kernel(x_ref, mod_ref, oa_ref, ob_ref, oc_ref, wo_ref, w1_ref, w2_ref, fg_ref, o_ref,
                    *, final, ff_chunk):
    mix = jnp.concatenate([oa_ref[0], ob_ref[0], oc_ref[0]], axis=-1)
    x1 = x_ref[0] + mod_ref[0, 2:3, :] * _dot(mix, wo_ref[0])
    h = (_rms_rows(x1) * (1.0 + mod_ref[0, 4:5, :]) + mod_ref[0, 3:4, :]).astype(BF16)
    d_ff = w1_ref.shape[2]
    acc = jnp.zeros_like(x1)
    for j in range(d_ff // ff_chunk):
        u = _dot(h, w1_ref[0, :, j * ff_chunk:(j + 1) * ff_chunk])
        u = jnp.square(jnp.maximum(u, 0.0)).astype(BF16)
        acc = acc + _dot(u, w2_ref[0, j * ff_chunk:(j + 1) * ff_chunk, :])
    x2 = x1 + mod_ref[0, 5:6, :] * acc
    if final:
        x2 = _rms_rows(x2) * fg_ref[...]
    o_ref[0] = x2


def _out_ffn(x, mod_l, oa, ob, oc, w_out, w_ff1, w_ff2, final_gain, layer, final, tm):
    bsz, s, d = x.shape
    d_ff = w_ff1.shape[2]
    row_spec = lambda width: pl.BlockSpec((1, tm, width), lambda b, i: (b, i, 0))
    weight = lambda a: pl.BlockSpec((1,) + a.shape[1:], lambda b, i: (layer, 0, 0),
                                    pipeline_mode=pl.Buffered(1))
    return pl.pallas_call(
        functools.partial(_out_ffn_kernel, final=final, ff_chunk=1024),
        grid=(bsz, s // tm),
        in_specs=[
            row_spec(d),
            pl.BlockSpec((1, 6, d), lambda b, i: (b, 0, 0)),
            row_spec(A_WIDTH), row_spec(B_WIDTH), row_spec(C_WIDTH),
            weight(w_out), weight(w_ff1), weight(w_ff2),
            pl.BlockSpec((1, d), lambda b, i: (0, 0)),
        ],
        out_specs=row_spec(d),
        out_shape=jax.ShapeDtypeStruct((bsz, s, d), F32),
        compiler_params=pltpu.CompilerParams(
            dimension_semantics=("arbitrary", "arbitrary"), vmem_limit_bytes=VMEM_LIMIT),
        name="out_proj_ffn",
    )(x, mod_l, oa, ob, oc, w_out, w_ff1, w_ff2, final_gain)


def _rope_tables(s):
    half = HEAD_DIM // 2
    inv = ROPE_THETA ** (-jnp.arange(0, half, 2, dtype=F32) / half)
    t = jnp.arange(s)
    ang_row = (t // GRID_W).astype(F32)[:, None] * inv[None, :]
    ang_col = (t % GRID_W).astype(F32)[:, None] * inv[None, :]
    ang_1d = t.astype(F32)[:, None] * inv[None, :]

    def pair(ang):
        return (jnp.concatenate([jnp.cos(ang), jnp.cos(ang)], axis=-1),
                jnp.concatenate([-jnp.sin(ang), jnp.sin(ang)], axis=-1))

    cr, sr = pair(ang_row)
    cc, sc = pair(ang_col)
    c1, s1 = pair(ang_1d)
    tile = lambda a: jnp.tile(a, (1, LANES // a.shape[1]))
    return (tile(jnp.concatenate([cr, cc], axis=-1)), tile(jnp.concatenate([sr, sc], axis=-1)),
            tile(c1), tile(s1))


def kernel(x, c, w_mod, b_mod, w_in, a_qk_norm, diff_lambda, diff_subln, hgrn_lower_bounds, hgrn_norm,
           w_out, w_ff1, w_ff2, final_norm):
    bsz, s, d = x.shape
    depth = w_in.shape[0]
    tm = min(512, s)
    tq = min(256, s)
    tables = _rope_tables(s)
    mod = _modulation(c, w_mod, b_mod).reshape(depth, bsz, 6, d)
    w_in_b, w_out_b, w_ff1_b, w_ff2_b = (w.astype(BF16) for w in (w_in, w_out, w_ff1, w_ff2))
    final_gain = final_norm.reshape(1, d)
    for l in range(depth):
        qk_gain = jnp.tile(a_qk_norm[l], (1, LANES // HEAD_DIM))
        qa, kat, va, qb, kbt, vb, zc = _in_projection(x, mod[l], w_in_b, l, tables, qk_gain, tm)
        oa = _attention_a(qa, kat, va, tq)
        lam_init = 0.8 - 0.6 * math.exp(-0.3 * l)
        subln = jnp.tile(diff_subln[l].reshape(1, HEAD_DIM), (1, LANES // HEAD_DIM))
        ob = _attention_b(qb, kbt, vb, diff_lambda[l], subln, lam_init, tq)
        hg_gain = jnp.tile(hgrn_norm[l].reshape(1, C_DIM), (1, C_HEADS))
        oc = _hgrn(zc, hgrn_lower_bounds, hg_gain, l)
        x = _out_ffn(x, mod[l], oa, ob, oc, w_out_b, w_ff1_b, w_ff2_b, final_gain, l, l == depth - 1, tm)
    return x
```

```python
import functools
import math

import numpy as np
import jax
import jax.numpy as jnp
from jax import lax
from jax.experimental import pallas as pl
from jax.experimental.pallas import tpu as pltpu

F32 = jnp.float32
BF16 = jnp.bfloat16

HEAD_DIM = 64
GRID_W = 64
ROPE_THETA = 10000.0
EPS = 1e-6
FORGET_FLOOR = 1e-6
A_HEADS, A_KV_HEADS = 6, 2
B_HEADS, B_QK_DIM = 6, 32
C_HEADS, C_DIM = 4, 64
A_WIDTH = A_HEADS * HEAD_DIM
A_KV_WIDTH = A_KV_HEADS * HEAD_DIM
B_WIDTH = B_HEADS * HEAD_DIM
C_WIDTH = C_HEADS * C_DIM
LANES = 128
HGRN_CHUNK = 64
HGRN_LEAF = 8
ONES_ROWS = 16
VT_ROWS = HEAD_DIM + ONES_ROWS
KV_CHUNK = 256
QK_AHEAD = 16
LOG2_E = math.log2(math.e)
VMEM_LIMIT = 56 * 1024 * 1024

OFF_AQ, OFF_AK, OFF_AV = 0, 384, 512
OFF_BQ, OFF_BK, OFF_BV = 640, 1024, 1408
OFF_C = 1792
IN_TOTAL = 3072


def _dot(a, b):
    return jnp.dot(a, b, preferred_element_type=F32)


def _dot_nt(a, b):
    return lax.dot_general(a, b, (((1,), (1,)), ((), ())), preferred_element_type=F32)


def _dot_tn(a, b):
    return lax.dot_general(a, b, (((0,), (0,)), ((), ())), preferred_element_type=F32)


def _split_dot(x, w_bf16):
    hi = x.astype(BF16)
    lo = (x - hi.astype(F32)).astype(BF16)
    return _dot(hi, w_bf16) + _dot(lo, w_bf16)


def _group_ones(n, group):
    r = lax.broadcasted_iota(jnp.int32, (n, n), 0) // group
    c = lax.broadcasted_iota(jnp.int32, (n, n), 1) // group
    return jnp.where(r == c, 1.0 / group, 0.0).astype(BF16)


def _rope(xb, cos, sin_signed):
    lane = lax.broadcasted_iota(jnp.int32, xb.shape, 1)
    low = (lane % 32) < 16
    partner = jnp.where(low, pltpu.roll(xb, LANES - 16, 1), pltpu.roll(xb, 16, 1))
    return xb * cos + partner * sin_signed


def _rms_rows(x):
    return x * lax.rsqrt(jnp.mean(x * x, axis=-1, keepdims=True) + EPS)


def _mod_kernel(c_ref, w_ref, b_ref, o_ref):
    c = c_ref[...]
    cond = (c * jax.nn.sigmoid(c)).astype(BF16)
    o_ref[0] = _dot(cond, w_ref[0]) + b_ref[0]


def _modulation(c, w_mod, b_mod):
    depth, d, n = w_mod.shape
    bsz = c.shape[0]
    tn = 1024
    return pl.pallas_call(
        _mod_kernel,
        grid=(depth, n // tn),
        in_specs=[
            pl.BlockSpec((bsz, d), lambda l, j: (0, 0)),
            pl.BlockSpec((1, d, tn), lambda l, j: (l, 0, j)),
            pl.BlockSpec((1, 1, tn), lambda l, j: (l, 0, j)),
        ],
        out_specs=pl.BlockSpec((1, bsz, tn), lambda l, j: (l, 0, j)),
        out_shape=jax.ShapeDtypeStruct((depth, bsz, n), F32),
        compiler_params=pltpu.CompilerParams(
            dimension_semantics=("arbitrary", "arbitrary"), vmem_limit_bytes=VMEM_LIMIT),
        name="adaln_mod",
    )(c, w_mod.astype(BF16), b_mod.reshape(depth, 1, n))


def _inproj_kernel(x_ref, mod_ref, w_ref, cosa_ref, sina_ref, cosb_ref, sinb_ref, gain_ref,
                   qat_ref, ka_ref, vat_ref, qbt_ref, kb_ref, vbt_ref, zc_ref, h_scr, z_scr):
    x = x_ref[0]
    h = _rms_rows(x) * (1.0 + mod_ref[0, 1:2, :]) + mod_ref[0, 0:1, :]
    h_scr[...] = h.astype(BF16)
    for j in range(IN_TOTAL // 1024):
        z_scr[:, j * 1024:(j + 1) * 1024] = _dot(h_scr[...], w_ref[0, :, j * 1024:(j + 1) * 1024])

    ones = _group_ones(LANES, HEAD_DIM)
    cosa, sina = cosa_ref[...], sina_ref[...]
    cosb, sinb = cosb_ref[...], sinb_ref[...]

    def a_norm_rope(zb, gain):
        ms = _split_dot(zb * zb, ones)
        return _rope(zb * lax.rsqrt(ms + EPS) * gain, cosa, sina)

    def store_vt(vt_ref, first_head, blk_t):
        ones_rows = jnp.ones((ONES_ROWS, blk_t.shape[1]), BF16)
        for j in range(LANES // HEAD_DIM):
            r0 = (first_head + j) * VT_ROWS
            vt_ref[0, r0:r0 + HEAD_DIM, :] = blk_t[j * HEAD_DIM:(j + 1) * HEAD_DIM].astype(BF16)
            vt_ref[0, r0 + HEAD_DIM:r0 + VT_ROWS, :] = ones_rows

    for cblk in range(A_WIDTH // LANES):
        zb = z_scr[:, OFF_AQ + cblk * LANES:OFF_AQ + (cblk + 1) * LANES]
        y = a_norm_rope(zb, gain_ref[0:1, :]) * (HEAD_DIM ** -0.5 * LOG2_E)
        qat_ref[0, cblk * LANES:(cblk + 1) * LANES, :] = y.T.astype(BF16)
    ka_ref[0] = a_norm_rope(z_scr[:, OFF_AK:OFF_AK + LANES], gain_ref[1:2, :]).astype(BF16)
    store_vt(vat_ref, 0, z_scr[:, OFF_AV:OFF_AV + LANES].T)

    for cblk in range(B_WIDTH // LANES):
        lo = cblk * LANES
        zq = z_scr[:, OFF_BQ + lo:OFF_BQ + lo + LANES]
        qbt_ref[0, lo:lo + LANES, :] = (_rope(zq, cosb, sinb) * (B_QK_DIM ** -0.5 * LOG2_E)).T.astype(BF16)
        zk = z_scr[:, OFF_BK + lo:OFF_BK + lo + LANES]
        kb_ref[0, :, lo:lo + LANES] = _rope(zk, cosb, sinb).astype(BF16)
        store_vt(vbt_ref, cblk * (LANES // HEAD_DIM), z_scr[:, OFF_BV + lo:OFF_BV + lo + LANES].T)
    zc_ref[0] = z_scr[:, OFF_C:IN_TOTAL]


def _in_projection(x, mod_l, w_in_bf16, layer, tables, qk_gain, tm):
    bsz, s, d = x.shape
    cosa, sina, cosb, sinb = tables
    row_spec = lambda width: pl.BlockSpec((1, tm, width), lambda b, i: (b, i, 0))
    tab_spec = pl.BlockSpec((tm, LANES), lambda b, i: (i, 0))
    col_spec = lambda height: pl.BlockSpec((1, height, tm), lambda b, i: (b, 0, i))
    return pl.pallas_call(
        _inproj_kernel,
        grid=(bsz, s // tm),
        in_specs=[
            row_spec(d),
            pl.BlockSpec((1, 6, d), lambda b, i: (b, 0, 0)),
            pl.BlockSpec((1, d, IN_TOTAL), lambda b, i: (layer, 0, 0)),
            tab_spec, tab_spec, tab_spec, tab_spec,
            pl.BlockSpec((2, LANES), lambda b, i: (0, 0)),
        ],
        out_specs=[
            col_spec(A_WIDTH), row_spec(A_KV_WIDTH), col_spec(A_KV_HEADS * VT_ROWS),
            col_spec(B_WIDTH), row_spec(B_WIDTH), col_spec(B_HEADS * VT_ROWS),
            row_spec(IN_TOTAL - OFF_C),
        ],
        out_shape=[
            jax.ShapeDtypeStruct((bsz, A_WIDTH, s), BF16),
            jax.ShapeDtypeStruct((bsz, s, A_KV_WIDTH), BF16),
            jax.ShapeDtypeStruct((bsz, A_KV_HEADS * VT_ROWS, s), BF16),
            jax.ShapeDtypeStruct((bsz, B_WIDTH, s), BF16),
            jax.ShapeDtypeStruct((bsz, s, B_WIDTH), BF16),
            jax.ShapeDtypeStruct((bsz, B_HEADS * VT_ROWS, s), BF16),
            jax.ShapeDtypeStruct((bsz, s, IN_TOTAL - OFF_C), F32),
        ],
        scratch_shapes=[pltpu.VMEM((tm, d), BF16), pltpu.VMEM((tm, IN_TOTAL), F32)],
        compiler_params=pltpu.CompilerParams(
            dimension_semantics=("arbitrary", "arbitrary"), vmem_limit_bytes=VMEM_LIMIT),
        name="in_projection",
    )(x, mod_l, w_in_bf16, cosa, sina, cosb, sinb, qk_gain)


def _place_rows(block, start, total):
    parts = []
    if start:
        parts.append(jnp.zeros((start, block.shape[1]), block.dtype))
    parts.append(block)
    rest = total - start - block.shape[0]
    if rest:
        parts.append(jnp.zeros((rest, block.shape[1]), block.dtype))
    return jnp.concatenate(parts, axis=0) if len(parts) > 1 else block


def _attention_items(items, k_ref, vt_ref, qz_scr, m_scr, acc_scr):
    s = k_ref.shape[1]
    tq = acc_scr.shape[2]
    m_scr[...] = jnp.full(m_scr.shape, -1e30, F32)
    acc_scr[...] = jnp.zeros_like(acc_scr)

    units = [(i, c) for c in range(s // KV_CHUNK) for i in range(len(items))]

    def scores(u):
        i, c = units[u]
        k_off = items[i][0]
        return _dot(k_ref[0, c * KV_CHUNK:(c + 1) * KV_CHUNK, k_off:k_off + LANES], qz_scr[i])

    pending = [scores(u) for u in range(min(QK_AHEAD, len(units)))]
    for u, (i, c) in enumerate(units):
        v_off = items[i][1]
        st = pending.pop(0)
        m_old = m_scr[i]
        part = jnp.max(st.reshape(KV_CHUNK // 8, 8, tq), axis=0)
        m_new = jnp.maximum(m_old, jnp.max(part, axis=0, keepdims=True))
        m_scr[i] = m_new
        p = jnp.exp2(st - m_new[0:1]).astype(BF16)
        d = _dot(vt_ref[0, v_off:v_off + VT_ROWS, c * KV_CHUNK:(c + 1) * KV_CHUNK], p)
        if u + QK_AHEAD < len(units):
            pending.append(scores(u + QK_AHEAD))
        acc_scr[i] = acc_scr[i] * jnp.exp2(m_old[0:1] - m_new[0:1]) + d
    results = []
    for i in range(len(items)):
        acc = acc_scr[i]
        results.append(acc[0:HEAD_DIM] * (1.0 / acc[HEAD_DIM:HEAD_DIM + 1]))
    return results


def _attention_scratch(n_items, tq):
    return [pltpu.VMEM((n_items, LANES, tq), BF16), pltpu.VMEM((n_items, 8, tq), F32),
            pltpu.VMEM((n_items, VT_ROWS, tq), F32)]


def _attn_a_kernel(qt_ref, k_ref, vt_ref, o_ref, qz_scr, m_scr, acc_scr):
    group = A_HEADS // A_KV_HEADS
    items = []
    for h in range(A_HEADS):
        g = h // group
        qz_scr[h] = _place_rows(qt_ref[0, h * HEAD_DIM:(h + 1) * HEAD_DIM, :], g * HEAD_DIM, LANES)
        items.append((0, g * VT_ROWS))
    outs = _attention_items(items, k_ref, vt_ref, qz_scr, m_scr, acc_scr)
    o_ref[0] = jnp.concatenate([o.T for o in outs], axis=-1).astype(BF16)


def _attention_a(qat, ka, vat, tq):
    bsz, s, _ = ka.shape
    return pl.pallas_call(
        _attn_a_kernel,
        grid=(bsz, s // tq),
        in_specs=[
            pl.BlockSpec((1, A_WIDTH, tq), lambda b, i: (b, 0, i)),
            pl.BlockSpec((1, s, A_KV_WIDTH), lambda b, i: (b, 0, 0)),
            pl.BlockSpec((1, A_KV_HEADS * VT_ROWS, s), lambda b, i: (b, 0, 0)),
        ],
        out_specs=pl.BlockSpec((1, tq, A_WIDTH), lambda b, i: (b, i, 0)),
        out_shape=jax.ShapeDtypeStruct((bsz, s, A_WIDTH), BF16),
        scratch_shapes=_attention_scratch(A_HEADS, tq),
        compiler_params=pltpu.CompilerParams(
            dimension_semantics=("arbitrary", "arbitrary"), vmem_limit_bytes=VMEM_LIMIT),
        name="attn_axial_gqa",
    )(qat, ka, vat)


def _attn_b_kernel(lam_ref, gain_ref, qt_ref, k_ref, vt_ref, o_ref, qz_scr, m_scr, acc_scr, *, lam_init):
    lp = lam_ref[...]
    lam = (jnp.exp(jnp.sum(lp[0:1] * lp[1:2], axis=-1, keepdims=True))
           - jnp.exp(jnp.sum(lp[2:3] * lp[3:4], axis=-1, keepdims=True)) + lam_init)
    items = []
    for h in range(B_HEADS):
        pair = (h // 2) * LANES
        for comp in range(2):
            r0 = h * HEAD_DIM + comp * B_QK_DIM
            qz_scr[len(items)] = _place_rows(qt_ref[0, r0:r0 + B_QK_DIM, :], r0 - pair, LANES)
            items.append((pair, h * VT_ROWS))
    comps = _attention_items(items, k_ref, vt_ref, qz_scr, m_scr, acc_scr)
    outs = []
    for h in range(B_HEADS):
        o = comps[2 * h] - lam * comps[2 * h + 1]
        o = o * lax.rsqrt(jnp.mean(o * o, axis=0, keepdims=True) + EPS)
        outs.append(o.T)
    y = jnp.concatenate(outs, axis=-1) * gain_ref[...] * (1.0 - lam_init)
    o_ref[0] = y.astype(BF16)


def _attention_b(qbt, kb, vbt, lam_params, subln_gain, lam_init, tq):
    bsz, s, _ = kb.shape
    return pl.pallas_call(
        functools.partial(_attn_b_kernel, lam_init=lam_init),
        grid=(bsz, s // tq),
        in_specs=[
            pl.BlockSpec((4, B_QK_DIM), lambda b, i: (0, 0)),
            pl.BlockSpec((1, B_WIDTH), lambda b, i: (0, 0)),
            pl.BlockSpec((1, B_WIDTH, tq), lambda b, i: (b, 0, i)),
            pl.BlockSpec((1, s, B_WIDTH), lambda b, i: (b, 0, 0)),
            pl.BlockSpec((1, B_HEADS * VT_ROWS, s), lambda b, i: (b, 0, 0)),
        ],
        out_specs=pl.BlockSpec((1, tq, B_WIDTH), lambda b, i: (b, i, 0)),
        out_shape=jax.ShapeDtypeStruct((bsz, s, B_WIDTH), BF16),
        scratch_shapes=_attention_scratch(2 * B_HEADS, tq),
        compiler_params=pltpu.CompilerParams(
            dimension_semantics=("arbitrary", "arbitrary"), vmem_limit_bytes=VMEM_LIMIT),
        name="attn_differential",
    )(lam_params, subln_gain, qbt, kb, vbt)


def _hgrn_levels():
    sizes = []
    bs = HGRN_CHUNK
    while bs > HGRN_LEAF:
        sizes.append(bs)
        bs //= 2
    return sizes


def _hgrn_constants():
    c, w = HGRN_CHUNK, C_WIDTH
    t = np.arange(c)
    tri = np.stack([(t[:, None] >= t[None, :]), (t[:, None] <= t[None, :])]).astype(np.float32)
    s_of_lane = np.arange(w) % C_DIM
    sizes = _hgrn_levels()
    qmask = np.zeros((2, len(sizes), c, w), np.float32)
    amask = np.zeros((2, len(sizes) + 1, c, w), np.float32)
    for d in range(2):
        for li, bs in enumerate(sizes):
            upper = (t % bs) >= bs // 2
            qrow = upper if d == 0 else ~upper
            qmask[d, li] = np.broadcast_to(qrow[:, None], (c, w))
            amask[d, li] = (t[:, None] // bs) == (s_of_lane[None, :] // bs)
        same_leaf = (t[:, None] // HGRN_LEAF) == (s_of_lane[None, :] // HGRN_LEAF)
        order = (s_of_lane[None, :] <= t[:, None]) if d == 0 else (s_of_lane[None, :] >= t[:, None])
        amask[d, len(sizes)] = same_leaf & order
    head_r = np.arange(w)[:, None] // C_DIM
    head_c = np.arange(w)[None, :] // C_DIM
    bdmask = (head_r == head_c).astype(np.float32)
    return tri, qmask, amask, bdmask


def _block_diag(rows_f32, bdmask):
    c = rows_f32.shape[0]
    reps = C_WIDTH // c
    tiled = jnp.concatenate([rows_f32] * reps, axis=0)
    return tiled * bdmask


def _ref_rows(b, bs, offset):
    parts = []
    for lo in range(0, b.shape[0], bs):
        parts.append(jnp.broadcast_to(b[lo + offset:lo + offset + 1, :], (bs, b.shape[1])))
    return jnp.concatenate(parts, axis=0)


def _hgrn_kernel(z_ref, lb_ref, gain_ref, tri_ref, qmask_ref, amask_ref, bd_ref, o_ref,
                 of_scr, ob_scr, st_scr, *, layer, depth, n_chunks):
    c, w = HGRN_CHUNK, C_WIDTH
    sizes = _hgrn_levels()

    rows = [lb_ref[i:i + 1, :] for i in range(depth)]
    mx = functools.reduce(jnp.maximum, rows)
    es = [jnp.exp(r - mx) for r in rows]
    tot = functools.reduce(lambda a, b_: a + b_, es)
    ps = [e / tot for e in es]
    cum = functools.reduce(lambda a, b_: a + b_, ps[:layer + 1])
    lb = jnp.clip(cum - ps[0], 0.0, 1.0)

    bdmask = bd_ref[...]
    st_scr[...] = jnp.zeros_like(st_scr)

    def chunk(ci, d):
        r0 = pl.multiple_of(ci * c, c)
        zq = z_ref[0, pl.ds(r0, c), 0:w]
        zf = z_ref[0, pl.ds(r0, c), (1 + d) * w:(2 + d) * w]
        v = z_ref[0, pl.ds(r0, c), 3 * w:4 * w]
        q = zq * jax.nn.sigmoid(zq) * (C_DIM ** -0.5)
        e = jnp.exp(-jnp.abs(zf))
        r = 1.0 / (1.0 + e)
        pos = zf >= 0.0
        sig_p = jnp.where(pos, r, e * r)
        sig_n = jnp.where(pos, e * r, r)
        g = jnp.log(jnp.maximum(lb + (1.0 - lb) * sig_p, FORGET_FLOOR))
        k = (1.0 - lb) * sig_n

        g_hi = g.astype(BF16)
        g_r = g - g_hi.astype(F32)
        g_mid = g_r.astype(BF16)
        g_lo = (g_r - g_mid.astype(F32)).astype(BF16)
        tri = tri_ref[d]
        b = _dot(tri, g_hi) + _dot(tri, g_mid) + _dot(tri, g_lo)
        total = b[c - 1:c, :] if d == 0 else b[0:1, :]

        kbf = None
        a = jnp.zeros((c, w), F32)
        for li, bs in enumerate(sizes):
            ref = _ref_rows(b, bs, bs // 2 - 1 if d == 0 else bs // 2)
            qm = qmask_ref[d, li]
            arg = jnp.where(qm > 0.5, b - ref, ref - b)
            ex = jnp.exp(arg)
            qp = (q * ex * qm).astype(BF16)
            kp = _block_diag(k * ex * (1.0 - qm), bdmask).astype(BF16)
            a = a + _dot_nt(qp, kp) * amask_ref[d, li]
        ref = _ref_rows(b, HGRN_LEAF, HGRN_LEAF // 2 - 1 if d == 0 else HGRN_LEAF // 2)
        arg = b - ref
        qp = (q * jnp.exp(arg)).astype(BF16)
        kp = _block_diag(k * jnp.exp(-arg), bdmask).astype(BF16)
        a = a + _dot_nt(qp, kp) * amask_ref[d, len(sizes)]

        st = st_scr[d]
        o = _dot(a.astype(BF16), _block_diag(v, bdmask).astype(BF16))
        o = o + _dot_nt((q * jnp.exp(b)).astype(BF16), st.astype(BF16))
        kb = (k * jnp.exp(total - b)).astype(BF16)
        st_scr[d] = st * jnp.exp(total) + _dot_tn(v.astype(BF16), kb) * bdmask
        if d == 0:
            of_scr[pl.ds(r0, c), :] = o
        else:
            ob_scr[pl.ds(r0, c), :] = o

    def body(i, carry):
        chunk(i, 0)
        chunk(n_chunks - 1 - i, 1)
        return carry

    lax.fori_loop(0, n_chunks, body, 0)

    ones = _group_ones(w, C_DIM)
    tile = 256
    for r0 in range(0, n_chunks * c, tile):
        o = of_scr[r0:r0 + tile, :] + ob_scr[r0:r0 + tile, :]
        ms = _split_dot(o * o, ones)
        zg = z_ref[0, r0:r0 + tile, 4 * w:5 * w]
        y = o * lax.rsqrt(ms + EPS) * gain_ref[...] * (zg * jax.nn.sigmoid(zg))
        o_ref[0, r0:r0 + tile, :] = y.astype(BF16)


def _hgrn(zc, lower_bounds, norm_gain, layer):
    bsz, s, zw = zc.shape
    depth = lower_bounds.shape[0]
    n_chunks = s // HGRN_CHUNK
    tri, qmask, amask, bdmask = _hgrn_constants()
    full = lambda a: pl.BlockSpec(a.shape, lambda b: (0,) * a.ndim)
    consts = (jnp.asarray(tri, BF16), jnp.asarray(qmask), jnp.asarray(amask), jnp.asarray(bdmask))
    return pl.pallas_call(
        functools.partial(_hgrn_kernel, layer=layer, depth=depth, n_chunks=n_chunks),
        grid=(bsz,),
        in_specs=[
            pl.BlockSpec((1, s, zw), lambda b: (b, 0, 0)),
            full(lower_bounds), full(norm_gain),
        ] + [full(a) for a in consts],
        out_specs=pl.BlockSpec((1, s, C_WIDTH), lambda b: (b, 0, 0)),
        out_shape=jax.ShapeDtypeStruct((bsz, s, C_WIDTH), BF16),
        scratch_shapes=[
            pltpu.VMEM((s, C_WIDTH), F32), pltpu.VMEM((s, C_WIDTH), F32),
            pltpu.VMEM((2, C_WIDTH, C_WIDTH), F32),
        ],
        compiler_params=pltpu.CompilerParams(
            dimension_semantics=("arbitrary",), vmem_limit_bytes=VMEM_LIMIT),
        name="hgrn2_bidir",
    )(zc, lower_bounds, norm_gain, *consts)


def _out_ffn_kernel(x_ref, mod_ref, oa_ref, ob_ref, oc_ref, wo_ref, w1_ref, w2_ref, fg_ref, o_ref,
                    *, final, ff_chunk):
    mix = jnp.concatenate([oa_ref[0], ob_ref[0], oc_ref[0]], axis=-1)
    x1 = x_ref[0] + mod_ref[0, 2:3, :] * _dot(mix, wo_ref[0])
    h = (_rms_rows(x1) * (1.0 + mod_ref[0, 4:5, :]) + mod_ref[0, 3:4, :]).astype(BF16)
    d_ff = w1_ref.shape[2]
    acc = jnp.zeros_like(x1)
    for j in range(d_ff // ff_chunk):
        u = _dot(h, w1_ref[0, :, j * ff_chunk:(j + 1) * ff_chunk])
        u = jnp.square(jnp.maximum(u, 0.0)).astype(BF16)
        acc = acc + _dot(u, w2_ref[0, j * ff_chunk:(j + 1) * ff_chunk, :])
    x2 = x1 + mod_ref[0, 5:6, :] * acc
    if final:
        x2 = _rms_rows(x2) * fg_ref[...]
    o_ref[0] = x2


def _out_ffn(x, mod_l, oa, ob, oc, w_out, w_ff1, w_ff2, final_gain, layer, final, tm):
    bsz, s, d = x.shape
    d_ff = w_ff1.shape[2]
    row_spec = lambda width: pl.BlockSpec((1, tm, width), lambda b, i: (b, i, 0))
    weight = lambda a: pl.BlockSpec((1,) + a.shape[1:], lambda b, i: (layer, 0, 0),
                                    pipeline_mode=pl.Buffered(1))
    return pl.pallas_call(
        functools.partial(_out_ffn_kernel, final=final, ff_chunk=1024),
        grid=(bsz, s // tm),
        in_specs=[
            row_spec(d),
            pl.BlockSpec((1, 6, d), lambda b, i: (b, 0, 0)),
            row_spec(A_WIDTH), row_spec(B_WIDTH), row_spec(C_WIDTH),
            weight(w_out), weight(w_ff1), weight(w_ff2),
            pl.BlockSpec((1, d), lambda b, i: (0, 0)),
        ],
        out_specs=row_spec(d),
        out_shape=jax.ShapeDtypeStruct((bsz, s, d), F32),
        compiler_params=pltpu.CompilerParams(
            dimension_semantics=("arbitrary", "arbitrary"), vmem_limit_bytes=VMEM_LIMIT),
        name="out_proj_ffn",
    )(x, mod_l, oa, ob, oc, w_out, w_ff1, w_ff2, final_gain)


def _rope_tables(s):
    half = HEAD_DIM // 2
    inv = ROPE_THETA ** (-jnp.arange(0, half, 2, dtype=F32) / half)
    t = jnp.arange(s)
    ang_row = (t // GRID_W).astype(F32)[:, None] * inv[None, :]
    ang_col = (t % GRID_W).astype(F32)[:, None] * inv[None, :]
    ang_1d = t.astype(F32)[:, None] * inv[None, :]

    def pair(ang):
        return (jnp.concatenate([jnp.cos(ang), jnp.cos(ang)], axis=-1),
                jnp.concatenate([-jnp.sin(ang), jnp.sin(ang)], axis=-1))

    cr, sr = pair(ang_row)
    cc, sc = pair(ang_col)
    c1, s1 = pair(ang_1d)
    tile = lambda a: jnp.tile(a, (1, LANES // a.shape[1]))
    return (tile(jnp.concatenate([cr, cc], axis=-1)), tile(jnp.concatenate([sr, sc], axis=-1)),
            tile(c1), tile(s1))


def kernel(x, c, w_mod, b_mod, w_in, a_qk_norm, diff_lambda, diff_subln, hgrn_lower_bounds, hgrn_norm,
           w_out, w_ff1, w_ff2, final_norm):
    bsz, s, d = x.shape
    depth = w_in.shape[0]
    tm = min(512, s)
    tq = min(256, s)
    tables = _rope_tables(s)
    mod = _modulation(c, w_mod, b_mod).reshape(depth, bsz, 6, d)
    w_in_b, w_out_b, w_ff1_b, w_ff2_b = (w.astype(BF16) for w in (w_in, w_out, w_ff1, w_ff2))
    final_gain = final_norm.reshape(1, d)
    for l in range(depth):
        qk_gain = jnp.tile(a_qk_norm[l], (1, LANES // HEAD_DIM))
        qat, ka, vat, qbt, kb, vbt, zc = _in_projection(x, mod[l], w_in_b, l, tables, qk_gain, tm)
        oa = _attention_a(qat, ka, vat, tq)
        lam_init = 0.8 - 0.6 * math.exp(-0.3 * l)
        subln = jnp.tile(diff_subln[l].reshape(1, HEAD_DIM), (1, B_HEADS))
        ob = _attention_b(qbt, kb, vbt, diff_lambda[l], subln, lam_init, tq)
        hg_gain = jnp.tile(hgrn_norm[l].reshape(1, C_DIM), (1, C_HEADS))
        oc = _hgrn(zc, hgrn_lower_bounds, hg_gain, l)
        x = _out_ffn(x, mod[l], oa, ob, oc, w_out_b, w_ff1_b, w_ff2_b, final_gain, l, l == depth - 1, tm)
    return x
```

```python
import functools
import math

import numpy as np
import jax
import jax.numpy as jnp
from jax import lax
from jax.experimental import pallas as pl
from jax.experimental.pallas import tpu as pltpu

F32 = jnp.float32
BF16 = jnp.bfloat16

HEAD_DIM = 64
GRID_W = 64
ROPE_THETA = 10000.0
EPS = 1e-6
FORGET_FLOOR = 1e-6
A_HEADS, A_KV_HEADS = 6, 2
B_HEADS, B_QK_DIM = 6, 32
C_HEADS, C_DIM = 4, 64
A_WIDTH = A_HEADS * HEAD_DIM
A_KV_WIDTH = A_KV_HEADS * HEAD_DIM
B_WIDTH = B_HEADS * HEAD_DIM
C_WIDTH = C_HEADS * C_DIM
LANES = 128
HGRN_CHUNK = 64
HGRN_LEAF = 8
HGRN_BATCH = 4
ONES_ROWS = 16
VT_ROWS = HEAD_DIM + ONES_ROWS
KV_CHUNK = 256
QK_AHEAD = 16
LOG2_E = math.log2(math.e)
VMEM_LIMIT = 56 * 1024 * 1024

OFF_AQ, OFF_AK, OFF_AV = 0, 384, 512
OFF_BQ, OFF_BK, OFF_BV = 640, 1024, 1408
OFF_C = 1792
IN_TOTAL = 3072


def _dot(a, b):
    return jnp.dot(a, b, preferred_element_type=F32)


def _dot_nt(a, b):
    return lax.dot_general(a, b, (((1,), (1,)), ((), ())), preferred_element_type=F32)


def _dot_tn(a, b):
    return lax.dot_general(a, b, (((0,), (0,)), ((), ())), preferred_element_type=F32)


def _split_dot(x, w_bf16):
    hi = x.astype(BF16)
    lo = (x - hi.astype(F32)).astype(BF16)
    return _dot(hi, w_bf16) + _dot(lo, w_bf16)


def _group_ones(n, group):
    r = lax.broadcasted_iota(jnp.int32, (n, n), 0) // group
    c = lax.broadcasted_iota(jnp.int32, (n, n), 1) // group
    return jnp.where(r == c, 1.0 / group, 0.0).astype(BF16)


def _rope(xb, cos, sin_signed):
    lane = lax.broadcasted_iota(jnp.int32, xb.shape, 1)
    low = (lane % 32) < 16
    partner = jnp.where(low, pltpu.roll(xb, LANES - 16, 1), pltpu.roll(xb, 16, 1))
    return xb * cos + partner * sin_signed


def _rms_rows(x):
    return x * lax.rsqrt(jnp.mean(x * x, axis=-1, keepdims=True) + EPS)


def _mod_kernel(c_ref, w_ref, b_ref, o_ref):
    c = c_ref[...]
    cond = (c * jax.nn.sigmoid(c)).astype(BF16)
    o_ref[0] = _dot(cond, w_ref[0]) + b_ref[0]


def _modulation(c, w_mod, b_mod):
    depth, d, n = w_mod.shape
    bsz = c.shape[0]
    tn = 1024
    return pl.pallas_call(
        _mod_kernel,
        grid=(depth, n // tn),
        in_specs=[
            pl.BlockSpec((bsz, d), lambda l, j: (0, 0)),
            pl.BlockSpec((1, d, tn), lambda l, j: (l, 0, j)),
            pl.BlockSpec((1, 1, tn), lambda l, j: (l, 0, j)),
        ],
        out_specs=pl.BlockSpec((1, bsz, tn), lambda l, j: (l, 0, j)),
        out_shape=jax.ShapeDtypeStruct((depth, bsz, n), F32),
        compiler_params=pltpu.CompilerParams(
            dimension_semantics=("arbitrary", "arbitrary"), vmem_limit_bytes=VMEM_LIMIT),
        name="adaln_mod",
    )(c, w_mod.astype(BF16), b_mod.reshape(depth, 1, n))


def _inproj_kernel(x_ref, mod_ref, w_ref, cosa_ref, sina_ref, cosb_ref, sinb_ref, gain_ref,
                   qat_ref, ka_ref, vat_ref, qbt_ref, kb_ref, vbt_ref, zc_ref, h_scr, z_scr):
    x = x_ref[0]
    h = _rms_rows(x) * (1.0 + mod_ref[0, 1:2, :]) + mod_ref[0, 0:1, :]
    h_scr[...] = h.astype(BF16)
    for j in range(IN_TOTAL // 1024):
        z_scr[:, j * 1024:(j + 1) * 1024] = _dot(h_scr[...], w_ref[0, :, j * 1024:(j + 1) * 1024])

    ones = _group_ones(LANES, HEAD_DIM)
    cosa, sina = cosa_ref[...], sina_ref[...]
    cosb, sinb = cosb_ref[...], sinb_ref[...]

    def a_norm_rope(zb, gain):
        ms = _split_dot(zb * zb, ones)
        return _rope(zb * lax.rsqrt(ms + EPS) * gain, cosa, sina)

    def store_vt(vt_ref, first_head, blk_t):
        ones_rows = jnp.ones((ONES_ROWS, blk_t.shape[1]), BF16)
        for j in range(LANES // HEAD_DIM):
            r0 = (first_head + j) * VT_ROWS
            vt_ref[0, r0:r0 + HEAD_DIM, :] = blk_t[j * HEAD_DIM:(j + 1) * HEAD_DIM].astype(BF16)
            vt_ref[0, r0 + HEAD_DIM:r0 + VT_ROWS, :] = ones_rows

    for cblk in range(A_WIDTH // LANES):
        zb = z_scr[:, OFF_AQ + cblk * LANES:OFF_AQ + (cblk + 1) * LANES]
        y = a_norm_rope(zb, gain_ref[0:1, :]) * (HEAD_DIM ** -0.5 * LOG2_E)
        qat_ref[0, cblk * LANES:(cblk + 1) * LANES, :] = y.T.astype(BF16)
    ka_ref[0] = a_norm_rope(z_scr[:, OFF_AK:OFF_AK + LANES], gain_ref[1:2, :]).astype(BF16)
    store_vt(vat_ref, 0, z_scr[:, OFF_AV:OFF_AV + LANES].T)

    for cblk in range(B_WIDTH // LANES):
        lo = cblk * LANES
        zq = z_scr[:, OFF_BQ + lo:OFF_BQ + lo + LANES]
        qbt_ref[0, lo:lo + LANES, :] = (_rope(zq, cosb, sinb) * (B_QK_DIM ** -0.5 * LOG2_E)).T.astype(BF16)
        zk = z_scr[:, OFF_BK + lo:OFF_BK + lo + LANES]
        kb_ref[0, :, lo:lo + LANES] = _rope(zk, cosb, sinb).astype(BF16)
        store_vt(vbt_ref, cblk * (LANES // HEAD_DIM), z_scr[:, OFF_BV + lo:OFF_BV + lo + LANES].T)
    zc_ref[0] = z_scr[:, OFF_C:IN_TOTAL]


def _in_projection(x, mod_l, w_in_bf16, layer, tables, qk_gain, tm):
    bsz, s, d = x.shape
    cosa, sina, cosb, sinb = tables
    row_spec = lambda width: pl.BlockSpec((1, tm, width), lambda b, i: (b, i, 0))
    tab_spec = pl.BlockSpec((tm, LANES), lambda b, i: (i, 0))
    col_spec = lambda height: pl.BlockSpec((1, height, tm), lambda b, i: (b, 0, i))
    return pl.pallas_call(
        _inproj_kernel,
        grid=(bsz, s // tm),
        in_specs=[
            row_spec(d),
            pl.BlockSpec((1, 6, d), lambda b, i: (b, 0, 0)),
            pl.BlockSpec((1, d, IN_TOTAL), lambda b, i: (layer, 0, 0)),
            tab_spec, tab_spec, tab_spec, tab_spec,
            pl.BlockSpec((2, LANES), lambda b, i: (0, 0)),
        ],
        out_specs=[
            col_spec(A_WIDTH), row_spec(A_KV_WIDTH), col_spec(A_KV_HEADS * VT_ROWS),
            col_spec(B_WIDTH), row_spec(B_WIDTH), col_spec(B_HEADS * VT_ROWS),
            row_spec(IN_TOTAL - OFF_C),
        ],
        out_shape=[
            jax.ShapeDtypeStruct((bsz, A_WIDTH, s), BF16),
            jax.ShapeDtypeStruct((bsz, s, A_KV_WIDTH), BF16),
            jax.ShapeDtypeStruct((bsz, A_KV_HEADS * VT_ROWS, s), BF16),
            jax.ShapeDtypeStruct((bsz, B_WIDTH, s), BF16),
            jax.ShapeDtypeStruct((bsz, s, B_WIDTH), BF16),
            jax.ShapeDtypeStruct((bsz, B_HEADS * VT_ROWS, s), BF16),
            jax.ShapeDtypeStruct((bsz, s, IN_TOTAL - OFF_C), F32),
        ],
        scratch_shapes=[pltpu.VMEM((tm, d), BF16), pltpu.VMEM((tm, IN_TOTAL), F32)],
        compiler_params=pltpu.CompilerParams(
            dimension_semantics=("arbitrary", "arbitrary"), vmem_limit_bytes=VMEM_LIMIT),
        name="in_projection",
    )(x, mod_l, w_in_bf16, cosa, sina, cosb, sinb, qk_gain)


def _place_rows(block, start, total):
    parts = []
    if start:
        parts.append(jnp.zeros((start, block.shape[1]), block.dtype))
    parts.append(block)
    rest = total - start - block.shape[0]
    if rest:
        parts.append(jnp.zeros((rest, block.shape[1]), block.dtype))
    return jnp.concatenate(parts, axis=0) if len(parts) > 1 else block


def _attention_items(items, k_ref, vt_ref, qz_scr, m_scr, acc_scr):
    s = k_ref.shape[1]
    tq = acc_scr.shape[2]
    m_scr[...] = jnp.full(m_scr.shape, -1e30, F32)
    acc_scr[...] = jnp.zeros_like(acc_scr)

    units = [(i, c) for c in range(s // KV_CHUNK) for i in range(len(items))]

    def scores(u):
        i, c = units[u]
        k_off = items[i][0]
        return _dot(k_ref[0, c * KV_CHUNK:(c + 1) * KV_CHUNK, k_off:k_off + LANES], qz_scr[i])

    pending = [scores(u) for u in range(min(QK_AHEAD, len(units)))]
    for u, (i, c) in enumerate(units):
        v_off = items[i][1]
        st = pending.pop(0)
        m_old = m_scr[i]
        part = jnp.max(st.reshape(KV_CHUNK // 8, 8, tq), axis=0)
        m_new = jnp.maximum(m_old, jnp.max(part, axis=0, keepdims=True))
        m_scr[i] = m_new
        p = jnp.exp2(st - m_new[0:1]).astype(BF16)
        d = _dot(vt_ref[0, v_off:v_off + VT_ROWS, c * KV_CHUNK:(c + 1) * KV_CHUNK], p)
        if u + QK_AHEAD < len(units):
            pending.append(scores(u + QK_AHEAD))
        acc_scr[i] = acc_scr[i] * jnp.exp2(m_old[0:1] - m_new[0:1]) + d
    results = []
    for i in range(len(items)):
        acc = acc_scr[i]
        results.append(acc[0:HEAD_DIM] * (1.0 / acc[HEAD_DIM:HEAD_DIM + 1]))
    return results


def _attention_scratch(n_items, tq):
    return [pltpu.VMEM((n_items, LANES, tq), BF16), pltpu.VMEM((n_items, 8, tq), F32),
            pltpu.VMEM((n_items, VT_ROWS, tq), F32)]


def _attn_a_kernel(qt_ref, k_ref, vt_ref, o_ref, qz_scr, m_scr, acc_scr):
    group = A_HEADS // A_KV_HEADS
    items = []
    for h in range(A_HEADS):
        g = h // group
        qz_scr[h] = _place_rows(qt_ref[0, h * HEAD_DIM:(h + 1) * HEAD_DIM, :], g * HEAD_DIM, LANES)
        items.append((0, g * VT_ROWS))
    outs = _attention_items(items, k_ref, vt_ref, qz_scr, m_scr, acc_scr)
    o_ref[0] = jnp.concatenate([o.T for o in outs], axis=-1).astype(BF16)


def _attention_a(qat, ka, vat, tq):
    bsz, s, _ = ka.shape
    return pl.pallas_call(
        _attn_a_kernel,
        grid=(bsz, s // tq),
        in_specs=[
            pl.BlockSpec((1, A_WIDTH, tq), lambda b, i: (b, 0, i)),
            pl.BlockSpec((1, s, A_KV_WIDTH), lambda b, i: (b, 0, 0)),
            pl.BlockSpec((1, A_KV_HEADS * VT_ROWS, s), lambda b, i: (b, 0, 0)),
        ],
        out_specs=pl.BlockSpec((1, tq, A_WIDTH), lambda b, i: (b, i, 0)),
        out_shape=jax.ShapeDtypeStruct((bsz, s, A_WIDTH), BF16),
        scratch_shapes=_attention_scratch(A_HEADS, tq),
        compiler_params=pltpu.CompilerParams(
            dimension_semantics=("arbitrary", "arbitrary"), vmem_limit_bytes=VMEM_LIMIT),
        name="attn_axial_gqa",
    )(qat, ka, vat)


def _attn_b_kernel(lam_ref, gain_ref, qt_ref, k_ref, vt_ref, o_ref, qz_scr, m_scr, acc_scr, *, lam_init):
    lp = lam_ref[...]
    lam = (jnp.exp(jnp.sum(lp[0:1] * lp[1:2], axis=-1, keepdims=True))
           - jnp.exp(jnp.sum(lp[2:3] * lp[3:4], axis=-1, keepdims=True)) + lam_init)
    items = []
    for h in range(B_HEADS):
        pair = (h // 2) * LANES
        for comp in range(2):
            r0 = h * HEAD_DIM + comp * B_QK_DIM
            qz_scr[len(items)] = _place_rows(qt_ref[0, r0:r0 + B_QK_DIM, :], r0 - pair, LANES)
            items.append((pair, h * VT_ROWS))
    comps = _attention_items(items, k_ref, vt_ref, qz_scr, m_scr, acc_scr)
    outs = []
    for h in range(B_HEADS):
        o = comps[2 * h] - lam * comps[2 * h + 1]
        o = o * lax.rsqrt(jnp.mean(o * o, axis=0, keepdims=True) + EPS)
        outs.append(o.T)
    y = jnp.concatenate(outs, axis=-1) * gain_ref[...] * (1.0 - lam_init)
    o_ref[0] = y.astype(BF16)


def _attention_b(qbt, kb, vbt, lam_params, subln_gain, lam_init, tq):
    bsz, s, _ = kb.shape
    return pl.pallas_call(
        functools.partial(_attn_b_kernel, lam_init=lam_init),
        grid=(bsz, s // tq),
        in_specs=[
            pl.BlockSpec((4, B_QK_DIM), lambda b, i: (0, 0)),
            pl.BlockSpec((1, B_WIDTH), lambda b, i: (0, 0)),
            pl.BlockSpec((1, B_WIDTH, tq), lambda b, i: (b, 0, i)),
            pl.BlockSpec((1, s, B_WIDTH), lambda b, i: (b, 0, 0)),
            pl.BlockSpec((1, B_HEADS * VT_ROWS, s), lambda b, i: (b, 0, 0)),
        ],
        out_specs=pl.BlockSpec((1, tq, B_WIDTH), lambda b, i: (b, i, 0)),
        out_shape=jax.ShapeDtypeStruct((bsz, s, B_WIDTH), BF16),
        scratch_shapes=_attention_scratch(2 * B_HEADS, tq),
        compiler_params=pltpu.CompilerParams(
            dimension_semantics=("arbitrary", "arbitrary"), vmem_limit_bytes=VMEM_LIMIT),
        name="attn_differential",
    )(lam_params, subln_gain, qbt, kb, vbt)


def _hgrn_levels():
    sizes = []
    bs = HGRN_CHUNK
    while bs > HGRN_LEAF:
        sizes.append(bs)
        bs //= 2
    return sizes


def _hgrn_constants():
    c, w = HGRN_CHUNK, C_WIDTH
    t = np.arange(c)
    tri = np.stack([(t[:, None] >= t[None, :]), (t[:, None] <= t[None, :])]).astype(np.float32)
    s_of_lane = np.arange(w) % C_DIM
    sizes = _hgrn_levels()
    qmask = np.zeros((2, len(sizes), 3, c, w), np.float32)
    level = np.full((2, c, w), float(len(sizes) + 1), np.float32)
    for d in range(2):
        for li, bs in enumerate(sizes):
            upper = (t % bs) >= bs // 2
            qrow = upper if d == 0 else ~upper
            qmask[d, li, 0] = np.broadcast_to(qrow[:, None], (c, w))
            qmask[d, li, 1] = 1.0 - qmask[d, li, 0]
            qmask[d, li, 2] = qmask[d, li, 0] - qmask[d, li, 1]
            same = (t[:, None] // bs) == (s_of_lane[None, :] // bs)
            cross = qrow[:, None] & ~qrow[s_of_lane][None, :]
            level[d][same & cross] = li
        same_leaf = (t[:, None] // HGRN_LEAF) == (s_of_lane[None, :] // HGRN_LEAF)
        order = (s_of_lane[None, :] <= t[:, None]) if d == 0 else (s_of_lane[None, :] >= t[:, None])
        level[d][same_leaf & order] = len(sizes)
    half = (np.arange(LANES)[None, :] < C_DIM).astype(np.float32) * np.ones((c, 1), np.float32)
    return tri, qmask, level, half


def _block_diag(x, low_half):
    zero = jnp.zeros((x.shape[0], LANES), x.dtype)
    blocks = []
    for h in range(C_HEADS):
        tile = x[:, (h // 2) * LANES:(h // 2 + 1) * LANES]
        keep = tile * (low_half if h % 2 == 0 else 1 - low_half)
        blocks.append(jnp.concatenate([keep, zero] if h // 2 == 0 else [zero, keep], axis=1))
    return jnp.concatenate(blocks, axis=0)


def _ref_rows(b, bs, offset):
    parts = []
    for lo in range(0, b.shape[0], bs):
        parts.append(jnp.broadcast_to(b[lo + offset:lo + offset + 1, :], (bs, b.shape[1])))
    return jnp.concatenate(parts, axis=0)


def _hgrn_kernel(lb_ref, tri_ref, qmask_ref, level_ref, half_ref,
                 zq_f, zf_f, zv_f, zq_b, zf_b, zv_b, of_ref, ob_ref, st_scr, *, layer, depth, nb):
    c, w = HGRN_CHUNK, C_WIDTH
    sizes = _hgrn_levels()
    n_lvl = len(sizes)

    @pl.when(pl.program_id(1) == 0)
    def _():
        st_scr[...] = jnp.zeros_like(st_scr)

    rows = [lb_ref[i:i + 1, :] for i in range(depth)]
    mx = functools.reduce(jnp.maximum, rows)
    es = [jnp.exp(r - mx) for r in rows]
    tot = functools.reduce(lambda a, b_: a + b_, es)
    ps = [e / tot for e in es]
    cum = functools.reduce(lambda a, b_: a + b_, ps[:layer + 1])
    one_minus_lb = 1.0 - jnp.clip(cum - ps[0], 0.0, 1.0)

    low_half = half_ref[...].astype(BF16)
    low_bool = half_ref[...] > 0.5
    owner = [[level_ref[d] == float(li) for li in range(n_lvl + 1)] for d in range(2)]
    chains = [(j, d) for j in range(nb) for d in range(2)]
    zrefs = {0: (zq_f, zf_f, zv_f), 1: (zq_b, zf_b, zv_b)}

    q, k, v, b, total = {}, {}, {}, {}, {}
    g_split = {}
    for ch in chains:
        j, d = ch
        zq, zf = zrefs[d][0][j], zrefs[d][1][j]
        v[ch] = zrefs[d][2][j].astype(BF16)
        q[ch] = (zq * (C_DIM ** -0.5)) / (1.0 + jnp.exp2(zq * -LOG2_E))
        k[ch] = one_minus_lb * (1.0 - 1.0 / (1.0 + jnp.exp2(zf * -LOG2_E)))
        g = jnp.log2(jnp.maximum(1.0 - k[ch], FORGET_FLOOR))
        g_hi = g.astype(BF16)
        g_r = g - g_hi.astype(F32)
        g_mid = g_r.astype(BF16)
        g_lo = (g_r - g_mid.astype(F32)).astype(BF16)
        g_split[ch] = jnp.concatenate([g_hi, g_mid, g_lo], axis=1)
    for ch in chains:
        d = ch[1]
        bb = _dot(tri_ref[d], g_split[ch])
        b[ch] = bb[:, 0:w] + bb[:, w:2 * w] + bb[:, 2 * w:3 * w]
        total[ch] = b[ch][c - 1:c, :] if d == 0 else b[ch][0:1, :]

    parts = {ch: [] for ch in chains}
    for li in range(n_lvl + 1):
        for ch in chains:
            d = ch[1]
            if li < n_lvl:
                bs = sizes[li]
                ref = _ref_rows(b[ch], bs, bs // 2 - 1 if d == 0 else bs // 2)
                ex = jnp.exp2((b[ch] - ref) * qmask_ref[d, li, 2])
                qp = (q[ch] * ex * qmask_ref[d, li, 0]).astype(BF16)
                kp = (k[ch] * ex * qmask_ref[d, li, 1]).astype(BF16)
            else:
                ref = _ref_rows(b[ch], HGRN_LEAF, HGRN_LEAF // 2 - 1 if d == 0 else HGRN_LEAF // 2)
                arg = b[ch] - ref
                qp = (q[ch] * jnp.exp2(arg)).astype(BF16)
                kp = (k[ch] * jnp.exp2(-arg)).astype(BF16)
            parts[ch].append(_dot_nt(qp, _block_diag(kp, low_half)))
    a = {}
    for ch in chains:
        acc = jnp.zeros((c, w), F32)
        for li in range(n_lvl, -1, -1):
            acc = jnp.where(owner[ch[1]][li], parts[ch][li], acc)
        a[ch] = acc.astype(BF16)

    o_intra, o_inter, upd = {}, {}, {}
    for ch in chains:
        j, d = ch
        st = st_scr[d, j]
        o_intra[ch] = _dot(a[ch], _block_diag(v[ch], low_half))
        o_inter[ch] = _dot_nt((q[ch] * jnp.exp2(b[ch])).astype(BF16), _block_diag(st.astype(BF16), low_half))
        kb = (k[ch] * jnp.exp2(total[ch] - b[ch])).astype(BF16)
        tiles = []
        for t in range(w // LANES):
            full = _dot_tn(v[ch][:, t * LANES:(t + 1) * LANES], kb[:, t * LANES:(t + 1) * LANES])
            tiles.append(jnp.where(low_bool, full[0:C_DIM], full[C_DIM:2 * C_DIM]))
        upd[ch] = jnp.concatenate(tiles, axis=1)
    for ch in chains:
        j, d = ch
        st_scr[d, j] = st_scr[d, j] * jnp.exp2(total[ch]) + upd[ch]
        (of_ref if d == 0 else ob_ref)[j] = o_intra[ch] + o_inter[ch]


def _hgrn(zc, lower_bounds, layer, nb):
    bsz, s, _ = zc.shape
    depth = lower_bounds.shape[0]
    n_chunks = s // HGRN_CHUNK
    c, w = HGRN_CHUNK, C_WIDTH
    tri, qmask, level, half = _hgrn_constants()
    consts = (jnp.asarray(tri, BF16), jnp.asarray(qmask), jnp.asarray(level), jnp.asarray(half))
    full = lambda a: pl.BlockSpec(a.shape, lambda g, i: (0,) * a.ndim)
    fwd = lambda col: pl.BlockSpec((nb, c, w), lambda g, i: (g, i, col))
    bwd = lambda col: pl.BlockSpec((nb, c, w), lambda g, i: (g, n_chunks - 1 - i, col))
    return pl.pallas_call(
        functools.partial(_hgrn_kernel, layer=layer, depth=depth, nb=nb),
        grid=(bsz // nb, n_chunks),
        in_specs=[full(lower_bounds)] + [full(a) for a in consts]
                 + [fwd(0), fwd(1), fwd(3), bwd(0), bwd(2), bwd(3)],
        out_specs=[pl.BlockSpec((nb, c, w), lambda g, i: (g, i, 0)),
                   pl.BlockSpec((nb, c, w), lambda g, i: (g, n_chunks - 1 - i, 0))],
        out_shape=[jax.ShapeDtypeStruct((bsz, s, w), F32)] * 2,
        scratch_shapes=[pltpu.VMEM((2, nb, C_DIM, w), F32)],
        compiler_params=pltpu.CompilerParams(
            dimension_semantics=("arbitrary", "arbitrary"), vmem_limit_bytes=VMEM_LIMIT),
        name="hgrn2_bidir",
    )(lower_bounds, *consts, zc, zc, zc, zc, zc, zc)


def _out_ffn_kernel(x_ref, mod_ref, oa_ref, ob_ref, cf_ref, cb_ref, zg_ref, cgain_ref,
                    wo_ref, w1_ref, w2_ref, fg_ref, o_ref, *, final, ff_chunk):
    oc = cf_ref[0] + cb_ref[0]
    ms = _split_dot(oc * oc, _group_ones(C_WIDTH, C_DIM))
    zg = zg_ref[0]
    oc = oc * lax.rsqrt(ms + EPS) * cgain_ref[...] * (zg * jax.nn.sigmoid(zg))
    mix = jnp.concatenate([oa_ref[0], ob_ref[0], oc.astype(BF16)], axis=-1)
    x1 = x_ref[0] + mod_ref[0, 2:3, :] * _dot(mix, wo_ref[0])
    h = (_rms_rows(x1) * (1.0 + mod_ref[0, 4:5, :]) + mod_ref[0, 3:4, :]).astype(BF16)
    d_ff = w1_ref.shape[2]
    acc = jnp.zeros_like(x1)
    for j in range(d_ff // ff_chunk):
        u = _dot(h, w1_ref[0, :, j * ff_chunk:(j + 1) * ff_chunk])
        u = jnp.square(jnp.maximum(u, 0.0)).astype(BF16)
        acc = acc + _dot(u, w2_ref[0, j * ff_chunk:(j + 1) * ff_chunk, :])
    x2 = x1 + mod_ref[0, 5:6, :] * acc
    if final:
        x2 = _rms_rows(x2) * fg_ref[...]
    o_ref[0] = x2


def _out_ffn(x, mod_l, oa, ob, oc_f, oc_b, zc, c_gain, w_out, w_ff1, w_ff2, final_gain, layer, final, tm):
    bsz, s, d = x.shape
    d_ff = w_ff1.shape[2]
    row_spec = lambda width: pl.BlockSpec((1, tm, width), lambda b, i: (b, i, 0))
    weight = lambda a: pl.BlockSpec((1,) + a.shape[1:], lambda b, i: (layer, 0, 0),
                                    pipeline_mode=pl.Buffered(1))
    return pl.pallas_call(
        functools.partial(_out_ffn_kernel, final=final, ff_chunk=1024),
        grid=(bsz, s // tm),
        in_specs=[
            row_spec(d),
            pl.BlockSpec((1, 6, d), lambda b, i: (b, 0, 0)),
            row_spec(A_WIDTH), row_spec(B_WIDTH), row_spec(C_WIDTH), row_spec(C_WIDTH),
            pl.BlockSpec((1, tm, C_WIDTH), lambda b, i: (b, i, 4)),
            pl.BlockSpec((1, C_WIDTH), lambda b, i: (0, 0)),
            weight(w_out), weight(w_ff1), weight(w_ff2),
            pl.BlockSpec((1, d), lambda b, i: (0, 0)),
        ],
        out_specs=row_spec(d),
        out_shape=jax.ShapeDtypeStruct((bsz, s, d), F32),
        compiler_params=pltpu.CompilerParams(
            dimension_semantics=("arbitrary", "arbitrary"), vmem_limit_bytes=VMEM_LIMIT),
        name="out_proj_ffn",
    )(x, mod_l, oa, ob, oc_f, oc_b, zc, c_gain, w_out, w_ff1, w_ff2, final_gain)


def _rope_tables(s):
    half = HEAD_DIM // 2
    inv = ROPE_THETA ** (-jnp.arange(0, half, 2, dtype=F32) / half)
    t = jnp.arange(s)
    ang_row = (t // GRID_W).astype(F32)[:, None] * inv[None, :]
    ang_col = (t % GRID_W).astype(F32)[:, None] * inv[None, :]
    ang_1d = t.astype(F32)[:, None] * inv[None, :]

    def pair(ang):
        return (jnp.concatenate([jnp.cos(ang), jnp.cos(ang)], axis=-1),
                jnp.concatenate([-jnp.sin(ang), jnp.sin(ang)], axis=-1))

    cr, sr = pair(ang_row)
    cc, sc = pair(ang_col)
    c1, s1 = pair(ang_1d)
    tile = lambda a: jnp.tile(a, (1, LANES // a.shape[1]))
    return (tile(jnp.concatenate([cr, cc], axis=-1)), tile(jnp.concatenate([sr, sc], axis=-1)),
            tile(c1), tile(s1))


def kernel(x, c, w_mod, b_mod, w_in, a_qk_norm, diff_lambda, diff_subln, hgrn_lower_bounds, hgrn_norm,
           w_out, w_ff1, w_ff2, final_norm):
    bsz, s, d = x.shape
    depth = w_in.shape[0]
    tm = min(512, s)
    tq = min(256, s)
    tables = _rope_tables(s)
    mod = _modulation(c, w_mod, b_mod).reshape(depth, bsz, 6, d)
    w_in_b, w_out_b, w_ff1_b, w_ff2_b = (w.astype(BF16) for w in (w_in, w_out, w_ff1, w_ff2))
    final_gain = final_norm.reshape(1, d)
    for l in range(depth):
        qk_gain = jnp.tile(a_qk_norm[l], (1, LANES // HEAD_DIM))
        qat, ka, vat, qbt, kb, vbt, zc = _in_projection(x, mod[l], w_in_b, l, tables, qk_gain, tm)
        oa = _attention_a(qat, ka, vat, tq)
        lam_init = 0.8 - 0.6 * math.exp(-0.3 * l)
        subln = jnp.tile(diff_subln[l].reshape(1, HEAD_DIM), (1, B_HEADS))
        ob = _attention_b(qbt, kb, vbt, diff_lambda[l], subln, lam_init, tq)
        hg_gain = jnp.tile(hgrn_norm[l].reshape(1, C_DIM), (1, C_HEADS))
        oc_f, oc_b = _hgrn(zc, hgrn_lower_bounds, l, min(HGRN_BATCH, bsz))
        x = _out_ffn(x, mod[l], oa, ob, oc_f, oc_b, zc, hg_gain, w_out_b, w_ff1_b, w_ff2_b, final_gain,
                     l, l == depth - 1, tm)
    return x
```

```python
import functools
import math

import numpy as np
import jax
import jax.numpy as jnp
from jax import lax
from jax.experimental import pallas as pl
from jax.experimental.pallas import tpu as pltpu

F32 = jnp.float32
BF16 = jnp.bfloat16

HEAD_DIM = 64
GRID_W = 64
ROPE_THETA = 10000.0
EPS = 1e-6
FORGET_FLOOR = 1e-6
A_HEADS, A_KV_HEADS = 6, 2
B_HEADS, B_QK_DIM = 6, 32
C_HEADS, C_DIM = 4, 64
A_WIDTH = A_HEADS * HEAD_DIM
A_KV_WIDTH = A_KV_HEADS * HEAD_DIM
B_WIDTH = B_HEADS * HEAD_DIM
C_WIDTH = C_HEADS * C_DIM
LANES = 128
HGRN_CHUNK = 64
HGRN_LEAF = 8
HGRN_BATCH = 8
ONES_ROWS = 16
VT_ROWS = HEAD_DIM + ONES_ROWS
KV_CHUNK = 256
Q_TILE = 256
QK_AHEAD = 16
LOG2_E = math.log2(math.e)
VMEM_LIMIT = 56 * 1024 * 1024

OFF_AQ, OFF_AK, OFF_AV = 0, 384, 512
OFF_BQ, OFF_BK, OFF_BV = 640, 1024, 1408
OFF_C = 1792
IN_TOTAL = 3072


def _dot(a, b):
    return jnp.dot(a, b, preferred_element_type=F32)


def _dot_nt(a, b):
    return lax.dot_general(a, b, (((1,), (1,)), ((), ())), preferred_element_type=F32)


def _dot_tn(a, b):
    return lax.dot_general(a, b, (((0,), (0,)), ((), ())), preferred_element_type=F32)


def _split_dot(x, w_bf16):
    hi = x.astype(BF16)
    lo = (x - hi.astype(F32)).astype(BF16)
    return _dot(hi, w_bf16) + _dot(lo, w_bf16)


def _group_ones(n, group):
    r = lax.broadcasted_iota(jnp.int32, (n, n), 0) // group
    c = lax.broadcasted_iota(jnp.int32, (n, n), 1) // group
    return jnp.where(r == c, 1.0 / group, 0.0).astype(BF16)


def _rope(xb, cos, sin_signed):
    lane = lax.broadcasted_iota(jnp.int32, xb.shape, 1)
    low = (lane % 32) < 16
    partner = jnp.where(low, pltpu.roll(xb, LANES - 16, 1), pltpu.roll(xb, 16, 1))
    return xb * cos + partner * sin_signed


def _rms_rows(x):
    return x * lax.rsqrt(jnp.mean(x * x, axis=-1, keepdims=True) + EPS)


def _mod_kernel(c_ref, w_ref, b_ref, o_ref):
    c = c_ref[...]
    cond = (c * jax.nn.sigmoid(c)).astype(BF16)
    o_ref[0] = _dot(cond, w_ref[0]) + b_ref[0]


def _modulation(c, w_mod, b_mod):
    depth, d, n = w_mod.shape
    bsz = c.shape[0]
    tn = 1024
    return pl.pallas_call(
        _mod_kernel,
        grid=(depth, n // tn),
        in_specs=[
            pl.BlockSpec((bsz, d), lambda l, j: (0, 0)),
            pl.BlockSpec((1, d, tn), lambda l, j: (l, 0, j)),
            pl.BlockSpec((1, 1, tn), lambda l, j: (l, 0, j)),
        ],
        out_specs=pl.BlockSpec((1, bsz, tn), lambda l, j: (l, 0, j)),
        out_shape=jax.ShapeDtypeStruct((depth, bsz, n), F32),
        compiler_params=pltpu.CompilerParams(
            dimension_semantics=("arbitrary", "arbitrary"), vmem_limit_bytes=VMEM_LIMIT),
        name="adaln_mod",
    )(c, w_mod.astype(BF16), b_mod.reshape(depth, 1, n))


def _inproj_kernel(x_ref, mod_ref, w_ref, cosa_ref, sina_ref, cosb_ref, sinb_ref, gain_ref,
                   qat_ref, ka_ref, vat_ref, qbt_ref, kb_ref, vbt_ref, zc_ref, h_scr, z_scr):
    x = x_ref[0]
    h = _rms_rows(x) * (1.0 + mod_ref[0, 1:2, :]) + mod_ref[0, 0:1, :]
    h_scr[...] = h.astype(BF16)

    def project(j):
        z_scr[:, j * 1024:(j + 1) * 1024] = _dot(h_scr[...], w_ref[0, :, j * 1024:(j + 1) * 1024])

    project(0)
    project(1)

    ones = _group_ones(LANES, HEAD_DIM)
    cosa, sina = cosa_ref[...], sina_ref[...]
    cosb, sinb = cosb_ref[...], sinb_ref[...]

    def a_norm_rope(zb, gain):
        ms = _split_dot(zb * zb, ones)
        return _rope(zb * lax.rsqrt(ms + EPS) * gain, cosa, sina)

    def store_vt(vt_ref, first_head, blk_t):
        ones_rows = jnp.ones((ONES_ROWS, blk_t.shape[1]), BF16)
        for j in range(LANES // HEAD_DIM):
            r0 = (first_head + j) * VT_ROWS
            vt_ref[0, r0:r0 + HEAD_DIM, :] = blk_t[j * HEAD_DIM:(j + 1) * HEAD_DIM].astype(BF16)
            vt_ref[0, r0 + HEAD_DIM:r0 + VT_ROWS, :] = ones_rows

    for cblk in range(A_WIDTH // LANES):
        zb = z_scr[:, OFF_AQ + cblk * LANES:OFF_AQ + (cblk + 1) * LANES]
        y = a_norm_rope(zb, gain_ref[0:1, :]) * (HEAD_DIM ** -0.5 * LOG2_E)
        qat_ref[0, cblk * LANES:(cblk + 1) * LANES, :] = y.T.astype(BF16)
    ka_ref[0] = a_norm_rope(z_scr[:, OFF_AK:OFF_AK + LANES], gain_ref[1:2, :]).astype(BF16)
    store_vt(vat_ref, 0, z_scr[:, OFF_AV:OFF_AV + LANES].T)
    project(2)

    for cblk in range(B_WIDTH // LANES):
        lo = cblk * LANES
        zq = z_scr[:, OFF_BQ + lo:OFF_BQ + lo + LANES]
        qbt_ref[0, lo:lo + LANES, :] = (_rope(zq, cosb, sinb) * (B_QK_DIM ** -0.5 * LOG2_E)).T.astype(BF16)
    for cblk in range(B_WIDTH // LANES):
        lo = cblk * LANES
        zk = z_scr[:, OFF_BK + lo:OFF_BK + lo + LANES]
        kb_ref[0, :, lo:lo + LANES] = _rope(zk, cosb, sinb).astype(BF16)
        store_vt(vbt_ref, cblk * (LANES // HEAD_DIM), z_scr[:, OFF_BV + lo:OFF_BV + lo + LANES].T)
    zc_ref[0] = z_scr[:, OFF_C:IN_TOTAL]


def _in_projection(x, mod_l, w_in_bf16, layer, tables, qk_gain, tm):
    bsz, s, d = x.shape
    cosa, sina, cosb, sinb = tables
    row_spec = lambda width: pl.BlockSpec((1, tm, width), lambda b, i: (b, i, 0))
    tab_spec = pl.BlockSpec((tm, LANES), lambda b, i: (i, 0))
    col_spec = lambda height: pl.BlockSpec((1, height, tm), lambda b, i: (b, 0, i))
    return pl.pallas_call(
        _inproj_kernel,
        grid=(bsz, s // tm),
        in_specs=[
            row_spec(d),
            pl.BlockSpec((1, 6, d), lambda b, i: (b, 0, 0)),
            pl.BlockSpec((1, d, IN_TOTAL), lambda b, i: (layer, 0, 0)),
            tab_spec, tab_spec, tab_spec, tab_spec,
            pl.BlockSpec((2, LANES), lambda b, i: (0, 0)),
        ],
        out_specs=[
            col_spec(A_WIDTH), row_spec(A_KV_WIDTH), col_spec(A_KV_HEADS * VT_ROWS),
            col_spec(B_WIDTH), row_spec(B_WIDTH), col_spec(B_HEADS * VT_ROWS),
            row_spec(IN_TOTAL - OFF_C),
        ],
        out_shape=[
            jax.ShapeDtypeStruct((bsz, A_WIDTH, s), BF16),
            jax.ShapeDtypeStruct((bsz, s, A_KV_WIDTH), BF16),
            jax.ShapeDtypeStruct((bsz, A_KV_HEADS * VT_ROWS, s), BF16),
            jax.ShapeDtypeStruct((bsz, B_WIDTH, s), BF16),
            jax.ShapeDtypeStruct((bsz, s, B_WIDTH), BF16),
            jax.ShapeDtypeStruct((bsz, B_HEADS * VT_ROWS, s), BF16),
            jax.ShapeDtypeStruct((bsz, s, IN_TOTAL - OFF_C), F32),
        ],
        scratch_shapes=[pltpu.VMEM((tm, d), BF16), pltpu.VMEM((tm, IN_TOTAL), F32)],
        compiler_params=pltpu.CompilerParams(
            dimension_semantics=("arbitrary", "arbitrary"), vmem_limit_bytes=VMEM_LIMIT),
        name="in_projection",
    )(x, mod_l, w_in_bf16, cosa, sina, cosb, sinb, qk_gain)


def _place_rows(block, start, total):
    parts = []
    if start:
        parts.append(jnp.zeros((start, block.shape[1]), block.dtype))
    parts.append(block)
    rest = total - start - block.shape[0]
    if rest:
        parts.append(jnp.zeros((rest, block.shape[1]), block.dtype))
    return jnp.concatenate(parts, axis=0) if len(parts) > 1 else block


def _attention_items(items, k_ref, vt_ref, qz_scr, m_scr, acc_scr):
    s = k_ref.shape[1]
    tq = acc_scr.shape[2]
    m_scr[...] = jnp.full(m_scr.shape, -1e30, F32)
    acc_scr[...] = jnp.zeros_like(acc_scr)

    units = [(i, c) for c in range(s // KV_CHUNK) for i in range(len(items))]

    def scores(u):
        i, c = units[u]
        k_off = items[i][0]
        return _dot(k_ref[0, c * KV_CHUNK:(c + 1) * KV_CHUNK, k_off:k_off + LANES], qz_scr[i])

    pending = [scores(u) for u in range(min(QK_AHEAD, len(units)))]
    for u, (i, c) in enumerate(units):
        v_off = items[i][1]
        st = pending.pop(0)
        m_old = m_scr[i]
        part = jnp.max(st.reshape(KV_CHUNK // 8, 8, tq), axis=0)
        m_new = jnp.maximum(m_old, jnp.max(part, axis=0, keepdims=True))
        m_scr[i] = m_new
        p = jnp.exp2(st - m_new[0:1]).astype(BF16)
        d = _dot(vt_ref[0, v_off:v_off + VT_ROWS, c * KV_CHUNK:(c + 1) * KV_CHUNK], p)
        if u + QK_AHEAD < len(units):
            pending.append(scores(u + QK_AHEAD))
        acc_scr[i] = acc_scr[i] * jnp.exp2(m_old[0:1] - m_new[0:1]) + d
    results = []
    for i in range(len(items)):
        acc = acc_scr[i]
        results.append(acc[0:HEAD_DIM] * (1.0 / acc[HEAD_DIM:HEAD_DIM + 1]))
    return results


def _attention_scratch(heads, tq):
    n_items = heads * (tq // Q_TILE)
    return [pltpu.VMEM((n_items, LANES, Q_TILE), BF16), pltpu.VMEM((n_items, 8, Q_TILE), F32),
            pltpu.VMEM((n_items, VT_ROWS, Q_TILE), F32)]


def _attn_a_kernel(qt_ref, k_ref, vt_ref, o_ref, qz_scr, m_scr, acc_scr):
    group = A_HEADS // A_KV_HEADS
    n_tiles = qt_ref.shape[2] // Q_TILE
    items = []
    for t in range(n_tiles):
        cols = slice(t * Q_TILE, (t + 1) * Q_TILE)
        for h in range(A_HEADS):
            g = h // group
            qz_scr[len(items)] = _place_rows(qt_ref[0, h * HEAD_DIM:(h + 1) * HEAD_DIM, cols], g * HEAD_DIM, LANES)
            items.append((0, g * VT_ROWS))
    outs = _attention_items(items, k_ref, vt_ref, qz_scr, m_scr, acc_scr)
    for t in range(n_tiles):
        tile = outs[t * A_HEADS:(t + 1) * A_HEADS]
        o_ref[0, t * Q_TILE:(t + 1) * Q_TILE, :] = jnp.concatenate([o.T for o in tile], axis=-1).astype(BF16)


def _attention_a(qat, ka, vat, tq):
    bsz, s, _ = ka.shape
    return pl.pallas_call(
        _attn_a_kernel,
        grid=(bsz, s // tq),
        in_specs=[
            pl.BlockSpec((1, A_WIDTH, tq), lambda b, i: (b, 0, i)),
            pl.BlockSpec((1, s, A_KV_WIDTH), lambda b, i: (b, 0, 0)),
            pl.BlockSpec((1, A_KV_HEADS * VT_ROWS, s), lambda b, i: (b, 0, 0)),
        ],
        out_specs=pl.BlockSpec((1, tq, A_WIDTH), lambda b, i: (b, i, 0)),
        out_shape=jax.ShapeDtypeStruct((bsz, s, A_WIDTH), BF16),
        scratch_shapes=_attention_scratch(A_HEADS, tq),
        compiler_params=pltpu.CompilerParams(
            dimension_semantics=("arbitrary", "arbitrary"), vmem_limit_bytes=VMEM_LIMIT),
        name="attn_axial_gqa",
    )(qat, ka, vat)


def _attn_b_kernel(lam_ref, gain_ref, qt_ref, k_ref, vt_ref, o_ref, qz_scr, m_scr, acc_scr, *, lam_init):
    lp = lam_ref[...]
    lam = (jnp.exp(jnp.sum(lp[0:1] * lp[1:2], axis=-1, keepdims=True))
           - jnp.exp(jnp.sum(lp[2:3] * lp[3:4], axis=-1, keepdims=True)) + lam_init)
    n_tiles = qt_ref.shape[2] // Q_TILE
    items = []
    for t in range(n_tiles):
        cols = slice(t * Q_TILE, (t + 1) * Q_TILE)
        for h in range(B_HEADS):
            pair = (h // 2) * LANES
            for comp in range(2):
                r0 = h * HEAD_DIM + comp * B_QK_DIM
                qz_scr[len(items)] = _place_rows(qt_ref[0, r0:r0 + B_QK_DIM, cols], r0 - pair, LANES)
                items.append((pair, h * VT_ROWS))
    comps = _attention_items(items, k_ref, vt_ref, qz_scr, m_scr, acc_scr)
    for t in range(n_tiles):
        outs = []
        for h in range(B_HEADS):
            i = (t * B_HEADS + h) * 2
            o = comps[i] - lam * comps[i + 1]
            o = o * lax.rsqrt(jnp.mean(o * o, axis=0, keepdims=True) + EPS)
            outs.append(o.T)
        y = jnp.concatenate(outs, axis=-1) * gain_ref[...] * (1.0 - lam_init)
        o_ref[0, t * Q_TILE:(t + 1) * Q_TILE, :] = y.astype(BF16)


def _attention_b(qbt, kb, vbt, lam_params, subln_gain, lam_init, tq):
    bsz, s, _ = kb.shape
    return pl.pallas_call(
        functools.partial(_attn_b_kernel, lam_init=lam_init),
        grid=(bsz, s // tq),
        in_specs=[
            pl.BlockSpec((4, B_QK_DIM), lambda b, i: (0, 0)),
            pl.BlockSpec((1, B_WIDTH), lambda b, i: (0, 0)),
            pl.BlockSpec((1, B_WIDTH, tq), lambda b, i: (b, 0, i)),
            pl.BlockSpec((1, s, B_WIDTH), lambda b, i: (b, 0, 0)),
            pl.BlockSpec((1, B_HEADS * VT_ROWS, s), lambda b, i: (b, 0, 0)),
        ],
        out_specs=pl.BlockSpec((1, tq, B_WIDTH), lambda b, i: (b, i, 0)),
        out_shape=jax.ShapeDtypeStruct((bsz, s, B_WIDTH), BF16),
        scratch_shapes=_attention_scratch(2 * B_HEADS, tq),
        compiler_params=pltpu.CompilerParams(
            dimension_semantics=("arbitrary", "arbitrary"), vmem_limit_bytes=VMEM_LIMIT),
        name="attn_differential",
    )(lam_params, subln_gain, qbt, kb, vbt)


def _hgrn_levels():
    sizes = []
    bs = HGRN_CHUNK
    while bs > HGRN_LEAF:
        sizes.append(bs)
        bs //= 2
    return sizes


def _hgrn_constants():
    c, w = HGRN_CHUNK, C_WIDTH
    t = np.arange(c)
    tri = np.stack([(t[:, None] >= t[None, :]), (t[:, None] <= t[None, :])]).astype(np.float32)
    s_of_lane = np.arange(w) % C_DIM
    sizes = _hgrn_levels()
    qmask = np.zeros((2, len(sizes), 3, c, w), np.float32)
    level = np.full((2, c, w), float(len(sizes) + 1), np.float32)
    for d in range(2):
        for li, bs in enumerate(sizes):
            upper = (t % bs) >= bs // 2
            qrow = upper if d == 0 else ~upper
            qmask[d, li, 0] = np.broadcast_to(qrow[:, None], (c, w))
            qmask[d, li, 1] = 1.0 - qmask[d, li, 0]
            qmask[d, li, 2] = qmask[d, li, 0] - qmask[d, li, 1]
            same = (t[:, None] // bs) == (s_of_lane[None, :] // bs)
            cross = qrow[:, None] & ~qrow[s_of_lane][None, :]
            level[d][same & cross] = li
        same_leaf = (t[:, None] // HGRN_LEAF) == (s_of_lane[None, :] // HGRN_LEAF)
        order = (s_of_lane[None, :] <= t[:, None]) if d == 0 else (s_of_lane[None, :] >= t[:, None])
        level[d][same_leaf & order] = len(sizes)
    half = (np.arange(LANES)[None, :] < C_DIM).astype(np.float32) * np.ones((c, 1), np.float32)
    return tri, qmask, level, half


def _block_diag(x, low_half):
    zero = jnp.zeros((x.shape[0], LANES), x.dtype)
    blocks = []
    for h in range(C_HEADS):
        tile = x[:, (h // 2) * LANES:(h // 2 + 1) * LANES]
        keep = tile * (low_half if h % 2 == 0 else 1 - low_half)
        blocks.append(jnp.concatenate([keep, zero] if h // 2 == 0 else [zero, keep], axis=1))
    return jnp.concatenate(blocks, axis=0)


def _ref_rows(b, bs, offset):
    parts = []
    for lo in range(0, b.shape[0], bs):
        parts.append(jnp.broadcast_to(b[lo + offset:lo + offset + 1, :], (bs, b.shape[1])))
    return jnp.concatenate(parts, axis=0)


def _hgrn_kernel(lb_ref, tri_ref, qmask_ref, level_ref, half_ref,
                 zq_f, zf_f, zv_f, zq_b, zf_b, zv_b, of_ref, ob_ref, st_scr, *, layer, depth, nb):
    c, w = HGRN_CHUNK, C_WIDTH
    sizes = _hgrn_levels()
    n_lvl = len(sizes)

    @pl.when(pl.program_id(1) == 0)
    def _():
        st_scr[...] = jnp.zeros_like(st_scr)

    rows = [lb_ref[i:i + 1, :] for i in range(depth)]
    mx = functools.reduce(jnp.maximum, rows)
    es = [jnp.exp(r - mx) for r in rows]
    tot = functools.reduce(lambda a, b_: a + b_, es)
    ps = [e / tot for e in es]
    cum = functools.reduce(lambda a, b_: a + b_, ps[:layer + 1])
    one_minus_lb = 1.0 - jnp.clip(cum - ps[0], 0.0, 1.0)

    low_half = half_ref[...].astype(BF16)
    low_bool = half_ref[...] > 0.5
    owner = [[level_ref[d] == float(li) for li in range(n_lvl + 1)] for d in range(2)]
    chains = [(j, d) for j in range(nb) for d in range(2)]
    zrefs = {0: (zq_f, zf_f, zv_f), 1: (zq_b, zf_b, zv_b)}

    q, k, v, b, total = {}, {}, {}, {}, {}
    g_split = {}
    for ch in chains:
        j, d = ch
        zq, zf = zrefs[d][0][j], zrefs[d][1][j]
        v[ch] = zrefs[d][2][j].astype(BF16)
        q[ch] = (zq * (C_DIM ** -0.5)) / (1.0 + jnp.exp2(zq * -LOG2_E))
        k[ch] = one_minus_lb * (1.0 - 1.0 / (1.0 + jnp.exp2(zf * -LOG2_E)))
        g = jnp.log2(jnp.maximum(1.0 - k[ch], FORGET_FLOOR))
        g_hi = g.astype(BF16)
        g_r = g - g_hi.astype(F32)
        g_mid = g_r.astype(BF16)
        g_lo = (g_r - g_mid.astype(F32)).astype(BF16)
        g_split[ch] = jnp.concatenate([g_hi, g_mid, g_lo], axis=1)
    for ch in chains:
        d = ch[1]
        bb = _dot(tri_ref[d], g_split[ch])
        b[ch] = bb[:, 0:w] + bb[:, w:2 * w] + bb[:, 2 * w:3 * w]
        total[ch] = b[ch][c - 1:c, :] if d == 0 else b[ch][0:1, :]

    parts = {ch: [] for ch in chains}
    for li in range(n_lvl + 1):
        for ch in chains:
            d = ch[1]
            if li < n_lvl:
                bs = sizes[li]
                ref = _ref_rows(b[ch], bs, bs // 2 - 1 if d == 0 else bs // 2)
                ex = jnp.exp2((b[ch] - ref) * qmask_ref[d, li, 2])
                qp = (q[ch] * ex * qmask_ref[d, li, 0]).astype(BF16)
                kp = (k[ch] * ex * qmask_ref[d, li, 1]).astype(BF16)
            else:
                ref = _ref_rows(b[ch], HGRN_LEAF, HGRN_LEAF // 2 - 1 if d == 0 else HGRN_LEAF // 2)
                arg = b[ch] - ref
                qp = (q[ch] * jnp.exp2(arg)).astype(BF16)
                kp = (k[ch] * jnp.exp2(-arg)).astype(BF16)
            parts[ch].append(_dot_nt(qp, _block_diag(kp, low_half)))
    a = {}
    for ch in chains:
        acc = jnp.zeros((c, w), F32)
        for li in range(n_lvl, -1, -1):
            acc = jnp.where(owner[ch[1]][li], parts[ch][li], acc)
        a[ch] = acc.astype(BF16)

    o_intra, o_inter, upd = {}, {}, {}
    for ch in chains:
        j, d = ch
        st = st_scr[d, j]
        o_intra[ch] = _dot(a[ch], _block_diag(v[ch], low_half))
        o_inter[ch] = _dot_nt((q[ch] * jnp.exp2(b[ch])).astype(BF16), _block_diag(st.astype(BF16), low_half))
        kb = (k[ch] * jnp.exp2(total[ch] - b[ch])).astype(BF16)
        tiles = []
        for t in range(w // LANES):
            full = _dot_tn(v[ch][:, t * LANES:(t + 1) * LANES], kb[:, t * LANES:(t + 1) * LANES])
            tiles.append(jnp.where(low_bool, full[0:C_DIM], full[C_DIM:2 * C_DIM]))
        upd[ch] = jnp.concatenate(tiles, axis=1)
    for ch in chains:
        j, d = ch
        st_scr[d, j] = st_scr[d, j] * jnp.exp2(total[ch]) + upd[ch]
        (of_ref if d == 0 else ob_ref)[j] = o_intra[ch] + o_inter[ch]


def _hgrn(zc, lower_bounds, layer, nb):
    bsz, s, _ = zc.shape
    depth = lower_bounds.shape[0]
    n_chunks = s // HGRN_CHUNK
    c, w = HGRN_CHUNK, C_WIDTH
    tri, qmask, level, half = _hgrn_constants()
    consts = (jnp.asarray(tri, BF16), jnp.asarray(qmask), jnp.asarray(level), jnp.asarray(half))
    full = lambda a: pl.BlockSpec(a.shape, lambda g, i: (0,) * a.ndim)
    fwd = lambda col: pl.BlockSpec((nb, c, w), lambda g, i: (g, i, col))
    bwd = lambda col: pl.BlockSpec((nb, c, w), lambda g, i: (g, n_chunks - 1 - i, col))
    return pl.pallas_call(
        functools.partial(_hgrn_kernel, layer=layer, depth=depth, nb=nb),
        grid=(bsz // nb, n_chunks),
        in_specs=[full(lower_bounds)] + [full(a) for a in consts]
                 + [fwd(0), fwd(1), fwd(3), bwd(0), bwd(2), bwd(3)],
        out_specs=[pl.BlockSpec((nb, c, w), lambda g, i: (g, i, 0)),
                   pl.BlockSpec((nb, c, w), lambda g, i: (g, n_chunks - 1 - i, 0))],
        out_shape=[jax.ShapeDtypeStruct((bsz, s, w), F32)] * 2,
        scratch_shapes=[pltpu.VMEM((2, nb, C_DIM, w), F32)],
        compiler_params=pltpu.CompilerParams(
            dimension_semantics=("arbitrary", "arbitrary"), vmem_limit_bytes=VMEM_LIMIT),
        name="hgrn2_bidir",
    )(lower_bounds, *consts, zc, zc, zc, zc, zc, zc)


def _out_ffn_kernel(x_ref, mod_ref, oa_ref, ob_ref, cf_ref, cb_ref, zg_ref, cgain_ref,
                    wo_ref, w1_ref, w2_ref, fg_ref, o_ref, *, final, ff_chunk):
    oc = cf_ref[0] + cb_ref[0]
    ms = _split_dot(oc * oc, _group_ones(C_WIDTH, C_DIM))
    zg = zg_ref[0]
    oc = oc * lax.rsqrt(ms + EPS) * cgain_ref[...] * (zg * jax.nn.sigmoid(zg))
    mix = jnp.concatenate([oa_ref[0], ob_ref[0], oc.astype(BF16)], axis=-1)
    x1 = x_ref[0] + mod_ref[0, 2:3, :] * _dot(mix, wo_ref[0])
    h = (_rms_rows(x1) * (1.0 + mod_ref[0, 4:5, :]) + mod_ref[0, 3:4, :]).astype(BF16)
    d_ff = w1_ref.shape[2]
    acc = jnp.zeros_like(x1)
    for j in range(d_ff // ff_chunk):
        u = _dot(h, w1_ref[0, :, j * ff_chunk:(j + 1) * ff_chunk])
        u = jnp.square(jnp.maximum(u, 0.0)).astype(BF16)
        acc = acc + _dot(u, w2_ref[0, j * ff_chunk:(j + 1) * ff_chunk, :])
    x2 = x1 + mod_ref[0, 5:6, :] * acc
    if final:
        x2 = _rms_rows(x2) * fg_ref[...]
    o_ref[0] = x2


def _out_ffn(x, mod_l, oa, ob, oc_f, oc_b, zc, c_gain, w_out, w_ff1, w_ff2, final_gain, layer, final, tm):
    bsz, s, d = x.shape
    d_ff = w_ff1.shape[2]
    row_spec = lambda width: pl.BlockSpec((1, tm, width), lambda b, i: (b, i, 0))
    weight = lambda a: pl.BlockSpec((1,) + a.shape[1:], lambda b, i: (layer, 0, 0),
                                    pipeline_mode=pl.Buffered(1))
    return pl.pallas_call(
        functools.partial(_out_ffn_kernel, final=final, ff_chunk=1024),
        grid=(bsz, s // tm),
        in_specs=[
            row_spec(d),
            pl.BlockSpec((1, 6, d), lambda b, i: (b, 0, 0)),
            row_spec(A_WIDTH), row_spec(B_WIDTH), row_spec(C_WIDTH), row_spec(C_WIDTH),
            pl.BlockSpec((1, tm, C_WIDTH), lambda b, i: (b, i, 4)),
            pl.BlockSpec((1, C_WIDTH), lambda b, i: (0, 0)),
            weight(w_out), weight(w_ff1), weight(w_ff2),
            pl.BlockSpec((1, d), lambda b, i: (0, 0)),
        ],
        out_specs=row_spec(d),
        out_shape=jax.ShapeDtypeStruct((bsz, s, d), F32),
        compiler_params=pltpu.CompilerParams(
            dimension_semantics=("arbitrary", "arbitrary"), vmem_limit_bytes=VMEM_LIMIT),
        name="out_proj_ffn",
    )(x, mod_l, oa, ob, oc_f, oc_b, zc, c_gain, w_out, w_ff1, w_ff2, final_gain)


def _rope_tables(s):
    half = HEAD_DIM // 2
    inv = ROPE_THETA ** (-jnp.arange(0, half, 2, dtype=F32) / half)
    t = jnp.arange(s)
    ang_row = (t // GRID_W).astype(F32)[:, None] * inv[None, :]
    ang_col = (t % GRID_W).astype(F32)[:, None] * inv[None, :]
    ang_1d = t.astype(F32)[:, None] * inv[None, :]

    def pair(ang):
        return (jnp.concatenate([jnp.cos(ang), jnp.cos(ang)], axis=-1),
                jnp.concatenate([-jnp.sin(ang), jnp.sin(ang)], axis=-1))

    cr, sr = pair(ang_row)
    cc, sc = pair(ang_col)
    c1, s1 = pair(ang_1d)
    tile = lambda a: jnp.tile(a, (1, LANES // a.shape[1]))
    return (tile(jnp.concatenate([cr, cc], axis=-1)), tile(jnp.concatenate([sr, sc], axis=-1)),
            tile(c1), tile(s1))


def kernel(x, c, w_mod, b_mod, w_in, a_qk_norm, diff_lambda, diff_subln, hgrn_lower_bounds, hgrn_norm,
           w_out, w_ff1, w_ff2, final_norm):
    bsz, s, d = x.shape
    depth = w_in.shape[0]
    tm = min(512, s)
    tq = min(2 * Q_TILE, s)
    tables = _rope_tables(s)
    mod = _modulation(c, w_mod, b_mod).reshape(depth, bsz, 6, d)
    w_in_b, w_out_b, w_ff1_b, w_ff2_b = (w.astype(BF16) for w in (w_in, w_out, w_ff1, w_ff2))
    final_gain = final_norm.reshape(1, d)
    for l in range(depth):
        qk_gain = jnp.tile(a_qk_norm[l], (1, LANES // HEAD_DIM))
        qat, ka, vat, qbt, kb, vbt, zc = _in_projection(x, mod[l], w_in_b, l, tables, qk_gain, tm)
        oa = _attention_a(qat, ka, vat, tq)
        lam_init = 0.8 - 0.6 * math.exp(-0.3 * l)
        subln = jnp.tile(diff_subln[l].reshape(1, HEAD_DIM), (1, B_HEADS))
        ob = _attention_b(qbt, kb, vbt, diff_lambda[l], subln, lam_init, tq)
        hg_gain = jnp.tile(hgrn_norm[l].reshape(1, C_DIM), (1, C_HEADS))
        oc_f, oc_b = _hgrn(zc, hgrn_lower_bounds, l, min(HGRN_BATCH, bsz))
        x = _out_ffn(x, mod[l], oa, ob, oc_f, oc_b, zc, hg_gain, w_out_b, w_ff1_b, w_ff2_b, final_gain,
                     l, l == depth - 1, tm)
    return x
```

```python
import functools
import math

import numpy as np
import jax
import jax.numpy as jnp
from jax import lax
from jax.experimental import pallas as pl
from jax.experimental.pallas import tpu as pltpu

F32 = jnp.float32
BF16 = jnp.bfloat16

HEAD_DIM = 64
GRID_W = 64
ROPE_THETA = 10000.0
EPS = 1e-6
FORGET_FLOOR = 1e-6
A_HEADS, A_KV_HEADS = 6, 2
B_HEADS, B_QK_DIM = 6, 32
C_HEADS, C_DIM = 4, 64
A_WIDTH = A_HEADS * HEAD_DIM
A_KV_WIDTH = A_KV_HEADS * HEAD_DIM
B_WIDTH = B_HEADS * HEAD_DIM
C_WIDTH = C_HEADS * C_DIM
LANES = 128
HGRN_CHUNK = 64
HGRN_LEAF = 8
HGRN_BATCH = 8
ONES_ROWS = 16
VT_ROWS = HEAD_DIM + ONES_ROWS
KV_CHUNK = 256
Q_TILE = 256
QK_AHEAD = 16
LOG2_E = math.log2(math.e)
VMEM_LIMIT = 56 * 1024 * 1024

OFF_AQ, OFF_AK, OFF_AV = 0, 384, 512
OFF_BQ, OFF_BK, OFF_BV = 640, 1024, 1408
OFF_C = 1792
IN_TOTAL = 3072


def _dot(a, b):
    return jnp.dot(a, b, preferred_element_type=F32)


def _dot_nt(a, b):
    return lax.dot_general(a, b, (((1,), (1,)), ((), ())), preferred_element_type=F32)


def _dot_tn(a, b):
    return lax.dot_general(a, b, (((0,), (0,)), ((), ())), preferred_element_type=F32)


def _split_dot(x, w_bf16):
    hi = x.astype(BF16)
    lo = (x - hi.astype(F32)).astype(BF16)
    return _dot(hi, w_bf16) + _dot(lo, w_bf16)


def _group_ones(n, group):
    r = lax.broadcasted_iota(jnp.int32, (n, n), 0) // group
    c = lax.broadcasted_iota(jnp.int32, (n, n), 1) // group
    return jnp.where(r == c, 1.0 / group, 0.0).astype(BF16)


def _rope(xb, cos, sin_signed):
    lane = lax.broadcasted_iota(jnp.int32, xb.shape, 1)
    low = (lane % 32) < 16
    partner = jnp.where(low, pltpu.roll(xb, LANES - 16, 1), pltpu.roll(xb, 16, 1))
    return xb * cos + partner * sin_signed


def _rms_rows(x):
    return x * lax.rsqrt(jnp.mean(x * x, axis=-1, keepdims=True) + EPS)


def _mod_kernel(c_ref, w_ref, b_ref, o_ref):
    c = c_ref[...]
    cond = (c * jax.nn.sigmoid(c)).astype(BF16)
    o_ref[0] = _dot(cond, w_ref[0]) + b_ref[0]


def _modulation(c, w_mod, b_mod):
    depth, d, n = w_mod.shape
    bsz = c.shape[0]
    tn = 1024
    return pl.pallas_call(
        _mod_kernel,
        grid=(depth, n // tn),
        in_specs=[
            pl.BlockSpec((bsz, d), lambda l, j: (0, 0)),
            pl.BlockSpec((1, d, tn), lambda l, j: (l, 0, j)),
            pl.BlockSpec((1, 1, tn), lambda l, j: (l, 0, j)),
        ],
        out_specs=pl.BlockSpec((1, bsz, tn), lambda l, j: (l, 0, j)),
        out_shape=jax.ShapeDtypeStruct((depth, bsz, n), F32),
        compiler_params=pltpu.CompilerParams(
            dimension_semantics=("arbitrary", "arbitrary"), vmem_limit_bytes=VMEM_LIMIT),
        name="adaln_mod",
    )(c, w_mod.astype(BF16), b_mod.reshape(depth, 1, n))


def _inproj_kernel(x_ref, mod_ref, w_ref, cosa_ref, sina_ref, cosb_ref, sinb_ref, gain_ref,
                   qat_ref, ka_ref, vat_ref, qbt_ref, kb_ref, vbt_ref, zc_ref, h_scr, z_scr):
    x = x_ref[0]
    h = _rms_rows(x) * (1.0 + mod_ref[0, 1:2, :]) + mod_ref[0, 0:1, :]
    h_scr[...] = h.astype(BF16)

    def project(j):
        z_scr[:, j * 1024:(j + 1) * 1024] = _dot(h_scr[...], w_ref[0, :, j * 1024:(j + 1) * 1024])

    project(0)
    project(1)

    ones = _group_ones(LANES, HEAD_DIM)
    cosa, sina = cosa_ref[...], sina_ref[...]
    cosb, sinb = cosb_ref[...], sinb_ref[...]

    def a_norm_rope(zb, gain):
        ms = _split_dot(zb * zb, ones)
        return _rope(zb * lax.rsqrt(ms + EPS) * gain, cosa, sina)

    def store_vt(vt_ref, first_head, blk_t):
        ones_rows = jnp.ones((ONES_ROWS, blk_t.shape[1]), BF16)
        for j in range(LANES // HEAD_DIM):
            r0 = (first_head + j) * VT_ROWS
            vt_ref[0, r0:r0 + HEAD_DIM, :] = blk_t[j * HEAD_DIM:(j + 1) * HEAD_DIM].astype(BF16)
            vt_ref[0, r0 + HEAD_DIM:r0 + VT_ROWS, :] = ones_rows

    for cblk in range(A_WIDTH // LANES):
        zb = z_scr[:, OFF_AQ + cblk * LANES:OFF_AQ + (cblk + 1) * LANES]
        y = a_norm_rope(zb, gain_ref[0:1, :]) * (HEAD_DIM ** -0.5 * LOG2_E)
        qat_ref[0, cblk * LANES:(cblk + 1) * LANES, :] = y.T.astype(BF16)
    ka_ref[0] = a_norm_rope(z_scr[:, OFF_AK:OFF_AK + LANES], gain_ref[1:2, :]).astype(BF16)
    store_vt(vat_ref, 0, z_scr[:, OFF_AV:OFF_AV + LANES].T)
    project(2)

    for cblk in range(B_WIDTH // LANES):
        lo = cblk * LANES
        zq = z_scr[:, OFF_BQ + lo:OFF_BQ + lo + LANES]
        qbt_ref[0, lo:lo + LANES, :] = (_rope(zq, cosb, sinb) * (B_QK_DIM ** -0.5 * LOG2_E)).T.astype(BF16)
    for cblk in range(B_WIDTH // LANES):
        lo = cblk * LANES
        zk = z_scr[:, OFF_BK + lo:OFF_BK + lo + LANES]
        kb_ref[0, :, lo:lo + LANES] = _rope(zk, cosb, sinb).astype(BF16)
        store_vt(vbt_ref, cblk * (LANES // HEAD_DIM), z_scr[:, OFF_BV + lo:OFF_BV + lo + LANES].T)
    zc_ref[0] = z_scr[:, OFF_C:IN_TOTAL]


def _in_projection(x, mod_l, w_in_bf16, layer, tables, qk_gain, tm):
    bsz, s, d = x.shape
    cosa, sina, cosb, sinb = tables
    row_spec = lambda width: pl.BlockSpec((1, tm, width), lambda b, i: (b, i, 0))
    tab_spec = pl.BlockSpec((tm, LANES), lambda b, i: (i, 0))
    col_spec = lambda height: pl.BlockSpec((1, height, tm), lambda b, i: (b, 0, i))
    return pl.pallas_call(
        _inproj_kernel,
        grid=(bsz, s // tm),
        in_specs=[
            row_spec(d),
            pl.BlockSpec((1, 6, d), lambda b, i: (b, 0, 0)),
            pl.BlockSpec((1, d, IN_TOTAL), lambda b, i: (layer, 0, 0)),
            tab_spec, tab_spec, tab_spec, tab_spec,
            pl.BlockSpec((2, LANES), lambda b, i: (0, 0)),
        ],
        out_specs=[
            col_spec(A_WIDTH), row_spec(A_KV_WIDTH), col_spec(A_KV_HEADS * VT_ROWS),
            col_spec(B_WIDTH), row_spec(B_WIDTH), col_spec(B_HEADS * VT_ROWS),
            row_spec(IN_TOTAL - OFF_C),
        ],
        out_shape=[
            jax.ShapeDtypeStruct((bsz, A_WIDTH, s), BF16),
            jax.ShapeDtypeStruct((bsz, s, A_KV_WIDTH), BF16),
            jax.ShapeDtypeStruct((bsz, A_KV_HEADS * VT_ROWS, s), BF16),
            jax.ShapeDtypeStruct((bsz, B_WIDTH, s), BF16),
            jax.ShapeDtypeStruct((bsz, s, B_WIDTH), BF16),
            jax.ShapeDtypeStruct((bsz, B_HEADS * VT_ROWS, s), BF16),
            jax.ShapeDtypeStruct((bsz, s, IN_TOTAL - OFF_C), F32),
        ],
        scratch_shapes=[pltpu.VMEM((tm, d), BF16), pltpu.VMEM((tm, IN_TOTAL), F32)],
        compiler_params=pltpu.CompilerParams(
            dimension_semantics=("arbitrary", "arbitrary"), vmem_limit_bytes=VMEM_LIMIT),
        name="in_projection",
    )(x, mod_l, w_in_bf16, cosa, sina, cosb, sinb, qk_gain)


def _place_rows(block, start, total):
    parts = []
    if start:
        parts.append(jnp.zeros((start, block.shape[1]), block.dtype))
    parts.append(block)
    rest = total - start - block.shape[0]
    if rest:
        parts.append(jnp.zeros((rest, block.shape[1]), block.dtype))
    return jnp.concatenate(parts, axis=0) if len(parts) > 1 else block


def _attention_items(items, k_ref, vt_ref, qz_scr, m_scr, acc_scr):
    s = k_ref.shape[1]
    tq = acc_scr.shape[2]
    m_scr[...] = jnp.full(m_scr.shape, -1e30, F32)
    acc_scr[...] = jnp.zeros_like(acc_scr)

    units = [(i, c) for c in range(s // KV_CHUNK) for i in range(len(items))]

    def scores(u):
        i, c = units[u]
        k_off = items[i][0]
        return _dot(k_ref[0, c * KV_CHUNK:(c + 1) * KV_CHUNK, k_off:k_off + LANES], qz_scr[i])

    pending = [scores(u) for u in range(min(QK_AHEAD, len(units)))]
    for u, (i, c) in enumerate(units):
        v_off = items[i][1]
        st = pending.pop(0)
        m_old = m_scr[i]
        part = jnp.max(st.reshape(KV_CHUNK // 8, 8, tq), axis=0)
        m_new = jnp.maximum(m_old, jnp.max(part, axis=0, keepdims=True))
        m_scr[i] = m_new
        p = jnp.exp2(st - m_new[0:1]).astype(BF16)
        d = _dot(vt_ref[0, v_off:v_off + VT_ROWS, c * KV_CHUNK:(c + 1) * KV_CHUNK], p)
        if u + QK_AHEAD < len(units):
            pending.append(scores(u + QK_AHEAD))
        acc_scr[i] = acc_scr[i] * jnp.exp2(m_old[0:1] - m_new[0:1]) + d
    results = []
    for i in range(len(items)):
        acc = acc_scr[i]
        results.append(acc[0:HEAD_DIM] * (1.0 / acc[HEAD_DIM:HEAD_DIM + 1]))
    return results


def _attention_scratch(heads, tq):
    n_items = heads * (tq // Q_TILE)
    return [pltpu.VMEM((n_items, LANES, Q_TILE), BF16), pltpu.VMEM((n_items, 8, Q_TILE), F32),
            pltpu.VMEM((n_items, VT_ROWS, Q_TILE), F32)]


def _attn_a_kernel(qt_ref, k_ref, vt_ref, o_ref, qz_scr, m_scr, acc_scr):
    group = A_HEADS // A_KV_HEADS
    n_tiles = qt_ref.shape[2] // Q_TILE
    items = []
    for t in range(n_tiles):
        cols = slice(t * Q_TILE, (t + 1) * Q_TILE)
        for h in range(A_HEADS):
            g = h // group
            qz_scr[len(items)] = _place_rows(qt_ref[0, h * HEAD_DIM:(h + 1) * HEAD_DIM, cols], g * HEAD_DIM, LANES)
            items.append((0, g * VT_ROWS))
    outs = _attention_items(items, k_ref, vt_ref, qz_scr, m_scr, acc_scr)
    for t in range(n_tiles):
        tile = outs[t * A_HEADS:(t + 1) * A_HEADS]
        o_ref[0, t * Q_TILE:(t + 1) * Q_TILE, :] = jnp.concatenate([o.T for o in tile], axis=-1).astype(BF16)


def _attention_a(qat, ka, vat, tq):
    bsz, s, _ = ka.shape
    return pl.pallas_call(
        _attn_a_kernel,
        grid=(bsz, s // tq),
        in_specs=[
            pl.BlockSpec((1, A_WIDTH, tq), lambda b, i: (b, 0, i)),
            pl.BlockSpec((1, s, A_KV_WIDTH), lambda b, i: (b, 0, 0)),
            pl.BlockSpec((1, A_KV_HEADS * VT_ROWS, s), lambda b, i: (b, 0, 0)),
        ],
        out_specs=pl.BlockSpec((1, tq, A_WIDTH), lambda b, i: (b, i, 0)),
        out_shape=jax.ShapeDtypeStruct((bsz, s, A_WIDTH), BF16),
        scratch_shapes=_attention_scratch(A_HEADS, tq),
        compiler_params=pltpu.CompilerParams(
            dimension_semantics=("arbitrary", "arbitrary"), vmem_limit_bytes=VMEM_LIMIT),
        name="attn_axial_gqa",
    )(qat, ka, vat)


def _attn_b_kernel(lam_ref, gain_ref, qt_ref, k_ref, vt_ref, o_ref, qz_scr, m_scr, acc_scr, *, lam_init):
    lp = lam_ref[...]
    lam = (jnp.exp(jnp.sum(lp[0:1] * lp[1:2], axis=-1, keepdims=True))
           - jnp.exp(jnp.sum(lp[2:3] * lp[3:4], axis=-1, keepdims=True)) + lam_init)
    n_tiles = qt_ref.shape[2] // Q_TILE
    items = []
    for t in range(n_tiles):
        cols = slice(t * Q_TILE, (t + 1) * Q_TILE)
        for h in range(B_HEADS):
            pair = (h // 2) * LANES
            for comp in range(2):
                r0 = h * HEAD_DIM + comp * B_QK_DIM
                qz_scr[len(items)] = _place_rows(qt_ref[0, r0:r0 + B_QK_DIM, cols], r0 - pair, LANES)
                items.append((pair, h * VT_ROWS))
    comps = _attention_items(items, k_ref, vt_ref, qz_scr, m_scr, acc_scr)
    for t in range(n_tiles):
        outs = []
        for h in range(B_HEADS):
            i = (t * B_HEADS + h) * 2
            o = comps[i] - lam * comps[i + 1]
            o = o * lax.rsqrt(jnp.mean(o * o, axis=0, keepdims=True) + EPS)
            outs.append(o.T)
        y = jnp.concatenate(outs, axis=-1) * gain_ref[...] * (1.0 - lam_init)
        o_ref[0, t * Q_TILE:(t + 1) * Q_TILE, :] = y.astype(BF16)


def _attention_b(qbt, kb, vbt, lam_params, subln_gain, lam_init, tq):
    bsz, s, _ = kb.shape
    return pl.pallas_call(
        functools.partial(_attn_b_kernel, lam_init=lam_init),
        grid=(bsz, s // tq),
        in_specs=[
            pl.BlockSpec((4, B_QK_DIM), lambda b, i: (0, 0)),
            pl.BlockSpec((1, B_WIDTH), lambda b, i: (0, 0)),
            pl.BlockSpec((1, B_WIDTH, tq), lambda b, i: (b, 0, i)),
            pl.BlockSpec((1, s, B_WIDTH), lambda b, i: (b, 0, 0)),
            pl.BlockSpec((1, B_HEADS * VT_ROWS, s), lambda b, i: (b, 0, 0)),
        ],
        out_specs=pl.BlockSpec((1, tq, B_WIDTH), lambda b, i: (b, i, 0)),
        out_shape=jax.ShapeDtypeStruct((bsz, s, B_WIDTH), BF16),
        scratch_shapes=_attention_scratch(2 * B_HEADS, tq),
        compiler_params=pltpu.CompilerParams(
            dimension_semantics=("arbitrary", "arbitrary"), vmem_limit_bytes=VMEM_LIMIT),
        name="attn_differential",
    )(lam_params, subln_gain, qbt, kb, vbt)


def _hgrn_levels():
    sizes = []
    bs = HGRN_CHUNK
    while bs > HGRN_LEAF:
        sizes.append(bs)
        bs //= 2
    return sizes


def _hgrn_constants():
    c, w = HGRN_CHUNK, C_WIDTH
    t = np.arange(c)
    tri = np.stack([(t[:, None] >= t[None, :]), (t[:, None] <= t[None, :])]).astype(np.float32)
    s_of_lane = np.arange(w) % C_DIM
    sizes = _hgrn_levels()
    level = np.full((2, c, w), float(len(sizes) + 1), np.float32)
    for d in range(2):
        for li, bs in enumerate(sizes):
            upper = (t % bs) >= bs // 2
            qrow = upper if d == 0 else ~upper
            same = (t[:, None] // bs) == (s_of_lane[None, :] // bs)
            cross = qrow[:, None] & ~qrow[s_of_lane][None, :]
            level[d][same & cross] = li
        same_leaf = (t[:, None] // HGRN_LEAF) == (s_of_lane[None, :] // HGRN_LEAF)
        order = (s_of_lane[None, :] <= t[:, None]) if d == 0 else (s_of_lane[None, :] >= t[:, None])
        level[d][same_leaf & order] = len(sizes)
    half = (np.arange(LANES)[None, :] < C_DIM).astype(np.float32) * np.ones((c, 1), np.float32)
    return tri, level, half


def _block_diag(x, low_half):
    zero = jnp.zeros((x.shape[0], LANES), x.dtype)
    blocks = []
    for h in range(C_HEADS):
        tile = x[:, (h // 2) * LANES:(h // 2 + 1) * LANES]
        keep = tile * (low_half if h % 2 == 0 else 1 - low_half)
        blocks.append(jnp.concatenate([keep, zero] if h // 2 == 0 else [zero, keep], axis=1))
    return jnp.concatenate(blocks, axis=0)


def _ref_rows(b, bs, offset):
    parts = []
    for lo in range(0, b.shape[0], bs):
        parts.append(jnp.broadcast_to(b[lo + offset:lo + offset + 1, :], (bs, b.shape[1])))
    return jnp.concatenate(parts, axis=0)


def _level_operands(q, k, b, bs, d):
    half = bs // 2
    ref_off = half - 1 if d == 0 else half
    zeros = jnp.zeros((8, b.shape[1]), F32)
    qparts, kparts = [], []
    for r0 in range(0, b.shape[0], 8):
        lo = (r0 // bs) * bs
        ref = b[lo + ref_off:lo + ref_off + 1, :]
        upper = (r0 % bs) >= half
        rows = slice(r0, r0 + 8)
        if upper == (d == 0):
            qparts.append(q[rows] * jnp.exp2(b[rows] - ref))
            kparts.append(zeros)
        else:
            qparts.append(zeros)
            kparts.append(k[rows] * jnp.exp2(ref - b[rows]))
    return jnp.concatenate(qparts, axis=0).astype(BF16), jnp.concatenate(kparts, axis=0).astype(BF16)


def _hgrn_kernel(lb_ref, tri_ref, level_ref, half_ref, zq_f, zf_f, zv_f, zq_b, zf_b, zv_b,
                 of_ref, ob_ref, st_scr, *, layer, depth, nb):
    c, w = HGRN_CHUNK, C_WIDTH
    sizes = _hgrn_levels()
    n_lvl = len(sizes)

    @pl.when(pl.program_id(1) == 0)
    def _():
        st_scr[...] = jnp.zeros_like(st_scr)

    rows = [lb_ref[i:i + 1, :] for i in range(depth)]
    mx = functools.reduce(jnp.maximum, rows)
    es = [jnp.exp(r - mx) for r in rows]
    tot = functools.reduce(lambda a, b_: a + b_, es)
    ps = [e / tot for e in es]
    cum = functools.reduce(lambda a, b_: a + b_, ps[:layer + 1])
    one_minus_lb = 1.0 - jnp.clip(cum - ps[0], 0.0, 1.0)

    low_half = half_ref[...].astype(BF16)
    low_bool = half_ref[...] > 0.5
    owner = [[level_ref[d] == float(li) for li in range(n_lvl + 1)] for d in range(2)]
    chains = [(j, d) for j in range(nb) for d in range(2)]
    zrefs = {0: (zq_f, zf_f, zv_f), 1: (zq_b, zf_b, zv_b)}

    q, k, v, b, total = {}, {}, {}, {}, {}
    g_split = {}
    for ch in chains:
        j, d = ch
        zq, zf = zrefs[d][0][j], zrefs[d][1][j]
        v[ch] = zrefs[d][2][j].astype(BF16)
        q[ch] = (zq * (C_DIM ** -0.5)) / (1.0 + jnp.exp2(zq * -LOG2_E))
        k[ch] = one_minus_lb * (1.0 - 1.0 / (1.0 + jnp.exp2(zf * -LOG2_E)))
        g = jnp.log2(jnp.maximum(1.0 - k[ch], FORGET_FLOOR))
        g_hi = g.astype(BF16)
        g_r = g - g_hi.astype(F32)
        g_mid = g_r.astype(BF16)
        g_lo = (g_r - g_mid.astype(F32)).astype(BF16)
        g_split[ch] = jnp.concatenate([g_hi, g_mid, g_lo], axis=1)
    for ch in chains:
        d = ch[1]
        bb = _dot(tri_ref[d], g_split[ch])
        b[ch] = bb[:, 0:w] + bb[:, w:2 * w] + bb[:, 2 * w:3 * w]
        total[ch] = b[ch][c - 1:c, :] if d == 0 else b[ch][0:1, :]

    parts = {ch: [] for ch in chains}
    for li in range(n_lvl + 1):
        for ch in chains:
            d = ch[1]
            if li < n_lvl:
                qp, kp = _level_operands(q[ch], k[ch], b[ch], sizes[li], d)
            else:
                ref = _ref_rows(b[ch], HGRN_LEAF, HGRN_LEAF // 2 - 1 if d == 0 else HGRN_LEAF // 2)
                arg = b[ch] - ref
                qp = (q[ch] * jnp.exp2(arg)).astype(BF16)
                kp = (k[ch] * jnp.exp2(-arg)).astype(BF16)
            parts[ch].append(_dot_nt(qp, _block_diag(kp, low_half)))
    a = {}
    for ch in chains:
        acc = jnp.zeros((c, w), F32)
        for li in range(n_lvl, -1, -1):
            acc = jnp.where(owner[ch[1]][li], parts[ch][li], acc)
        a[ch] = acc.astype(BF16)

    o_intra, o_inter, upd = {}, {}, {}
    for ch in chains:
        j, d = ch
        o_intra[ch] = _dot(a[ch], _block_diag(v[ch], low_half))
        qe = (q[ch] * jnp.exp2(b[ch])).astype(BF16)
        o_inter[ch] = _dot_nt(qe, _block_diag(st_scr[d, j].astype(BF16), low_half))
        kb = (k[ch] * jnp.exp2(total[ch] - b[ch])).astype(BF16)
        tiles = []
        for t in range(w // LANES):
            lanes = slice(t * LANES, (t + 1) * LANES)
            full = _dot_tn(v[ch][:, lanes], kb[:, lanes])
            tiles.append(jnp.where(low_bool, full[0:C_DIM], full[C_DIM:2 * C_DIM]))
        upd[ch] = jnp.concatenate(tiles, axis=1)
    for ch in chains:
        j, d = ch
        st_scr[d, j] = st_scr[d, j] * jnp.exp2(total[ch]) + upd[ch]
        (of_ref if d == 0 else ob_ref)[j] = o_intra[ch] + o_inter[ch]


def _hgrn(zc, lower_bounds, layer, nb):
    bsz, s, _ = zc.shape
    depth = lower_bounds.shape[0]
    n_chunks = s // HGRN_CHUNK
    c, w = HGRN_CHUNK, C_WIDTH
    tri, level, half = _hgrn_constants()
    consts = (jnp.asarray(tri, BF16), jnp.asarray(level), jnp.asarray(half))
    full = lambda a: pl.BlockSpec(a.shape, lambda g, i: (0,) * a.ndim)
    last = n_chunks - 1
    fwd = lambda col: pl.BlockSpec((nb, c, w), lambda g, i: (g, i, col))
    bwd = lambda col: pl.BlockSpec((nb, c, w), lambda g, i: (g, last - i, col))
    return pl.pallas_call(
        functools.partial(_hgrn_kernel, layer=layer, depth=depth, nb=nb),
        grid=(bsz // nb, n_chunks),
        in_specs=[full(lower_bounds)] + [full(a) for a in consts]
                 + [fwd(0), fwd(1), fwd(3), bwd(0), bwd(2), bwd(3)],
        out_specs=[pl.BlockSpec((nb, c, w), lambda g, i: (g, i, 0)),
                   pl.BlockSpec((nb, c, w), lambda g, i: (g, last - i, 0))],
        out_shape=[jax.ShapeDtypeStruct((bsz, s, w), F32)] * 2,
        scratch_shapes=[pltpu.VMEM((2, nb, C_DIM, w), F32)],
        compiler_params=pltpu.CompilerParams(
            dimension_semantics=("arbitrary", "arbitrary"), vmem_limit_bytes=VMEM_LIMIT),
        name="hgrn2_bidir",
    )(lower_bounds, *consts, zc, zc, zc, zc, zc, zc)


def _out_ffn_kernel(x_ref, mod_ref, oa_ref, ob_ref, cf_ref, cb_ref, zg_ref, cgain_ref,
                    wo_ref, w1_ref, w2_ref, fg_ref, o_ref, *, final, ff_chunk):
    oc = cf_ref[0] + cb_ref[0]
    ms = _split_dot(oc * oc, _group_ones(C_WIDTH, C_DIM))
    zg = zg_ref[0]
    oc = oc * lax.rsqrt(ms + EPS) * cgain_ref[...] * (zg * jax.nn.sigmoid(zg))
    mix = jnp.concatenate([oa_ref[0], ob_ref[0], oc.astype(BF16)], axis=-1)
    x1 = x_ref[0] + mod_ref[0, 2:3, :] * _dot(mix, wo_ref[0])
    h = (_rms_rows(x1) * (1.0 + mod_ref[0, 4:5, :]) + mod_ref[0, 3:4, :]).astype(BF16)
    d_ff = w1_ref.shape[2]
    acc = jnp.zeros_like(x1)
    for j in range(d_ff // ff_chunk):
        u = _dot(h, w1_ref[0, :, j * ff_chunk:(j + 1) * ff_chunk])
        u = jnp.square(jnp.maximum(u, 0.0)).astype(BF16)
        acc = acc + _dot(u, w2_ref[0, j * ff_chunk:(j + 1) * ff_chunk, :])
    x2 = x1 + mod_ref[0, 5:6, :] * acc
    if final:
        x2 = _rms_rows(x2) * fg_ref[...]
    o_ref[0] = x2


def _out_ffn(x, mod_l, oa, ob, oc_f, oc_b, zc, c_gain, w_out, w_ff1, w_ff2, final_gain, layer, final, tm):
    bsz, s, d = x.shape
    d_ff = w_ff1.shape[2]
    row_spec = lambda width: pl.BlockSpec((1, tm, width), lambda b, i: (b, i, 0))
    weight = lambda a: pl.BlockSpec((1,) + a.shape[1:], lambda b, i: (layer, 0, 0),
                                    pipeline_mode=pl.Buffered(1))
    return pl.pallas_call(
        functools.partial(_out_ffn_kernel, final=final, ff_chunk=1024),
        grid=(bsz, s // tm),
        in_specs=[
            row_spec(d),
            pl.BlockSpec((1, 6, d), lambda b, i: (b, 0, 0)),
            row_spec(A_WIDTH), row_spec(B_WIDTH), row_spec(C_WIDTH), row_spec(C_WIDTH),
            pl.BlockSpec((1, tm, C_WIDTH), lambda b, i: (b, i, 4)),
            pl.BlockSpec((1, C_WIDTH), lambda b, i: (0, 0)),
            weight(w_out), weight(w_ff1), weight(w_ff2),
            pl.BlockSpec((1, d), lambda b, i: (0, 0)),
        ],
        out_specs=row_spec(d),
        out_shape=jax.ShapeDtypeStruct((bsz, s, d), F32),
        compiler_params=pltpu.CompilerParams(
            dimension_semantics=("arbitrary", "arbitrary"), vmem_limit_bytes=VMEM_LIMIT),
        name="out_proj_ffn",
    )(x, mod_l, oa, ob, oc_f, oc_b, zc, c_gain, w_out, w_ff1, w_ff2, final_gain)


def _rope_tables(s):
    half = HEAD_DIM // 2
    inv = ROPE_THETA ** (-jnp.arange(0, half, 2, dtype=F32) / half)
    t = jnp.arange(s)
    ang_row = (t // GRID_W).astype(F32)[:, None] * inv[None, :]
    ang_col = (t % GRID_W).astype(F32)[:, None] * inv[None, :]
    ang_1d = t.astype(F32)[:, None] * inv[None, :]

    def pair(ang):
        return (jnp.concatenate([jnp.cos(ang), jnp.cos(ang)], axis=-1),
                jnp.concatenate([-jnp.sin(ang), jnp.sin(ang)], axis=-1))

    cr, sr = pair(ang_row)
    cc, sc = pair(ang_col)
    c1, s1 = pair(ang_1d)
    tile = lambda a: jnp.tile(a, (1, LANES // a.shape[1]))
    return (tile(jnp.concatenate([cr, cc], axis=-1)), tile(jnp.concatenate([sr, sc], axis=-1)),
            tile(c1), tile(s1))


def kernel(x, c, w_mod, b_mod, w_in, a_qk_norm, diff_lambda, diff_subln, hgrn_lower_bounds, hgrn_norm,
           w_out, w_ff1, w_ff2, final_norm):
    bsz, s, d = x.shape
    depth = w_in.shape[0]
    tm = min(512, s)
    tq = min(2 * Q_TILE, s)
    tables = _rope_tables(s)
    mod = _modulation(c, w_mod, b_mod).reshape(depth, bsz, 6, d)
    w_in_b, w_out_b, w_ff1_b, w_ff2_b = (w.astype(BF16) for w in (w_in, w_out, w_ff1, w_ff2))
    final_gain = final_norm.reshape(1, d)
    for l in range(depth):
        qk_gain = jnp.tile(a_qk_norm[l], (1, LANES // HEAD_DIM))
        qat, ka, vat, qbt, kb, vbt, zc = _in_projection(x, mod[l], w_in_b, l, tables, qk_gain, tm)
        oa = _attention_a(qat, ka, vat, tq)
        lam_init = 0.8 - 0.6 * math.exp(-0.3 * l)
        subln = jnp.tile(diff_subln[l].reshape(1, HEAD_DIM), (1, B_HEADS))
        ob = _attention_b(qbt, kb, vbt, diff_lambda[l], subln, lam_init, tq)
        hg_gain = jnp.tile(hgrn_norm[l].reshape(1, C_DIM), (1, C_HEADS))
        oc_f, oc_b = _hgrn(zc, hgrn_lower_bounds, l, min(HGRN_BATCH, bsz))
        x = _out_ffn(x, mod[l], oa, ob, oc_f, oc_b, zc, hg_gain, w_out_b, w_ff1_b, w_ff2_b, final_gain,
                     l, l == depth - 1, tm)
    return x
```

```python
import functools
import math

import numpy as np
import jax
import jax.numpy as jnp
from jax import lax
from jax.experimental import pallas as pl
from jax.experimental.pallas import tpu as pltpu

F32 = jnp.float32
BF16 = jnp.bfloat16

HEAD_DIM = 64
GRID_W = 64
ROPE_THETA = 10000.0
EPS = 1e-6
FORGET_FLOOR = 1e-6
A_HEADS, A_KV_HEADS = 6, 2
B_HEADS, B_QK_DIM = 6, 32
C_HEADS, C_DIM = 4, 64
A_WIDTH = A_HEADS * HEAD_DIM
A_KV_WIDTH = A_KV_HEADS * HEAD_DIM
B_WIDTH = B_HEADS * HEAD_DIM
C_WIDTH = C_HEADS * C_DIM
LANES = 128
HGRN_CHUNK = 64
HGRN_LEAF = 8
HGRN_BATCH = 8
ONES_ROWS = 16
VT_ROWS = HEAD_DIM + ONES_ROWS
KV_CHUNK = 256
Q_TILE = 256
QK_AHEAD = 16
LOG2_E = math.log2(math.e)
VMEM_LIMIT = 56 * 1024 * 1024

OFF_AQ, OFF_AK, OFF_AV = 0, 384, 512
OFF_BQ, OFF_BK, OFF_BV = 640, 1024, 1408
OFF_C = 1792
IN_TOTAL = 3072


def _dot(a, b):
    return jnp.dot(a, b, preferred_element_type=F32)


def _dot_nt(a, b):
    return lax.dot_general(a, b, (((1,), (1,)), ((), ())), preferred_element_type=F32)


def _dot_tn(a, b):
    return lax.dot_general(a, b, (((0,), (0,)), ((), ())), preferred_element_type=F32)


def _split_dot(x, w_bf16):
    hi = x.astype(BF16)
    lo = (x - hi.astype(F32)).astype(BF16)
    return _dot(hi, w_bf16) + _dot(lo, w_bf16)


def _group_ones(n, group):
    r = lax.broadcasted_iota(jnp.int32, (n, n), 0) // group
    c = lax.broadcasted_iota(jnp.int32, (n, n), 1) // group
    return jnp.where(r == c, 1.0 / group, 0.0).astype(BF16)


def _rope(xb, cos, sin_signed):
    lane = lax.broadcasted_iota(jnp.int32, xb.shape, 1)
    low = (lane % 32) < 16
    partner = jnp.where(low, pltpu.roll(xb, LANES - 16, 1), pltpu.roll(xb, 16, 1))
    return xb * cos + partner * sin_signed


def _rms_rows(x):
    return x * lax.rsqrt(jnp.mean(x * x, axis=-1, keepdims=True) + EPS)


def _mod_kernel(c_ref, w_ref, b_ref, o_ref):
    c = c_ref[...]
    cond = (c * jax.nn.sigmoid(c)).astype(BF16)
    o_ref[0] = _dot(cond, w_ref[0]) + b_ref[0]


def _modulation(c, w_mod, b_mod):
    depth, d, n = w_mod.shape
    bsz = c.shape[0]
    tn = 1024
    return pl.pallas_call(
        _mod_kernel,
        grid=(depth, n // tn),
        in_specs=[
            pl.BlockSpec((bsz, d), lambda l, j: (0, 0)),
            pl.BlockSpec((1, d, tn), lambda l, j: (l, 0, j)),
            pl.BlockSpec((1, 1, tn), lambda l, j: (l, 0, j)),
        ],
        out_specs=pl.BlockSpec((1, bsz, tn), lambda l, j: (l, 0, j)),
        out_shape=jax.ShapeDtypeStruct((depth, bsz, n), F32),
        compiler_params=pltpu.CompilerParams(
            dimension_semantics=("arbitrary", "arbitrary"), vmem_limit_bytes=VMEM_LIMIT),
        name="adaln_mod",
    )(c, w_mod.astype(BF16), b_mod.reshape(depth, 1, n))


def _inproj_kernel(x_ref, mod_ref, w_ref, cosa_ref, sina_ref, cosb_ref, sinb_ref, gain_ref,
                   qat_ref, ka_ref, vat_ref, qbt_ref, kb_ref, vbt_ref, zc_ref, h_scr, z_scr):
    x = x_ref[0]
    h = _rms_rows(x) * (1.0 + mod_ref[0, 1:2, :]) + mod_ref[0, 0:1, :]
    h_scr[...] = h.astype(BF16)

    def project(j):
        z_scr[:, j * 1024:(j + 1) * 1024] = _dot(h_scr[...], w_ref[0, :, j * 1024:(j + 1) * 1024])

    project(0)
    project(1)

    ones = _group_ones(LANES, HEAD_DIM)
    cosa, sina = cosa_ref[...], sina_ref[...]
    cosb, sinb = cosb_ref[...], sinb_ref[...]

    def a_norm_rope(zb, gain):
        ms = _split_dot(zb * zb, ones)
        return _rope(zb * lax.rsqrt(ms + EPS) * gain, cosa, sina)

    def store_vt(vt_ref, first_head, blk_t):
        ones_rows = jnp.ones((ONES_ROWS, blk_t.shape[1]), BF16)
        for j in range(LANES // HEAD_DIM):
            r0 = (first_head + j) * VT_ROWS
            vt_ref[0, r0:r0 + HEAD_DIM, :] = blk_t[j * HEAD_DIM:(j + 1) * HEAD_DIM].astype(BF16)
            vt_ref[0, r0 + HEAD_DIM:r0 + VT_ROWS, :] = ones_rows

    for cblk in range(A_WIDTH // LANES):
        zb = z_scr[:, OFF_AQ + cblk * LANES:OFF_AQ + (cblk + 1) * LANES]
        y = a_norm_rope(zb, gain_ref[0:1, :]) * (HEAD_DIM ** -0.5 * LOG2_E)
        qat_ref[0, cblk * LANES:(cblk + 1) * LANES, :] = y.T.astype(BF16)
    ka_ref[0] = a_norm_rope(z_scr[:, OFF_AK:OFF_AK + LANES], gain_ref[1:2, :]).astype(BF16)
    store_vt(vat_ref, 0, z_scr[:, OFF_AV:OFF_AV + LANES].T)
    project(2)

    for cblk in range(B_WIDTH // LANES):
        lo = cblk * LANES
        zq = z_scr[:, OFF_BQ + lo:OFF_BQ + lo + LANES]
        qbt_ref[0, lo:lo + LANES, :] = (_rope(zq, cosb, sinb) * (B_QK_DIM ** -0.5 * LOG2_E)).T.astype(BF16)
    for cblk in range(B_WIDTH // LANES):
        lo = cblk * LANES
        zk = z_scr[:, OFF_BK + lo:OFF_BK + lo + LANES]
        kb_ref[0, :, lo:lo + LANES] = _rope(zk, cosb, sinb).astype(BF16)
        store_vt(vbt_ref, cblk * (LANES // HEAD_DIM), z_scr[:, OFF_BV + lo:OFF_BV + lo + LANES].T)
    zc_ref[0] = z_scr[:, OFF_C:IN_TOTAL]


def _in_projection(x, mod_l, w_in_bf16, layer, tables, qk_gain, tm):
    bsz, s, d = x.shape
    cosa, sina, cosb, sinb = tables
    row_spec = lambda width: pl.BlockSpec((1, tm, width), lambda b, i: (b, i, 0))
    tab_spec = pl.BlockSpec((tm, LANES), lambda b, i: (i, 0))
    col_spec = lambda height: pl.BlockSpec((1, height, tm), lambda b, i: (b, 0, i))
    return pl.pallas_call(
        _inproj_kernel,
        grid=(bsz, s // tm),
        in_specs=[
            row_spec(d),
            pl.BlockSpec((1, 6, d), lambda b, i: (b, 0, 0)),
            pl.BlockSpec((1, d, IN_TOTAL), lambda b, i: (layer, 0, 0)),
            tab_spec, tab_spec, tab_spec, tab_spec,
            pl.BlockSpec((2, LANES), lambda b, i: (0, 0)),
        ],
        out_specs=[
            col_spec(A_WIDTH), row_spec(A_KV_WIDTH), col_spec(A_KV_HEADS * VT_ROWS),
            col_spec(B_WIDTH), row_spec(B_WIDTH), col_spec(B_HEADS * VT_ROWS),
            row_spec(IN_TOTAL - OFF_C),
        ],
        out_shape=[
            jax.ShapeDtypeStruct((bsz, A_WIDTH, s), BF16),
            jax.ShapeDtypeStruct((bsz, s, A_KV_WIDTH), BF16),
            jax.ShapeDtypeStruct((bsz, A_KV_HEADS * VT_ROWS, s), BF16),
            jax.ShapeDtypeStruct((bsz, B_WIDTH, s), BF16),
            jax.ShapeDtypeStruct((bsz, s, B_WIDTH), BF16),
            jax.ShapeDtypeStruct((bsz, B_HEADS * VT_ROWS, s), BF16),
            jax.ShapeDtypeStruct((bsz, s, IN_TOTAL - OFF_C), F32),
        ],
        scratch_shapes=[pltpu.VMEM((tm, d), BF16), pltpu.VMEM((tm, IN_TOTAL), F32)],
        compiler_params=pltpu.CompilerParams(
            dimension_semantics=("arbitrary", "arbitrary"), vmem_limit_bytes=VMEM_LIMIT),
        name="in_projection",
    )(x, mod_l, w_in_bf16, cosa, sina, cosb, sinb, qk_gain)


def _place_rows(block, start, total):
    parts = []
    if start:
        parts.append(jnp.zeros((start, block.shape[1]), block.dtype))
    parts.append(block)
    rest = total - start - block.shape[0]
    if rest:
        parts.append(jnp.zeros((rest, block.shape[1]), block.dtype))
    return jnp.concatenate(parts, axis=0) if len(parts) > 1 else block


def _attention_items(items, k_ref, vt_ref, qz_scr, m_scr, acc_scr):
    s = k_ref.shape[1]
    tq = acc_scr.shape[2]
    m_scr[...] = jnp.full(m_scr.shape, -1e30, F32)
    acc_scr[...] = jnp.zeros_like(acc_scr)

    units = [(i, c) for c in range(s // KV_CHUNK) for i in range(len(items))]

    def scores(u):
        i, c = units[u]
        k_off = items[i][0]
        return _dot(k_ref[0, c * KV_CHUNK:(c + 1) * KV_CHUNK, k_off:k_off + LANES], qz_scr[i])

    pending = [scores(u) for u in range(min(QK_AHEAD, len(units)))]
    for u, (i, c) in enumerate(units):
        v_off = items[i][1]
        st = pending.pop(0)
        m_old = m_scr[i]
        part = jnp.max(st.reshape(KV_CHUNK // 8, 8, tq), axis=0)
        m_new = jnp.maximum(m_old, jnp.max(part, axis=0, keepdims=True))
        m_scr[i] = m_new
        p = jnp.exp2(st - m_new[0:1]).astype(BF16)
        d = _dot(vt_ref[0, v_off:v_off + VT_ROWS, c * KV_CHUNK:(c + 1) * KV_CHUNK], p)
        if u + QK_AHEAD < len(units):
            pending.append(scores(u + QK_AHEAD))
        acc_scr[i] = acc_scr[i] * jnp.exp2(m_old[0:1] - m_new[0:1]) + d
    results = []
    for i in range(len(items)):
        acc = acc_scr[i]
        results.append(acc[0:HEAD_DIM] * (1.0 / acc[HEAD_DIM:HEAD_DIM + 1]))
    return results


def _attention_scratch(heads, tq):
    n_items = heads * (tq // Q_TILE)
    return [pltpu.VMEM((n_items, LANES, Q_TILE), BF16), pltpu.VMEM((n_items, 8, Q_TILE), F32),
            pltpu.VMEM((n_items, VT_ROWS, Q_TILE), F32)]


def _attn_a_kernel(qt_ref, k_ref, vt_ref, o_ref, qz_scr, m_scr, acc_scr):
    group = A_HEADS // A_KV_HEADS
    n_tiles = qt_ref.shape[2] // Q_TILE
    items = []
    for t in range(n_tiles):
        cols = slice(t * Q_TILE, (t + 1) * Q_TILE)
        for h in range(A_HEADS):
            g = h // group
            qz_scr[len(items)] = _place_rows(qt_ref[0, h * HEAD_DIM:(h + 1) * HEAD_DIM, cols], g * HEAD_DIM, LANES)
            items.append((0, g * VT_ROWS))
    outs = _attention_items(items, k_ref, vt_ref, qz_scr, m_scr, acc_scr)
    for t in range(n_tiles):
        tile = outs[t * A_HEADS:(t + 1) * A_HEADS]
        o_ref[0, t * Q_TILE:(t + 1) * Q_TILE, :] = jnp.concatenate([o.T for o in tile], axis=-1).astype(BF16)


def _attention_a(qat, ka, vat, tq):
    bsz, s, _ = ka.shape
    return pl.pallas_call(
        _attn_a_kernel,
        grid=(bsz, s // tq),
        in_specs=[
            pl.BlockSpec((1, A_WIDTH, tq), lambda b, i: (b, 0, i)),
            pl.BlockSpec((1, s, A_KV_WIDTH), lambda b, i: (b, 0, 0)),
            pl.BlockSpec((1, A_KV_HEADS * VT_ROWS, s), lambda b, i: (b, 0, 0)),
        ],
        out_specs=pl.BlockSpec((1, tq, A_WIDTH), lambda b, i: (b, i, 0)),
        out_shape=jax.ShapeDtypeStruct((bsz, s, A_WIDTH), BF16),
        scratch_shapes=_attention_scratch(A_HEADS, tq),
        compiler_params=pltpu.CompilerParams(
            dimension_semantics=("arbitrary", "arbitrary"), vmem_limit_bytes=VMEM_LIMIT),
        name="attn_axial_gqa",
    )(qat, ka, vat)


def _attn_b_kernel(lam_ref, gain_ref, qt_ref, k_ref, vt_ref, o_ref, qz_scr, m_scr, acc_scr, *, lam_init):
    lp = lam_ref[...]
    lam = (jnp.exp(jnp.sum(lp[0:1] * lp[1:2], axis=-1, keepdims=True))
           - jnp.exp(jnp.sum(lp[2:3] * lp[3:4], axis=-1, keepdims=True)) + lam_init)
    n_tiles = qt_ref.shape[2] // Q_TILE
    items = []
    for t in range(n_tiles):
        cols = slice(t * Q_TILE, (t + 1) * Q_TILE)
        for h in range(B_HEADS):
            pair = (h // 2) * LANES
            for comp in range(2):
                r0 = h * HEAD_DIM + comp * B_QK_DIM
                qz_scr[len(items)] = _place_rows(qt_ref[0, r0:r0 + B_QK_DIM, cols], r0 - pair, LANES)
                items.append((pair, h * VT_ROWS))
    comps = _attention_items(items, k_ref, vt_ref, qz_scr, m_scr, acc_scr)
    for t in range(n_tiles):
        outs = []
        for h in range(B_HEADS):
            i = (t * B_HEADS + h) * 2
            o = comps[i] - lam * comps[i + 1]
            o = o * lax.rsqrt(jnp.mean(o * o, axis=0, keepdims=True) + EPS)
            outs.append(o.T)
        y = jnp.concatenate(outs, axis=-1) * gain_ref[...] * (1.0 - lam_init)
        o_ref[0, t * Q_TILE:(t + 1) * Q_TILE, :] = y.astype(BF16)


def _attention_b(qbt, kb, vbt, lam_params, subln_gain, lam_init, tq):
    bsz, s, _ = kb.shape
    return pl.pallas_call(
        functools.partial(_attn_b_kernel, lam_init=lam_init),
        grid=(bsz, s // tq),
        in_specs=[
            pl.BlockSpec((4, B_QK_DIM), lambda b, i: (0, 0)),
            pl.BlockSpec((1, B_WIDTH), lambda b, i: (0, 0)),
            pl.BlockSpec((1, B_WIDTH, tq), lambda b, i: (b, 0, i)),
            pl.BlockSpec((1, s, B_WIDTH), lambda b, i: (b, 0, 0)),
            pl.BlockSpec((1, B_HEADS * VT_ROWS, s), lambda b, i: (b, 0, 0)),
        ],
        out_specs=pl.BlockSpec((1, tq, B_WIDTH), lambda b, i: (b, i, 0)),
        out_shape=jax.ShapeDtypeStruct((bsz, s, B_WIDTH), BF16),
        scratch_shapes=_attention_scratch(2 * B_HEADS, tq),
        compiler_params=pltpu.CompilerParams(
            dimension_semantics=("arbitrary", "arbitrary"), vmem_limit_bytes=VMEM_LIMIT),
        name="attn_differential",
    )(lam_params, subln_gain, qbt, kb, vbt)


def _hgrn_levels():
    sizes = []
    bs = HGRN_CHUNK
    while bs > HGRN_LEAF:
        sizes.append(bs)
        bs //= 2
    return sizes


def _hgrn_constants():
    c, w = HGRN_CHUNK, C_WIDTH
    t = np.arange(c)
    tri = np.stack([(t[:, None] >= t[None, :]), (t[:, None] <= t[None, :])]).astype(np.float32)
    s_of_lane = np.arange(w) % C_DIM
    sizes = _hgrn_levels()
    level = np.full((2, c, w), float(len(sizes) + 1), np.float32)
    for d in range(2):
        for li, bs in enumerate(sizes):
            upper = (t % bs) >= bs // 2
            qrow = upper if d == 0 else ~upper
            same = (t[:, None] // bs) == (s_of_lane[None, :] // bs)
            cross = qrow[:, None] & ~qrow[s_of_lane][None, :]
            level[d][same & cross] = li
        same_leaf = (t[:, None] // HGRN_LEAF) == (s_of_lane[None, :] // HGRN_LEAF)
        order = (s_of_lane[None, :] <= t[:, None]) if d == 0 else (s_of_lane[None, :] >= t[:, None])
        level[d][same_leaf & order] = len(sizes)
    half = (np.arange(LANES)[None, :] < C_DIM).astype(np.float32) * np.ones((c, 1), np.float32)
    return tri, level, half


def _block_diag(x, low_half):
    zero = jnp.zeros((x.shape[0], LANES), x.dtype)
    blocks = []
    for h in range(C_HEADS):
        tile = x[:, (h // 2) * LANES:(h // 2 + 1) * LANES]
        keep = tile * (low_half if h % 2 == 0 else 1 - low_half)
        blocks.append(jnp.concatenate([keep, zero] if h // 2 == 0 else [zero, keep], axis=1))
    return jnp.concatenate(blocks, axis=0)


def _block_diag_t(x, low_half):
    zero = jnp.zeros((LANES, LANES), x.dtype)
    rows = []
    for t in range(C_WIDTH // LANES):
        tile = x[:, t * LANES:(t + 1) * LANES]
        pair = jnp.concatenate([tile * low_half, tile * (1 - low_half)], axis=0).T
        rows.append(jnp.concatenate([pair, zero] if t == 0 else [zero, pair], axis=1))
    return jnp.concatenate(rows, axis=0)


def _ref_rows(b, bs, offset):
    parts = []
    for lo in range(0, b.shape[0], bs):
        parts.append(jnp.broadcast_to(b[lo + offset:lo + offset + 1, :], (bs, b.shape[1])))
    return jnp.concatenate(parts, axis=0)


def _level_operands(q, k, b, bs, d):
    half = bs // 2
    ref_off = half - 1 if d == 0 else half
    zeros = jnp.zeros((8, b.shape[1]), F32)
    qparts, kparts = [], []
    for r0 in range(0, b.shape[0], 8):
        lo = (r0 // bs) * bs
        ref = b[lo + ref_off:lo + ref_off + 1, :]
        upper = (r0 % bs) >= half
        rows = slice(r0, r0 + 8)
        if upper == (d == 0):
            qparts.append(q[rows] * jnp.exp2(b[rows] - ref))
            kparts.append(zeros)
        else:
            qparts.append(zeros)
            kparts.append(k[rows] * jnp.exp2(ref - b[rows]))
    return jnp.concatenate(qparts, axis=0).astype(BF16), jnp.concatenate(kparts, axis=0).astype(BF16)


def _hgrn_kernel(lb_ref, tri_ref, level_ref, half_ref, zq_f, zf_f, zv_f, zq_b, zf_b, zv_b,
                 of_ref, ob_ref, st_scr, *, layer, depth, nb):
    c, w = HGRN_CHUNK, C_WIDTH
    sizes = _hgrn_levels()
    n_lvl = len(sizes)

    @pl.when(pl.program_id(1) == 0)
    def _():
        st_scr[...] = jnp.zeros_like(st_scr)

    rows = [lb_ref[i:i + 1, :] for i in range(depth)]
    mx = functools.reduce(jnp.maximum, rows)
    es = [jnp.exp(r - mx) for r in rows]
    tot = functools.reduce(lambda a, b_: a + b_, es)
    ps = [e / tot for e in es]
    cum = functools.reduce(lambda a, b_: a + b_, ps[:layer + 1])
    one_minus_lb = 1.0 - jnp.clip(cum - ps[0], 0.0, 1.0)

    low_half = half_ref[...].astype(BF16)
    low_bool = half_ref[...] > 0.5
    owner = [[level_ref[d] == float(li) for li in range(n_lvl + 1)] for d in range(2)]
    chains = [(j, d) for j in range(nb) for d in range(2)]
    zrefs = {0: (zq_f, zf_f, zv_f), 1: (zq_b, zf_b, zv_b)}

    q, k, v, b, total = {}, {}, {}, {}, {}
    g_split = {}
    for ch in chains:
        j, d = ch
        zq, zf = zrefs[d][0][j], zrefs[d][1][j]
        v[ch] = zrefs[d][2][j].astype(BF16)
        q[ch] = (zq * (C_DIM ** -0.5)) / (1.0 + jnp.exp2(zq * -LOG2_E))
        k[ch] = one_minus_lb * (1.0 - 1.0 / (1.0 + jnp.exp2(zf * -LOG2_E)))
        g = jnp.log2(jnp.maximum(1.0 - k[ch], FORGET_FLOOR))
        g_hi = g.astype(BF16)
        g_r = g - g_hi.astype(F32)
        g_mid = g_r.astype(BF16)
        g_lo = (g_r - g_mid.astype(F32)).astype(BF16)
        g_split[ch] = jnp.concatenate([g_hi, g_mid, g_lo], axis=1)
    for ch in chains:
        d = ch[1]
        bb = _dot(tri_ref[d], g_split[ch])
        b[ch] = bb[:, 0:w] + bb[:, w:2 * w] + bb[:, 2 * w:3 * w]
        total[ch] = b[ch][c - 1:c, :] if d == 0 else b[ch][0:1, :]

    parts = {ch: [] for ch in chains}
    for li in range(n_lvl + 1):
        for ch in chains:
            d = ch[1]
            if li < n_lvl:
                qp, kp = _level_operands(q[ch], k[ch], b[ch], sizes[li], d)
            else:
                ref = _ref_rows(b[ch], HGRN_LEAF, HGRN_LEAF // 2 - 1 if d == 0 else HGRN_LEAF // 2)
                arg = b[ch] - ref
                qp = (q[ch] * jnp.exp2(arg)).astype(BF16)
                kp = (k[ch] * jnp.exp2(-arg)).astype(BF16)
            parts[ch].append(_dot(qp, _block_diag_t(kp, low_half)))
    a = {}
    for ch in chains:
        acc = jnp.zeros((c, w), F32)
        for li in range(n_lvl, -1, -1):
            acc = jnp.where(owner[ch[1]][li], parts[ch][li], acc)
        a[ch] = acc.astype(BF16)

    o_intra, o_inter, upd = {}, {}, {}
    for ch in chains:
        j, d = ch
        o_intra[ch] = _dot(a[ch], _block_diag(v[ch], low_half))
        qe = (q[ch] * jnp.exp2(b[ch])).astype(BF16)
        o_inter[ch] = _dot(qe, _block_diag_t(st_scr[d, j].astype(BF16), low_half))
        kb = (k[ch] * jnp.exp2(total[ch] - b[ch])).astype(BF16)
        tiles = []
        for t in range(w // LANES):
            lanes = slice(t * LANES, (t + 1) * LANES)
            full = _dot_tn(v[ch][:, lanes], kb[:, lanes])
            tiles.append(jnp.where(low_bool, full[0:C_DIM], full[C_DIM:2 * C_DIM]))
        upd[ch] = jnp.concatenate(tiles, axis=1)
    for ch in chains:
        j, d = ch
        st_scr[d, j] = st_scr[d, j] * jnp.exp2(total[ch]) + upd[ch]
        (of_ref if d == 0 else ob_ref)[j] = o_intra[ch] + o_inter[ch]


def _hgrn(zc, lower_bounds, layer, nb):
    bsz, s, _ = zc.shape
    depth = lower_bounds.shape[0]
    n_chunks = s // HGRN_CHUNK
    c, w = HGRN_CHUNK, C_WIDTH
    tri, level, half = _hgrn_constants()
    consts = (jnp.asarray(tri, BF16), jnp.asarray(level), jnp.asarray(half))
    full = lambda a: pl.BlockSpec(a.shape, lambda g, i: (0,) * a.ndim)
    last = n_chunks - 1
    fwd = lambda col: pl.BlockSpec((nb, c, w), lambda g, i: (g, i, col))
    bwd = lambda col: pl.BlockSpec((nb, c, w), lambda g, i: (g, last - i, col))
    return pl.pallas_call(
        functools.partial(_hgrn_kernel, layer=layer, depth=depth, nb=nb),
        grid=(bsz // nb, n_chunks),
        in_specs=[full(lower_bounds)] + [full(a) for a in consts]
                 + [fwd(0), fwd(1), fwd(3), bwd(0), bwd(2), bwd(3)],
        out_specs=[pl.BlockSpec((nb, c, w), lambda g, i: (g, i, 0)),
                   pl.BlockSpec((nb, c, w), lambda g, i: (g, last - i, 0))],
        out_shape=[jax.ShapeDtypeStruct((bsz, s, w), F32)] * 2,
        scratch_shapes=[pltpu.VMEM((2, nb, C_DIM, w), F32)],
        compiler_params=pltpu.CompilerParams(
            dimension_semantics=("arbitrary", "arbitrary"), vmem_limit_bytes=VMEM_LIMIT),
        name="hgrn2_bidir",
    )(lower_bounds, *consts, zc, zc, zc, zc, zc, zc)


def _out_ffn_kernel(x_ref, mod_ref, oa_ref, ob_ref, cf_ref, cb_ref, zg_ref, cgain_ref,
                    wo_ref, w1_ref, w2_ref, fg_ref, o_ref, *, final, ff_chunk):
    oc = cf_ref[0] + cb_ref[0]
    ms = _split_dot(oc * oc, _group_ones(C_WIDTH, C_DIM))
    zg = zg_ref[0]
    oc = oc * lax.rsqrt(ms + EPS) * cgain_ref[...] * (zg * jax.nn.sigmoid(zg))
    mix = jnp.concatenate([oa_ref[0], ob_ref[0], oc.astype(BF16)], axis=-1)
    x1 = x_ref[0] + mod_ref[0, 2:3, :] * _dot(mix, wo_ref[0])
    h = (_rms_rows(x1) * (1.0 + mod_ref[0, 4:5, :]) + mod_ref[0, 3:4, :]).astype(BF16)
    d_ff = w1_ref.shape[2]
    acc = jnp.zeros_like(x1)
    for j in range(d_ff // ff_chunk):
        u = _dot(h, w1_ref[0, :, j * ff_chunk:(j + 1) * ff_chunk])
        u = jnp.square(jnp.maximum(u, 0.0)).astype(BF16)
        acc = acc + _dot(u, w2_ref[0, j * ff_chunk:(j + 1) * ff_chunk, :])
    x2 = x1 + mod_ref[0, 5:6, :] * acc
    if final:
        x2 = _rms_rows(x2) * fg_ref[...]
    o_ref[0] = x2


def _out_ffn(x, mod_l, oa, ob, oc_f, oc_b, zc, c_gain, w_out, w_ff1, w_ff2, final_gain, layer, final, tm):
    bsz, s, d = x.shape
    d_ff = w_ff1.shape[2]
    row_spec = lambda width: pl.BlockSpec((1, tm, width), lambda b, i: (b, i, 0))
    weight = lambda a: pl.BlockSpec((1,) + a.shape[1:], lambda b, i: (layer, 0, 0),
                                    pipeline_mode=pl.Buffered(1))
    return pl.pallas_call(
        functools.partial(_out_ffn_kernel, final=final, ff_chunk=1024),
        grid=(bsz, s // tm),
        in_specs=[
            row_spec(d),
            pl.BlockSpec((1, 6, d), lambda b, i: (b, 0, 0)),
            row_spec(A_WIDTH), row_spec(B_WIDTH), row_spec(C_WIDTH), row_spec(C_WIDTH),
            pl.BlockSpec((1, tm, C_WIDTH), lambda b, i: (b, i, 4)),
            pl.BlockSpec((1, C_WIDTH), lambda b, i: (0, 0)),
            weight(w_out), weight(w_ff1), weight(w_ff2),
            pl.BlockSpec((1, d), lambda b, i: (0, 0)),
        ],
        out_specs=row_spec(d),
        out_shape=jax.ShapeDtypeStruct((bsz, s, d), F32),
        compiler_params=pltpu.CompilerParams(
            dimension_semantics=("arbitrary", "arbitrary"), vmem_limit_bytes=VMEM_LIMIT),
        name="out_proj_ffn",
    )(x, mod_l, oa, ob, oc_f, oc_b, zc, c_gain, w_out, w_ff1, w_ff2, final_gain)


def _rope_tables(s):
    half = HEAD_DIM // 2
    inv = ROPE_THETA ** (-jnp.arange(0, half, 2, dtype=F32) / half)
    t = jnp.arange(s)
    ang_row = (t // GRID_W).astype(F32)[:, None] * inv[None, :]
    ang_col = (t % GRID_W).astype(F32)[:, None] * inv[None, :]
    ang_1d = t.astype(F32)[:, None] * inv[None, :]

    def pair(ang):
        return (jnp.concatenate([jnp.cos(ang), jnp.cos(ang)], axis=-1),
                jnp.concatenate([-jnp.sin(ang), jnp.sin(ang)], axis=-1))

    cr, sr = pair(ang_row)
    cc, sc = pair(ang_col)
    c1, s1 = pair(ang_1d)
    tile = lambda a: jnp.tile(a, (1, LANES // a.shape[1]))
    return (tile(jnp.concatenate([cr, cc], axis=-1)), tile(jnp.concatenate([sr, sc], axis=-1)),
            tile(c1), tile(s1))


def kernel(x, c, w_mod, b_mod, w_in, a_qk_norm, diff_lambda, diff_subln, hgrn_lower_bounds, hgrn_norm,
           w_out, w_ff1, w_ff2, final_norm):
    bsz, s, d = x.shape
    depth = w_in.shape[0]
    tm = min(512, s)
    tq = min(2 * Q_TILE, s)
    tables = _rope_tables(s)
    mod = _modulation(c, w_mod, b_mod).reshape(depth, bsz, 6, d)
    w_in_b, w_out_b, w_ff1_b, w_ff2_b = (w.astype(BF16) for w in (w_in, w_out, w_ff1, w_ff2))
    final_gain = final_norm.reshape(1, d)
    for l in range(depth):
        qk_gain = jnp.tile(a_qk_norm[l], (1, LANES // HEAD_DIM))
        qat, ka, vat, qbt, kb, vbt, zc = _in_projection(x, mod[l], w_in_b, l, tables, qk_gain, tm)
        oa = _attention_a(qat, ka, vat, tq)
        lam_init = 0.8 - 0.6 * math.exp(-0.3 * l)
        subln = jnp.tile(diff_subln[l].reshape(1, HEAD_DIM), (1, B_HEADS))
        ob = _attention_b(qbt, kb, vbt, diff_lambda[l], subln, lam_init, tq)
        hg_gain = jnp.tile(hgrn_norm[l].reshape(1, C_DIM), (1, C_HEADS))
        oc_f, oc_b = _hgrn(zc, hgrn_lower_bounds, l, min(HGRN_BATCH, bsz))
        x = _out_ffn(x, mod[l], oa, ob, oc_f, oc_b, zc, hg_gain, w_out_b, w_ff1_b, w_ff2_b, final_gain,
                     l, l == depth - 1, tm)
    return x
```

```python
import functools
import math

import numpy as np
import jax
import jax.numpy as jnp
from jax import lax
from jax.experimental import pallas as pl
from jax.experimental.pallas import tpu as pltpu

F32 = jnp.float32
BF16 = jnp.bfloat16

HEAD_DIM = 64
GRID_W = 64
ROPE_THETA = 10000.0
EPS = 1e-6
FORGET_FLOOR = 1e-6
A_HEADS, A_KV_HEADS = 6, 2
B_HEADS, B_QK_DIM = 6, 32
C_HEADS, C_DIM = 4, 64
A_WIDTH = A_HEADS * HEAD_DIM
A_KV_WIDTH = A_KV_HEADS * HEAD_DIM
B_WIDTH = B_HEADS * HEAD_DIM
C_WIDTH = C_HEADS * C_DIM
LANES = 128
HGRN_CHUNK = 64
HGRN_LEAF = 8
HGRN_BATCH = 16
ONES_ROWS = 16
VT_ROWS = HEAD_DIM + ONES_ROWS
KV_CHUNK = 256
Q_TILE = 256
QK_AHEAD = 16
LOG2_E = math.log2(math.e)
VMEM_LIMIT = 56 * 1024 * 1024

OFF_AQ, OFF_AK, OFF_AV = 0, 384, 512
OFF_BQ, OFF_BK, OFF_BV = 640, 1024, 1408
OFF_C = 1792
IN_TOTAL = 3072


def _dot(a, b):
    return jnp.dot(a, b, preferred_element_type=F32)


def _dot_nt(a, b):
    return lax.dot_general(a, b, (((1,), (1,)), ((), ())), preferred_element_type=F32)


def _dot_tn(a, b):
    return lax.dot_general(a, b, (((0,), (0,)), ((), ())), preferred_element_type=F32)


def _split_dot(x, w_bf16):
    hi = x.astype(BF16)
    lo = (x - hi.astype(F32)).astype(BF16)
    return _dot(hi, w_bf16) + _dot(lo, w_bf16)


def _group_ones(n, group):
    r = lax.broadcasted_iota(jnp.int32, (n, n), 0) // group
    c = lax.broadcasted_iota(jnp.int32, (n, n), 1) // group
    return jnp.where(r == c, 1.0 / group, 0.0).astype(BF16)


def _rope(xb, cos, sin_signed):
    lane = lax.broadcasted_iota(jnp.int32, xb.shape, 1)
    low = (lane % 32) < 16
    partner = jnp.where(low, pltpu.roll(xb, LANES - 16, 1), pltpu.roll(xb, 16, 1))
    return xb * cos + partner * sin_signed


def _rms_rows(x):
    return x * lax.rsqrt(jnp.mean(x * x, axis=-1, keepdims=True) + EPS)


def _mod_kernel(c_ref, w_ref, b_ref, o_ref):
    c = c_ref[...]
    cond = (c * jax.nn.sigmoid(c)).astype(BF16)
    o_ref[0] = _dot(cond, w_ref[0].astype(BF16)) + b_ref[0]


def _modulation(c, w_mod, b_mod):
    depth, d, n = w_mod.shape
    bsz = c.shape[0]
    tn = 1024
    return pl.pallas_call(
        _mod_kernel,
        grid=(depth, n // tn),
        in_specs=[
            pl.BlockSpec((bsz, d), lambda l, j: (0, 0)),
            pl.BlockSpec((1, d, tn), lambda l, j: (l, 0, j)),
            pl.BlockSpec((1, 1, tn), lambda l, j: (l, 0, j)),
        ],
        out_specs=pl.BlockSpec((1, bsz, tn), lambda l, j: (l, 0, j)),
        out_shape=jax.ShapeDtypeStruct((depth, bsz, n), F32),
        compiler_params=pltpu.CompilerParams(
            dimension_semantics=("arbitrary", "arbitrary"), vmem_limit_bytes=VMEM_LIMIT),
        name="adaln_mod",
    )(c, w_mod, b_mod.reshape(depth, 1, n))


def _inproj_kernel(x_ref, mod_ref, w_ref, cosa_ref, sina_ref, cosb_ref, sinb_ref, gain_ref,
                   qat_ref, ka_ref, vat_ref, qbt_ref, kb_ref, vbt_ref, zc_ref, h_scr, z_scr):
    x = x_ref[0]
    h = _rms_rows(x) * (1.0 + mod_ref[0, 1:2, :]) + mod_ref[0, 0:1, :]
    h_scr[...] = h.astype(BF16)

    def project(j):
        z_scr[:, j * 1024:(j + 1) * 1024] = _dot(h_scr[...], w_ref[0, :, j * 1024:(j + 1) * 1024])

    project(0)
    project(1)

    ones = _group_ones(LANES, HEAD_DIM)
    cosa, sina = cosa_ref[...], sina_ref[...]
    cosb, sinb = cosb_ref[...], sinb_ref[...]

    def a_norm_rope(zb, gain):
        ms = _split_dot(zb * zb, ones)
        return _rope(zb * lax.rsqrt(ms + EPS) * gain, cosa, sina)

    def store_vt(vt_ref, first_head, blk_t):
        ones_rows = jnp.ones((ONES_ROWS, blk_t.shape[1]), BF16)
        for j in range(LANES // HEAD_DIM):
            r0 = (first_head + j) * VT_ROWS
            vt_ref[0, r0:r0 + HEAD_DIM, :] = blk_t[j * HEAD_DIM:(j + 1) * HEAD_DIM].astype(BF16)
            vt_ref[0, r0 + HEAD_DIM:r0 + VT_ROWS, :] = ones_rows

    for cblk in range(A_WIDTH // LANES):
        zb = z_scr[:, OFF_AQ + cblk * LANES:OFF_AQ + (cblk + 1) * LANES]
        y = a_norm_rope(zb, gain_ref[0:1, :]) * (HEAD_DIM ** -0.5 * LOG2_E)
        qat_ref[0, cblk * LANES:(cblk + 1) * LANES, :] = y.T.astype(BF16)
    ka_ref[0] = a_norm_rope(z_scr[:, OFF_AK:OFF_AK + LANES], gain_ref[1:2, :]).astype(BF16)
    store_vt(vat_ref, 0, z_scr[:, OFF_AV:OFF_AV + LANES].T)
    project(2)

    for cblk in range(B_WIDTH // LANES):
        lo = cblk * LANES
        zq = z_scr[:, OFF_BQ + lo:OFF_BQ + lo + LANES]
        qbt_ref[0, lo:lo + LANES, :] = (_rope(zq, cosb, sinb) * (B_QK_DIM ** -0.5 * LOG2_E)).T.astype(BF16)
    for cblk in range(B_WIDTH // LANES):
        lo = cblk * LANES
        zk = z_scr[:, OFF_BK + lo:OFF_BK + lo + LANES]
        kb_ref[0, :, lo:lo + LANES] = _rope(zk, cosb, sinb).astype(BF16)
        store_vt(vbt_ref, cblk * (LANES // HEAD_DIM), z_scr[:, OFF_BV + lo:OFF_BV + lo + LANES].T)
    zc_ref[0] = z_scr[:, OFF_C:IN_TOTAL]


def _in_projection(x, mod_l, w_in_bf16, layer, tables, qk_gain, tm):
    bsz, s, d = x.shape
    cosa, sina, cosb, sinb = tables
    row_spec = lambda width: pl.BlockSpec((1, tm, width), lambda b, i: (b, i, 0))
    tab_spec = pl.BlockSpec((tm, LANES), lambda b, i: (i, 0))
    col_spec = lambda height: pl.BlockSpec((1, height, tm), lambda b, i: (b, 0, i))
    return pl.pallas_call(
        _inproj_kernel,
        grid=(bsz, s // tm),
        in_specs=[
            row_spec(d),
            pl.BlockSpec((1, 6, d), lambda b, i: (b, 0, 0)),
            pl.BlockSpec((1, d, IN_TOTAL), lambda b, i: (layer, 0, 0)),
            tab_spec, tab_spec, tab_spec, tab_spec,
            pl.BlockSpec((2, LANES), lambda b, i: (0, 0)),
        ],
        out_specs=[
            col_spec(A_WIDTH), row_spec(A_KV_WIDTH), col_spec(A_KV_HEADS * VT_ROWS),
            col_spec(B_WIDTH), row_spec(B_WIDTH), col_spec(B_HEADS * VT_ROWS),
            row_spec(IN_TOTAL - OFF_C),
        ],
        out_shape=[
            jax.ShapeDtypeStruct((bsz, A_WIDTH, s), BF16),
            jax.ShapeDtypeStruct((bsz, s, A_KV_WIDTH), BF16),
            jax.ShapeDtypeStruct((bsz, A_KV_HEADS * VT_ROWS, s), BF16),
            jax.ShapeDtypeStruct((bsz, B_WIDTH, s), BF16),
            jax.ShapeDtypeStruct((bsz, s, B_WIDTH), BF16),
            jax.ShapeDtypeStruct((bsz, B_HEADS * VT_ROWS, s), BF16),
            jax.ShapeDtypeStruct((bsz, s, IN_TOTAL - OFF_C), F32),
        ],
        scratch_shapes=[pltpu.VMEM((tm, d), BF16), pltpu.VMEM((tm, IN_TOTAL), F32)],
        compiler_params=pltpu.CompilerParams(
            dimension_semantics=("arbitrary", "arbitrary"), vmem_limit_bytes=VMEM_LIMIT),
        name="in_projection",
    )(x, mod_l, w_in_bf16, cosa, sina, cosb, sinb, qk_gain)


def _place_rows(block, start, total):
    parts = []
    if start:
        parts.append(jnp.zeros((start, block.shape[1]), block.dtype))
    parts.append(block)
    rest = total - start - block.shape[0]
    if rest:
        parts.append(jnp.zeros((rest, block.shape[1]), block.dtype))
    return jnp.concatenate(parts, axis=0) if len(parts) > 1 else block


def _attention_items(items, k_ref, vt_ref, qz_scr, m_scr, acc_scr):
    s = k_ref.shape[1]
    tq = acc_scr.shape[2]
    m_scr[...] = jnp.full(m_scr.shape, -1e30, F32)
    acc_scr[...] = jnp.zeros_like(acc_scr)

    units = [(i, c) for c in range(s // KV_CHUNK) for i in range(len(items))]

    def scores(u):
        i, c = units[u]
        k_off = items[i][0]
        return _dot(k_ref[0, c * KV_CHUNK:(c + 1) * KV_CHUNK, k_off:k_off + LANES], qz_scr[i])

    pending = [scores(u) for u in range(min(QK_AHEAD, len(units)))]
    for u, (i, c) in enumerate(units):
        v_off = items[i][1]
        st = pending.pop(0)
        m_old = m_scr[i]
        part = jnp.max(st.reshape(KV_CHUNK // 8, 8, tq), axis=0)
        m_new = jnp.maximum(m_old, jnp.max(part, axis=0, keepdims=True))
        m_scr[i] = m_new
        p = jnp.exp2(st - m_new[0:1]).astype(BF16)
        d = _dot(vt_ref[0, v_off:v_off + VT_ROWS, c * KV_CHUNK:(c + 1) * KV_CHUNK], p)
        if u + QK_AHEAD < len(units):
            pending.append(scores(u + QK_AHEAD))
        acc_scr[i] = acc_scr[i] * jnp.exp2(m_old[0:1] - m_new[0:1]) + d
    results = []
    for i in range(len(items)):
        acc = acc_scr[i]
        results.append(acc[0:HEAD_DIM] * (1.0 / acc[HEAD_DIM:HEAD_DIM + 1]))
    return results


def _attention_scratch(heads, tq):
    n_items = heads * (tq // Q_TILE)
    return [pltpu.VMEM((n_items, LANES, Q_TILE), BF16), pltpu.VMEM((n_items, 8, Q_TILE), F32),
            pltpu.VMEM((n_items, VT_ROWS, Q_TILE), F32)]


def _attn_a_kernel(qt_ref, k_ref, vt_ref, o_ref, qz_scr, m_scr, acc_scr):
    group = A_HEADS // A_KV_HEADS
    n_tiles = qt_ref.shape[2] // Q_TILE
    items = []
    for t in range(n_tiles):
        cols = slice(t * Q_TILE, (t + 1) * Q_TILE)
        for h in range(A_HEADS):
            g = h // group
            qz_scr[len(items)] = _place_rows(qt_ref[0, h * HEAD_DIM:(h + 1) * HEAD_DIM, cols], g * HEAD_DIM, LANES)
            items.append((0, g * VT_ROWS))
    outs = _attention_items(items, k_ref, vt_ref, qz_scr, m_scr, acc_scr)
    for t in range(n_tiles):
        tile = outs[t * A_HEADS:(t + 1) * A_HEADS]
        o_ref[0, t * Q_TILE:(t + 1) * Q_TILE, :] = jnp.concatenate([o.T for o in tile], axis=-1).astype(BF16)


def _attention_a(qat, ka, vat, tq):
    bsz, s, _ = ka.shape
    return pl.pallas_call(
        _attn_a_kernel,
        grid=(bsz, s // tq),
        in_specs=[
            pl.BlockSpec((1, A_WIDTH, tq), lambda b, i: (b, 0, i)),
            pl.BlockSpec((1, s, A_KV_WIDTH), lambda b, i: (b, 0, 0)),
            pl.BlockSpec((1, A_KV_HEADS * VT_ROWS, s), lambda b, i: (b, 0, 0)),
        ],
        out_specs=pl.BlockSpec((1, tq, A_WIDTH), lambda b, i: (b, i, 0)),
        out_shape=jax.ShapeDtypeStruct((bsz, s, A_WIDTH), BF16),
        scratch_shapes=_attention_scratch(A_HEADS, tq),
        compiler_params=pltpu.CompilerParams(
            dimension_semantics=("arbitrary", "arbitrary"), vmem_limit_bytes=VMEM_LIMIT),
        name="attn_axial_gqa",
    )(qat, ka, vat)


def _attn_b_kernel(lam_ref, gain_ref, qt_ref, k_ref, vt_ref, o_ref, qz_scr, m_scr, acc_scr, *, lam_init):
    lp = lam_ref[...]
    lam = (jnp.exp(jnp.sum(lp[0:1] * lp[1:2], axis=-1, keepdims=True))
           - jnp.exp(jnp.sum(lp[2:3] * lp[3:4], axis=-1, keepdims=True)) + lam_init)
    n_tiles = qt_ref.shape[2] // Q_TILE
    items = []
    for t in range(n_tiles):
        cols = slice(t * Q_TILE, (t + 1) * Q_TILE)
        for h in range(B_HEADS):
            pair = (h // 2) * LANES
            for comp in range(2):
                r0 = h * HEAD_DIM + comp * B_QK_DIM
                qz_scr[len(items)] = _place_rows(qt_ref[0, r0:r0 + B_QK_DIM, cols], r0 - pair, LANES)
                items.append((pair, h * VT_ROWS))
    comps = _attention_items(items, k_ref, vt_ref, qz_scr, m_scr, acc_scr)
    for t in range(n_tiles):
        outs = []
        for h in range(B_HEADS):
            i = (t * B_HEADS + h) * 2
            o = comps[i] - lam * comps[i + 1]
            o = o * lax.rsqrt(jnp.mean(o * o, axis=0, keepdims=True) + EPS)
            outs.append(o.T)
        y = jnp.concatenate(outs, axis=-1) * gain_ref[...] * (1.0 - lam_init)
        o_ref[0, t * Q_TILE:(t + 1) * Q_TILE, :] = y.astype(BF16)


def _attention_b(qbt, kb, vbt, lam_params, subln_gain, lam_init, tq):
    bsz, s, _ = kb.shape
    return pl.pallas_call(
        functools.partial(_attn_b_kernel, lam_init=lam_init),
        grid=(bsz, s // tq),
        in_specs=[
            pl.BlockSpec((4, B_QK_DIM), lambda b, i: (0, 0)),
            pl.BlockSpec((1, B_WIDTH), lambda b, i: (0, 0)),
            pl.BlockSpec((1, B_WIDTH, tq), lambda b, i: (b, 0, i)),
            pl.BlockSpec((1, s, B_WIDTH), lambda b, i: (b, 0, 0)),
            pl.BlockSpec((1, B_HEADS * VT_ROWS, s), lambda b, i: (b, 0, 0)),
        ],
        out_specs=pl.BlockSpec((1, tq, B_WIDTH), lambda b, i: (b, i, 0)),
        out_shape=jax.ShapeDtypeStruct((bsz, s, B_WIDTH), BF16),
        scratch_shapes=_attention_scratch(2 * B_HEADS, tq),
        compiler_params=pltpu.CompilerParams(
            dimension_semantics=("arbitrary", "arbitrary"), vmem_limit_bytes=VMEM_LIMIT),
        name="attn_differential",
    )(lam_params, subln_gain, qbt, kb, vbt)


def _hgrn_levels():
    sizes = []
    bs = HGRN_CHUNK
    while bs > HGRN_LEAF:
        sizes.append(bs)
        bs //= 2
    return sizes


def _hgrn_constants():
    c, w = HGRN_CHUNK, C_WIDTH
    t = np.arange(c)
    tri = np.stack([(t[:, None] >= t[None, :]), (t[:, None] <= t[None, :])]).astype(np.float32)
    s_of_lane = np.arange(w) % C_DIM
    sizes = _hgrn_levels()
    level = np.full((2, c, w), float(len(sizes) + 1), np.float32)
    for d in range(2):
        for li, bs in enumerate(sizes):
            upper = (t % bs) >= bs // 2
            qrow = upper if d == 0 else ~upper
            same = (t[:, None] // bs) == (s_of_lane[None, :] // bs)
            cross = qrow[:, None] & ~qrow[s_of_lane][None, :]
            level[d][same & cross] = li
        same_leaf = (t[:, None] // HGRN_LEAF) == (s_of_lane[None, :] // HGRN_LEAF)
        order = (s_of_lane[None, :] <= t[:, None]) if d == 0 else (s_of_lane[None, :] >= t[:, None])
        level[d][same_leaf & order] = len(sizes)
    half = (np.arange(LANES)[None, :] < C_DIM).astype(np.float32) * np.ones((c, 1), np.float32)
    return tri, level, half


def _block_diag(x, low_half):
    zero = jnp.zeros((x.shape[0], LANES), x.dtype)
    blocks = []
    for h in range(C_HEADS):
        tile = x[:, (h // 2) * LANES:(h // 2 + 1) * LANES]
        keep = tile * (low_half if h % 2 == 0 else 1 - low_half)
        blocks.append(jnp.concatenate([keep, zero] if h // 2 == 0 else [zero, keep], axis=1))
    return jnp.concatenate(blocks, axis=0)


def _block_diag_t(x, low_half):
    zero = jnp.zeros((LANES, LANES), x.dtype)
    rows = []
    for t in range(C_WIDTH // LANES):
        tile = x[:, t * LANES:(t + 1) * LANES]
        pair = jnp.concatenate([tile * low_half, tile * (1 - low_half)], axis=0).T
        rows.append(jnp.concatenate([pair, zero] if t == 0 else [zero, pair], axis=1))
    return jnp.concatenate(rows, axis=0)


def _ref_rows(b, bs, offset):
    parts = []
    for lo in range(0, b.shape[0], bs):
        parts.append(jnp.broadcast_to(b[lo + offset:lo + offset + 1, :], (bs, b.shape[1])))
    return jnp.concatenate(parts, axis=0)


def _level_operands(q, k, b, bs, d):
    half = bs // 2
    ref_off = half - 1 if d == 0 else half
    zeros = jnp.zeros((8, b.shape[1]), F32)
    qparts, kparts = [], []
    for r0 in range(0, b.shape[0], 8):
        lo = (r0 // bs) * bs
        ref = b[lo + ref_off:lo + ref_off + 1, :]
        upper = (r0 % bs) >= half
        rows = slice(r0, r0 + 8)
        if upper == (d == 0):
            qparts.append(q[rows] * jnp.exp2(b[rows] - ref))
            kparts.append(zeros)
        else:
            qparts.append(zeros)
            kparts.append(k[rows] * jnp.exp2(ref - b[rows]))
    return jnp.concatenate(qparts, axis=0).astype(BF16), jnp.concatenate(kparts, axis=0).astype(BF16)


def _hgrn_kernel(lb_ref, tri_ref, level_ref, half_ref, zq_f, zf_f, zv_f, zq_b, zf_b, zv_b,
                 of_ref, ob_ref, st_scr, *, layer, depth, nb):
    c, w = HGRN_CHUNK, C_WIDTH
    sizes = _hgrn_levels()
    n_lvl = len(sizes)

    @pl.when(pl.program_id(1) == 0)
    def _():
        st_scr[...] = jnp.zeros_like(st_scr)

    rows = [lb_ref[i:i + 1, :] for i in range(depth)]
    mx = functools.reduce(jnp.maximum, rows)
    es = [jnp.exp(r - mx) for r in rows]
    tot = functools.reduce(lambda a, b_: a + b_, es)
    ps = [e / tot for e in es]
    cum = functools.reduce(lambda a, b_: a + b_, ps[:layer + 1])
    one_minus_lb = 1.0 - jnp.clip(cum - ps[0], 0.0, 1.0)

    low_half = half_ref[...].astype(BF16)
    low_bool = half_ref[...] > 0.5
    owner = [[level_ref[d] == float(li) for li in range(n_lvl + 1)] for d in range(2)]
    chains = [(j, d) for j in range(nb) for d in range(2)]
    zrefs = {0: (zq_f, zf_f, zv_f), 1: (zq_b, zf_b, zv_b)}

    q, k, v, b, total = {}, {}, {}, {}, {}
    g_split = {}
    for ch in chains:
        j, d = ch
        zq, zf = zrefs[d][0][j], zrefs[d][1][j]
        v[ch] = zrefs[d][2][j].astype(BF16)
        q[ch] = (zq * (C_DIM ** -0.5)) / (1.0 + jnp.exp2(zq * -LOG2_E))
        k[ch] = one_minus_lb * (1.0 - 1.0 / (1.0 + jnp.exp2(zf * -LOG2_E)))
        g = jnp.log2(jnp.maximum(1.0 - k[ch], FORGET_FLOOR))
        g_hi = g.astype(BF16)
        g_r = g - g_hi.astype(F32)
        g_mid = g_r.astype(BF16)
        g_lo = (g_r - g_mid.astype(F32)).astype(BF16)
        g_split[ch] = jnp.concatenate([g_hi, g_mid, g_lo], axis=1)
    for ch in chains:
        d = ch[1]
        bb = _dot(tri_ref[d], g_split[ch])
        b[ch] = bb[:, 0:w] + bb[:, w:2 * w] + bb[:, 2 * w:3 * w]
        total[ch] = b[ch][c - 1:c, :] if d == 0 else b[ch][0:1, :]

    parts = {ch: [] for ch in chains}
    for li in range(n_lvl + 1):
        for ch in chains:
            d = ch[1]
            if li < n_lvl:
                qp, kp = _level_operands(q[ch], k[ch], b[ch], sizes[li], d)
            else:
                ref = _ref_rows(b[ch], HGRN_LEAF, HGRN_LEAF // 2 - 1 if d == 0 else HGRN_LEAF // 2)
                arg = b[ch] - ref
                qp = (q[ch] * jnp.exp2(arg)).astype(BF16)
                kp = (k[ch] * jnp.exp2(-arg)).astype(BF16)
            parts[ch].append(_dot(qp, _block_diag_t(kp, low_half)))
    a = {}
    for ch in chains:
        acc = parts[ch][0]
        for li in range(1, n_lvl + 1):
            acc = jnp.where(owner[ch[1]][li], parts[ch][li], acc)
        a[ch] = acc.astype(BF16)

    o_intra, o_inter, upd = {}, {}, {}
    for ch in chains:
        j, d = ch
        o_intra[ch] = _dot(a[ch], _block_diag(v[ch], low_half))
        qe = (q[ch] * jnp.exp2(b[ch])).astype(BF16)
        o_inter[ch] = _dot(qe, _block_diag_t(st_scr[d, j].astype(BF16), low_half))
        kb = (k[ch] * jnp.exp2(total[ch] - b[ch])).astype(BF16)
        tiles = []
        for t in range(w // LANES):
            lanes = slice(t * LANES, (t + 1) * LANES)
            full = _dot_tn(v[ch][:, lanes], kb[:, lanes])
            tiles.append(jnp.where(low_bool, full[0:C_DIM], full[C_DIM:2 * C_DIM]))
        upd[ch] = jnp.concatenate(tiles, axis=1)
    for ch in chains:
        j, d = ch
        st_scr[d, j] = st_scr[d, j] * jnp.exp2(total[ch]) + upd[ch]
        (of_ref if d == 0 else ob_ref)[j] = o_intra[ch] + o_inter[ch]


def _hgrn(zc, lower_bounds, layer, nb):
    bsz, s, _ = zc.shape
    depth = lower_bounds.shape[0]
    n_chunks = s // HGRN_CHUNK
    c, w = HGRN_CHUNK, C_WIDTH
    tri, level, half = _hgrn_constants()
    consts = (jnp.asarray(tri, BF16), jnp.asarray(level), jnp.asarray(half))
    full = lambda a: pl.BlockSpec(a.shape, lambda g, i: (0,) * a.ndim)
    last = n_chunks - 1
    fwd = lambda col: pl.BlockSpec((nb, c, w), lambda g, i: (g, i, col))
    bwd = lambda col: pl.BlockSpec((nb, c, w), lambda g, i: (g, last - i, col))
    return pl.pallas_call(
        functools.partial(_hgrn_kernel, layer=layer, depth=depth, nb=nb),
        grid=(bsz // nb, n_chunks),
        in_specs=[full(lower_bounds)] + [full(a) for a in consts]
                 + [fwd(0), fwd(1), fwd(3), bwd(0), bwd(2), bwd(3)],
        out_specs=[pl.BlockSpec((nb, c, w), lambda g, i: (g, i, 0)),
                   pl.BlockSpec((nb, c, w), lambda g, i: (g, last - i, 0))],
        out_shape=[jax.ShapeDtypeStruct((bsz, s, w), F32)] * 2,
        scratch_shapes=[pltpu.VMEM((2, nb, C_DIM, w), F32)],
        compiler_params=pltpu.CompilerParams(
            dimension_semantics=("arbitrary", "arbitrary"), vmem_limit_bytes=VMEM_LIMIT),
        name="hgrn2_bidir",
    )(lower_bounds, *consts, zc, zc, zc, zc, zc, zc)


def _out_ffn_kernel(x_ref, mod_ref, oa_ref, ob_ref, cf_ref, cb_ref, zg_ref, cgain_ref,
                    wo_ref, w1_ref, w2_ref, fg_ref, o_ref, *, final, ff_chunk):
    oc = cf_ref[0] + cb_ref[0]
    ms = _split_dot(oc * oc, _group_ones(C_WIDTH, C_DIM))
    zg = zg_ref[0]
    oc = oc * lax.rsqrt(ms + EPS) * cgain_ref[...] * (zg * jax.nn.sigmoid(zg))
    mix = jnp.concatenate([oa_ref[0], ob_ref[0], oc.astype(BF16)], axis=-1)
    x1 = x_ref[0] + mod_ref[0, 2:3, :] * _dot(mix, wo_ref[0])
    h = (_rms_rows(x1) * (1.0 + mod_ref[0, 4:5, :]) + mod_ref[0, 3:4, :]).astype(BF16)
    d_ff = w1_ref.shape[2]
    acc = jnp.zeros_like(x1)
    for j in range(d_ff // ff_chunk):
        u = _dot(h, w1_ref[0, :, j * ff_chunk:(j + 1) * ff_chunk])
        u = jnp.square(jnp.maximum(u, 0.0)).astype(BF16)
        acc = acc + _dot(u, w2_ref[0, j * ff_chunk:(j + 1) * ff_chunk, :])
    x2 = x1 + mod_ref[0, 5:6, :] * acc
    if final:
        x2 = _rms_rows(x2) * fg_ref[...]
    o_ref[0] = x2


def _out_ffn(x, mod_l, oa, ob, oc_f, oc_b, zc, c_gain, w_out, w_ff1, w_ff2, final_gain, layer, final, tm):
    bsz, s, d = x.shape
    d_ff = w_ff1.shape[2]
    row_spec = lambda width: pl.BlockSpec((1, tm, width), lambda b, i: (b, i, 0))
    weight = lambda a: pl.BlockSpec((1,) + a.shape[1:], lambda b, i: (layer, 0, 0),
                                    pipeline_mode=pl.Buffered(1))
    return pl.pallas_call(
        functools.partial(_out_ffn_kernel, final=final, ff_chunk=1024),
        grid=(bsz, s // tm),
        in_specs=[
            row_spec(d),
            pl.BlockSpec((1, 6, d), lambda b, i: (b, 0, 0)),
            row_spec(A_WIDTH), row_spec(B_WIDTH), row_spec(C_WIDTH), row_spec(C_WIDTH),
            pl.BlockSpec((1, tm, C_WIDTH), lambda b, i: (b, i, 4)),
            pl.BlockSpec((1, C_WIDTH), lambda b, i: (0, 0)),
            weight(w_out), weight(w_ff1), weight(w_ff2),
            pl.BlockSpec((1, d), lambda b, i: (0, 0)),
        ],
        out_specs=row_spec(d),
        out_shape=jax.ShapeDtypeStruct((bsz, s, d), F32),
        compiler_params=pltpu.CompilerParams(
            dimension_semantics=("arbitrary", "arbitrary"), vmem_limit_bytes=VMEM_LIMIT),
        name="out_proj_ffn",
    )(x, mod_l, oa, ob, oc_f, oc_b, zc, c_gain, w_out, w_ff1, w_ff2, final_gain)


def _rope_tables(s):
    half = HEAD_DIM // 2
    inv = ROPE_THETA ** (-jnp.arange(0, half, 2, dtype=F32) / half)
    t = jnp.arange(s)
    ang_row = (t // GRID_W).astype(F32)[:, None] * inv[None, :]
    ang_col = (t % GRID_W).astype(F32)[:, None] * inv[None, :]
    ang_1d = t.astype(F32)[:, None] * inv[None, :]

    def pair(ang):
        return (jnp.concatenate([jnp.cos(ang), jnp.cos(ang)], axis=-1),
                jnp.concatenate([-jnp.sin(ang), jnp.sin(ang)], axis=-1))

    cr, sr = pair(ang_row)
    cc, sc = pair(ang_col)
    c1, s1 = pair(ang_1d)
    tile = lambda a: jnp.tile(a, (1, LANES // a.shape[1]))
    return (tile(jnp.concatenate([cr, cc], axis=-1)), tile(jnp.concatenate([sr, sc], axis=-1)),
            tile(c1), tile(s1))


def kernel(x, c, w_mod, b_mod, w_in, a_qk_norm, diff_lambda, diff_subln, hgrn_lower_bounds, hgrn_norm,
           w_out, w_ff1, w_ff2, final_norm):
    bsz, s, d = x.shape
    depth = w_in.shape[0]
    tm = min(512, s)
    tq = min(2 * Q_TILE, s)
    tables = _rope_tables(s)
    mod = _modulation(c, w_mod, b_mod).reshape(depth, bsz, 6, d)
    w_in_b, w_out_b, w_ff1_b, w_ff2_b = (w.astype(BF16) for w in (w_in, w_out, w_ff1, w_ff2))
    final_gain = final_norm.reshape(1, d)
    for l in range(depth):
        qk_gain = jnp.tile(a_qk_norm[l], (1, LANES // HEAD_DIM))
        qat, ka, vat, qbt, kb, vbt, zc = _in_projection(x, mod[l], w_in_b, l, tables, qk_gain, tm)
        oa = _attention_a(qat, ka, vat, tq)
        lam_init = 0.8 - 0.6 * math.exp(-0.3 * l)
        subln = jnp.tile(diff_subln[l].reshape(1, HEAD_DIM), (1, B_HEADS))
        ob = _attention_b(qbt, kb, vbt, diff_lambda[l], subln, lam_init, tq)
        hg_gain = jnp.tile(hgrn_norm[l].reshape(1, C_DIM), (1, C_HEADS))
        oc_f, oc_b = _hgrn(zc, hgrn_lower_bounds, l, min(HGRN_BATCH, bsz))
        x = _out_ffn(x, mod[l], oa, ob, oc_f, oc_b, zc, hg_gain, w_out_b, w_ff1_b, w_ff2_b, final_gain,
                     l, l == depth - 1, tm)
    return x
```

```python
import functools
import math

import numpy as np
import jax
import jax.numpy as jnp
from jax import lax
from jax.experimental import pallas as pl
from jax.experimental.pallas import tpu as pltpu

F32 = jnp.float32
BF16 = jnp.bfloat16

HEAD_DIM = 64
GRID_W = 64
ROPE_THETA = 10000.0
EPS = 1e-6
FORGET_FLOOR = 1e-6
A_HEADS, A_KV_HEADS = 6, 2
B_HEADS, B_QK_DIM = 6, 32
C_HEADS, C_DIM = 4, 64
A_WIDTH = A_HEADS * HEAD_DIM
A_KV_WIDTH = A_KV_HEADS * HEAD_DIM
B_WIDTH = B_HEADS * HEAD_DIM
C_WIDTH = C_HEADS * C_DIM
LANES = 128
HGRN_CHUNK = 64
HGRN_LEAF = 8
HGRN_BATCH = 16
ONES_ROWS = 16
VT_ROWS = HEAD_DIM + ONES_ROWS
KV_CHUNK = 256
Q_TILE = 256
QK_AHEAD = 16
LOG2_E = math.log2(math.e)
VMEM_LIMIT = 56 * 1024 * 1024

OFF_AQ, OFF_AK, OFF_AV = 0, 384, 512
OFF_BQ, OFF_BK, OFF_BV = 640, 1024, 1408
OFF_C = 1792
C_DTYPES = (F32, F32, F32, BF16, F32)
IN_TOTAL = 3072


def _dot(a, b):
    return jnp.dot(a, b, preferred_element_type=F32)


def _dot_nt(a, b):
    return lax.dot_general(a, b, (((1,), (1,)), ((), ())), preferred_element_type=F32)


def _dot_tn(a, b):
    return lax.dot_general(a, b, (((0,), (0,)), ((), ())), preferred_element_type=F32)


def _split_dot(x, w_bf16):
    hi = x.astype(BF16)
    lo = (x - hi.astype(F32)).astype(BF16)
    return _dot(hi, w_bf16) + _dot(lo, w_bf16)


def _group_ones(n, group):
    r = lax.broadcasted_iota(jnp.int32, (n, n), 0) // group
    c = lax.broadcasted_iota(jnp.int32, (n, n), 1) // group
    return jnp.where(r == c, 1.0 / group, 0.0).astype(BF16)


def _rope(xb, cos, sin_signed):
    lane = lax.broadcasted_iota(jnp.int32, xb.shape, 1)
    low = (lane % 32) < 16
    partner = jnp.where(low, pltpu.roll(xb, LANES - 16, 1), pltpu.roll(xb, 16, 1))
    return xb * cos + partner * sin_signed


def _rms_rows(x):
    return x * lax.rsqrt(jnp.mean(x * x, axis=-1, keepdims=True) + EPS)


def _mod_kernel(c_ref, w_ref, b_ref, o_ref):
    c = c_ref[...]
    cond = (c * jax.nn.sigmoid(c)).astype(BF16)
    o_ref[0] = _dot(cond, w_ref[0].astype(BF16)) + b_ref[0]


def _modulation(c, w_mod, b_mod):
    depth, d, n = w_mod.shape
    bsz = c.shape[0]
    tn = 1024
    return pl.pallas_call(
        _mod_kernel,
        grid=(depth, n // tn),
        in_specs=[
            pl.BlockSpec((bsz, d), lambda l, j: (0, 0)),
            pl.BlockSpec((1, d, tn), lambda l, j: (l, 0, j)),
            pl.BlockSpec((1, 1, tn), lambda l, j: (l, 0, j)),
        ],
        out_specs=pl.BlockSpec((1, bsz, tn), lambda l, j: (l, 0, j)),
        out_shape=jax.ShapeDtypeStruct((depth, bsz, n), F32),
        compiler_params=pltpu.CompilerParams(
            dimension_semantics=("arbitrary", "arbitrary"), vmem_limit_bytes=VMEM_LIMIT),
        name="adaln_mod",
    )(c, w_mod, b_mod.reshape(depth, 1, n))


def _inproj_kernel(x_ref, mod_ref, w_ref, cosa_ref, sina_ref, cosb_ref, sinb_ref, gain_ref,
                   qat_ref, ka_ref, vat_ref, qbt_ref, kb_ref, vbt_ref,
                   cq_ref, cff_ref, cfb_ref, ci_ref, cg_ref, h_scr, z_scr):
    x = x_ref[0]
    h = _rms_rows(x) * (1.0 + mod_ref[0, 1:2, :]) + mod_ref[0, 0:1, :]
    h_scr[...] = h.astype(BF16)

    def project(j):
        z_scr[:, j * 1024:(j + 1) * 1024] = _dot(h_scr[...], w_ref[0, :, j * 1024:(j + 1) * 1024])

    project(0)
    project(1)

    ones = _group_ones(LANES, HEAD_DIM)
    cosa, sina = cosa_ref[...], sina_ref[...]
    cosb, sinb = cosb_ref[...], sinb_ref[...]

    def a_norm_rope(zb, gain):
        ms = _split_dot(zb * zb, ones)
        return _rope(zb * lax.rsqrt(ms + EPS) * gain, cosa, sina)

    def store_vt(vt_ref, first_head, blk_t):
        ones_rows = jnp.ones((ONES_ROWS, blk_t.shape[1]), BF16)
        for j in range(LANES // HEAD_DIM):
            r0 = (first_head + j) * VT_ROWS
            vt_ref[0, r0:r0 + HEAD_DIM, :] = blk_t[j * HEAD_DIM:(j + 1) * HEAD_DIM].astype(BF16)
            vt_ref[0, r0 + HEAD_DIM:r0 + VT_ROWS, :] = ones_rows

    for cblk in range(A_WIDTH // LANES):
        zb = z_scr[:, OFF_AQ + cblk * LANES:OFF_AQ + (cblk + 1) * LANES]
        y = a_norm_rope(zb, gain_ref[0:1, :]) * (HEAD_DIM ** -0.5 * LOG2_E)
        qat_ref[0, cblk * LANES:(cblk + 1) * LANES, :] = y.T.astype(BF16)
    ka_ref[0] = a_norm_rope(z_scr[:, OFF_AK:OFF_AK + LANES], gain_ref[1:2, :]).astype(BF16)
    store_vt(vat_ref, 0, z_scr[:, OFF_AV:OFF_AV + LANES].T)
    project(2)

    for cblk in range(B_WIDTH // LANES):
        lo = cblk * LANES
        zq = z_scr[:, OFF_BQ + lo:OFF_BQ + lo + LANES]
        qbt_ref[0, lo:lo + LANES, :] = (_rope(zq, cosb, sinb) * (B_QK_DIM ** -0.5 * LOG2_E)).T.astype(BF16)
    for cblk in range(B_WIDTH // LANES):
        lo = cblk * LANES
        zk = z_scr[:, OFF_BK + lo:OFF_BK + lo + LANES]
        kb_ref[0, :, lo:lo + LANES] = _rope(zk, cosb, sinb).astype(BF16)
        store_vt(vbt_ref, cblk * (LANES // HEAD_DIM), z_scr[:, OFF_BV + lo:OFF_BV + lo + LANES].T)
    for j, ref in enumerate((cq_ref, cff_ref, cfb_ref, ci_ref, cg_ref)):
        ref[0] = z_scr[:, OFF_C + j * C_WIDTH:OFF_C + (j + 1) * C_WIDTH].astype(ref.dtype)


def _in_projection(x, mod_l, w_in_bf16, layer, tables, qk_gain, tm):
    bsz, s, d = x.shape
    cosa, sina, cosb, sinb = tables
    row_spec = lambda width: pl.BlockSpec((1, tm, width), lambda b, i: (b, i, 0))
    tab_spec = pl.BlockSpec((tm, LANES), lambda b, i: (i, 0))
    col_spec = lambda height: pl.BlockSpec((1, height, tm), lambda b, i: (b, 0, i))
    return pl.pallas_call(
        _inproj_kernel,
        grid=(bsz, s // tm),
        in_specs=[
            row_spec(d),
            pl.BlockSpec((1, 6, d), lambda b, i: (b, 0, 0)),
            pl.BlockSpec((1, d, IN_TOTAL), lambda b, i: (layer, 0, 0)),
            tab_spec, tab_spec, tab_spec, tab_spec,
            pl.BlockSpec((2, LANES), lambda b, i: (0, 0)),
        ],
        out_specs=[
            col_spec(A_WIDTH), row_spec(A_KV_WIDTH), col_spec(A_KV_HEADS * VT_ROWS),
            col_spec(B_WIDTH), row_spec(B_WIDTH), col_spec(B_HEADS * VT_ROWS),
        ] + [row_spec(C_WIDTH)] * len(C_DTYPES),
        out_shape=[
            jax.ShapeDtypeStruct((bsz, A_WIDTH, s), BF16),
            jax.ShapeDtypeStruct((bsz, s, A_KV_WIDTH), BF16),
            jax.ShapeDtypeStruct((bsz, A_KV_HEADS * VT_ROWS, s), BF16),
            jax.ShapeDtypeStruct((bsz, B_WIDTH, s), BF16),
            jax.ShapeDtypeStruct((bsz, s, B_WIDTH), BF16),
            jax.ShapeDtypeStruct((bsz, B_HEADS * VT_ROWS, s), BF16),
        ] + [jax.ShapeDtypeStruct((bsz, s, C_WIDTH), dt) for dt in C_DTYPES],
        scratch_shapes=[pltpu.VMEM((tm, d), BF16), pltpu.VMEM((tm, IN_TOTAL), F32)],
        compiler_params=pltpu.CompilerParams(
            dimension_semantics=("arbitrary", "arbitrary"), vmem_limit_bytes=VMEM_LIMIT),
        name="in_projection",
    )(x, mod_l, w_in_bf16, cosa, sina, cosb, sinb, qk_gain)


def _place_rows(block, start, total):
    parts = []
    if start:
        parts.append(jnp.zeros((start, block.shape[1]), block.dtype))
    parts.append(block)
    rest = total - start - block.shape[0]
    if rest:
        parts.append(jnp.zeros((rest, block.shape[1]), block.dtype))
    return jnp.concatenate(parts, axis=0) if len(parts) > 1 else block


def _attention_items(items, k_ref, vt_ref, qz_scr, m_scr, acc_scr):
    s = k_ref.shape[1]
    tq = acc_scr.shape[2]
    m_scr[...] = jnp.full(m_scr.shape, -1e30, F32)
    acc_scr[...] = jnp.zeros_like(acc_scr)

    units = [(i, c) for c in range(s // KV_CHUNK) for i in range(len(items))]

    def scores(u):
        i, c = units[u]
        k_off = items[i][0]
        return _dot(k_ref[0, c * KV_CHUNK:(c + 1) * KV_CHUNK, k_off:k_off + LANES], qz_scr[i])

    pending = [scores(u) for u in range(min(QK_AHEAD, len(units)))]
    for u, (i, c) in enumerate(units):
        v_off = items[i][1]
        st = pending.pop(0)
        m_old = m_scr[i]
        part = jnp.max(st.reshape(KV_CHUNK // 8, 8, tq), axis=0)
        m_new = jnp.maximum(m_old, jnp.max(part, axis=0, keepdims=True))
        m_scr[i] = m_new
        p = jnp.exp2(st - m_new[0:1]).astype(BF16)
        d = _dot(vt_ref[0, v_off:v_off + VT_ROWS, c * KV_CHUNK:(c + 1) * KV_CHUNK], p)
        if u + QK_AHEAD < len(units):
            pending.append(scores(u + QK_AHEAD))
        acc_scr[i] = acc_scr[i] * jnp.exp2(m_old[0:1] - m_new[0:1]) + d
    results = []
    for i in range(len(items)):
        acc = acc_scr[i]
        results.append(acc[0:HEAD_DIM] * (1.0 / acc[HEAD_DIM:HEAD_DIM + 1]))
    return results


def _attention_scratch(heads, tq):
    n_items = heads * (tq // Q_TILE)
    return [pltpu.VMEM((n_items, LANES, Q_TILE), BF16), pltpu.VMEM((n_items, 8, Q_TILE), F32),
            pltpu.VMEM((n_items, VT_ROWS, Q_TILE), F32)]


def _attn_a_kernel(qt_ref, k_ref, vt_ref, o_ref, qz_scr, m_scr, acc_scr):
    group = A_HEADS // A_KV_HEADS
    n_tiles = qt_ref.shape[2] // Q_TILE
    items = []
    for t in range(n_tiles):
        cols = slice(t * Q_TILE, (t + 1) * Q_TILE)
        for h in range(A_HEADS):
            g = h // group
            qz_scr[len(items)] = _place_rows(qt_ref[0, h * HEAD_DIM:(h + 1) * HEAD_DIM, cols], g * HEAD_DIM, LANES)
            items.append((0, g * VT_ROWS))
    outs = _attention_items(items, k_ref, vt_ref, qz_scr, m_scr, acc_scr)
    for t in range(n_tiles):
        tile = outs[t * A_HEADS:(t + 1) * A_HEADS]
        o_ref[0, t * Q_TILE:(t + 1) * Q_TILE, :] = jnp.concatenate([o.T for o in tile], axis=-1).astype(BF16)


def _attention_a(qat, ka, vat, tq):
    bsz, s, _ = ka.shape
    return pl.pallas_call(
        _attn_a_kernel,
        grid=(bsz, s // tq),
        in_specs=[
            pl.BlockSpec((1, A_WIDTH, tq), lambda b, i: (b, 0, i)),
            pl.BlockSpec((1, s, A_KV_WIDTH), lambda b, i: (b, 0, 0)),
            pl.BlockSpec((1, A_KV_HEADS * VT_ROWS, s), lambda b, i: (b, 0, 0)),
        ],
        out_specs=pl.BlockSpec((1, tq, A_WIDTH), lambda b, i: (b, i, 0)),
        out_shape=jax.ShapeDtypeStruct((bsz, s, A_WIDTH), BF16),
        scratch_shapes=_attention_scratch(A_HEADS, tq),
        compiler_params=pltpu.CompilerParams(
            dimension_semantics=("arbitrary", "arbitrary"), vmem_limit_bytes=VMEM_LIMIT),
        name="attn_axial_gqa",
    )(qat, ka, vat)


def _attn_b_kernel(lam_ref, gain_ref, qt_ref, k_ref, vt_ref, o_ref, qz_scr, m_scr, acc_scr, *, lam_init):
    lp = lam_ref[...]
    lam = (jnp.exp(jnp.sum(lp[0:1] * lp[1:2], axis=-1, keepdims=True))
           - jnp.exp(jnp.sum(lp[2:3] * lp[3:4], axis=-1, keepdims=True)) + lam_init)
    n_tiles = qt_ref.shape[2] // Q_TILE
    items = []
    for t in range(n_tiles):
        cols = slice(t * Q_TILE, (t + 1) * Q_TILE)
        for h in range(B_HEADS):
            pair = (h // 2) * LANES
            for comp in range(2):
                r0 = h * HEAD_DIM + comp * B_QK_DIM
                qz_scr[len(items)] = _place_rows(qt_ref[0, r0:r0 + B_QK_DIM, cols], r0 - pair, LANES)
                items.append((pair, h * VT_ROWS))
    comps = _attention_items(items, k_ref, vt_ref, qz_scr, m_scr, acc_scr)
    for t in range(n_tiles):
        outs = []
        for h in range(B_HEADS):
            i = (t * B_HEADS + h) * 2
            o = comps[i] - lam * comps[i + 1]
            o = o * lax.rsqrt(jnp.mean(o * o, axis=0, keepdims=True) + EPS)
            outs.append(o.T)
        y = jnp.concatenate(outs, axis=-1) * gain_ref[...] * (1.0 - lam_init)
        o_ref[0, t * Q_TILE:(t + 1) * Q_TILE, :] = y.astype(BF16)


def _attention_b(qbt, kb, vbt, lam_params, subln_gain, lam_init, tq):
    bsz, s, _ = kb.shape
    return pl.pallas_call(
        functools.partial(_attn_b_kernel, lam_init=lam_init),
        grid=(bsz, s // tq),
        in_specs=[
            pl.BlockSpec((4, B_QK_DIM), lambda b, i: (0, 0)),
            pl.BlockSpec((1, B_WIDTH), lambda b, i: (0, 0)),
            pl.BlockSpec((1, B_WIDTH, tq), lambda b, i: (b, 0, i)),
            pl.BlockSpec((1, s, B_WIDTH), lambda b, i: (b, 0, 0)),
            pl.BlockSpec((1, B_HEADS * VT_ROWS, s), lambda b, i: (b, 0, 0)),
        ],
        out_specs=pl.BlockSpec((1, tq, B_WIDTH), lambda b, i: (b, i, 0)),
        out_shape=jax.ShapeDtypeStruct((bsz, s, B_WIDTH), BF16),
        scratch_shapes=_attention_scratch(2 * B_HEADS, tq),
        compiler_params=pltpu.CompilerParams(
            dimension_semantics=("arbitrary", "arbitrary"), vmem_limit_bytes=VMEM_LIMIT),
        name="attn_differential",
    )(lam_params, subln_gain, qbt, kb, vbt)


def _hgrn_levels():
    sizes = []
    bs = HGRN_CHUNK
    while bs > HGRN_LEAF:
        sizes.append(bs)
        bs //= 2
    return sizes


def _hgrn_constants():
    c, w = HGRN_CHUNK, C_WIDTH
    t = np.arange(c)
    tri = np.stack([(t[:, None] >= t[None, :]), (t[:, None] <= t[None, :])]).astype(np.float32)
    s_of_lane = np.arange(w) % C_DIM
    sizes = _hgrn_levels()
    level = np.full((2, c, w), float(len(sizes) + 1), np.float32)
    for d in range(2):
        for li, bs in enumerate(sizes):
            upper = (t % bs) >= bs // 2
            qrow = upper if d == 0 else ~upper
            same = (t[:, None] // bs) == (s_of_lane[None, :] // bs)
            cross = qrow[:, None] & ~qrow[s_of_lane][None, :]
            level[d][same & cross] = li
        same_leaf = (t[:, None] // HGRN_LEAF) == (s_of_lane[None, :] // HGRN_LEAF)
        order = (s_of_lane[None, :] <= t[:, None]) if d == 0 else (s_of_lane[None, :] >= t[:, None])
        level[d][same_leaf & order] = len(sizes)
    half = (np.arange(LANES)[None, :] < C_DIM).astype(np.float32) * np.ones((c, 1), np.float32)
    return tri, level, half


def _block_diag(x, low_half):
    zero = jnp.zeros((x.shape[0], LANES), x.dtype)
    blocks = []
    for h in range(C_HEADS):
        tile = x[:, (h // 2) * LANES:(h // 2 + 1) * LANES]
        keep = tile * (low_half if h % 2 == 0 else 1 - low_half)
        blocks.append(jnp.concatenate([keep, zero] if h // 2 == 0 else [zero, keep], axis=1))
    return jnp.concatenate(blocks, axis=0)


def _block_diag_t(x, low_half):
    zero = jnp.zeros((LANES, LANES), x.dtype)
    rows = []
    for t in range(C_WIDTH // LANES):
        tile = x[:, t * LANES:(t + 1) * LANES]
        pair = jnp.concatenate([tile * low_half, tile * (1 - low_half)], axis=0).T
        rows.append(jnp.concatenate([pair, zero] if t == 0 else [zero, pair], axis=1))
    return jnp.concatenate(rows, axis=0)


def _ref_rows(b, bs, offset):
    parts = []
    for lo in range(0, b.shape[0], bs):
        parts.append(jnp.broadcast_to(b[lo + offset:lo + offset + 1, :], (bs, b.shape[1])))
    return jnp.concatenate(parts, axis=0)


def _level_operands(q, k, b, bs, d):
    half = bs // 2
    ref_off = half - 1 if d == 0 else half
    zeros = jnp.zeros((8, b.shape[1]), F32)
    qparts, kparts = [], []
    for r0 in range(0, b.shape[0], 8):
        lo = (r0 // bs) * bs
        ref = b[lo + ref_off:lo + ref_off + 1, :]
        upper = (r0 % bs) >= half
        rows = slice(r0, r0 + 8)
        if upper == (d == 0):
            qparts.append(q[rows] * jnp.exp2(b[rows] - ref))
            kparts.append(zeros)
        else:
            qparts.append(zeros)
            kparts.append(k[rows] * jnp.exp2(ref - b[rows]))
    return jnp.concatenate(qparts, axis=0).astype(BF16), jnp.concatenate(kparts, axis=0).astype(BF16)


def _hgrn_kernel(lb_ref, tri_ref, level_ref, half_ref, zq_f, zf_f, zv_f, zq_b, zf_b, zv_b,
                 of_ref, ob_ref, st_scr, *, layer, depth, nb):
    c, w = HGRN_CHUNK, C_WIDTH
    sizes = _hgrn_levels()
    n_lvl = len(sizes)

    @pl.when(pl.program_id(1) == 0)
    def _():
        st_scr[...] = jnp.zeros_like(st_scr)

    rows = [lb_ref[i:i + 1, :] for i in range(depth)]
    mx = functools.reduce(jnp.maximum, rows)
    es = [jnp.exp(r - mx) for r in rows]
    tot = functools.reduce(lambda a, b_: a + b_, es)
    ps = [e / tot for e in es]
    cum = functools.reduce(lambda a, b_: a + b_, ps[:layer + 1])
    one_minus_lb = 1.0 - jnp.clip(cum - ps[0], 0.0, 1.0)

    low_half = half_ref[...].astype(BF16)
    low_bool = half_ref[...] > 0.5
    owner = [[level_ref[d] == float(li) for li in range(n_lvl + 1)] for d in range(2)]
    chains = [(j, d) for j in range(nb) for d in range(2)]
    zrefs = {0: (zq_f, zf_f, zv_f), 1: (zq_b, zf_b, zv_b)}

    q, k, v, b, total = {}, {}, {}, {}, {}
    g_split = {}
    for ch in chains:
        j, d = ch
        zq, zf = zrefs[d][0][j], zrefs[d][1][j]
        v[ch] = zrefs[d][2][j]
        q[ch] = (zq * (C_DIM ** -0.5)) / (1.0 + jnp.exp2(zq * -LOG2_E))
        k[ch] = one_minus_lb * (1.0 - 1.0 / (1.0 + jnp.exp2(zf * -LOG2_E)))
        g = jnp.log2(jnp.maximum(1.0 - k[ch], FORGET_FLOOR))
        g_hi = g.astype(BF16)
        g_r = g - g_hi.astype(F32)
        g_mid = g_r.astype(BF16)
        g_lo = (g_r - g_mid.astype(F32)).astype(BF16)
        g_split[ch] = jnp.concatenate([g_hi, g_mid, g_lo], axis=1)
    for ch in chains:
        d = ch[1]
        bb = _dot(tri_ref[d], g_split[ch])
        b[ch] = bb[:, 0:w] + bb[:, w:2 * w] + bb[:, 2 * w:3 * w]
        total[ch] = b[ch][c - 1:c, :] if d == 0 else b[ch][0:1, :]

    parts = {ch: [] for ch in chains}
    for li in range(n_lvl + 1):
        for ch in chains:
            d = ch[1]
            if li < n_lvl:
                qp, kp = _level_operands(q[ch], k[ch], b[ch], sizes[li], d)
            else:
                ref = _ref_rows(b[ch], HGRN_LEAF, HGRN_LEAF // 2 - 1 if d == 0 else HGRN_LEAF // 2)
                arg = b[ch] - ref
                qp = (q[ch] * jnp.exp2(arg)).astype(BF16)
                kp = (k[ch] * jnp.exp2(-arg)).astype(BF16)
            parts[ch].append(_dot(qp, _block_diag_t(kp, low_half)))
    a = {}
    for ch in chains:
        acc = parts[ch][0]
        for li in range(1, n_lvl + 1):
            acc = jnp.where(owner[ch[1]][li], parts[ch][li], acc)
        a[ch] = acc.astype(BF16)

    o_intra, o_inter, upd = {}, {}, {}
    for ch in chains:
        j, d = ch
        o_intra[ch] = _dot(a[ch], _block_diag(v[ch], low_half))
        qe = (q[ch] * jnp.exp2(b[ch])).astype(BF16)
        o_inter[ch] = _dot(qe, _block_diag_t(st_scr[d, j].astype(BF16), low_half))
        kb = (k[ch] * jnp.exp2(total[ch] - b[ch])).astype(BF16)
        tiles = []
        for t in range(w // LANES):
            lanes = slice(t * LANES, (t + 1) * LANES)
            full = _dot_tn(v[ch][:, lanes], kb[:, lanes])
            tiles.append(jnp.where(low_bool, full[0:C_DIM], full[C_DIM:2 * C_DIM]))
        upd[ch] = jnp.concatenate(tiles, axis=1)
    for ch in chains:
        j, d = ch
        st_scr[d, j] = st_scr[d, j] * jnp.exp2(total[ch]) + upd[ch]
        (of_ref if d == 0 else ob_ref)[j] = o_intra[ch] + o_inter[ch]


def _hgrn(cq, cff, cfb, ci, lower_bounds, layer, nb):
    bsz, s, _ = cq.shape
    depth = lower_bounds.shape[0]
    n_chunks = s // HGRN_CHUNK
    c, w = HGRN_CHUNK, C_WIDTH
    tri, level, half = _hgrn_constants()
    consts = (jnp.asarray(tri, BF16), jnp.asarray(level), jnp.asarray(half))
    full = lambda a: pl.BlockSpec(a.shape, lambda g, i: (0,) * a.ndim)
    last = n_chunks - 1
    fwd = pl.BlockSpec((nb, c, w), lambda g, i: (g, i, 0))
    bwd = pl.BlockSpec((nb, c, w), lambda g, i: (g, last - i, 0))
    return pl.pallas_call(
        functools.partial(_hgrn_kernel, layer=layer, depth=depth, nb=nb),
        grid=(bsz // nb, n_chunks),
        in_specs=[full(lower_bounds)] + [full(a) for a in consts] + [fwd, fwd, fwd, bwd, bwd, bwd],
        out_specs=[fwd, bwd],
        out_shape=[jax.ShapeDtypeStruct((bsz, s, w), F32)] * 2,
        scratch_shapes=[pltpu.VMEM((2, nb, C_DIM, w), F32)],
        compiler_params=pltpu.CompilerParams(
            dimension_semantics=("arbitrary", "arbitrary"), vmem_limit_bytes=VMEM_LIMIT),
        name="hgrn2_bidir",
    )(lower_bounds, *consts, cq, cff, ci, cq, cfb, ci)


def _out_ffn_kernel(x_ref, mod_ref, oa_ref, ob_ref, cf_ref, cb_ref, zg_ref, cgain_ref,
                    wo_ref, w1_ref, w2_ref, fg_ref, o_ref, *, final, ff_chunk):
    oc = cf_ref[0] + cb_ref[0]
    ms = _split_dot(oc * oc, _group_ones(C_WIDTH, C_DIM))
    zg = zg_ref[0]
    oc = oc * lax.rsqrt(ms + EPS) * cgain_ref[...] * (zg * jax.nn.sigmoid(zg))
    mix = jnp.concatenate([oa_ref[0], ob_ref[0], oc.astype(BF16)], axis=-1)
    x1 = x_ref[0] + mod_ref[0, 2:3, :] * _dot(mix, wo_ref[0])
    h = (_rms_rows(x1) * (1.0 + mod_ref[0, 4:5, :]) + mod_ref[0, 3:4, :]).astype(BF16)
    d_ff = w1_ref.shape[2]
    acc = jnp.zeros_like(x1)
    for j in range(d_ff // ff_chunk):
        u = _dot(h, w1_ref[0, :, j * ff_chunk:(j + 1) * ff_chunk])
        u = jnp.square(jnp.maximum(u, 0.0)).astype(BF16)
        acc = acc + _dot(u, w2_ref[0, j * ff_chunk:(j + 1) * ff_chunk, :])
    x2 = x1 + mod_ref[0, 5:6, :] * acc
    if final:
        x2 = _rms_rows(x2) * fg_ref[...]
    o_ref[0] = x2


def _out_ffn(x, mod_l, oa, ob, oc_f, oc_b, cg, c_gain, w_out, w_ff1, w_ff2, final_gain, layer, final, tm):
    bsz, s, d = x.shape
    d_ff = w_ff1.shape[2]
    row_spec = lambda width: pl.BlockSpec((1, tm, width), lambda b, i: (b, i, 0))
    weight = lambda a: pl.BlockSpec((1,) + a.shape[1:], lambda b, i: (layer, 0, 0),
                                    pipeline_mode=pl.Buffered(1))
    return pl.pallas_call(
        functools.partial(_out_ffn_kernel, final=final, ff_chunk=1024),
        grid=(bsz, s // tm),
        in_specs=[
            row_spec(d),
            pl.BlockSpec((1, 6, d), lambda b, i: (b, 0, 0)),
            row_spec(A_WIDTH), row_spec(B_WIDTH), row_spec(C_WIDTH), row_spec(C_WIDTH), row_spec(C_WIDTH),
            pl.BlockSpec((1, C_WIDTH), lambda b, i: (0, 0)),
            weight(w_out), weight(w_ff1), weight(w_ff2),
            pl.BlockSpec((1, d), lambda b, i: (0, 0)),
        ],
        out_specs=row_spec(d),
        out_shape=jax.ShapeDtypeStruct((bsz, s, d), F32),
        compiler_params=pltpu.CompilerParams(
            dimension_semantics=("arbitrary", "arbitrary"), vmem_limit_bytes=VMEM_LIMIT),
        name="out_proj_ffn",
    )(x, mod_l, oa, ob, oc_f, oc_b, cg, c_gain, w_out, w_ff1, w_ff2, final_gain)


def _rope_tables(s):
    half = HEAD_DIM // 2
    inv = ROPE_THETA ** (-jnp.arange(0, half, 2, dtype=F32) / half)
    t = jnp.arange(s)
    ang_row = (t // GRID_W).astype(F32)[:, None] * inv[None, :]
    ang_col = (t % GRID_W).astype(F32)[:, None] * inv[None, :]
    ang_1d = t.astype(F32)[:, None] * inv[None, :]

    def pair(ang):
        return (jnp.concatenate([jnp.cos(ang), jnp.cos(ang)], axis=-1),
                jnp.concatenate([-jnp.sin(ang), jnp.sin(ang)], axis=-1))

    cr, sr = pair(ang_row)
    cc, sc = pair(ang_col)
    c1, s1 = pair(ang_1d)
    tile = lambda a: jnp.tile(a, (1, LANES // a.shape[1]))
    return (tile(jnp.concatenate([cr, cc], axis=-1)), tile(jnp.concatenate([sr, sc], axis=-1)),
            tile(c1), tile(s1))


def kernel(x, c, w_mod, b_mod, w_in, a_qk_norm, diff_lambda, diff_subln, hgrn_lower_bounds, hgrn_norm,
           w_out, w_ff1, w_ff2, final_norm):
    bsz, s, d = x.shape
    depth = w_in.shape[0]
    tm = min(512, s)
    tq = min(2 * Q_TILE, s)
    tables = _rope_tables(s)
    mod = _modulation(c, w_mod, b_mod).reshape(depth, bsz, 6, d)
    w_in_b, w_out_b, w_ff1_b, w_ff2_b = (w.astype(BF16) for w in (w_in, w_out, w_ff1, w_ff2))
    final_gain = final_norm.reshape(1, d)
    for l in range(depth):
        qk_gain = jnp.tile(a_qk_norm[l], (1, LANES // HEAD_DIM))
        qat, ka, vat, qbt, kb, vbt, cq, cff, cfb, ci, cg = _in_projection(
            x, mod[l], w_in_b, l, tables, qk_gain, tm)
        oa = _attention_a(qat, ka, vat, tq)
        lam_init = 0.8 - 0.6 * math.exp(-0.3 * l)
        subln = jnp.tile(diff_subln[l].reshape(1, HEAD_DIM), (1, B_HEADS))
        ob = _attention_b(qbt, kb, vbt, diff_lambda[l], subln, lam_init, tq)
        hg_gain = jnp.tile(hgrn_norm[l].reshape(1, C_DIM), (1, C_HEADS))
        oc_f, oc_b = _hgrn(cq, cff, cfb, ci, hgrn_lower_bounds, l, min(HGRN_BATCH, bsz))
        x = _out_ffn(x, mod[l], oa, ob, oc_f, oc_b, cg, hg_gain, w_out_b, w_ff1_b, w_ff2_b, final_gain,
                     l, l == depth - 1, tm)
    return x
```

```python
import functools
import math

import numpy as np
import jax
import jax.numpy as jnp
from jax import lax
from jax.experimental import pallas as pl
from jax.experimental.pallas import tpu as pltpu

F32 = jnp.float32
BF16 = jnp.bfloat16

HEAD_DIM = 64
GRID_W = 64
ROPE_THETA = 10000.0
EPS = 1e-6
FORGET_FLOOR = 1e-6
A_HEADS, A_KV_HEADS = 6, 2
B_HEADS, B_QK_DIM = 6, 32
C_HEADS, C_DIM = 4, 64
A_WIDTH = A_HEADS * HEAD_DIM
A_KV_WIDTH = A_KV_HEADS * HEAD_DIM
B_WIDTH = B_HEADS * HEAD_DIM
C_WIDTH = C_HEADS * C_DIM
LANES = 128
HGRN_CHUNK = 64
HGRN_LEAF = 8
HGRN_BATCH = 16
ONES_ROWS = 16
VT_ROWS = HEAD_DIM + ONES_ROWS
KV_CHUNK = 256
Q_TILE = 256
QK_AHEAD = 16
LOG2_E = math.log2(math.e)
VMEM_LIMIT = 56 * 1024 * 1024

OFF_AQ, OFF_AK, OFF_AV = 0, 384, 512
OFF_BQ, OFF_BK, OFF_BV = 640, 1024, 1408
OFF_C = 1792
C_DTYPES = (F32, F32, F32, BF16, F32)
IN_TOTAL = 3072


def _dot(a, b):
    return jnp.dot(a, b, preferred_element_type=F32)


def _dot_nt(a, b):
    return lax.dot_general(a, b, (((1,), (1,)), ((), ())), preferred_element_type=F32)


def _dot_tn(a, b):
    return lax.dot_general(a, b, (((0,), (0,)), ((), ())), preferred_element_type=F32)


def _split_dot(x, w_bf16):
    hi = x.astype(BF16)
    lo = (x - hi.astype(F32)).astype(BF16)
    return _dot(hi, w_bf16) + _dot(lo, w_bf16)


def _group_ones(n, group):
    r = lax.broadcasted_iota(jnp.int32, (n, n), 0) // group
    c = lax.broadcasted_iota(jnp.int32, (n, n), 1) // group
    return jnp.where(r == c, 1.0 / group, 0.0).astype(BF16)


def _rope(xb, cos, sin_signed):
    lane = lax.broadcasted_iota(jnp.int32, xb.shape, 1)
    low = (lane % 32) < 16
    partner = jnp.where(low, pltpu.roll(xb, LANES - 16, 1), pltpu.roll(xb, 16, 1))
    return xb * cos + partner * sin_signed


def _rms_rows(x):
    return x * lax.rsqrt(jnp.mean(x * x, axis=-1, keepdims=True) + EPS)


def _mod_kernel(c_ref, w_ref, b_ref, o_ref):
    c = c_ref[...]
    cond = (c * jax.nn.sigmoid(c)).astype(BF16)
    o_ref[0] = _dot(cond, w_ref[0].astype(BF16)) + b_ref[0]


def _modulation(c, w_mod, b_mod):
    depth, d, n = w_mod.shape
    bsz = c.shape[0]
    tn = 1024
    return pl.pallas_call(
        _mod_kernel,
        grid=(depth, n // tn),
        in_specs=[
            pl.BlockSpec((bsz, d), lambda l, j: (0, 0)),
            pl.BlockSpec((1, d, tn), lambda l, j: (l, 0, j)),
            pl.BlockSpec((1, 1, tn), lambda l, j: (l, 0, j)),
        ],
        out_specs=pl.BlockSpec((1, bsz, tn), lambda l, j: (l, 0, j)),
        out_shape=jax.ShapeDtypeStruct((depth, bsz, n), F32),
        compiler_params=pltpu.CompilerParams(
            dimension_semantics=("arbitrary", "arbitrary"), vmem_limit_bytes=VMEM_LIMIT),
        name="adaln_mod",
    )(c, w_mod, b_mod.reshape(depth, 1, n))


def _inproj_kernel(x_ref, mod_ref, w_ref, cosa_ref, sina_ref, cosb_ref, sinb_ref, gain_ref,
                   qat_ref, ka_ref, vat_ref, qbt_ref, kb_ref, vbt_ref,
                   cq_ref, cff_ref, cfb_ref, ci_ref, cg_ref, h_scr, z_scr):
    x = x_ref[0]
    h = _rms_rows(x) * (1.0 + mod_ref[0, 1:2, :]) + mod_ref[0, 0:1, :]
    h_scr[...] = h.astype(BF16)

    def project(j):
        z_scr[:, j * 1024:(j + 1) * 1024] = _dot(h_scr[...], w_ref[0, :, j * 1024:(j + 1) * 1024])

    project(0)
    project(1)

    ones = _group_ones(LANES, HEAD_DIM)
    cosa, sina = cosa_ref[...], sina_ref[...]
    cosb, sinb = cosb_ref[...], sinb_ref[...]

    def a_norm_rope(zb, gain):
        ms = _split_dot(zb * zb, ones)
        return _rope(zb * lax.rsqrt(ms + EPS) * gain, cosa, sina)

    def store_vt(vt_ref, first_head, blk_t):
        ones_rows = jnp.ones((ONES_ROWS, blk_t.shape[1]), BF16)
        for j in range(LANES // HEAD_DIM):
            r0 = (first_head + j) * VT_ROWS
            vt_ref[0, r0:r0 + HEAD_DIM, :] = blk_t[j * HEAD_DIM:(j + 1) * HEAD_DIM].astype(BF16)
            vt_ref[0, r0 + HEAD_DIM:r0 + VT_ROWS, :] = ones_rows

    for cblk in range(A_WIDTH // LANES):
        zb = z_scr[:, OFF_AQ + cblk * LANES:OFF_AQ + (cblk + 1) * LANES]
        y = a_norm_rope(zb, gain_ref[0:1, :]) * (HEAD_DIM ** -0.5 * LOG2_E)
        qat_ref[0, cblk * LANES:(cblk + 1) * LANES, :] = y.T.astype(BF16)
    ka_ref[0] = a_norm_rope(z_scr[:, OFF_AK:OFF_AK + LANES], gain_ref[1:2, :]).astype(BF16)
    store_vt(vat_ref, 0, z_scr[:, OFF_AV:OFF_AV + LANES].T)
    project(2)

    for cblk in range(B_WIDTH // LANES):
        lo = cblk * LANES
        zq = z_scr[:, OFF_BQ + lo:OFF_BQ + lo + LANES]
        qbt_ref[0, lo:lo + LANES, :] = (_rope(zq, cosb, sinb) * (B_QK_DIM ** -0.5 * LOG2_E)).T.astype(BF16)
    for cblk in range(B_WIDTH // LANES):
        lo = cblk * LANES
        zk = z_scr[:, OFF_BK + lo:OFF_BK + lo + LANES]
        kb_ref[0, :, lo:lo + LANES] = _rope(zk, cosb, sinb).astype(BF16)
        store_vt(vbt_ref, cblk * (LANES // HEAD_DIM), z_scr[:, OFF_BV + lo:OFF_BV + lo + LANES].T)
    for j, ref in enumerate((cq_ref, cff_ref, cfb_ref, ci_ref, cg_ref)):
        ref[0] = z_scr[:, OFF_C + j * C_WIDTH:OFF_C + (j + 1) * C_WIDTH].astype(ref.dtype)


def _in_projection(x, mod_l, w_in_bf16, layer, tables, qk_gain, tm):
    bsz, s, d = x.shape
    cosa, sina, cosb, sinb = tables
    row_spec = lambda width: pl.BlockSpec((1, tm, width), lambda b, i: (b, i, 0))
    tab_spec = pl.BlockSpec((tm, LANES), lambda b, i: (i, 0))
    col_spec = lambda height: pl.BlockSpec((1, height, tm), lambda b, i: (b, 0, i))
    return pl.pallas_call(
        _inproj_kernel,
        grid=(bsz, s // tm),
        in_specs=[
            row_spec(d),
            pl.BlockSpec((1, 6, d), lambda b, i: (b, 0, 0)),
            pl.BlockSpec((1, d, IN_TOTAL), lambda b, i: (layer, 0, 0)),
            tab_spec, tab_spec, tab_spec, tab_spec,
            pl.BlockSpec((2, LANES), lambda b, i: (0, 0)),
        ],
        out_specs=[
            col_spec(A_WIDTH), row_spec(A_KV_WIDTH), col_spec(A_KV_HEADS * VT_ROWS),
            col_spec(B_WIDTH), row_spec(B_WIDTH), col_spec(B_HEADS * VT_ROWS),
        ] + [row_spec(C_WIDTH)] * len(C_DTYPES),
        out_shape=[
            jax.ShapeDtypeStruct((bsz, A_WIDTH, s), BF16),
            jax.ShapeDtypeStruct((bsz, s, A_KV_WIDTH), BF16),
            jax.ShapeDtypeStruct((bsz, A_KV_HEADS * VT_ROWS, s), BF16),
            jax.ShapeDtypeStruct((bsz, B_WIDTH, s), BF16),
            jax.ShapeDtypeStruct((bsz, s, B_WIDTH), BF16),
            jax.ShapeDtypeStruct((bsz, B_HEADS * VT_ROWS, s), BF16),
        ] + [jax.ShapeDtypeStruct((bsz, s, C_WIDTH), dt) for dt in C_DTYPES],
        scratch_shapes=[pltpu.VMEM((tm, d), BF16), pltpu.VMEM((tm, IN_TOTAL), F32)],
        compiler_params=pltpu.CompilerParams(
            dimension_semantics=("arbitrary", "arbitrary"), vmem_limit_bytes=VMEM_LIMIT),
        name="in_projection",
    )(x, mod_l, w_in_bf16, cosa, sina, cosb, sinb, qk_gain)


def _place_rows(block, start, total):
    parts = []
    if start:
        parts.append(jnp.zeros((start, block.shape[1]), block.dtype))
    parts.append(block)
    rest = total - start - block.shape[0]
    if rest:
        parts.append(jnp.zeros((rest, block.shape[1]), block.dtype))
    return jnp.concatenate(parts, axis=0) if len(parts) > 1 else block


def _attention_items(items, k_ref, vt_ref, qz_scr, m_scr, acc_scr, exp_dtype=F32):
    s = k_ref.shape[1]
    tq = acc_scr.shape[2]
    m_scr[...] = jnp.full(m_scr.shape, -1e30, F32)
    acc_scr[...] = jnp.zeros_like(acc_scr)

    units = [(i, c) for c in range(s // KV_CHUNK) for i in range(len(items))]

    def scores(u):
        i, c = units[u]
        k_off = items[i][0]
        return _dot(k_ref[0, c * KV_CHUNK:(c + 1) * KV_CHUNK, k_off:k_off + LANES], qz_scr[i])

    pending = [scores(u) for u in range(min(QK_AHEAD, len(units)))]
    for u, (i, c) in enumerate(units):
        v_off = items[i][1]
        st = pending.pop(0)
        m_old = m_scr[i]
        part = jnp.max(st.reshape(KV_CHUNK // 8, 8, tq), axis=0)
        m_new = jnp.maximum(m_old, jnp.max(part, axis=0, keepdims=True))
        m_scr[i] = m_new
        p = jnp.exp2((st - m_new[0:1]).astype(exp_dtype)).astype(BF16)
        d = _dot(vt_ref[0, v_off:v_off + VT_ROWS, c * KV_CHUNK:(c + 1) * KV_CHUNK], p)
        if u + QK_AHEAD < len(units):
            pending.append(scores(u + QK_AHEAD))
        acc_scr[i] = acc_scr[i] * jnp.exp2(m_old[0:1] - m_new[0:1]) + d
    results = []
    for i in range(len(items)):
        acc = acc_scr[i]
        results.append(acc[0:HEAD_DIM] * (1.0 / acc[HEAD_DIM:HEAD_DIM + 1]))
    return results


def _attention_scratch(heads, tq):
    n_items = heads * (tq // Q_TILE)
    return [pltpu.VMEM((n_items, LANES, Q_TILE), BF16), pltpu.VMEM((n_items, 8, Q_TILE), F32),
            pltpu.VMEM((n_items, VT_ROWS, Q_TILE), F32)]


def _attn_a_kernel(qt_ref, k_ref, vt_ref, o_ref, qz_scr, m_scr, acc_scr):
    group = A_HEADS // A_KV_HEADS
    n_tiles = qt_ref.shape[2] // Q_TILE
    items = []
    for t in range(n_tiles):
        cols = slice(t * Q_TILE, (t + 1) * Q_TILE)
        for h in range(A_HEADS):
            g = h // group
            qz_scr[len(items)] = _place_rows(qt_ref[0, h * HEAD_DIM:(h + 1) * HEAD_DIM, cols], g * HEAD_DIM, LANES)
            items.append((0, g * VT_ROWS))
    outs = _attention_items(items, k_ref, vt_ref, qz_scr, m_scr, acc_scr)
    for t in range(n_tiles):
        tile = outs[t * A_HEADS:(t + 1) * A_HEADS]
        o_ref[0, t * Q_TILE:(t + 1) * Q_TILE, :] = jnp.concatenate([o.T for o in tile], axis=-1).astype(BF16)


def _attention_a(qat, ka, vat, tq):
    bsz, s, _ = ka.shape
    return pl.pallas_call(
        _attn_a_kernel,
        grid=(bsz, s // tq),
        in_specs=[
            pl.BlockSpec((1, A_WIDTH, tq), lambda b, i: (b, 0, i)),
            pl.BlockSpec((1, s, A_KV_WIDTH), lambda b, i: (b, 0, 0)),
            pl.BlockSpec((1, A_KV_HEADS * VT_ROWS, s), lambda b, i: (b, 0, 0)),
        ],
        out_specs=pl.BlockSpec((1, tq, A_WIDTH), lambda b, i: (b, i, 0)),
        out_shape=jax.ShapeDtypeStruct((bsz, s, A_WIDTH), BF16),
        scratch_shapes=_attention_scratch(A_HEADS, tq),
        compiler_params=pltpu.CompilerParams(
            dimension_semantics=("arbitrary", "arbitrary"), vmem_limit_bytes=VMEM_LIMIT),
        name="attn_axial_gqa",
    )(qat, ka, vat)


def _attn_b_kernel(lam_ref, gain_ref, qt_ref, k_ref, vt_ref, o_ref, qz_scr, m_scr, acc_scr, *, lam_init):
    lp = lam_ref[...]
    lam = (jnp.exp(jnp.sum(lp[0:1] * lp[1:2], axis=-1, keepdims=True))
           - jnp.exp(jnp.sum(lp[2:3] * lp[3:4], axis=-1, keepdims=True)) + lam_init)
    n_tiles = qt_ref.shape[2] // Q_TILE
    items = []
    for t in range(n_tiles):
        cols = slice(t * Q_TILE, (t + 1) * Q_TILE)
        for h in range(B_HEADS):
            pair = (h // 2) * LANES
            for comp in range(2):
                r0 = h * HEAD_DIM + comp * B_QK_DIM
                qz_scr[len(items)] = _place_rows(qt_ref[0, r0:r0 + B_QK_DIM, cols], r0 - pair, LANES)
                items.append((pair, h * VT_ROWS))
    comps = _attention_items(items, k_ref, vt_ref, qz_scr, m_scr, acc_scr, exp_dtype=BF16)
    for t in range(n_tiles):
        outs = []
        for h in range(B_HEADS):
            i = (t * B_HEADS + h) * 2
            o = comps[i] - lam * comps[i + 1]
            o = o * lax.rsqrt(jnp.mean(o * o, axis=0, keepdims=True) + EPS)
            outs.append(o.T)
        y = jnp.concatenate(outs, axis=-1) * gain_ref[...] * (1.0 - lam_init)
        o_ref[0, t * Q_TILE:(t + 1) * Q_TILE, :] = y.astype(BF16)


def _attention_b(qbt, kb, vbt, lam_params, subln_gain, lam_init, tq):
    bsz, s, _ = kb.shape
    return pl.pallas_call(
        functools.partial(_attn_b_kernel, lam_init=lam_init),
        grid=(bsz, s // tq),
        in_specs=[
            pl.BlockSpec((4, B_QK_DIM), lambda b, i: (0, 0)),
            pl.BlockSpec((1, B_WIDTH), lambda b, i: (0, 0)),
            pl.BlockSpec((1, B_WIDTH, tq), lambda b, i: (b, 0, i)),
            pl.BlockSpec((1, s, B_WIDTH), lambda b, i: (b, 0, 0)),
            pl.BlockSpec((1, B_HEADS * VT_ROWS, s), lambda b, i: (b, 0, 0)),
        ],
        out_specs=pl.BlockSpec((1, tq, B_WIDTH), lambda b, i: (b, i, 0)),
        out_shape=jax.ShapeDtypeStruct((bsz, s, B_WIDTH), BF16),
        scratch_shapes=_attention_scratch(2 * B_HEADS, tq),
        compiler_params=pltpu.CompilerParams(
            dimension_semantics=("arbitrary", "arbitrary"), vmem_limit_bytes=VMEM_LIMIT),
        name="attn_differential",
    )(lam_params, subln_gain, qbt, kb, vbt)


def _hgrn_levels():
    sizes = []
    bs = HGRN_CHUNK
    while bs > HGRN_LEAF:
        sizes.append(bs)
        bs //= 2
    return sizes


def _hgrn_constants():
    c, w = HGRN_CHUNK, C_WIDTH
    t = np.arange(c)
    tri = np.stack([(t[:, None] >= t[None, :]), (t[:, None] <= t[None, :])]).astype(np.float32)
    s_of_lane = np.arange(w) % C_DIM
    sizes = _hgrn_levels()
    level = np.full((2, c, w), float(len(sizes) + 1), np.float32)
    for d in range(2):
        for li, bs in enumerate(sizes):
            upper = (t % bs) >= bs // 2
            qrow = upper if d == 0 else ~upper
            same = (t[:, None] // bs) == (s_of_lane[None, :] // bs)
            cross = qrow[:, None] & ~qrow[s_of_lane][None, :]
            level[d][same & cross] = li
        same_leaf = (t[:, None] // HGRN_LEAF) == (s_of_lane[None, :] // HGRN_LEAF)
        order = (s_of_lane[None, :] <= t[:, None]) if d == 0 else (s_of_lane[None, :] >= t[:, None])
        level[d][same_leaf & order] = len(sizes)
    half = (np.arange(LANES)[None, :] < C_DIM).astype(np.float32) * np.ones((c, 1), np.float32)
    return tri, level, half


def _block_diag(x, low_half):
    zero = jnp.zeros((x.shape[0], LANES), x.dtype)
    blocks = []
    for h in range(C_HEADS):
        tile = x[:, (h // 2) * LANES:(h // 2 + 1) * LANES]
        keep = tile * (low_half if h % 2 == 0 else 1 - low_half)
        blocks.append(jnp.concatenate([keep, zero] if h // 2 == 0 else [zero, keep], axis=1))
    return jnp.concatenate(blocks, axis=0)


def _block_diag_t(x, low_half):
    zero = jnp.zeros((LANES, LANES), x.dtype)
    rows = []
    for t in range(C_WIDTH // LANES):
        tile = x[:, t * LANES:(t + 1) * LANES]
        pair = jnp.concatenate([tile * low_half, tile * (1 - low_half)], axis=0).T
        rows.append(jnp.concatenate([pair, zero] if t == 0 else [zero, pair], axis=1))
    return jnp.concatenate(rows, axis=0)


def _ref_rows(b, bs, offset):
    parts = []
    for lo in range(0, b.shape[0], bs):
        parts.append(jnp.broadcast_to(b[lo + offset:lo + offset + 1, :], (bs, b.shape[1])))
    return jnp.concatenate(parts, axis=0)


def _level_operands(q, k, b, bs, d):
    half = bs // 2
    ref_off = half - 1 if d == 0 else half
    zeros = jnp.zeros((8, b.shape[1]), F32)
    qparts, kparts = [], []
    for r0 in range(0, b.shape[0], 8):
        lo = (r0 // bs) * bs
        ref = b[lo + ref_off:lo + ref_off + 1, :]
        upper = (r0 % bs) >= half
        rows = slice(r0, r0 + 8)
        if upper == (d == 0):
            qparts.append(q[rows] * jnp.exp2(b[rows] - ref))
            kparts.append(zeros)
        else:
            qparts.append(zeros)
            kparts.append(k[rows] * jnp.exp2(ref - b[rows]))
    return jnp.concatenate(qparts, axis=0).astype(BF16), jnp.concatenate(kparts, axis=0).astype(BF16)


def _hgrn_kernel(lb_ref, tri_ref, level_ref, half_ref, zq_f, zf_f, zv_f, zq_b, zf_b, zv_b,
                 of_ref, ob_ref, st_scr, *, layer, depth, nb):
    c, w = HGRN_CHUNK, C_WIDTH
    sizes = _hgrn_levels()
    n_lvl = len(sizes)

    @pl.when(pl.program_id(1) == 0)
    def _():
        st_scr[...] = jnp.zeros_like(st_scr)

    rows = [lb_ref[i:i + 1, :] for i in range(depth)]
    mx = functools.reduce(jnp.maximum, rows)
    es = [jnp.exp(r - mx) for r in rows]
    tot = functools.reduce(lambda a, b_: a + b_, es)
    ps = [e / tot for e in es]
    cum = functools.reduce(lambda a, b_: a + b_, ps[:layer + 1])
    one_minus_lb = 1.0 - jnp.clip(cum - ps[0], 0.0, 1.0)

    low_half = half_ref[...].astype(BF16)
    low_bool = half_ref[...] > 0.5
    owner = [[level_ref[d] == float(li) for li in range(n_lvl + 1)] for d in range(2)]
    chains = [(j, d) for j in range(nb) for d in range(2)]
    zrefs = {0: (zq_f, zf_f, zv_f), 1: (zq_b, zf_b, zv_b)}

    q, k, v, b, total = {}, {}, {}, {}, {}
    g_split = {}
    for ch in chains:
        j, d = ch
        zq, zf = zrefs[d][0][j], zrefs[d][1][j]
        v[ch] = zrefs[d][2][j]
        q[ch] = (zq * (C_DIM ** -0.5)) / (1.0 + jnp.exp2(zq * -LOG2_E))
        k[ch] = one_minus_lb * (1.0 - 1.0 / (1.0 + jnp.exp2(zf * -LOG2_E)))
        g = jnp.log2(jnp.maximum(1.0 - k[ch], FORGET_FLOOR))
        g_hi = g.astype(BF16)
        g_r = g - g_hi.astype(F32)
        g_mid = g_r.astype(BF16)
        g_lo = (g_r - g_mid.astype(F32)).astype(BF16)
        g_split[ch] = jnp.concatenate([g_hi, g_mid, g_lo], axis=1)
    for ch in chains:
        d = ch[1]
        bb = _dot(tri_ref[d], g_split[ch])
        b[ch] = bb[:, 0:w] + bb[:, w:2 * w] + bb[:, 2 * w:3 * w]
        total[ch] = b[ch][c - 1:c, :] if d == 0 else b[ch][0:1, :]

    parts = {ch: [] for ch in chains}
    for li in range(n_lvl + 1):
        for ch in chains:
            d = ch[1]
            if li < n_lvl:
                qp, kp = _level_operands(q[ch], k[ch], b[ch], sizes[li], d)
            else:
                ref = _ref_rows(b[ch], HGRN_LEAF, HGRN_LEAF // 2 - 1 if d == 0 else HGRN_LEAF // 2)
                arg = b[ch] - ref
                qp = (q[ch] * jnp.exp2(arg)).astype(BF16)
                kp = (k[ch] * jnp.exp2(-arg)).astype(BF16)
            parts[ch].append(_dot(qp, _block_diag_t(kp, low_half)))
    a = {}
    for ch in chains:
        acc = parts[ch][0]
        for li in range(1, n_lvl + 1):
            acc = jnp.where(owner[ch[1]][li], parts[ch][li], acc)
        a[ch] = acc.astype(BF16)

    o_intra, o_inter, upd = {}, {}, {}
    for ch in chains:
        j, d = ch
        o_intra[ch] = _dot(a[ch], _block_diag(v[ch], low_half))
        qe = (q[ch] * jnp.exp2(b[ch])).astype(BF16)
        o_inter[ch] = _dot(qe, _block_diag_t(st_scr[d, j].astype(BF16), low_half))
        kb = (k[ch] * jnp.exp2(total[ch] - b[ch])).astype(BF16)
        tiles = []
        for t in range(w // LANES):
            lanes = slice(t * LANES, (t + 1) * LANES)
            full = _dot_tn(v[ch][:, lanes], kb[:, lanes])
            tiles.append(jnp.where(low_bool, full[0:C_DIM], full[C_DIM:2 * C_DIM]))
        upd[ch] = jnp.concatenate(tiles, axis=1)
    for ch in chains:
        j, d = ch
        st_scr[d, j] = st_scr[d, j] * jnp.exp2(total[ch]) + upd[ch]
        (of_ref if d == 0 else ob_ref)[j] = o_intra[ch] + o_inter[ch]


def _hgrn(cq, cff, cfb, ci, lower_bounds, layer, nb):
    bsz, s, _ = cq.shape
    depth = lower_bounds.shape[0]
    n_chunks = s // HGRN_CHUNK
    c, w = HGRN_CHUNK, C_WIDTH
    tri, level, half = _hgrn_constants()
    consts = (jnp.asarray(tri, BF16), jnp.asarray(level), jnp.asarray(half))
    full = lambda a: pl.BlockSpec(a.shape, lambda g, i: (0,) * a.ndim)
    last = n_chunks - 1
    fwd = pl.BlockSpec((nb, c, w), lambda g, i: (g, i, 0))
    bwd = pl.BlockSpec((nb, c, w), lambda g, i: (g, last - i, 0))
    return pl.pallas_call(
        functools.partial(_hgrn_kernel, layer=layer, depth=depth, nb=nb),
        grid=(bsz // nb, n_chunks),
        in_specs=[full(lower_bounds)] + [full(a) for a in consts] + [fwd, fwd, fwd, bwd, bwd, bwd],
        out_specs=[fwd, bwd],
        out_shape=[jax.ShapeDtypeStruct((bsz, s, w), F32)] * 2,
        scratch_shapes=[pltpu.VMEM((2, nb, C_DIM, w), F32)],
        compiler_params=pltpu.CompilerParams(
            dimension_semantics=("arbitrary", "arbitrary"), vmem_limit_bytes=VMEM_LIMIT),
        name="hgrn2_bidir",
    )(lower_bounds, *consts, cq, cff, ci, cq, cfb, ci)


def _out_ffn_kernel(x_ref, mod_ref, oa_ref, ob_ref, cf_ref, cb_ref, zg_ref, cgain_ref,
                    wo_ref, w1_ref, w2_ref, fg_ref, o_ref, *, final, ff_chunk):
    oc = cf_ref[0] + cb_ref[0]
    ms = _split_dot(oc * oc, _group_ones(C_WIDTH, C_DIM))
    zg = zg_ref[0]
    oc = oc * lax.rsqrt(ms + EPS) * cgain_ref[...] * (zg * jax.nn.sigmoid(zg))
    mix = jnp.concatenate([oa_ref[0], ob_ref[0], oc.astype(BF16)], axis=-1)
    x1 = x_ref[0] + mod_ref[0, 2:3, :] * _dot(mix, wo_ref[0])
    h = (_rms_rows(x1) * (1.0 + mod_ref[0, 4:5, :]) + mod_ref[0, 3:4, :]).astype(BF16)
    d_ff = w1_ref.shape[2]
    acc = jnp.zeros_like(x1)
    for j in range(d_ff // ff_chunk):
        u = _dot(h, w1_ref[0, :, j * ff_chunk:(j + 1) * ff_chunk])
        u = jnp.square(jnp.maximum(u, 0.0)).astype(BF16)
        acc = acc + _dot(u, w2_ref[0, j * ff_chunk:(j + 1) * ff_chunk, :])
    x2 = x1 + mod_ref[0, 5:6, :] * acc
    if final:
        x2 = _rms_rows(x2) * fg_ref[...]
    o_ref[0] = x2


def _out_ffn(x, mod_l, oa, ob, oc_f, oc_b, cg, c_gain, w_out, w_ff1, w_ff2, final_gain, layer, final, tm):
    bsz, s, d = x.shape
    d_ff = w_ff1.shape[2]
    row_spec = lambda width: pl.BlockSpec((1, tm, width), lambda b, i: (b, i, 0))
    weight = lambda a: pl.BlockSpec((1,) + a.shape[1:], lambda b, i: (layer, 0, 0),
                                    pipeline_mode=pl.Buffered(1))
    return pl.pallas_call(
        functools.partial(_out_ffn_kernel, final=final, ff_chunk=1024),
        grid=(bsz, s // tm),
        in_specs=[
            row_spec(d),
            pl.BlockSpec((1, 6, d), lambda b, i: (b, 0, 0)),
            row_spec(A_WIDTH), row_spec(B_WIDTH), row_spec(C_WIDTH), row_spec(C_WIDTH), row_spec(C_WIDTH),
            pl.BlockSpec((1, C_WIDTH), lambda b, i: (0, 0)),
            weight(w_out), weight(w_ff1), weight(w_ff2),
            pl.BlockSpec((1, d), lambda b, i: (0, 0)),
        ],
        out_specs=row_spec(d),
        out_shape=jax.ShapeDtypeStruct((bsz, s, d), F32),
        compiler_params=pltpu.CompilerParams(
            dimension_semantics=("arbitrary", "arbitrary"), vmem_limit_bytes=VMEM_LIMIT),
        name="out_proj_ffn",
    )(x, mod_l, oa, ob, oc_f, oc_b, cg, c_gain, w_out, w_ff1, w_ff2, final_gain)


def _rope_tables(s):
    half = HEAD_DIM // 2
    inv = ROPE_THETA ** (-jnp.arange(0, half, 2, dtype=F32) / half)
    t = jnp.arange(s)
    ang_row = (t // GRID_W).astype(F32)[:, None] * inv[None, :]
    ang_col = (t % GRID_W).astype(F32)[:, None] * inv[None, :]
    ang_1d = t.astype(F32)[:, None] * inv[None, :]

    def pair(ang):
        return (jnp.concatenate([jnp.cos(ang), jnp.cos(ang)], axis=-1),
                jnp.concatenate([-jnp.sin(ang), jnp.sin(ang)], axis=-1))

    cr, sr = pair(ang_row)
    cc, sc = pair(ang_col)
    c1, s1 = pair(ang_1d)
    tile = lambda a: jnp.tile(a, (1, LANES // a.shape[1]))
    return (tile(jnp.concatenate([cr, cc], axis=-1)), tile(jnp.concatenate([sr, sc], axis=-1)),
            tile(c1), tile(s1))


def kernel(x, c, w_mod, b_mod, w_in, a_qk_norm, diff_lambda, diff_subln, hgrn_lower_bounds, hgrn_norm,
           w_out, w_ff1, w_ff2, final_norm):
    bsz, s, d = x.shape
    depth = w_in.shape[0]
    tm = min(512, s)
    tq = min(2 * Q_TILE, s)
    tables = _rope_tables(s)
    mod = _modulation(c, w_mod, b_mod).reshape(depth, bsz, 6, d)
    w_in_b, w_out_b, w_ff1_b, w_ff2_b = (w.astype(BF16) for w in (w_in, w_out, w_ff1, w_ff2))
    final_gain = final_norm.reshape(1, d)
    for l in range(depth):
        qk_gain = jnp.tile(a_qk_norm[l], (1, LANES // HEAD_DIM))
        qat, ka, vat, qbt, kb, vbt, cq, cff, cfb, ci, cg = _in_projection(
            x, mod[l], w_in_b, l, tables, qk_gain, tm)
        oa = _attention_a(qat, ka, vat, tq)
        lam_init = 0.8 - 0.6 * math.exp(-0.3 * l)
        subln = jnp.tile(diff_subln[l].reshape(1, HEAD_DIM), (1, B_HEADS))
        ob = _attention_b(qbt, kb, vbt, diff_lambda[l], subln, lam_init, tq)
        hg_gain = jnp.tile(hgrn_norm[l].reshape(1, C_DIM), (1, C_HEADS))
        oc_f, oc_b = _hgrn(cq, cff, cfb, ci, hgrn_lower_bounds, l, min(HGRN_BATCH, bsz))
        x = _out_ffn(x, mod[l], oa, ob, oc_f, oc_b, cg, hg_gain, w_out_b, w_ff1_b, w_ff2_b, final_gain,
                     l, l == depth - 1, tm)
    return x
```

```python
import functools
import math

import numpy as np
import jax
import jax.numpy as jnp
from jax import lax
from jax.experimental import pallas as pl
from jax.experimental.pallas import tpu as pltpu

F32 = jnp.float32
BF16 = jnp.bfloat16

HEAD_DIM = 64
GRID_W = 64
ROPE_THETA = 10000.0
EPS = 1e-6
FORGET_FLOOR = 1e-6
A_HEADS, A_KV_HEADS = 6, 2
B_HEADS, B_QK_DIM = 6, 32
C_HEADS, C_DIM = 4, 64
A_WIDTH = A_HEADS * HEAD_DIM
A_KV_WIDTH = A_KV_HEADS * HEAD_DIM
B_WIDTH = B_HEADS * HEAD_DIM
C_WIDTH = C_HEADS * C_DIM
LANES = 128
HGRN_CHUNK = 64
HGRN_LEAF = 8
HGRN_BATCH = 16
ONES_ROWS = 16
VT_ROWS = HEAD_DIM + ONES_ROWS
KV_CHUNK = 256
Q_TILE = 256
A_Q_TILES = 4
QK_AHEAD = 16
LOG2_E = math.log2(math.e)
VMEM_LIMIT = 56 * 1024 * 1024

OFF_AQ, OFF_AK, OFF_AV = 0, 384, 512
OFF_BQ, OFF_BK, OFF_BV = 640, 1024, 1408
OFF_C = 1792
C_DTYPES = (F32, F32, F32, BF16, F32)
IN_TOTAL = 3072


def _dot(a, b):
    return jnp.dot(a, b, preferred_element_type=F32)


def _dot_nt(a, b):
    return lax.dot_general(a, b, (((1,), (1,)), ((), ())), preferred_element_type=F32)


def _dot_tn(a, b):
    return lax.dot_general(a, b, (((0,), (0,)), ((), ())), preferred_element_type=F32)


def _split_dot(x, w_bf16):
    hi = x.astype(BF16)
    lo = (x - hi.astype(F32)).astype(BF16)
    return _dot(hi, w_bf16) + _dot(lo, w_bf16)


def _group_ones(n, group):
    r = lax.broadcasted_iota(jnp.int32, (n, n), 0) // group
    c = lax.broadcasted_iota(jnp.int32, (n, n), 1) // group
    return jnp.where(r == c, 1.0 / group, 0.0).astype(BF16)


def _rope(xb, cos, sin_signed):
    lane = lax.broadcasted_iota(jnp.int32, xb.shape, 1)
    low = (lane % 32) < 16
    partner = jnp.where(low, pltpu.roll(xb, LANES - 16, 1), pltpu.roll(xb, 16, 1))
    return xb * cos + partner * sin_signed


def _rms_rows(x):
    return x * lax.rsqrt(jnp.mean(x * x, axis=-1, keepdims=True) + EPS)


def _mod_kernel(c_ref, w_ref, b_ref, o_ref):
    c = c_ref[...]
    cond = (c * jax.nn.sigmoid(c)).astype(BF16)
    o_ref[0] = _dot(cond, w_ref[0].astype(BF16)) + b_ref[0]


def _modulation(c, w_mod, b_mod):
    depth, d, n = w_mod.shape
    bsz = c.shape[0]
    tn = 1024
    return pl.pallas_call(
        _mod_kernel,
        grid=(depth, n // tn),
        in_specs=[
            pl.BlockSpec((bsz, d), lambda l, j: (0, 0)),
            pl.BlockSpec((1, d, tn), lambda l, j: (l, 0, j)),
            pl.BlockSpec((1, 1, tn), lambda l, j: (l, 0, j)),
        ],
        out_specs=pl.BlockSpec((1, bsz, tn), lambda l, j: (l, 0, j)),
        out_shape=jax.ShapeDtypeStruct((depth, bsz, n), F32),
        compiler_params=pltpu.CompilerParams(
            dimension_semantics=("arbitrary", "arbitrary"), vmem_limit_bytes=VMEM_LIMIT),
        name="adaln_mod",
    )(c, w_mod, b_mod.reshape(depth, 1, n))


def _inproj_kernel(x_ref, mod_ref, w_ref, cosa_ref, sina_ref, cosb_ref, sinb_ref, gain_ref,
                   qat_ref, ka_ref, vat_ref, qbt_ref, kb_ref, vbt_ref,
                   cq_ref, cff_ref, cfb_ref, ci_ref, cg_ref, h_scr, z_scr):
    x = x_ref[0]
    h = _rms_rows(x) * (1.0 + mod_ref[0, 1:2, :]) + mod_ref[0, 0:1, :]
    h_scr[...] = h.astype(BF16)

    def project(j):
        z_scr[:, j * 1024:(j + 1) * 1024] = _dot(h_scr[...], w_ref[0, :, j * 1024:(j + 1) * 1024])

    project(0)
    project(1)

    ones = _group_ones(LANES, HEAD_DIM)
    cosa, sina = cosa_ref[...], sina_ref[...]
    cosb, sinb = cosb_ref[...], sinb_ref[...]

    def a_norm_rope(zb, gain):
        ms = _split_dot(zb * zb, ones)
        return _rope(zb * lax.rsqrt(ms + EPS) * gain, cosa, sina)

    def store_vt(vt_ref, first_head, blk_t):
        ones_rows = jnp.ones((ONES_ROWS, blk_t.shape[1]), BF16)
        for j in range(LANES // HEAD_DIM):
            r0 = (first_head + j) * VT_ROWS
            vt_ref[0, r0:r0 + HEAD_DIM, :] = blk_t[j * HEAD_DIM:(j + 1) * HEAD_DIM].astype(BF16)
            vt_ref[0, r0 + HEAD_DIM:r0 + VT_ROWS, :] = ones_rows

    for cblk in range(A_WIDTH // LANES):
        zb = z_scr[:, OFF_AQ + cblk * LANES:OFF_AQ + (cblk + 1) * LANES]
        y = a_norm_rope(zb, gain_ref[0:1, :]) * (HEAD_DIM ** -0.5 * LOG2_E)
        qat_ref[0, cblk * LANES:(cblk + 1) * LANES, :] = y.T.astype(BF16)
    ka_ref[0] = a_norm_rope(z_scr[:, OFF_AK:OFF_AK + LANES], gain_ref[1:2, :]).astype(BF16)
    store_vt(vat_ref, 0, z_scr[:, OFF_AV:OFF_AV + LANES].T)
    project(2)

    for cblk in range(B_WIDTH // LANES):
        lo = cblk * LANES
        zq = z_scr[:, OFF_BQ + lo:OFF_BQ + lo + LANES]
        qbt_ref[0, lo:lo + LANES, :] = (_rope(zq, cosb, sinb) * (B_QK_DIM ** -0.5 * LOG2_E)).T.astype(BF16)
    for cblk in range(B_WIDTH // LANES):
        lo = cblk * LANES
        zk = z_scr[:, OFF_BK + lo:OFF_BK + lo + LANES]
        kb_ref[0, :, lo:lo + LANES] = _rope(zk, cosb, sinb).astype(BF16)
        store_vt(vbt_ref, cblk * (LANES // HEAD_DIM), z_scr[:, OFF_BV + lo:OFF_BV + lo + LANES].T)
    for j, ref in enumerate((cq_ref, cff_ref, cfb_ref, ci_ref, cg_ref)):
        ref[0] = z_scr[:, OFF_C + j * C_WIDTH:OFF_C + (j + 1) * C_WIDTH].astype(ref.dtype)


def _in_projection(x, mod_l, w_in_bf16, layer, tables, qk_gain, tm):
    bsz, s, d = x.shape
    cosa, sina, cosb, sinb = tables
    row_spec = lambda width: pl.BlockSpec((1, tm, width), lambda b, i: (b, i, 0))
    tab_spec = pl.BlockSpec((tm, LANES), lambda b, i: (i, 0))
    col_spec = lambda height: pl.BlockSpec((1, height, tm), lambda b, i: (b, 0, i))
    return pl.pallas_call(
        _inproj_kernel,
        grid=(bsz, s // tm),
        in_specs=[
            row_spec(d),
            pl.BlockSpec((1, 6, d), lambda b, i: (b, 0, 0)),
            pl.BlockSpec((1, d, IN_TOTAL), lambda b, i: (layer, 0, 0)),
            tab_spec, tab_spec, tab_spec, tab_spec,
            pl.BlockSpec((2, LANES), lambda b, i: (0, 0)),
        ],
        out_specs=[
            col_spec(A_WIDTH), row_spec(A_KV_WIDTH), col_spec(A_KV_HEADS * VT_ROWS),
            col_spec(B_WIDTH), row_spec(B_WIDTH), col_spec(B_HEADS * VT_ROWS),
        ] + [row_spec(C_WIDTH)] * len(C_DTYPES),
        out_shape=[
            jax.ShapeDtypeStruct((bsz, A_WIDTH, s), BF16),
            jax.ShapeDtypeStruct((bsz, s, A_KV_WIDTH), BF16),
            jax.ShapeDtypeStruct((bsz, A_KV_HEADS * VT_ROWS, s), BF16),
            jax.ShapeDtypeStruct((bsz, B_WIDTH, s), BF16),
            jax.ShapeDtypeStruct((bsz, s, B_WIDTH), BF16),
            jax.ShapeDtypeStruct((bsz, B_HEADS * VT_ROWS, s), BF16),
        ] + [jax.ShapeDtypeStruct((bsz, s, C_WIDTH), dt) for dt in C_DTYPES],
        scratch_shapes=[pltpu.VMEM((tm, d), BF16), pltpu.VMEM((tm, IN_TOTAL), F32)],
        compiler_params=pltpu.CompilerParams(
            dimension_semantics=("arbitrary", "arbitrary"), vmem_limit_bytes=VMEM_LIMIT),
        name="in_projection",
    )(x, mod_l, w_in_bf16, cosa, sina, cosb, sinb, qk_gain)


def _place_rows(block, start, total):
    parts = []
    if start:
        parts.append(jnp.zeros((start, block.shape[1]), block.dtype))
    parts.append(block)
    rest = total - start - block.shape[0]
    if rest:
        parts.append(jnp.zeros((rest, block.shape[1]), block.dtype))
    return jnp.concatenate(parts, axis=0) if len(parts) > 1 else block


def _attention_items(items, k_ref, vt_ref, qz_scr, m_scr, acc_scr, exp_dtype=F32):
    s = k_ref.shape[1]
    tq = acc_scr.shape[2]
    m_scr[...] = jnp.full(m_scr.shape, -1e30, F32)
    acc_scr[...] = jnp.zeros_like(acc_scr)

    units = [(i, c) for c in range(s // KV_CHUNK) for i in range(len(items))]

    def scores(u):
        i, c = units[u]
        k_off = items[i][0]
        return _dot(k_ref[0, c * KV_CHUNK:(c + 1) * KV_CHUNK, k_off:k_off + LANES], qz_scr[i])

    pending = [scores(u) for u in range(min(QK_AHEAD, len(units)))]
    for u, (i, c) in enumerate(units):
        v_off = items[i][1]
        st = pending.pop(0)
        m_old = m_scr[i]
        part = jnp.max(st.reshape(KV_CHUNK // 8, 8, tq), axis=0)
        m_new = jnp.maximum(m_old, jnp.max(part, axis=0, keepdims=True))
        m_scr[i] = m_new
        p = jnp.exp2((st - m_new[0:1]).astype(exp_dtype)).astype(BF16)
        d = _dot(vt_ref[0, v_off:v_off + VT_ROWS, c * KV_CHUNK:(c + 1) * KV_CHUNK], p)
        if u + QK_AHEAD < len(units):
            pending.append(scores(u + QK_AHEAD))
        acc_scr[i] = acc_scr[i] * jnp.exp2(m_old[0:1] - m_new[0:1]) + d
    results = []
    for i in range(len(items)):
        acc = acc_scr[i]
        results.append(acc[0:HEAD_DIM] * (1.0 / acc[HEAD_DIM:HEAD_DIM + 1]))
    return results


def _attention_scratch(heads, tq):
    n_items = heads * (tq // Q_TILE)
    return [pltpu.VMEM((n_items, LANES, Q_TILE), BF16), pltpu.VMEM((n_items, 8, Q_TILE), F32),
            pltpu.VMEM((n_items, VT_ROWS, Q_TILE), F32)]


def _attn_a_kernel(qt_ref, k_ref, vt_ref, o_ref, qz_scr, m_scr, acc_scr):
    group = A_HEADS // A_KV_HEADS
    n_tiles = qt_ref.shape[2] // Q_TILE
    items = []
    for t in range(n_tiles):
        cols = slice(t * Q_TILE, (t + 1) * Q_TILE)
        for h in range(A_HEADS):
            g = h // group
            qz_scr[len(items)] = _place_rows(qt_ref[0, h * HEAD_DIM:(h + 1) * HEAD_DIM, cols], g * HEAD_DIM, LANES)
            items.append((0, g * VT_ROWS))
    outs = _attention_items(items, k_ref, vt_ref, qz_scr, m_scr, acc_scr)
    for t in range(n_tiles):
        tile = outs[t * A_HEADS:(t + 1) * A_HEADS]
        o_ref[0, t * Q_TILE:(t + 1) * Q_TILE, :] = jnp.concatenate([o.T for o in tile], axis=-1).astype(BF16)


def _attention_a(qat, ka, vat, tq):
    bsz, s, _ = ka.shape
    return pl.pallas_call(
        _attn_a_kernel,
        grid=(bsz, s // tq),
        in_specs=[
            pl.BlockSpec((1, A_WIDTH, tq), lambda b, i: (b, 0, i)),
            pl.BlockSpec((1, s, A_KV_WIDTH), lambda b, i: (b, 0, 0)),
            pl.BlockSpec((1, A_KV_HEADS * VT_ROWS, s), lambda b, i: (b, 0, 0)),
        ],
        out_specs=pl.BlockSpec((1, tq, A_WIDTH), lambda b, i: (b, i, 0)),
        out_shape=jax.ShapeDtypeStruct((bsz, s, A_WIDTH), BF16),
        scratch_shapes=_attention_scratch(A_HEADS, tq),
        compiler_params=pltpu.CompilerParams(
            dimension_semantics=("arbitrary", "arbitrary"), vmem_limit_bytes=VMEM_LIMIT),
        name="attn_axial_gqa",
    )(qat, ka, vat)


def _attn_b_kernel(lam_ref, gain_ref, qt_ref, k_ref, vt_ref, o_ref, qz_scr, m_scr, acc_scr, *, lam_init):
    lp = lam_ref[...]
    lam = (jnp.exp(jnp.sum(lp[0:1] * lp[1:2], axis=-1, keepdims=True))
           - jnp.exp(jnp.sum(lp[2:3] * lp[3:4], axis=-1, keepdims=True)) + lam_init)
    n_tiles = qt_ref.shape[2] // Q_TILE
    items = []
    for t in range(n_tiles):
        cols = slice(t * Q_TILE, (t + 1) * Q_TILE)
        for h in range(B_HEADS):
            pair = (h // 2) * LANES
            for comp in range(2):
                r0 = h * HEAD_DIM + comp * B_QK_DIM
                qz_scr[len(items)] = _place_rows(qt_ref[0, r0:r0 + B_QK_DIM, cols], r0 - pair, LANES)
                items.append((pair, h * VT_ROWS))
    comps = _attention_items(items, k_ref, vt_ref, qz_scr, m_scr, acc_scr, exp_dtype=BF16)
    for t in range(n_tiles):
        outs = []
        for h in range(B_HEADS):
            i = (t * B_HEADS + h) * 2
            o = comps[i] - lam * comps[i + 1]
            o = o * lax.rsqrt(jnp.mean(o * o, axis=0, keepdims=True) + EPS)
            outs.append(o.T)
        y = jnp.concatenate(outs, axis=-1) * gain_ref[...] * (1.0 - lam_init)
        o_ref[0, t * Q_TILE:(t + 1) * Q_TILE, :] = y.astype(BF16)


def _attention_b(qbt, kb, vbt, lam_params, subln_gain, lam_init, tq):
    bsz, s, _ = kb.shape
    return pl.pallas_call(
        functools.partial(_attn_b_kernel, lam_init=lam_init),
        grid=(bsz, s // tq),
        in_specs=[
            pl.BlockSpec((4, B_QK_DIM), lambda b, i: (0, 0)),
            pl.BlockSpec((1, B_WIDTH), lambda b, i: (0, 0)),
            pl.BlockSpec((1, B_WIDTH, tq), lambda b, i: (b, 0, i)),
            pl.BlockSpec((1, s, B_WIDTH), lambda b, i: (b, 0, 0)),
            pl.BlockSpec((1, B_HEADS * VT_ROWS, s), lambda b, i: (b, 0, 0)),
        ],
        out_specs=pl.BlockSpec((1, tq, B_WIDTH), lambda b, i: (b, i, 0)),
        out_shape=jax.ShapeDtypeStruct((bsz, s, B_WIDTH), BF16),
        scratch_shapes=_attention_scratch(2 * B_HEADS, tq),
        compiler_params=pltpu.CompilerParams(
            dimension_semantics=("arbitrary", "arbitrary"), vmem_limit_bytes=VMEM_LIMIT),
        name="attn_differential",
    )(lam_params, subln_gain, qbt, kb, vbt)


def _hgrn_levels():
    sizes = []
    bs = HGRN_CHUNK
    while bs > HGRN_LEAF:
        sizes.append(bs)
        bs //= 2
    return sizes


def _hgrn_constants():
    c, w = HGRN_CHUNK, C_WIDTH
    t = np.arange(c)
    tri = np.stack([(t[:, None] >= t[None, :]), (t[:, None] <= t[None, :])]).astype(np.float32)
    s_of_lane = np.arange(w) % C_DIM
    sizes = _hgrn_levels()
    level = np.full((2, c, w), float(len(sizes) + 1), np.float32)
    for d in range(2):
        for li, bs in enumerate(sizes):
            upper = (t % bs) >= bs // 2
            qrow = upper if d == 0 else ~upper
            same = (t[:, None] // bs) == (s_of_lane[None, :] // bs)
            cross = qrow[:, None] & ~qrow[s_of_lane][None, :]
            level[d][same & cross] = li
        same_leaf = (t[:, None] // HGRN_LEAF) == (s_of_lane[None, :] // HGRN_LEAF)
        order = (s_of_lane[None, :] <= t[:, None]) if d == 0 else (s_of_lane[None, :] >= t[:, None])
        level[d][same_leaf & order] = len(sizes)
    half = (np.arange(LANES)[None, :] < C_DIM).astype(np.float32) * np.ones((c, 1), np.float32)
    return tri, level, half


def _block_diag(x, low_half):
    zero = jnp.zeros((x.shape[0], LANES), x.dtype)
    blocks = []
    for h in range(C_HEADS):
        tile = x[:, (h // 2) * LANES:(h // 2 + 1) * LANES]
        keep = tile * (low_half if h % 2 == 0 else 1 - low_half)
        blocks.append(jnp.concatenate([keep, zero] if h // 2 == 0 else [zero, keep], axis=1))
    return jnp.concatenate(blocks, axis=0)


def _block_diag_t(x, low_half):
    zero = jnp.zeros((LANES, LANES), x.dtype)
    rows = []
    for t in range(C_WIDTH // LANES):
        tile = x[:, t * LANES:(t + 1) * LANES]
        pair = jnp.concatenate([tile * low_half, tile * (1 - low_half)], axis=0).T
        rows.append(jnp.concatenate([pair, zero] if t == 0 else [zero, pair], axis=1))
    return jnp.concatenate(rows, axis=0)


def _ref_rows(b, bs, offset):
    parts = []
    for lo in range(0, b.shape[0], bs):
        parts.append(jnp.broadcast_to(b[lo + offset:lo + offset + 1, :], (bs, b.shape[1])))
    return jnp.concatenate(parts, axis=0)


def _level_operands(q, k, b, bs, d):
    half = bs // 2
    ref_off = half - 1 if d == 0 else half
    zeros = jnp.zeros((8, b.shape[1]), F32)
    qparts, kparts = [], []
    for r0 in range(0, b.shape[0], 8):
        lo = (r0 // bs) * bs
        ref = b[lo + ref_off:lo + ref_off + 1, :]
        upper = (r0 % bs) >= half
        rows = slice(r0, r0 + 8)
        if upper == (d == 0):
            qparts.append(q[rows] * jnp.exp2(b[rows] - ref))
            kparts.append(zeros)
        else:
            qparts.append(zeros)
            kparts.append(k[rows] * jnp.exp2(ref - b[rows]))
    return jnp.concatenate(qparts, axis=0).astype(BF16), jnp.concatenate(kparts, axis=0).astype(BF16)


def _hgrn_kernel(lb_ref, tri_ref, level_ref, half_ref, zq_f, zf_f, zv_f, zq_b, zf_b, zv_b,
                 of_ref, ob_ref, st_scr, *, layer, depth, nb):
    c, w = HGRN_CHUNK, C_WIDTH
    sizes = _hgrn_levels()
    n_lvl = len(sizes)

    @pl.when(pl.program_id(1) == 0)
    def _():
        st_scr[...] = jnp.zeros_like(st_scr)

    rows = [lb_ref[i:i + 1, :] for i in range(depth)]
    mx = functools.reduce(jnp.maximum, rows)
    es = [jnp.exp(r - mx) for r in rows]
    tot = functools.reduce(lambda a, b_: a + b_, es)
    ps = [e / tot for e in es]
    cum = functools.reduce(lambda a, b_: a + b_, ps[:layer + 1])
    one_minus_lb = 1.0 - jnp.clip(cum - ps[0], 0.0, 1.0)

    low_half = half_ref[...].astype(BF16)
    low_bool = half_ref[...] > 0.5
    owner = [[level_ref[d] == float(li) for li in range(n_lvl + 1)] for d in range(2)]
    chains = [(j, d) for j in range(nb) for d in range(2)]
    zrefs = {0: (zq_f, zf_f, zv_f), 1: (zq_b, zf_b, zv_b)}

    q, k, v, b, total = {}, {}, {}, {}, {}
    g_split = {}
    for ch in chains:
        j, d = ch
        zq, zf = zrefs[d][0][j], zrefs[d][1][j]
        v[ch] = zrefs[d][2][j]
        q[ch] = (zq * (C_DIM ** -0.5)) / (1.0 + jnp.exp2(zq * -LOG2_E))
        k[ch] = one_minus_lb * (1.0 - 1.0 / (1.0 + jnp.exp2(zf * -LOG2_E)))
        g = jnp.log2(jnp.maximum(1.0 - k[ch], FORGET_FLOOR))
        g_hi = g.astype(BF16)
        g_r = g - g_hi.astype(F32)
        g_mid = g_r.astype(BF16)
        g_lo = (g_r - g_mid.astype(F32)).astype(BF16)
        g_split[ch] = jnp.concatenate([g_hi, g_mid, g_lo], axis=1)
    for ch in chains:
        d = ch[1]
        bb = _dot(tri_ref[d], g_split[ch])
        b[ch] = bb[:, 0:w] + bb[:, w:2 * w] + bb[:, 2 * w:3 * w]
        total[ch] = b[ch][c - 1:c, :] if d == 0 else b[ch][0:1, :]

    parts = {ch: [] for ch in chains}
    for li in range(n_lvl + 1):
        for ch in chains:
            d = ch[1]
            if li < n_lvl:
                qp, kp = _level_operands(q[ch], k[ch], b[ch], sizes[li], d)
            else:
                ref = _ref_rows(b[ch], HGRN_LEAF, HGRN_LEAF // 2 - 1 if d == 0 else HGRN_LEAF // 2)
                arg = b[ch] - ref
                qp = (q[ch] * jnp.exp2(arg)).astype(BF16)
                kp = (k[ch] * jnp.exp2(-arg)).astype(BF16)
            parts[ch].append(_dot(qp, _block_diag_t(kp, low_half)))
    a = {}
    for ch in chains:
        acc = parts[ch][0]
        for li in range(1, n_lvl + 1):
            acc = jnp.where(owner[ch[1]][li], parts[ch][li], acc)
        a[ch] = acc.astype(BF16)

    o_intra, o_inter, upd = {}, {}, {}
    for ch in chains:
        j, d = ch
        o_intra[ch] = _dot(a[ch], _block_diag(v[ch], low_half))
        qe = (q[ch] * jnp.exp2(b[ch])).astype(BF16)
        o_inter[ch] = _dot(qe, _block_diag_t(st_scr[d, j].astype(BF16), low_half))
        kb = (k[ch] * jnp.exp2(total[ch] - b[ch])).astype(BF16)
        tiles = []
        for t in range(w // LANES):
            lanes = slice(t * LANES, (t + 1) * LANES)
            full = _dot_tn(v[ch][:, lanes], kb[:, lanes])
            tiles.append(jnp.where(low_bool, full[0:C_DIM], full[C_DIM:2 * C_DIM]))
        upd[ch] = jnp.concatenate(tiles, axis=1)
    for ch in chains:
        j, d = ch
        st_scr[d, j] = st_scr[d, j] * jnp.exp2(total[ch]) + upd[ch]
        (of_ref if d == 0 else ob_ref)[j] = o_intra[ch] + o_inter[ch]


def _hgrn(cq, cff, cfb, ci, lower_bounds, layer, nb):
    bsz, s, _ = cq.shape
    depth = lower_bounds.shape[0]
    n_chunks = s // HGRN_CHUNK
    c, w = HGRN_CHUNK, C_WIDTH
    tri, level, half = _hgrn_constants()
    consts = (jnp.asarray(tri, BF16), jnp.asarray(level), jnp.asarray(half))
    full = lambda a: pl.BlockSpec(a.shape, lambda g, i: (0,) * a.ndim)
    last = n_chunks - 1
    fwd = pl.BlockSpec((nb, c, w), lambda g, i: (g, i, 0))
    bwd = pl.BlockSpec((nb, c, w), lambda g, i: (g, last - i, 0))
    return pl.pallas_call(
        functools.partial(_hgrn_kernel, layer=layer, depth=depth, nb=nb),
        grid=(bsz // nb, n_chunks),
        in_specs=[full(lower_bounds)] + [full(a) for a in consts] + [fwd, fwd, fwd, bwd, bwd, bwd],
        out_specs=[fwd, bwd],
        out_shape=[jax.ShapeDtypeStruct((bsz, s, w), F32)] * 2,
        scratch_shapes=[pltpu.VMEM((2, nb, C_DIM, w), F32)],
        compiler_params=pltpu.CompilerParams(
            dimension_semantics=("arbitrary", "arbitrary"), vmem_limit_bytes=VMEM_LIMIT),
        name="hgrn2_bidir",
    )(lower_bounds, *consts, cq, cff, ci, cq, cfb, ci)


def _out_ffn_kernel(x_ref, mod_ref, oa_ref, ob_ref, cf_ref, cb_ref, zg_ref, cgain_ref,
                    wo_ref, w1_ref, w2_ref, fg_ref, o_ref, *, final, ff_chunk):
    oc = cf_ref[0] + cb_ref[0]
    ms = _split_dot(oc * oc, _group_ones(C_WIDTH, C_DIM))
    zg = zg_ref[0]
    oc = oc * lax.rsqrt(ms + EPS) * cgain_ref[...] * (zg * jax.nn.sigmoid(zg))
    mix = jnp.concatenate([oa_ref[0], ob_ref[0], oc.astype(BF16)], axis=-1)
    x1 = x_ref[0] + mod_ref[0, 2:3, :] * _dot(mix, wo_ref[0])
    h = (_rms_rows(x1) * (1.0 + mod_ref[0, 4:5, :]) + mod_ref[0, 3:4, :]).astype(BF16)
    d_ff = w1_ref.shape[2]
    acc = jnp.zeros_like(x1)
    for j in range(d_ff // ff_chunk):
        u = _dot(h, w1_ref[0, :, j * ff_chunk:(j + 1) * ff_chunk])
        u = jnp.square(jnp.maximum(u, 0.0)).astype(BF16)
        acc = acc + _dot(u, w2_ref[0, j * ff_chunk:(j + 1) * ff_chunk, :])
    x2 = x1 + mod_ref[0, 5:6, :] * acc
    if final:
        x2 = _rms_rows(x2) * fg_ref[...]
    o_ref[0] = x2


def _out_ffn(x, mod_l, oa, ob, oc_f, oc_b, cg, c_gain, w_out, w_ff1, w_ff2, final_gain, layer, final, tm):
    bsz, s, d = x.shape
    d_ff = w_ff1.shape[2]
    row_spec = lambda width: pl.BlockSpec((1, tm, width), lambda b, i: (b, i, 0))
    weight = lambda a: pl.BlockSpec((1,) + a.shape[1:], lambda b, i: (layer, 0, 0),
                                    pipeline_mode=pl.Buffered(1))
    return pl.pallas_call(
        functools.partial(_out_ffn_kernel, final=final, ff_chunk=1024),
        grid=(bsz, s // tm),
        in_specs=[
            row_spec(d),
            pl.BlockSpec((1, 6, d), lambda b, i: (b, 0, 0)),
            row_spec(A_WIDTH), row_spec(B_WIDTH), row_spec(C_WIDTH), row_spec(C_WIDTH), row_spec(C_WIDTH),
            pl.BlockSpec((1, C_WIDTH), lambda b, i: (0, 0)),
            weight(w_out), weight(w_ff1), weight(w_ff2),
            pl.BlockSpec((1, d), lambda b, i: (0, 0)),
        ],
        out_specs=row_spec(d),
        out_shape=jax.ShapeDtypeStruct((bsz, s, d), F32),
        compiler_params=pltpu.CompilerParams(
            dimension_semantics=("arbitrary", "arbitrary"), vmem_limit_bytes=VMEM_LIMIT),
        name="out_proj_ffn",
    )(x, mod_l, oa, ob, oc_f, oc_b, cg, c_gain, w_out, w_ff1, w_ff2, final_gain)


def _rope_tables(s):
    half = HEAD_DIM // 2
    inv = ROPE_THETA ** (-jnp.arange(0, half, 2, dtype=F32) / half)
    t = jnp.arange(s)
    ang_row = (t // GRID_W).astype(F32)[:, None] * inv[None, :]
    ang_col = (t % GRID_W).astype(F32)[:, None] * inv[None, :]
    ang_1d = t.astype(F32)[:, None] * inv[None, :]

    def pair(ang):
        return (jnp.concatenate([jnp.cos(ang), jnp.cos(ang)], axis=-1),
                jnp.concatenate([-jnp.sin(ang), jnp.sin(ang)], axis=-1))

    cr, sr = pair(ang_row)
    cc, sc = pair(ang_col)
    c1, s1 = pair(ang_1d)
    tile = lambda a: jnp.tile(a, (1, LANES // a.shape[1]))
    return (tile(jnp.concatenate([cr, cc], axis=-1)), tile(jnp.concatenate([sr, sc], axis=-1)),
            tile(c1), tile(s1))


def kernel(x, c, w_mod, b_mod, w_in, a_qk_norm, diff_lambda, diff_subln, hgrn_lower_bounds, hgrn_norm,
           w_out, w_ff1, w_ff2, final_norm):
    bsz, s, d = x.shape
    depth = w_in.shape[0]
    tm = min(512, s)
    tq = min(2 * Q_TILE, s)
    tables = _rope_tables(s)
    mod = _modulation(c, w_mod, b_mod).reshape(depth, bsz, 6, d)
    w_in_b, w_out_b, w_ff1_b, w_ff2_b = (w.astype(BF16) for w in (w_in, w_out, w_ff1, w_ff2))
    final_gain = final_norm.reshape(1, d)
    for l in range(depth):
        qk_gain = jnp.tile(a_qk_norm[l], (1, LANES // HEAD_DIM))
        qat, ka, vat, qbt, kb, vbt, cq, cff, cfb, ci, cg = _in_projection(
            x, mod[l], w_in_b, l, tables, qk_gain, tm)
        oa = _attention_a(qat, ka, vat, min(A_Q_TILES * Q_TILE, s))
        lam_init = 0.8 - 0.6 * math.exp(-0.3 * l)
        subln = jnp.tile(diff_subln[l].reshape(1, HEAD_DIM), (1, B_HEADS))
        ob = _attention_b(qbt, kb, vbt, diff_lambda[l], subln, lam_init, tq)
        hg_gain = jnp.tile(hgrn_norm[l].reshape(1, C_DIM), (1, C_HEADS))
        oc_f, oc_b = _hgrn(cq, cff, cfb, ci, hgrn_lower_bounds, l, min(HGRN_BATCH, bsz))
        x = _out_ffn(x, mod[l], oa, ob, oc_f, oc_b, cg, hg_gain, w_out_b, w_ff1_b, w_ff2_b, final_gain,
                     l, l == depth - 1, tm)
    return x
```

```python
import functools
import math

import numpy as np
import jax
import jax.numpy as jnp
from jax import lax
from jax.experimental import pallas as pl
from jax.experimental.pallas import tpu as pltpu

F32 = jnp.float32
BF16 = jnp.bfloat16

HEAD_DIM = 64
GRID_W = 64
ROPE_THETA = 10000.0
EPS = 1e-6
FORGET_FLOOR = 1e-6
A_HEADS, A_KV_HEADS = 6, 2
B_HEADS, B_QK_DIM = 6, 32
C_HEADS, C_DIM = 4, 64
A_WIDTH = A_HEADS * HEAD_DIM
A_KV_WIDTH = A_KV_HEADS * HEAD_DIM
B_WIDTH = B_HEADS * HEAD_DIM
C_WIDTH = C_HEADS * C_DIM
LANES = 128
HGRN_CHUNK = 64
HGRN_LEAF = 8
HGRN_BATCH = 16
ONES_ROWS = 16
VT_ROWS = HEAD_DIM + ONES_ROWS
KV_CHUNK = 256
Q_TILE = 256
A_Q_TILES = 4
B_Q_TILES = 4
QK_AHEAD = 16
LOG2_E = math.log2(math.e)
VMEM_LIMIT = 56 * 1024 * 1024

OFF_AQ, OFF_AK, OFF_AV = 0, 384, 512
OFF_BQ, OFF_BK, OFF_BV = 640, 1024, 1408
OFF_C = 1792
C_DTYPES = (F32, F32, F32, BF16, F32)
IN_TOTAL = 3072


def _dot(a, b):
    return jnp.dot(a, b, preferred_element_type=F32)


def _dot_nt(a, b):
    return lax.dot_general(a, b, (((1,), (1,)), ((), ())), preferred_element_type=F32)


def _dot_tn(a, b):
    return lax.dot_general(a, b, (((0,), (0,)), ((), ())), preferred_element_type=F32)


def _split_dot(x, w_bf16):
    hi = x.astype(BF16)
    lo = (x - hi.astype(F32)).astype(BF16)
    return _dot(hi, w_bf16) + _dot(lo, w_bf16)


def _group_ones(n, group):
    r = lax.broadcasted_iota(jnp.int32, (n, n), 0) // group
    c = lax.broadcasted_iota(jnp.int32, (n, n), 1) // group
    return jnp.where(r == c, 1.0 / group, 0.0).astype(BF16)


def _rope(xb, cos, sin_signed):
    lane = lax.broadcasted_iota(jnp.int32, xb.shape, 1)
    low = (lane % 32) < 16
    partner = jnp.where(low, pltpu.roll(xb, LANES - 16, 1), pltpu.roll(xb, 16, 1))
    return xb * cos + partner * sin_signed


def _rms_rows(x):
    return x * lax.rsqrt(jnp.mean(x * x, axis=-1, keepdims=True) + EPS)


def _mod_kernel(c_ref, w_ref, b_ref, o_ref):
    c = c_ref[...]
    cond = (c * jax.nn.sigmoid(c)).astype(BF16)
    o_ref[0] = _dot(cond, w_ref[0].astype(BF16)) + b_ref[0]


def _modulation(c, w_mod, b_mod):
    depth, d, n = w_mod.shape
    bsz = c.shape[0]
    tn = 1024
    return pl.pallas_call(
        _mod_kernel,
        grid=(depth, n // tn),
        in_specs=[
            pl.BlockSpec((bsz, d), lambda l, j: (0, 0)),
            pl.BlockSpec((1, d, tn), lambda l, j: (l, 0, j)),
            pl.BlockSpec((1, 1, tn), lambda l, j: (l, 0, j)),
        ],
        out_specs=pl.BlockSpec((1, bsz, tn), lambda l, j: (l, 0, j)),
        out_shape=jax.ShapeDtypeStruct((depth, bsz, n), F32),
        compiler_params=pltpu.CompilerParams(
            dimension_semantics=("arbitrary", "arbitrary"), vmem_limit_bytes=VMEM_LIMIT),
        name="adaln_mod",
    )(c, w_mod, b_mod.reshape(depth, 1, n))


def _inproj_kernel(x_ref, mod_ref, w_ref, cosa_ref, sina_ref, cosb_ref, sinb_ref, gain_ref,
                   qat_ref, ka_ref, vat_ref, qbt_ref, kb_ref, vbt_ref,
                   cq_ref, cff_ref, cfb_ref, ci_ref, cg_ref, h_scr, z_scr):
    x = x_ref[0]
    h = _rms_rows(x) * (1.0 + mod_ref[0, 1:2, :]) + mod_ref[0, 0:1, :]
    h_scr[...] = h.astype(BF16)

    def project(j):
        z_scr[:, j * 1024:(j + 1) * 1024] = _dot(h_scr[...], w_ref[0, :, j * 1024:(j + 1) * 1024])

    project(0)
    project(1)

    ones = _group_ones(LANES, HEAD_DIM)
    cosa, sina = cosa_ref[...], sina_ref[...]
    cosb, sinb = cosb_ref[...], sinb_ref[...]

    def a_norm_rope(zb, gain):
        ms = _split_dot(zb * zb, ones)
        return _rope(zb * lax.rsqrt(ms + EPS) * gain, cosa, sina)

    def store_vt(vt_ref, first_head, blk_t):
        ones_rows = jnp.ones((ONES_ROWS, blk_t.shape[1]), BF16)
        for j in range(LANES // HEAD_DIM):
            r0 = (first_head + j) * VT_ROWS
            vt_ref[0, r0:r0 + HEAD_DIM, :] = blk_t[j * HEAD_DIM:(j + 1) * HEAD_DIM].astype(BF16)
            vt_ref[0, r0 + HEAD_DIM:r0 + VT_ROWS, :] = ones_rows

    for cblk in range(A_WIDTH // LANES):
        zb = z_scr[:, OFF_AQ + cblk * LANES:OFF_AQ + (cblk + 1) * LANES]
        y = a_norm_rope(zb, gain_ref[0:1, :]) * (HEAD_DIM ** -0.5 * LOG2_E)
        qat_ref[0, cblk * LANES:(cblk + 1) * LANES, :] = y.T.astype(BF16)
    ka_ref[0] = a_norm_rope(z_scr[:, OFF_AK:OFF_AK + LANES], gain_ref[1:2, :]).astype(BF16)
    store_vt(vat_ref, 0, z_scr[:, OFF_AV:OFF_AV + LANES].T)
    project(2)

    for cblk in range(B_WIDTH // LANES):
        lo = cblk * LANES
        zq = z_scr[:, OFF_BQ + lo:OFF_BQ + lo + LANES]
        qbt_ref[0, lo:lo + LANES, :] = (_rope(zq, cosb, sinb) * (B_QK_DIM ** -0.5 * LOG2_E)).T.astype(BF16)
    for cblk in range(B_WIDTH // LANES):
        lo = cblk * LANES
        zk = z_scr[:, OFF_BK + lo:OFF_BK + lo + LANES]
        kb_ref[0, :, lo:lo + LANES] = _rope(zk, cosb, sinb).astype(BF16)
        store_vt(vbt_ref, cblk * (LANES // HEAD_DIM), z_scr[:, OFF_BV + lo:OFF_BV + lo + LANES].T)
    for j, ref in enumerate((cq_ref, cff_ref, cfb_ref, ci_ref, cg_ref)):
        ref[0] = z_scr[:, OFF_C + j * C_WIDTH:OFF_C + (j + 1) * C_WIDTH].astype(ref.dtype)


def _in_projection(x, mod_l, w_in_bf16, layer, tables, qk_gain, tm):
    bsz, s, d = x.shape
    cosa, sina, cosb, sinb = tables
    row_spec = lambda width: pl.BlockSpec((1, tm, width), lambda b, i: (b, i, 0))
    tab_spec = pl.BlockSpec((tm, LANES), lambda b, i: (i, 0))
    col_spec = lambda height: pl.BlockSpec((1, height, tm), lambda b, i: (b, 0, i))
    return pl.pallas_call(
        _inproj_kernel,
        grid=(bsz, s // tm),
        in_specs=[
            row_spec(d),
            pl.BlockSpec((1, 6, d), lambda b, i: (b, 0, 0)),
            pl.BlockSpec((1, d, IN_TOTAL), lambda b, i: (layer, 0, 0)),
            tab_spec, tab_spec, tab_spec, tab_spec,
            pl.BlockSpec((2, LANES), lambda b, i: (0, 0)),
        ],
        out_specs=[
            col_spec(A_WIDTH), row_spec(A_KV_WIDTH), col_spec(A_KV_HEADS * VT_ROWS),
            col_spec(B_WIDTH), row_spec(B_WIDTH), col_spec(B_HEADS * VT_ROWS),
        ] + [row_spec(C_WIDTH)] * len(C_DTYPES),
        out_shape=[
            jax.ShapeDtypeStruct((bsz, A_WIDTH, s), BF16),
            jax.ShapeDtypeStruct((bsz, s, A_KV_WIDTH), BF16),
            jax.ShapeDtypeStruct((bsz, A_KV_HEADS * VT_ROWS, s), BF16),
            jax.ShapeDtypeStruct((bsz, B_WIDTH, s), BF16),
            jax.ShapeDtypeStruct((bsz, s, B_WIDTH), BF16),
            jax.ShapeDtypeStruct((bsz, B_HEADS * VT_ROWS, s), BF16),
        ] + [jax.ShapeDtypeStruct((bsz, s, C_WIDTH), dt) for dt in C_DTYPES],
        scratch_shapes=[pltpu.VMEM((tm, d), BF16), pltpu.VMEM((tm, IN_TOTAL), F32)],
        compiler_params=pltpu.CompilerParams(
            dimension_semantics=("arbitrary", "arbitrary"), vmem_limit_bytes=VMEM_LIMIT),
        name="in_projection",
    )(x, mod_l, w_in_bf16, cosa, sina, cosb, sinb, qk_gain)


def _place_rows(block, start, total):
    parts = []
    if start:
        parts.append(jnp.zeros((start, block.shape[1]), block.dtype))
    parts.append(block)
    rest = total - start - block.shape[0]
    if rest:
        parts.append(jnp.zeros((rest, block.shape[1]), block.dtype))
    return jnp.concatenate(parts, axis=0) if len(parts) > 1 else block


def _attention_items(items, k_ref, vt_ref, qz_scr, m_scr, acc_scr, exp_dtype=F32):
    s = k_ref.shape[1]
    tq = acc_scr.shape[2]
    m_scr[...] = jnp.full(m_scr.shape, -1e30, F32)
    acc_scr[...] = jnp.zeros_like(acc_scr)

    units = [(i, c) for c in range(s // KV_CHUNK) for i in range(len(items))]

    def scores(u):
        i, c = units[u]
        k_off = items[i][0]
        return _dot(k_ref[0, c * KV_CHUNK:(c + 1) * KV_CHUNK, k_off:k_off + LANES], qz_scr[i])

    pending = [scores(u) for u in range(min(QK_AHEAD, len(units)))]
    for u, (i, c) in enumerate(units):
        v_off = items[i][1]
        st = pending.pop(0)
        m_old = m_scr[i]
        part = jnp.max(st.reshape(KV_CHUNK // 8, 8, tq), axis=0)
        m_new = jnp.maximum(m_old, jnp.max(part, axis=0, keepdims=True))
        m_scr[i] = m_new
        p = jnp.exp2((st - m_new[0:1]).astype(exp_dtype)).astype(BF16)
        d = _dot(vt_ref[0, v_off:v_off + VT_ROWS, c * KV_CHUNK:(c + 1) * KV_CHUNK], p)
        if u + QK_AHEAD < len(units):
            pending.append(scores(u + QK_AHEAD))
        acc_scr[i] = acc_scr[i] * jnp.exp2(m_old[0:1] - m_new[0:1]) + d
    results = []
    for i in range(len(items)):
        acc = acc_scr[i]
        results.append(acc[0:HEAD_DIM] * (1.0 / acc[HEAD_DIM:HEAD_DIM + 1]))
    return results


def _attention_scratch(heads, tq):
    n_items = heads * (tq // Q_TILE)
    return [pltpu.VMEM((n_items, LANES, Q_TILE), BF16), pltpu.VMEM((n_items, 8, Q_TILE), F32),
            pltpu.VMEM((n_items, VT_ROWS, Q_TILE), F32)]


def _attn_a_kernel(qt_ref, k_ref, vt_ref, o_ref, qz_scr, m_scr, acc_scr):
    group = A_HEADS // A_KV_HEADS
    n_tiles = qt_ref.shape[2] // Q_TILE
    items = []
    for t in range(n_tiles):
        cols = slice(t * Q_TILE, (t + 1) * Q_TILE)
        for h in range(A_HEADS):
            g = h // group
            qz_scr[len(items)] = _place_rows(qt_ref[0, h * HEAD_DIM:(h + 1) * HEAD_DIM, cols], g * HEAD_DIM, LANES)
            items.append((0, g * VT_ROWS))
    outs = _attention_items(items, k_ref, vt_ref, qz_scr, m_scr, acc_scr)
    for t in range(n_tiles):
        tile = outs[t * A_HEADS:(t + 1) * A_HEADS]
        o_ref[0, t * Q_TILE:(t + 1) * Q_TILE, :] = jnp.concatenate([o.T for o in tile], axis=-1).astype(BF16)


def _attention_a(qat, ka, vat, tq):
    bsz, s, _ = ka.shape
    return pl.pallas_call(
        _attn_a_kernel,
        grid=(bsz, s // tq),
        in_specs=[
            pl.BlockSpec((1, A_WIDTH, tq), lambda b, i: (b, 0, i)),
            pl.BlockSpec((1, s, A_KV_WIDTH), lambda b, i: (b, 0, 0)),
            pl.BlockSpec((1, A_KV_HEADS * VT_ROWS, s), lambda b, i: (b, 0, 0)),
        ],
        out_specs=pl.BlockSpec((1, tq, A_WIDTH), lambda b, i: (b, i, 0)),
        out_shape=jax.ShapeDtypeStruct((bsz, s, A_WIDTH), BF16),
        scratch_shapes=_attention_scratch(A_HEADS, tq),
        compiler_params=pltpu.CompilerParams(
            dimension_semantics=("arbitrary", "arbitrary"), vmem_limit_bytes=VMEM_LIMIT),
        name="attn_axial_gqa",
    )(qat, ka, vat)


def _attn_b_kernel(lam_ref, gain_ref, qt_ref, k_ref, vt_ref, o_ref, qz_scr, m_scr, acc_scr, *, lam_init):
    lp = lam_ref[...]
    lam = (jnp.exp(jnp.sum(lp[0:1] * lp[1:2], axis=-1, keepdims=True))
           - jnp.exp(jnp.sum(lp[2:3] * lp[3:4], axis=-1, keepdims=True)) + lam_init)
    n_tiles = qt_ref.shape[2] // Q_TILE
    items = []
    for t in range(n_tiles):
        cols = slice(t * Q_TILE, (t + 1) * Q_TILE)
        for h in range(B_HEADS):
            pair = (h // 2) * LANES
            for comp in range(2):
                r0 = h * HEAD_DIM + comp * B_QK_DIM
                qz_scr[len(items)] = _place_rows(qt_ref[0, r0:r0 + B_QK_DIM, cols], r0 - pair, LANES)
                items.append((pair, h * VT_ROWS))
    comps = _attention_items(items, k_ref, vt_ref, qz_scr, m_scr, acc_scr, exp_dtype=BF16)
    for t in range(n_tiles):
        outs = []
        for h in range(B_HEADS):
            i = (t * B_HEADS + h) * 2
            o = comps[i] - lam * comps[i + 1]
            o = o * lax.rsqrt(jnp.mean(o * o, axis=0, keepdims=True) + EPS)
            outs.append(o.T)
        y = jnp.concatenate(outs, axis=-1) * gain_ref[...] * (1.0 - lam_init)
        o_ref[0, t * Q_TILE:(t + 1) * Q_TILE, :] = y.astype(BF16)


def _attention_b(qbt, kb, vbt, lam_params, subln_gain, lam_init, tq):
    bsz, s, _ = kb.shape
    return pl.pallas_call(
        functools.partial(_attn_b_kernel, lam_init=lam_init),
        grid=(bsz, s // tq),
        in_specs=[
            pl.BlockSpec((4, B_QK_DIM), lambda b, i: (0, 0)),
            pl.BlockSpec((1, B_WIDTH), lambda b, i: (0, 0)),
            pl.BlockSpec((1, B_WIDTH, tq), lambda b, i: (b, 0, i)),
            pl.BlockSpec((1, s, B_WIDTH), lambda b, i: (b, 0, 0)),
            pl.BlockSpec((1, B_HEADS * VT_ROWS, s), lambda b, i: (b, 0, 0)),
        ],
        out_specs=pl.BlockSpec((1, tq, B_WIDTH), lambda b, i: (b, i, 0)),
        out_shape=jax.ShapeDtypeStruct((bsz, s, B_WIDTH), BF16),
        scratch_shapes=_attention_scratch(2 * B_HEADS, tq),
        compiler_params=pltpu.CompilerParams(
            dimension_semantics=("arbitrary", "arbitrary"), vmem_limit_bytes=VMEM_LIMIT),
        name="attn_differential",
    )(lam_params, subln_gain, qbt, kb, vbt)


def _hgrn_levels():
    sizes = []
    bs = HGRN_CHUNK
    while bs > HGRN_LEAF:
        sizes.append(bs)
        bs //= 2
    return sizes


def _hgrn_constants():
    c, w = HGRN_CHUNK, C_WIDTH
    t = np.arange(c)
    tri = np.stack([(t[:, None] >= t[None, :]), (t[:, None] <= t[None, :])]).astype(np.float32)
    s_of_lane = np.arange(w) % C_DIM
    sizes = _hgrn_levels()
    level = np.full((2, c, w), float(len(sizes) + 1), np.float32)
    for d in range(2):
        for li, bs in enumerate(sizes):
            upper = (t % bs) >= bs // 2
            qrow = upper if d == 0 else ~upper
            same = (t[:, None] // bs) == (s_of_lane[None, :] // bs)
            cross = qrow[:, None] & ~qrow[s_of_lane][None, :]
            level[d][same & cross] = li
        same_leaf = (t[:, None] // HGRN_LEAF) == (s_of_lane[None, :] // HGRN_LEAF)
        order = (s_of_lane[None, :] <= t[:, None]) if d == 0 else (s_of_lane[None, :] >= t[:, None])
        level[d][same_leaf & order] = len(sizes)
    half = (np.arange(LANES)[None, :] < C_DIM).astype(np.float32) * np.ones((c, 1), np.float32)
    return tri, level, half


def _block_diag(x, low_half):
    zero = jnp.zeros((x.shape[0], LANES), x.dtype)
    blocks = []
    for h in range(C_HEADS):
        tile = x[:, (h // 2) * LANES:(h // 2 + 1) * LANES]
        keep = tile * (low_half if h % 2 == 0 else 1 - low_half)
        blocks.append(jnp.concatenate([keep, zero] if h // 2 == 0 else [zero, keep], axis=1))
    return jnp.concatenate(blocks, axis=0)


def _block_diag_t(x, low_half):
    zero = jnp.zeros((LANES, LANES), x.dtype)
    rows = []
    for t in range(C_WIDTH // LANES):
        tile = x[:, t * LANES:(t + 1) * LANES]
        pair = jnp.concatenate([tile * low_half, tile * (1 - low_half)], axis=0).T
        rows.append(jnp.concatenate([pair, zero] if t == 0 else [zero, pair], axis=1))
    return jnp.concatenate(rows, axis=0)


def _ref_rows(b, bs, offset):
    parts = []
    for lo in range(0, b.shape[0], bs):
        parts.append(jnp.broadcast_to(b[lo + offset:lo + offset + 1, :], (bs, b.shape[1])))
    return jnp.concatenate(parts, axis=0)


def _level_operands(q, k, b, bs, d):
    half = bs // 2
    ref_off = half - 1 if d == 0 else half
    zeros = jnp.zeros((8, b.shape[1]), F32)
    qparts, kparts = [], []
    for r0 in range(0, b.shape[0], 8):
        lo = (r0 // bs) * bs
        ref = b[lo + ref_off:lo + ref_off + 1, :]
        upper = (r0 % bs) >= half
        rows = slice(r0, r0 + 8)
        if upper == (d == 0):
            qparts.append(q[rows] * jnp.exp2(b[rows] - ref))
            kparts.append(zeros)
        else:
            qparts.append(zeros)
            kparts.append(k[rows] * jnp.exp2(ref - b[rows]))
    return jnp.concatenate(qparts, axis=0).astype(BF16), jnp.concatenate(kparts, axis=0).astype(BF16)


def _hgrn_kernel(lb_ref, tri_ref, level_ref, half_ref, zq_f, zf_f, zv_f, zq_b, zf_b, zv_b,
                 of_ref, ob_ref, st_scr, *, layer, depth, nb):
    c, w = HGRN_CHUNK, C_WIDTH
    sizes = _hgrn_levels()
    n_lvl = len(sizes)

    @pl.when(pl.program_id(1) == 0)
    def _():
        st_scr[...] = jnp.zeros_like(st_scr)

    rows = [lb_ref[i:i + 1, :] for i in range(depth)]
    mx = functools.reduce(jnp.maximum, rows)
    es = [jnp.exp(r - mx) for r in rows]
    tot = functools.reduce(lambda a, b_: a + b_, es)
    ps = [e / tot for e in es]
    cum = functools.reduce(lambda a, b_: a + b_, ps[:layer + 1])
    one_minus_lb = 1.0 - jnp.clip(cum - ps[0], 0.0, 1.0)

    low_half = half_ref[...].astype(BF16)
    low_bool = half_ref[...] > 0.5
    owner = [[level_ref[d] == float(li) for li in range(n_lvl + 1)] for d in range(2)]
    chains = [(j, d) for j in range(nb) for d in range(2)]
    zrefs = {0: (zq_f, zf_f, zv_f), 1: (zq_b, zf_b, zv_b)}

    q, k, v, b, total = {}, {}, {}, {}, {}
    g_split = {}
    for ch in chains:
        j, d = ch
        zq, zf = zrefs[d][0][j], zrefs[d][1][j]
        v[ch] = zrefs[d][2][j]
        q[ch] = (zq * (C_DIM ** -0.5)) / (1.0 + jnp.exp2(zq * -LOG2_E))
        k[ch] = one_minus_lb * (1.0 - 1.0 / (1.0 + jnp.exp2(zf * -LOG2_E)))
        g = jnp.log2(jnp.maximum(1.0 - k[ch], FORGET_FLOOR))
        g_hi = g.astype(BF16)
        g_r = g - g_hi.astype(F32)
        g_mid = g_r.astype(BF16)
        g_lo = (g_r - g_mid.astype(F32)).astype(BF16)
        g_split[ch] = jnp.concatenate([g_hi, g_mid, g_lo], axis=1)
    for ch in chains:
        d = ch[1]
        bb = _dot(tri_ref[d], g_split[ch])
        b[ch] = bb[:, 0:w] + bb[:, w:2 * w] + bb[:, 2 * w:3 * w]
        total[ch] = b[ch][c - 1:c, :] if d == 0 else b[ch][0:1, :]

    parts = {ch: [] for ch in chains}
    for li in range(n_lvl + 1):
        for ch in chains:
            d = ch[1]
            if li < n_lvl:
                qp, kp = _level_operands(q[ch], k[ch], b[ch], sizes[li], d)
            else:
                ref = _ref_rows(b[ch], HGRN_LEAF, HGRN_LEAF // 2 - 1 if d == 0 else HGRN_LEAF // 2)
                arg = b[ch] - ref
                qp = (q[ch] * jnp.exp2(arg)).astype(BF16)
                kp = (k[ch] * jnp.exp2(-arg)).astype(BF16)
            parts[ch].append(_dot(qp, _block_diag_t(kp, low_half)))
    a = {}
    for ch in chains:
        acc = parts[ch][0]
        for li in range(1, n_lvl + 1):
            acc = jnp.where(owner[ch[1]][li], parts[ch][li], acc)
        a[ch] = acc.astype(BF16)

    o_intra, o_inter, upd = {}, {}, {}
    for ch in chains:
        j, d = ch
        o_intra[ch] = _dot(a[ch], _block_diag(v[ch], low_half))
        qe = (q[ch] * jnp.exp2(b[ch])).astype(BF16)
        o_inter[ch] = _dot(qe, _block_diag_t(st_scr[d, j].astype(BF16), low_half))
        kb = (k[ch] * jnp.exp2(total[ch] - b[ch])).astype(BF16)
        tiles = []
        for t in range(w // LANES):
            lanes = slice(t * LANES, (t + 1) * LANES)
            full = _dot_tn(v[ch][:, lanes], kb[:, lanes])
            tiles.append(jnp.where(low_bool, full[0:C_DIM], full[C_DIM:2 * C_DIM]))
        upd[ch] = jnp.concatenate(tiles, axis=1)
    for ch in chains:
        j, d = ch
        st_scr[d, j] = st_scr[d, j] * jnp.exp2(total[ch]) + upd[ch]
        (of_ref if d == 0 else ob_ref)[j] = o_intra[ch] + o_inter[ch]


def _hgrn(cq, cff, cfb, ci, lower_bounds, layer, nb):
    bsz, s, _ = cq.shape
    depth = lower_bounds.shape[0]
    n_chunks = s // HGRN_CHUNK
    c, w = HGRN_CHUNK, C_WIDTH
    tri, level, half = _hgrn_constants()
    consts = (jnp.asarray(tri, BF16), jnp.asarray(level), jnp.asarray(half))
    full = lambda a: pl.BlockSpec(a.shape, lambda g, i: (0,) * a.ndim)
    last = n_chunks - 1
    fwd = pl.BlockSpec((nb, c, w), lambda g, i: (g, i, 0))
    bwd = pl.BlockSpec((nb, c, w), lambda g, i: (g, last - i, 0))
    return pl.pallas_call(
        functools.partial(_hgrn_kernel, layer=layer, depth=depth, nb=nb),
        grid=(bsz // nb, n_chunks),
        in_specs=[full(lower_bounds)] + [full(a) for a in consts] + [fwd, fwd, fwd, bwd, bwd, bwd],
        out_specs=[fwd, bwd],
        out_shape=[jax.ShapeDtypeStruct((bsz, s, w), F32)] * 2,
        scratch_shapes=[pltpu.VMEM((2, nb, C_DIM, w), F32)],
        compiler_params=pltpu.CompilerParams(
            dimension_semantics=("arbitrary", "arbitrary"), vmem_limit_bytes=VMEM_LIMIT),
        name="hgrn2_bidir",
    )(lower_bounds, *consts, cq, cff, ci, cq, cfb, ci)


def _out_ffn_kernel(x_ref, mod_ref, oa_ref, ob_ref, cf_ref, cb_ref, zg_ref, cgain_ref,
                    wo_ref, w1_ref, w2_ref, fg_ref, o_ref, *, final, ff_chunk):
    oc = cf_ref[0] + cb_ref[0]
    ms = _split_dot(oc * oc, _group_ones(C_WIDTH, C_DIM))
    zg = zg_ref[0]
    oc = oc * lax.rsqrt(ms + EPS) * cgain_ref[...] * (zg * jax.nn.sigmoid(zg))
    mix = jnp.concatenate([oa_ref[0], ob_ref[0], oc.astype(BF16)], axis=-1)
    x1 = x_ref[0] + mod_ref[0, 2:3, :] * _dot(mix, wo_ref[0])
    h = (_rms_rows(x1) * (1.0 + mod_ref[0, 4:5, :]) + mod_ref[0, 3:4, :]).astype(BF16)
    d_ff = w1_ref.shape[2]
    acc = jnp.zeros_like(x1)
    for j in range(d_ff // ff_chunk):
        u = _dot(h, w1_ref[0, :, j * ff_chunk:(j + 1) * ff_chunk])
        u = jnp.square(jnp.maximum(u, 0.0)).astype(BF16)
        acc = acc + _dot(u, w2_ref[0, j * ff_chunk:(j + 1) * ff_chunk, :])
    x2 = x1 + mod_ref[0, 5:6, :] * acc
    if final:
        x2 = _rms_rows(x2) * fg_ref[...]
    o_ref[0] = x2


def _out_ffn(x, mod_l, oa, ob, oc_f, oc_b, cg, c_gain, w_out, w_ff1, w_ff2, final_gain, layer, final, tm):
    bsz, s, d = x.shape
    d_ff = w_ff1.shape[2]
    row_spec = lambda width: pl.BlockSpec((1, tm, width), lambda b, i: (b, i, 0))
    weight = lambda a: pl.BlockSpec((1,) + a.shape[1:], lambda b, i: (layer, 0, 0),
                                    pipeline_mode=pl.Buffered(1))
    return pl.pallas_call(
        functools.partial(_out_ffn_kernel, final=final, ff_chunk=1024),
        grid=(bsz, s // tm),
        in_specs=[
            row_spec(d),
            pl.BlockSpec((1, 6, d), lambda b, i: (b, 0, 0)),
            row_spec(A_WIDTH), row_spec(B_WIDTH), row_spec(C_WIDTH), row_spec(C_WIDTH), row_spec(C_WIDTH),
            pl.BlockSpec((1, C_WIDTH), lambda b, i: (0, 0)),
            weight(w_out), weight(w_ff1), weight(w_ff2),
            pl.BlockSpec((1, d), lambda b, i: (0, 0)),
        ],
        out_specs=row_spec(d),
        out_shape=jax.ShapeDtypeStruct((bsz, s, d), F32),
        compiler_params=pltpu.CompilerParams(
            dimension_semantics=("arbitrary", "arbitrary"), vmem_limit_bytes=VMEM_LIMIT),
        name="out_proj_ffn",
    )(x, mod_l, oa, ob, oc_f, oc_b, cg, c_gain, w_out, w_ff1, w_ff2, final_gain)


def _rope_tables(s):
    half = HEAD_DIM // 2
    inv = ROPE_THETA ** (-jnp.arange(0, half, 2, dtype=F32) / half)
    t = jnp.arange(s)
    ang_row = (t // GRID_W).astype(F32)[:, None] * inv[None, :]
    ang_col = (t % GRID_W).astype(F32)[:, None] * inv[None, :]
    ang_1d = t.astype(F32)[:, None] * inv[None, :]

    def pair(ang):
        return (jnp.concatenate([jnp.cos(ang), jnp.cos(ang)], axis=-1),
                jnp.concatenate([-jnp.sin(ang), jnp.sin(ang)], axis=-1))

    cr, sr = pair(ang_row)
    cc, sc = pair(ang_col)
    c1, s1 = pair(ang_1d)
    tile = lambda a: jnp.tile(a, (1, LANES // a.shape[1]))
    return (tile(jnp.concatenate([cr, cc], axis=-1)), tile(jnp.concatenate([sr, sc], axis=-1)),
            tile(c1), tile(s1))


def kernel(x, c, w_mod, b_mod, w_in, a_qk_norm, diff_lambda, diff_subln, hgrn_lower_bounds, hgrn_norm,
           w_out, w_ff1, w_ff2, final_norm):
    bsz, s, d = x.shape
    depth = w_in.shape[0]
    tm = min(512, s)
    tq = min(2 * Q_TILE, s)
    tables = _rope_tables(s)
    mod = _modulation(c, w_mod, b_mod).reshape(depth, bsz, 6, d)
    w_in_b, w_out_b, w_ff1_b, w_ff2_b = (w.astype(BF16) for w in (w_in, w_out, w_ff1, w_ff2))
    final_gain = final_norm.reshape(1, d)
    for l in range(depth):
        qk_gain = jnp.tile(a_qk_norm[l], (1, LANES // HEAD_DIM))
        qat, ka, vat, qbt, kb, vbt, cq, cff, cfb, ci, cg = _in_projection(
            x, mod[l], w_in_b, l, tables, qk_gain, tm)
        oa = _attention_a(qat, ka, vat, min(A_Q_TILES * Q_TILE, s))
        lam_init = 0.8 - 0.6 * math.exp(-0.3 * l)
        subln = jnp.tile(diff_subln[l].reshape(1, HEAD_DIM), (1, B_HEADS))
        ob = _attention_b(qbt, kb, vbt, diff_lambda[l], subln, lam_init, min(B_Q_TILES * Q_TILE, s))
        hg_gain = jnp.tile(hgrn_norm[l].reshape(1, C_DIM), (1, C_HEADS))
        oc_f, oc_b = _hgrn(cq, cff, cfb, ci, hgrn_lower_bounds, l, min(HGRN_BATCH, bsz))
        x = _out_ffn(x, mod[l], oa, ob, oc_f, oc_b, cg, hg_gain, w_out_b, w_ff1_b, w_ff2_b, final_gain,
                     l, l == depth - 1, tm)
    return x
```

```python
import functools
import math

import numpy as np
import jax
import jax.numpy as jnp
from jax import lax
from jax.experimental import pallas as pl
from jax.experimental.pallas import tpu as pltpu

F32 = jnp.float32
BF16 = jnp.bfloat16

HEAD_DIM = 64
GRID_W = 64
ROPE_THETA = 10000.0
EPS = 1e-6
FORGET_FLOOR = 1e-6
A_HEADS, A_KV_HEADS = 6, 2
B_HEADS, B_QK_DIM = 6, 32
C_HEADS, C_DIM = 4, 64
A_WIDTH = A_HEADS * HEAD_DIM
A_KV_WIDTH = A_KV_HEADS * HEAD_DIM
B_WIDTH = B_HEADS * HEAD_DIM
C_WIDTH = C_HEADS * C_DIM
LANES = 128
HGRN_CHUNK = 64
HGRN_LEAF = 8
HGRN_BATCH = 16
ONES_ROWS = 16
VT_ROWS = HEAD_DIM + ONES_ROWS
KV_CHUNK = 256
Q_TILE = 256
A_Q_TILES = 8
B_Q_TILES = 4
QK_AHEAD = 16
LOG2_E = math.log2(math.e)
VMEM_LIMIT = 56 * 1024 * 1024

OFF_AQ, OFF_AK, OFF_AV = 0, 384, 512
OFF_BQ, OFF_BK, OFF_BV = 640, 1024, 1408
OFF_C = 1792
C_DTYPES = (F32, F32, F32, BF16, F32)
IN_TOTAL = 3072


def _dot(a, b):
    return jnp.dot(a, b, preferred_element_type=F32)


def _dot_nt(a, b):
    return lax.dot_general(a, b, (((1,), (1,)), ((), ())), preferred_element_type=F32)


def _dot_tn(a, b):
    return lax.dot_general(a, b, (((0,), (0,)), ((), ())), preferred_element_type=F32)


def _split_dot(x, w_bf16):
    hi = x.astype(BF16)
    lo = (x - hi.astype(F32)).astype(BF16)
    return _dot(hi, w_bf16) + _dot(lo, w_bf16)


def _group_ones(n, group):
    r = lax.broadcasted_iota(jnp.int32, (n, n), 0) // group
    c = lax.broadcasted_iota(jnp.int32, (n, n), 1) // group
    return jnp.where(r == c, 1.0 / group, 0.0).astype(BF16)


def _rope(xb, cos, sin_signed):
    lane = lax.broadcasted_iota(jnp.int32, xb.shape, 1)
    low = (lane % 32) < 16
    partner = jnp.where(low, pltpu.roll(xb, LANES - 16, 1), pltpu.roll(xb, 16, 1))
    return xb * cos + partner * sin_signed


def _rms_rows(x):
    return x * lax.rsqrt(jnp.mean(x * x, axis=-1, keepdims=True) + EPS)


def _mod_kernel(c_ref, w_ref, b_ref, o_ref):
    c = c_ref[...]
    cond = (c * jax.nn.sigmoid(c)).astype(BF16)
    o_ref[0] = _dot(cond, w_ref[0].astype(BF16)) + b_ref[0]


def _modulation(c, w_mod, b_mod):
    depth, d, n = w_mod.shape
    bsz = c.shape[0]
    tn = 1024
    return pl.pallas_call(
        _mod_kernel,
        grid=(depth, n // tn),
        in_specs=[
            pl.BlockSpec((bsz, d), lambda l, j: (0, 0)),
            pl.BlockSpec((1, d, tn), lambda l, j: (l, 0, j)),
            pl.BlockSpec((1, 1, tn), lambda l, j: (l, 0, j)),
        ],
        out_specs=pl.BlockSpec((1, bsz, tn), lambda l, j: (l, 0, j)),
        out_shape=jax.ShapeDtypeStruct((depth, bsz, n), F32),
        compiler_params=pltpu.CompilerParams(
            dimension_semantics=("arbitrary", "arbitrary"), vmem_limit_bytes=VMEM_LIMIT),
        name="adaln_mod",
    )(c, w_mod, b_mod.reshape(depth, 1, n))


def _inproj_kernel(x_ref, mod_ref, w_ref, cosa_ref, sina_ref, cosb_ref, sinb_ref, gain_ref,
                   qat_ref, ka_ref, vat_ref, qbt_ref, kb_ref, vbt_ref,
                   cq_ref, cff_ref, cfb_ref, ci_ref, cg_ref, h_scr, z_scr):
    x = x_ref[0]
    h = _rms_rows(x) * (1.0 + mod_ref[0, 1:2, :]) + mod_ref[0, 0:1, :]
    h_scr[...] = h.astype(BF16)

    def project(j):
        z_scr[:, j * 1024:(j + 1) * 1024] = _dot(h_scr[...], w_ref[0, :, j * 1024:(j + 1) * 1024])

    project(0)
    project(1)

    ones = _group_ones(LANES, HEAD_DIM)
    cosa, sina = cosa_ref[...], sina_ref[...]
    cosb, sinb = cosb_ref[...], sinb_ref[...]

    def a_norm_rope(zb, gain):
        ms = _split_dot(zb * zb, ones)
        return _rope(zb * lax.rsqrt(ms + EPS) * gain, cosa, sina)

    def store_vt(vt_ref, first_head, blk_t):
        ones_rows = jnp.ones((ONES_ROWS, blk_t.shape[1]), BF16)
        for j in range(LANES // HEAD_DIM):
            r0 = (first_head + j) * VT_ROWS
            vt_ref[0, r0:r0 + HEAD_DIM, :] = blk_t[j * HEAD_DIM:(j + 1) * HEAD_DIM].astype(BF16)
            vt_ref[0, r0 + HEAD_DIM:r0 + VT_ROWS, :] = ones_rows

    for cblk in range(A_WIDTH // LANES):
        zb = z_scr[:, OFF_AQ + cblk * LANES:OFF_AQ + (cblk + 1) * LANES]
        y = a_norm_rope(zb, gain_ref[0:1, :]) * (HEAD_DIM ** -0.5 * LOG2_E)
        qat_ref[0, cblk * LANES:(cblk + 1) * LANES, :] = y.T.astype(BF16)
    ka_ref[0] = a_norm_rope(z_scr[:, OFF_AK:OFF_AK + LANES], gain_ref[1:2, :]).astype(BF16)
    store_vt(vat_ref, 0, z_scr[:, OFF_AV:OFF_AV + LANES].T)
    project(2)

    for cblk in range(B_WIDTH // LANES):
        lo = cblk * LANES
        zq = z_scr[:, OFF_BQ + lo:OFF_BQ + lo + LANES]
        qbt_ref[0, lo:lo + LANES, :] = (_rope(zq, cosb, sinb) * (B_QK_DIM ** -0.5 * LOG2_E)).T.astype(BF16)
    for cblk in range(B_WIDTH // LANES):
        lo = cblk * LANES
        zk = z_scr[:, OFF_BK + lo:OFF_BK + lo + LANES]
        kb_ref[0, :, lo:lo + LANES] = _rope(zk, cosb, sinb).astype(BF16)
        store_vt(vbt_ref, cblk * (LANES // HEAD_DIM), z_scr[:, OFF_BV + lo:OFF_BV + lo + LANES].T)
    for j, ref in enumerate((cq_ref, cff_ref, cfb_ref, ci_ref, cg_ref)):
        ref[0] = z_scr[:, OFF_C + j * C_WIDTH:OFF_C + (j + 1) * C_WIDTH].astype(ref.dtype)


def _in_projection(x, mod_l, w_in_bf16, layer, tables, qk_gain, tm):
    bsz, s, d = x.shape
    cosa, sina, cosb, sinb = tables
    row_spec = lambda width: pl.BlockSpec((1, tm, width), lambda b, i: (b, i, 0))
    tab_spec = pl.BlockSpec((tm, LANES), lambda b, i: (i, 0))
    col_spec = lambda height: pl.BlockSpec((1, height, tm), lambda b, i: (b, 0, i))
    return pl.pallas_call(
        _inproj_kernel,
        grid=(bsz, s // tm),
        in_specs=[
            row_spec(d),
            pl.BlockSpec((1, 6, d), lambda b, i: (b, 0, 0)),
            pl.BlockSpec((1, d, IN_TOTAL), lambda b, i: (layer, 0, 0)),
            tab_spec, tab_spec, tab_spec, tab_spec,
            pl.BlockSpec((2, LANES), lambda b, i: (0, 0)),
        ],
        out_specs=[
            col_spec(A_WIDTH), row_spec(A_KV_WIDTH), col_spec(A_KV_HEADS * VT_ROWS),
            col_spec(B_WIDTH), row_spec(B_WIDTH), col_spec(B_HEADS * VT_ROWS),
        ] + [row_spec(C_WIDTH)] * len(C_DTYPES),
        out_shape=[
            jax.ShapeDtypeStruct((bsz, A_WIDTH, s), BF16),
            jax.ShapeDtypeStruct((bsz, s, A_KV_WIDTH), BF16),
            jax.ShapeDtypeStruct((bsz, A_KV_HEADS * VT_ROWS, s), BF16),
            jax.ShapeDtypeStruct((bsz, B_WIDTH, s), BF16),
            jax.ShapeDtypeStruct((bsz, s, B_WIDTH), BF16),
            jax.ShapeDtypeStruct((bsz, B_HEADS * VT_ROWS, s), BF16),
        ] + [jax.ShapeDtypeStruct((bsz, s, C_WIDTH), dt) for dt in C_DTYPES],
        scratch_shapes=[pltpu.VMEM((tm, d), BF16), pltpu.VMEM((tm, IN_TOTAL), F32)],
        compiler_params=pltpu.CompilerParams(
            dimension_semantics=("arbitrary", "arbitrary"), vmem_limit_bytes=VMEM_LIMIT),
        name="in_projection",
    )(x, mod_l, w_in_bf16, cosa, sina, cosb, sinb, qk_gain)


def _place_rows(block, start, total):
    parts = []
    if start:
        parts.append(jnp.zeros((start, block.shape[1]), block.dtype))
    parts.append(block)
    rest = total - start - block.shape[0]
    if rest:
        parts.append(jnp.zeros((rest, block.shape[1]), block.dtype))
    return jnp.concatenate(parts, axis=0) if len(parts) > 1 else block


def _attention_items(items, k_ref, vt_ref, qz_scr, m_scr, acc_scr, exp_dtype=F32):
    s = k_ref.shape[1]
    tq = acc_scr.shape[2]
    m_scr[...] = jnp.full(m_scr.shape, -1e30, F32)
    acc_scr[...] = jnp.zeros_like(acc_scr)

    units = [(i, c) for c in range(s // KV_CHUNK) for i in range(len(items))]

    def scores(u):
        i, c = units[u]
        k_off = items[i][0]
        return _dot(k_ref[0, c * KV_CHUNK:(c + 1) * KV_CHUNK, k_off:k_off + LANES], qz_scr[i])

    pending = [scores(u) for u in range(min(QK_AHEAD, len(units)))]
    for u, (i, c) in enumerate(units):
        v_off = items[i][1]
        st = pending.pop(0)
        m_old = m_scr[i]
        part = jnp.max(st.reshape(KV_CHUNK // 8, 8, tq), axis=0)
        m_new = jnp.maximum(m_old, jnp.max(part, axis=0, keepdims=True))
        m_scr[i] = m_new
        p = jnp.exp2((st - m_new[0:1]).astype(exp_dtype)).astype(BF16)
        d = _dot(vt_ref[0, v_off:v_off + VT_ROWS, c * KV_CHUNK:(c + 1) * KV_CHUNK], p)
        if u + QK_AHEAD < len(units):
            pending.append(scores(u + QK_AHEAD))
        acc_scr[i] = acc_scr[i] * jnp.exp2(m_old[0:1] - m_new[0:1]) + d
    results = []
    for i in range(len(items)):
        acc = acc_scr[i]
        results.append(acc[0:HEAD_DIM] * (1.0 / acc[HEAD_DIM:HEAD_DIM + 1]))
    return results


def _attention_scratch(heads, tq):
    n_items = heads * (tq // Q_TILE)
    return [pltpu.VMEM((n_items, LANES, Q_TILE), BF16), pltpu.VMEM((n_items, 8, Q_TILE), F32),
            pltpu.VMEM((n_items, VT_ROWS, Q_TILE), F32)]


def _attn_a_kernel(qt_ref, k_ref, vt_ref, o_ref, qz_scr, m_scr, acc_scr):
    group = A_HEADS // A_KV_HEADS
    n_tiles = qt_ref.shape[2] // Q_TILE
    items = []
    for t in range(n_tiles):
        cols = slice(t * Q_TILE, (t + 1) * Q_TILE)
        for h in range(A_HEADS):
            g = h // group
            qz_scr[len(items)] = _place_rows(qt_ref[0, h * HEAD_DIM:(h + 1) * HEAD_DIM, cols], g * HEAD_DIM, LANES)
            items.append((0, g * VT_ROWS))
    outs = _attention_items(items, k_ref, vt_ref, qz_scr, m_scr, acc_scr)
    for t in range(n_tiles):
        tile = outs[t * A_HEADS:(t + 1) * A_HEADS]
        o_ref[0, t * Q_TILE:(t + 1) * Q_TILE, :] = jnp.concatenate([o.T for o in tile], axis=-1).astype(BF16)


def _attention_a(qat, ka, vat, tq):
    bsz, s, _ = ka.shape
    return pl.pallas_call(
        _attn_a_kernel,
        grid=(bsz, s // tq),
        in_specs=[
            pl.BlockSpec((1, A_WIDTH, tq), lambda b, i: (b, 0, i)),
            pl.BlockSpec((1, s, A_KV_WIDTH), lambda b, i: (b, 0, 0)),
            pl.BlockSpec((1, A_KV_HEADS * VT_ROWS, s), lambda b, i: (b, 0, 0)),
        ],
        out_specs=pl.BlockSpec((1, tq, A_WIDTH), lambda b, i: (b, i, 0)),
        out_shape=jax.ShapeDtypeStruct((bsz, s, A_WIDTH), BF16),
        scratch_shapes=_attention_scratch(A_HEADS, tq),
        compiler_params=pltpu.CompilerParams(
            dimension_semantics=("arbitrary", "arbitrary"), vmem_limit_bytes=VMEM_LIMIT),
        name="attn_axial_gqa",
    )(qat, ka, vat)


def _attn_b_kernel(lam_ref, gain_ref, qt_ref, k_ref, vt_ref, o_ref, qz_scr, m_scr, acc_scr, *, lam_init):
    lp = lam_ref[...]
    lam = (jnp.exp(jnp.sum(lp[0:1] * lp[1:2], axis=-1, keepdims=True))
           - jnp.exp(jnp.sum(lp[2:3] * lp[3:4], axis=-1, keepdims=True)) + lam_init)
    n_tiles = qt_ref.shape[2] // Q_TILE
    items = []
    for t in range(n_tiles):
        cols = slice(t * Q_TILE, (t + 1) * Q_TILE)
        for h in range(B_HEADS):
            pair = (h // 2) * LANES
            for comp in range(2):
                r0 = h * HEAD_DIM + comp * B_QK_DIM
                qz_scr[len(items)] = _place_rows(qt_ref[0, r0:r0 + B_QK_DIM, cols], r0 - pair, LANES)
                items.append((pair, h * VT_ROWS))
    comps = _attention_items(items, k_ref, vt_ref, qz_scr, m_scr, acc_scr, exp_dtype=BF16)
    for t in range(n_tiles):
        outs = []
        for h in range(B_HEADS):
            i = (t * B_HEADS + h) * 2
            o = comps[i] - lam * comps[i + 1]
            o = o * lax.rsqrt(jnp.mean(o * o, axis=0, keepdims=True) + EPS)
            outs.append(o.T)
        y = jnp.concatenate(outs, axis=-1) * gain_ref[...] * (1.0 - lam_init)
        o_ref[0, t * Q_TILE:(t + 1) * Q_TILE, :] = y.astype(BF16)


def _attention_b(qbt, kb, vbt, lam_params, subln_gain, lam_init, tq):
    bsz, s, _ = kb.shape
    return pl.pallas_call(
        functools.partial(_attn_b_kernel, lam_init=lam_init),
        grid=(bsz, s // tq),
        in_specs=[
            pl.BlockSpec((4, B_QK_DIM), lambda b, i: (0, 0)),
            pl.BlockSpec((1, B_WIDTH), lambda b, i: (0, 0)),
            pl.BlockSpec((1, B_WIDTH, tq), lambda b, i: (b, 0, i)),
            pl.BlockSpec((1, s, B_WIDTH), lambda b, i: (b, 0, 0)),
            pl.BlockSpec((1, B_HEADS * VT_ROWS, s), lambda b, i: (b, 0, 0)),
        ],
        out_specs=pl.BlockSpec((1, tq, B_WIDTH), lambda b, i: (b, i, 0)),
        out_shape=jax.ShapeDtypeStruct((bsz, s, B_WIDTH), BF16),
        scratch_shapes=_attention_scratch(2 * B_HEADS, tq),
        compiler_params=pltpu.CompilerParams(
            dimension_semantics=("arbitrary", "arbitrary"), vmem_limit_bytes=VMEM_LIMIT),
        name="attn_differential",
    )(lam_params, subln_gain, qbt, kb, vbt)


def _hgrn_levels():
    sizes = []
    bs = HGRN_CHUNK
    while bs > HGRN_LEAF:
        sizes.append(bs)
        bs //= 2
    return sizes


def _hgrn_constants():
    c, w = HGRN_CHUNK, C_WIDTH
    t = np.arange(c)
    tri = np.stack([(t[:, None] >= t[None, :]), (t[:, None] <= t[None, :])]).astype(np.float32)
    s_of_lane = np.arange(w) % C_DIM
    sizes = _hgrn_levels()
    level = np.full((2, c, w), float(len(sizes) + 1), np.float32)
    for d in range(2):
        for li, bs in enumerate(sizes):
            upper = (t % bs) >= bs // 2
            qrow = upper if d == 0 else ~upper
            same = (t[:, None] // bs) == (s_of_lane[None, :] // bs)
            cross = qrow[:, None] & ~qrow[s_of_lane][None, :]
            level[d][same & cross] = li
        same_leaf = (t[:, None] // HGRN_LEAF) == (s_of_lane[None, :] // HGRN_LEAF)
        order = (s_of_lane[None, :] <= t[:, None]) if d == 0 else (s_of_lane[None, :] >= t[:, None])
        level[d][same_leaf & order] = len(sizes)
    half = (np.arange(LANES)[None, :] < C_DIM).astype(np.float32) * np.ones((c, 1), np.float32)
    return tri, level, half


def _block_diag(x, low_half):
    zero = jnp.zeros((x.shape[0], LANES), x.dtype)
    blocks = []
    for h in range(C_HEADS):
        tile = x[:, (h // 2) * LANES:(h // 2 + 1) * LANES]
        keep = tile * (low_half if h % 2 == 0 else 1 - low_half)
        blocks.append(jnp.concatenate([keep, zero] if h // 2 == 0 else [zero, keep], axis=1))
    return jnp.concatenate(blocks, axis=0)


def _block_diag_t(x, low_half):
    zero = jnp.zeros((LANES, LANES), x.dtype)
    rows = []
    for t in range(C_WIDTH // LANES):
        tile = x[:, t * LANES:(t + 1) * LANES]
        pair = jnp.concatenate([tile * low_half, tile * (1 - low_half)], axis=0).T
        rows.append(jnp.concatenate([pair, zero] if t == 0 else [zero, pair], axis=1))
    return jnp.concatenate(rows, axis=0)


def _ref_rows(b, bs, offset):
    parts = []
    for lo in range(0, b.shape[0], bs):
        parts.append(jnp.broadcast_to(b[lo + offset:lo + offset + 1, :], (bs, b.shape[1])))
    return jnp.concatenate(parts, axis=0)


def _level_operands(q, k, b, bs, d):
    half = bs // 2
    ref_off = half - 1 if d == 0 else half
    zeros = jnp.zeros((8, b.shape[1]), F32)
    qparts, kparts = [], []
    for r0 in range(0, b.shape[0], 8):
        lo = (r0 // bs) * bs
        ref = b[lo + ref_off:lo + ref_off + 1, :]
        upper = (r0 % bs) >= half
        rows = slice(r0, r0 + 8)
        if upper == (d == 0):
            qparts.append(q[rows] * jnp.exp2(b[rows] - ref))
            kparts.append(zeros)
        else:
            qparts.append(zeros)
            kparts.append(k[rows] * jnp.exp2(ref - b[rows]))
    return jnp.concatenate(qparts, axis=0).astype(BF16), jnp.concatenate(kparts, axis=0).astype(BF16)


def _hgrn_kernel(lb_ref, tri_ref, level_ref, half_ref, zq_f, zf_f, zv_f, zq_b, zf_b, zv_b,
                 of_ref, ob_ref, st_scr, *, layer, depth, nb):
    c, w = HGRN_CHUNK, C_WIDTH
    sizes = _hgrn_levels()
    n_lvl = len(sizes)

    @pl.when(pl.program_id(1) == 0)
    def _():
        st_scr[...] = jnp.zeros_like(st_scr)

    rows = [lb_ref[i:i + 1, :] for i in range(depth)]
    mx = functools.reduce(jnp.maximum, rows)
    es = [jnp.exp(r - mx) for r in rows]
    tot = functools.reduce(lambda a, b_: a + b_, es)
    ps = [e / tot for e in es]
    cum = functools.reduce(lambda a, b_: a + b_, ps[:layer + 1])
    one_minus_lb = 1.0 - jnp.clip(cum - ps[0], 0.0, 1.0)

    low_half = half_ref[...].astype(BF16)
    low_bool = half_ref[...] > 0.5
    owner = [[level_ref[d] == float(li) for li in range(n_lvl + 1)] for d in range(2)]
    chains = [(j, d) for j in range(nb) for d in range(2)]
    zrefs = {0: (zq_f, zf_f, zv_f), 1: (zq_b, zf_b, zv_b)}

    q, k, v, b, total = {}, {}, {}, {}, {}
    g_split = {}
    for ch in chains:
        j, d = ch
        zq, zf = zrefs[d][0][j], zrefs[d][1][j]
        v[ch] = zrefs[d][2][j]
        q[ch] = (zq * (C_DIM ** -0.5)) / (1.0 + jnp.exp2(zq * -LOG2_E))
        k[ch] = one_minus_lb * (1.0 - 1.0 / (1.0 + jnp.exp2(zf * -LOG2_E)))
        g = jnp.log2(jnp.maximum(1.0 - k[ch], FORGET_FLOOR))
        g_hi = g.astype(BF16)
        g_r = g - g_hi.astype(F32)
        g_mid = g_r.astype(BF16)
        g_lo = (g_r - g_mid.astype(F32)).astype(BF16)
        g_split[ch] = jnp.concatenate([g_hi, g_mid, g_lo], axis=1)
    for ch in chains:
        d = ch[1]
        bb = _dot(tri_ref[d], g_split[ch])
        b[ch] = bb[:, 0:w] + bb[:, w:2 * w] + bb[:, 2 * w:3 * w]
        total[ch] = b[ch][c - 1:c, :] if d == 0 else b[ch][0:1, :]

    parts = {ch: [] for ch in chains}
    for li in range(n_lvl + 1):
        for ch in chains:
            d = ch[1]
            if li < n_lvl:
                qp, kp = _level_operands(q[ch], k[ch], b[ch], sizes[li], d)
            else:
                ref = _ref_rows(b[ch], HGRN_LEAF, HGRN_LEAF // 2 - 1 if d == 0 else HGRN_LEAF // 2)
                arg = b[ch] - ref
                qp = (q[ch] * jnp.exp2(arg)).astype(BF16)
                kp = (k[ch] * jnp.exp2(-arg)).astype(BF16)
            parts[ch].append(_dot(qp, _block_diag_t(kp, low_half)))
    a = {}
    for ch in chains:
        acc = parts[ch][0]
        for li in range(1, n_lvl + 1):
            acc = jnp.where(owner[ch[1]][li], parts[ch][li], acc)
        a[ch] = acc.astype(BF16)

    o_intra, o_inter, upd = {}, {}, {}
    for ch in chains:
        j, d = ch
        o_intra[ch] = _dot(a[ch], _block_diag(v[ch], low_half))
        qe = (q[ch] * jnp.exp2(b[ch])).astype(BF16)
        o_inter[ch] = _dot(qe, _block_diag_t(st_scr[d, j].astype(BF16), low_half))
        kb = (k[ch] * jnp.exp2(total[ch] - b[ch])).astype(BF16)
        tiles = []
        for t in range(w // LANES):
            lanes = slice(t * LANES, (t + 1) * LANES)
            full = _dot_tn(v[ch][:, lanes], kb[:, lanes])
            tiles.append(jnp.where(low_bool, full[0:C_DIM], full[C_DIM:2 * C_DIM]))
        upd[ch] = jnp.concatenate(tiles, axis=1)
    for ch in chains:
        j, d = ch
        st_scr[d, j] = st_scr[d, j] * jnp.exp2(total[ch]) + upd[ch]
        (of_ref if d == 0 else ob_ref)[j] = o_intra[ch] + o_inter[ch]


def _hgrn(cq, cff, cfb, ci, lower_bounds, layer, nb):
    bsz, s, _ = cq.shape
    depth = lower_bounds.shape[0]
    n_chunks = s // HGRN_CHUNK
    c, w = HGRN_CHUNK, C_WIDTH
    tri, level, half = _hgrn_constants()
    consts = (jnp.asarray(tri, BF16), jnp.asarray(level), jnp.asarray(half))
    full = lambda a: pl.BlockSpec(a.shape, lambda g, i: (0,) * a.ndim)
    last = n_chunks - 1
    fwd = pl.BlockSpec((nb, c, w), lambda g, i: (g, i, 0))
    bwd = pl.BlockSpec((nb, c, w), lambda g, i: (g, last - i, 0))
    return pl.pallas_call(
        functools.partial(_hgrn_kernel, layer=layer, depth=depth, nb=nb),
        grid=(bsz // nb, n_chunks),
        in_specs=[full(lower_bounds)] + [full(a) for a in consts] + [fwd, fwd, fwd, bwd, bwd, bwd],
        out_specs=[fwd, bwd],
        out_shape=[jax.ShapeDtypeStruct((bsz, s, w), F32)] * 2,
        scratch_shapes=[pltpu.VMEM((2, nb, C_DIM, w), F32)],
        compiler_params=pltpu.CompilerParams(
            dimension_semantics=("arbitrary", "arbitrary"), vmem_limit_bytes=VMEM_LIMIT),
        name="hgrn2_bidir",
    )(lower_bounds, *consts, cq, cff, ci, cq, cfb, ci)


def _out_ffn_kernel(x_ref, mod_ref, oa_ref, ob_ref, cf_ref, cb_ref, zg_ref, cgain_ref,
                    wo_ref, w1_ref, w2_ref, fg_ref, o_ref, *, final, ff_chunk):
    oc = cf_ref[0] + cb_ref[0]
    ms = _split_dot(oc * oc, _group_ones(C_WIDTH, C_DIM))
    zg = zg_ref[0]
    oc = oc * lax.rsqrt(ms + EPS) * cgain_ref[...] * (zg * jax.nn.sigmoid(zg))
    mix = jnp.concatenate([oa_ref[0], ob_ref[0], oc.astype(BF16)], axis=-1)
    x1 = x_ref[0] + mod_ref[0, 2:3, :] * _dot(mix, wo_ref[0])
    h = (_rms_rows(x1) * (1.0 + mod_ref[0, 4:5, :]) + mod_ref[0, 3:4, :]).astype(BF16)
    d_ff = w1_ref.shape[2]
    acc = jnp.zeros_like(x1)
    for j in range(d_ff // ff_chunk):
        u = _dot(h, w1_ref[0, :, j * ff_chunk:(j + 1) * ff_chunk])
        u = jnp.square(jnp.maximum(u, 0.0)).astype(BF16)
        acc = acc + _dot(u, w2_ref[0, j * ff_chunk:(j + 1) * ff_chunk, :])
    x2 = x1 + mod_ref[0, 5:6, :] * acc
    if final:
        x2 = _rms_rows(x2) * fg_ref[...]
    o_ref[0] = x2


def _out_ffn(x, mod_l, oa, ob, oc_f, oc_b, cg, c_gain, w_out, w_ff1, w_ff2, final_gain, layer, final, tm):
    bsz, s, d = x.shape
    d_ff = w_ff1.shape[2]
    row_spec = lambda width: pl.BlockSpec((1, tm, width), lambda b, i: (b, i, 0))
    weight = lambda a: pl.BlockSpec((1,) + a.shape[1:], lambda b, i: (layer, 0, 0),
                                    pipeline_mode=pl.Buffered(1))
    return pl.pallas_call(
        functools.partial(_out_ffn_kernel, final=final, ff_chunk=1024),
        grid=(bsz, s // tm),
        in_specs=[
            row_spec(d),
            pl.BlockSpec((1, 6, d), lambda b, i: (b, 0, 0)),
            row_spec(A_WIDTH), row_spec(B_WIDTH), row_spec(C_WIDTH), row_spec(C_WIDTH), row_spec(C_WIDTH),
            pl.BlockSpec((1, C_WIDTH), lambda b, i: (0, 0)),
            weight(w_out), weight(w_ff1), weight(w_ff2),
            pl.BlockSpec((1, d), lambda b, i: (0, 0)),
        ],
        out_specs=row_spec(d),
        out_shape=jax.ShapeDtypeStruct((bsz, s, d), F32),
        compiler_params=pltpu.CompilerParams(
            dimension_semantics=("arbitrary", "arbitrary"), vmem_limit_bytes=VMEM_LIMIT),
        name="out_proj_ffn",
    )(x, mod_l, oa, ob, oc_f, oc_b, cg, c_gain, w_out, w_ff1, w_ff2, final_gain)


def _rope_tables(s):
    half = HEAD_DIM // 2
    inv = ROPE_THETA ** (-jnp.arange(0, half, 2, dtype=F32) / half)
    t = jnp.arange(s)
    ang_row = (t // GRID_W).astype(F32)[:, None] * inv[None, :]
    ang_col = (t % GRID_W).astype(F32)[:, None] * inv[None, :]
    ang_1d = t.astype(F32)[:, None] * inv[None, :]

    def pair(ang):
        return (jnp.concatenate([jnp.cos(ang), jnp.cos(ang)], axis=-1),
                jnp.concatenate([-jnp.sin(ang), jnp.sin(ang)], axis=-1))

    cr, sr = pair(ang_row)
    cc, sc = pair(ang_col)
    c1, s1 = pair(ang_1d)
    tile = lambda a: jnp.tile(a, (1, LANES // a.shape[1]))
    return (tile(jnp.concatenate([cr, cc], axis=-1)), tile(jnp.concatenate([sr, sc], axis=-1)),
            tile(c1), tile(s1))


def kernel(x, c, w_mod, b_mod, w_in, a_qk_norm, diff_lambda, diff_subln, hgrn_lower_bounds, hgrn_norm,
           w_out, w_ff1, w_ff2, final_norm):
    bsz, s, d = x.shape
    depth = w_in.shape[0]
    tm = min(512, s)
    tq = min(2 * Q_TILE, s)
    tables = _rope_tables(s)
    mod = _modulation(c, w_mod, b_mod).reshape(depth, bsz, 6, d)
    w_in_b, w_out_b, w_ff1_b, w_ff2_b = (w.astype(BF16) for w in (w_in, w_out, w_ff1, w_ff2))
    final_gain = final_norm.reshape(1, d)
    for l in range(depth):
        qk_gain = jnp.tile(a_qk_norm[l], (1, LANES // HEAD_DIM))
        qat, ka, vat, qbt, kb, vbt, cq, cff, cfb, ci, cg = _in_projection(
            x, mod[l], w_in_b, l, tables, qk_gain, tm)
        oa = _attention_a(qat, ka, vat, min(A_Q_TILES * Q_TILE, s))
        lam_init = 0.8 - 0.6 * math.exp(-0.3 * l)
        subln = jnp.tile(diff_subln[l].reshape(1, HEAD_DIM), (1, B_HEADS))
        ob = _attention_b(qbt, kb, vbt, diff_lambda[l], subln, lam_init, min(B_Q_TILES * Q_TILE, s))
        hg_gain = jnp.tile(hgrn_norm[l].reshape(1, C_DIM), (1, C_HEADS))
        oc_f, oc_b = _hgrn(cq, cff, cfb, ci, hgrn_lower_bounds, l, min(HGRN_BATCH, bsz))
        x = _out_ffn(x, mod[l], oa, ob, oc_f, oc_b, cg, hg_gain, w_out_b, w_ff1_b, w_ff2_b, final_gain,
                     l, l == depth - 1, tm)
    return x
```

```python
import functools
import math

import numpy as np
import jax
import jax.numpy as jnp
from jax import lax
from jax.experimental import pallas as pl
from jax.experimental.pallas import tpu as pltpu

F32 = jnp.float32
BF16 = jnp.bfloat16

HEAD_DIM = 64
GRID_W = 64
ROPE_THETA = 10000.0
EPS = 1e-6
FORGET_FLOOR = 1e-6
A_HEADS, A_KV_HEADS = 6, 2
B_HEADS, B_QK_DIM = 6, 32
C_HEADS, C_DIM = 4, 64
A_WIDTH = A_HEADS * HEAD_DIM
A_KV_WIDTH = A_KV_HEADS * HEAD_DIM
B_WIDTH = B_HEADS * HEAD_DIM
C_WIDTH = C_HEADS * C_DIM
LANES = 128
HGRN_CHUNK = 64
HGRN_LEAF = 8
HGRN_BATCH = 16
ONES_ROWS = 16
VT_ROWS = HEAD_DIM + ONES_ROWS
KV_CHUNK = 256
Q_TILE = 256
A_Q_TILES = 4
B_Q_TILES = 4
QK_AHEAD = 16
LOG2_E = math.log2(math.e)
VMEM_LIMIT = 56 * 1024 * 1024

OFF_AQ, OFF_AK, OFF_AV = 0, 384, 512
OFF_BQ, OFF_BK, OFF_BV = 640, 1024, 1408
OFF_C = 1792
C_DTYPES = (F32, F32, F32, BF16, F32)
IN_TOTAL = 3072


def _dot(a, b):
    return jnp.dot(a, b, preferred_element_type=F32)


def _dot_nt(a, b):
    return lax.dot_general(a, b, (((1,), (1,)), ((), ())), preferred_element_type=F32)


def _dot_tn(a, b):
    return lax.dot_general(a, b, (((0,), (0,)), ((), ())), preferred_element_type=F32)


def _split_dot(x, w_bf16):
    hi = x.astype(BF16)
    lo = (x - hi.astype(F32)).astype(BF16)
    return _dot(hi, w_bf16) + _dot(lo, w_bf16)


def _group_ones(n, group):
    r = lax.broadcasted_iota(jnp.int32, (n, n), 0) // group
    c = lax.broadcasted_iota(jnp.int32, (n, n), 1) // group
    return jnp.where(r == c, 1.0 / group, 0.0).astype(BF16)


def _rope(xb, cos, sin_signed):
    lane = lax.broadcasted_iota(jnp.int32, xb.shape, 1)
    low = (lane % 32) < 16
    partner = jnp.where(low, pltpu.roll(xb, LANES - 16, 1), pltpu.roll(xb, 16, 1))
    return xb * cos + partner * sin_signed


def _rms_rows(x):
    return x * lax.rsqrt(jnp.mean(x * x, axis=-1, keepdims=True) + EPS)


def _mod_kernel(c_ref, w_ref, b_ref, o_ref):
    c = c_ref[...]
    cond = (c * jax.nn.sigmoid(c)).astype(BF16)
    o_ref[0] = _dot(cond, w_ref[0].astype(BF16)) + b_ref[0]


def _modulation(c, w_mod, b_mod):
    depth, d, n = w_mod.shape
    bsz = c.shape[0]
    tn = 1024
    return pl.pallas_call(
        _mod_kernel,
        grid=(depth, n // tn),
        in_specs=[
            pl.BlockSpec((bsz, d), lambda l, j: (0, 0)),
            pl.BlockSpec((1, d, tn), lambda l, j: (l, 0, j)),
            pl.BlockSpec((1, 1, tn), lambda l, j: (l, 0, j)),
        ],
        out_specs=pl.BlockSpec((1, bsz, tn), lambda l, j: (l, 0, j)),
        out_shape=jax.ShapeDtypeStruct((depth, bsz, n), F32),
        compiler_params=pltpu.CompilerParams(
            dimension_semantics=("arbitrary", "arbitrary"), vmem_limit_bytes=VMEM_LIMIT),
        name="adaln_mod",
    )(c, w_mod, b_mod.reshape(depth, 1, n))


def _inproj_kernel(x_ref, mod_ref, w_ref, cosa_ref, sina_ref, cosb_ref, sinb_ref, gain_ref,
                   qat_ref, ka_ref, vat_ref, qbt_ref, kb_ref, vbt_ref,
                   cq_ref, cff_ref, cfb_ref, ci_ref, cg_ref, h_scr, z_scr):
    x = x_ref[0]
    h = _rms_rows(x) * (1.0 + mod_ref[0, 1:2, :]) + mod_ref[0, 0:1, :]
    h_scr[...] = h.astype(BF16)

    def project(j):
        z_scr[:, j * 1024:(j + 1) * 1024] = _dot(h_scr[...], w_ref[0, :, j * 1024:(j + 1) * 1024])

    project(0)
    project(1)

    ones = _group_ones(LANES, HEAD_DIM)
    cosa, sina = cosa_ref[...], sina_ref[...]
    cosb, sinb = cosb_ref[...], sinb_ref[...]

    def a_norm_rope(zb, gain):
        ms = _split_dot(zb * zb, ones)
        return _rope(zb * lax.rsqrt(ms + EPS) * gain, cosa, sina)

    def store_vt(vt_ref, first_head, blk_t):
        ones_rows = jnp.ones((ONES_ROWS, blk_t.shape[1]), BF16)
        for j in range(LANES // HEAD_DIM):
            r0 = (first_head + j) * VT_ROWS
            vt_ref[0, r0:r0 + HEAD_DIM, :] = blk_t[j * HEAD_DIM:(j + 1) * HEAD_DIM].astype(BF16)
            vt_ref[0, r0 + HEAD_DIM:r0 + VT_ROWS, :] = ones_rows

    for cblk in range(A_WIDTH // LANES):
        zb = z_scr[:, OFF_AQ + cblk * LANES:OFF_AQ + (cblk + 1) * LANES]
        y = a_norm_rope(zb, gain_ref[0:1, :]) * (HEAD_DIM ** -0.5 * LOG2_E)
        qat_ref[0, cblk * LANES:(cblk + 1) * LANES, :] = y.T.astype(BF16)
    ka_ref[0, 0] = a_norm_rope(z_scr[:, OFF_AK:OFF_AK + LANES], gain_ref[1:2, :]).astype(BF16)
    store_vt(vat_ref, 0, z_scr[:, OFF_AV:OFF_AV + LANES].T)
    project(2)

    for cblk in range(B_WIDTH // LANES):
        lo = cblk * LANES
        zq = z_scr[:, OFF_BQ + lo:OFF_BQ + lo + LANES]
        qbt_ref[0, lo:lo + LANES, :] = (_rope(zq, cosb, sinb) * (B_QK_DIM ** -0.5 * LOG2_E)).T.astype(BF16)
    for cblk in range(B_WIDTH // LANES):
        lo = cblk * LANES
        zk = z_scr[:, OFF_BK + lo:OFF_BK + lo + LANES]
        kb_ref[0, cblk] = _rope(zk, cosb, sinb).astype(BF16)
        store_vt(vbt_ref, cblk * (LANES // HEAD_DIM), z_scr[:, OFF_BV + lo:OFF_BV + lo + LANES].T)
    for j, ref in enumerate((cq_ref, cff_ref, cfb_ref, ci_ref, cg_ref)):
        ref[0] = z_scr[:, OFF_C + j * C_WIDTH:OFF_C + (j + 1) * C_WIDTH].astype(ref.dtype)


def _in_projection(x, mod_l, w_in_bf16, layer, tables, qk_gain, tm):
    bsz, s, d = x.shape
    cosa, sina, cosb, sinb = tables
    row_spec = lambda width: pl.BlockSpec((1, tm, width), lambda b, i: (b, i, 0))
    tab_spec = pl.BlockSpec((tm, LANES), lambda b, i: (i, 0))
    col_spec = lambda height: pl.BlockSpec((1, height, tm), lambda b, i: (b, 0, i))
    key_spec = lambda blocks: pl.BlockSpec((1, blocks, tm, LANES), lambda b, i: (b, 0, i, 0))
    return pl.pallas_call(
        _inproj_kernel,
        grid=(bsz, s // tm),
        in_specs=[
            row_spec(d),
            pl.BlockSpec((1, 6, d), lambda b, i: (b, 0, 0)),
            pl.BlockSpec((1, d, IN_TOTAL), lambda b, i: (layer, 0, 0)),
            tab_spec, tab_spec, tab_spec, tab_spec,
            pl.BlockSpec((2, LANES), lambda b, i: (0, 0)),
        ],
        out_specs=[
            col_spec(A_WIDTH), key_spec(A_KV_WIDTH // LANES), col_spec(A_KV_HEADS * VT_ROWS),
            col_spec(B_WIDTH), key_spec(B_WIDTH // LANES), col_spec(B_HEADS * VT_ROWS),
        ] + [row_spec(C_WIDTH)] * len(C_DTYPES),
        out_shape=[
            jax.ShapeDtypeStruct((bsz, A_WIDTH, s), BF16),
            jax.ShapeDtypeStruct((bsz, A_KV_WIDTH // LANES, s, LANES), BF16),
            jax.ShapeDtypeStruct((bsz, A_KV_HEADS * VT_ROWS, s), BF16),
            jax.ShapeDtypeStruct((bsz, B_WIDTH, s), BF16),
            jax.ShapeDtypeStruct((bsz, B_WIDTH // LANES, s, LANES), BF16),
            jax.ShapeDtypeStruct((bsz, B_HEADS * VT_ROWS, s), BF16),
        ] + [jax.ShapeDtypeStruct((bsz, s, C_WIDTH), dt) for dt in C_DTYPES],
        scratch_shapes=[pltpu.VMEM((tm, d), BF16), pltpu.VMEM((tm, IN_TOTAL), F32)],
        compiler_params=pltpu.CompilerParams(
            dimension_semantics=("arbitrary", "arbitrary"), vmem_limit_bytes=VMEM_LIMIT),
        name="in_projection",
    )(x, mod_l, w_in_bf16, cosa, sina, cosb, sinb, qk_gain)


def _place_rows(block, start, total):
    parts = []
    if start:
        parts.append(jnp.zeros((start, block.shape[1]), block.dtype))
    parts.append(block)
    rest = total - start - block.shape[0]
    if rest:
        parts.append(jnp.zeros((rest, block.shape[1]), block.dtype))
    return jnp.concatenate(parts, axis=0) if len(parts) > 1 else block


def _attention_items(items, k_ref, vt_ref, qz_scr, m_scr, acc_scr, exp_dtype=F32):
    s = k_ref.shape[2]
    tq = acc_scr.shape[2]
    m_scr[...] = jnp.full(m_scr.shape, -1e30, F32)
    acc_scr[...] = jnp.zeros_like(acc_scr)

    units = [(i, c) for c in range(s // KV_CHUNK) for i in range(len(items))]

    def scores(u):
        i, c = units[u]
        return _dot(k_ref[0, items[i][0], c * KV_CHUNK:(c + 1) * KV_CHUNK, :], qz_scr[i])

    pending = [scores(u) for u in range(min(QK_AHEAD, len(units)))]
    for u, (i, c) in enumerate(units):
        v_off = items[i][1]
        st = pending.pop(0)
        m_old = m_scr[i]
        part = jnp.max(st.reshape(KV_CHUNK // 8, 8, tq), axis=0)
        m_new = jnp.maximum(m_old, jnp.max(part, axis=0, keepdims=True))
        m_scr[i] = m_new
        p = jnp.exp2((st - m_new[0:1]).astype(exp_dtype)).astype(BF16)
        d = _dot(vt_ref[0, v_off:v_off + VT_ROWS, c * KV_CHUNK:(c + 1) * KV_CHUNK], p)
        if u + QK_AHEAD < len(units):
            pending.append(scores(u + QK_AHEAD))
        acc_scr[i] = acc_scr[i] * jnp.exp2(m_old[0:1] - m_new[0:1]) + d
    results = []
    for i in range(len(items)):
        acc = acc_scr[i]
        results.append(acc[0:HEAD_DIM] * (1.0 / acc[HEAD_DIM:HEAD_DIM + 1]))
    return results


def _attention_scratch(heads, tq):
    n_items = heads * (tq // Q_TILE)
    return [pltpu.VMEM((n_items, LANES, Q_TILE), BF16), pltpu.VMEM((n_items, 8, Q_TILE), F32),
            pltpu.VMEM((n_items, VT_ROWS, Q_TILE), F32)]


def _attn_a_kernel(qt_ref, k_ref, vt_ref, o_ref, qz_scr, m_scr, acc_scr):
    group = A_HEADS // A_KV_HEADS
    n_tiles = qt_ref.shape[2] // Q_TILE
    items = []
    for t in range(n_tiles):
        cols = slice(t * Q_TILE, (t + 1) * Q_TILE)
        for h in range(A_HEADS):
            g = h // group
            qz_scr[len(items)] = _place_rows(qt_ref[0, h * HEAD_DIM:(h + 1) * HEAD_DIM, cols], g * HEAD_DIM, LANES)
            items.append((0, g * VT_ROWS))
    outs = _attention_items(items, k_ref, vt_ref, qz_scr, m_scr, acc_scr)
    for t in range(n_tiles):
        tile = outs[t * A_HEADS:(t + 1) * A_HEADS]
        o_ref[0, t * Q_TILE:(t + 1) * Q_TILE, :] = jnp.concatenate([o.T for o in tile], axis=-1).astype(BF16)


def _attention_a(qat, ka, vat, tq):
    bsz, _, s, _ = ka.shape
    return pl.pallas_call(
        _attn_a_kernel,
        grid=(bsz, s // tq),
        in_specs=[
            pl.BlockSpec((1, A_WIDTH, tq), lambda b, i: (b, 0, i)),
            pl.BlockSpec((1, A_KV_WIDTH // LANES, s, LANES), lambda b, i: (b, 0, 0, 0)),
            pl.BlockSpec((1, A_KV_HEADS * VT_ROWS, s), lambda b, i: (b, 0, 0)),
        ],
        out_specs=pl.BlockSpec((1, tq, A_WIDTH), lambda b, i: (b, i, 0)),
        out_shape=jax.ShapeDtypeStruct((bsz, s, A_WIDTH), BF16),
        scratch_shapes=_attention_scratch(A_HEADS, tq),
        compiler_params=pltpu.CompilerParams(
            dimension_semantics=("arbitrary", "arbitrary"), vmem_limit_bytes=VMEM_LIMIT),
        name="attn_axial_gqa",
    )(qat, ka, vat)


def _attn_b_kernel(lam_ref, gain_ref, qt_ref, k_ref, vt_ref, o_ref, qz_scr, m_scr, acc_scr, *, lam_init):
    lp = lam_ref[...]
    lam = (jnp.exp(jnp.sum(lp[0:1] * lp[1:2], axis=-1, keepdims=True))
           - jnp.exp(jnp.sum(lp[2:3] * lp[3:4], axis=-1, keepdims=True)) + lam_init)
    n_tiles = qt_ref.shape[2] // Q_TILE
    items = []
    for t in range(n_tiles):
        cols = slice(t * Q_TILE, (t + 1) * Q_TILE)
        for h in range(B_HEADS):
            pair = (h // 2) * LANES
            for comp in range(2):
                r0 = h * HEAD_DIM + comp * B_QK_DIM
                qz_scr[len(items)] = _place_rows(qt_ref[0, r0:r0 + B_QK_DIM, cols], r0 - pair, LANES)
                items.append((h // 2, h * VT_ROWS))
    comps = _attention_items(items, k_ref, vt_ref, qz_scr, m_scr, acc_scr, exp_dtype=BF16)
    for t in range(n_tiles):
        outs = []
        for h in range(B_HEADS):
            i = (t * B_HEADS + h) * 2
            o = comps[i] - lam * comps[i + 1]
            o = o * lax.rsqrt(jnp.mean(o * o, axis=0, keepdims=True) + EPS)
            outs.append(o.T)
        y = jnp.concatenate(outs, axis=-1) * gain_ref[...] * (1.0 - lam_init)
        o_ref[0, t * Q_TILE:(t + 1) * Q_TILE, :] = y.astype(BF16)


def _attention_b(qbt, kb, vbt, lam_params, subln_gain, lam_init, tq):
    bsz, _, s, _ = kb.shape
    return pl.pallas_call(
        functools.partial(_attn_b_kernel, lam_init=lam_init),
        grid=(bsz, s // tq),
        in_specs=[
            pl.BlockSpec((4, B_QK_DIM), lambda b, i: (0, 0)),
            pl.BlockSpec((1, B_WIDTH), lambda b, i: (0, 0)),
            pl.BlockSpec((1, B_WIDTH, tq), lambda b, i: (b, 0, i)),
            pl.BlockSpec((1, B_WIDTH // LANES, s, LANES), lambda b, i: (b, 0, 0, 0)),
            pl.BlockSpec((1, B_HEADS * VT_ROWS, s), lambda b, i: (b, 0, 0)),
        ],
        out_specs=pl.BlockSpec((1, tq, B_WIDTH), lambda b, i: (b, i, 0)),
        out_shape=jax.ShapeDtypeStruct((bsz, s, B_WIDTH), BF16),
        scratch_shapes=_attention_scratch(2 * B_HEADS, tq),
        compiler_params=pltpu.CompilerParams(
            dimension_semantics=("arbitrary", "arbitrary"), vmem_limit_bytes=VMEM_LIMIT),
        name="attn_differential",
    )(lam_params, subln_gain, qbt, kb, vbt)


def _hgrn_levels():
    sizes = []
    bs = HGRN_CHUNK
    while bs > HGRN_LEAF:
        sizes.append(bs)
        bs //= 2
    return sizes


def _hgrn_constants():
    c, w = HGRN_CHUNK, C_WIDTH
    t = np.arange(c)
    tri = np.stack([(t[:, None] >= t[None, :]), (t[:, None] <= t[None, :])]).astype(np.float32)
    s_of_lane = np.arange(w) % C_DIM
    sizes = _hgrn_levels()
    level = np.full((2, c, w), float(len(sizes) + 1), np.float32)
    for d in range(2):
        for li, bs in enumerate(sizes):
            upper = (t % bs) >= bs // 2
            qrow = upper if d == 0 else ~upper
            same = (t[:, None] // bs) == (s_of_lane[None, :] // bs)
            cross = qrow[:, None] & ~qrow[s_of_lane][None, :]
            level[d][same & cross] = li
        same_leaf = (t[:, None] // HGRN_LEAF) == (s_of_lane[None, :] // HGRN_LEAF)
        order = (s_of_lane[None, :] <= t[:, None]) if d == 0 else (s_of_lane[None, :] >= t[:, None])
        level[d][same_leaf & order] = len(sizes)
    half = (np.arange(LANES)[None, :] < C_DIM).astype(np.float32) * np.ones((c, 1), np.float32)
    return tri, level, half


def _block_diag(x, low_half):
    zero = jnp.zeros((x.shape[0], LANES), x.dtype)
    blocks = []
    for h in range(C_HEADS):
        tile = x[:, (h // 2) * LANES:(h // 2 + 1) * LANES]
        keep = tile * (low_half if h % 2 == 0 else 1 - low_half)
        blocks.append(jnp.concatenate([keep, zero] if h // 2 == 0 else [zero, keep], axis=1))
    return jnp.concatenate(blocks, axis=0)


def _block_diag_t(x, low_half):
    zero = jnp.zeros((LANES, LANES), x.dtype)
    rows = []
    for t in range(C_WIDTH // LANES):
        tile = x[:, t * LANES:(t + 1) * LANES]
        pair = jnp.concatenate([tile * low_half, tile * (1 - low_half)], axis=0).T
        rows.append(jnp.concatenate([pair, zero] if t == 0 else [zero, pair], axis=1))
    return jnp.concatenate(rows, axis=0)


def _ref_rows(b, bs, offset):
    parts = []
    for lo in range(0, b.shape[0], bs):
        parts.append(jnp.broadcast_to(b[lo + offset:lo + offset + 1, :], (bs, b.shape[1])))
    return jnp.concatenate(parts, axis=0)


def _level_operands(q, k, b, bs, d):
    half = bs // 2
    ref_off = half - 1 if d == 0 else half
    zeros = jnp.zeros((8, b.shape[1]), F32)
    qparts, kparts = [], []
    for r0 in range(0, b.shape[0], 8):
        lo = (r0 // bs) * bs
        ref = b[lo + ref_off:lo + ref_off + 1, :]
        upper = (r0 % bs) >= half
        rows = slice(r0, r0 + 8)
        if upper == (d == 0):
            qparts.append(q[rows] * jnp.exp2(b[rows] - ref))
            kparts.append(zeros)
        else:
            qparts.append(zeros)
            kparts.append(k[rows] * jnp.exp2(ref - b[rows]))
    return jnp.concatenate(qparts, axis=0).astype(BF16), jnp.concatenate(kparts, axis=0).astype(BF16)


def _hgrn_kernel(lb_ref, tri_ref, level_ref, half_ref, zq_f, zf_f, zv_f, zq_b, zf_b, zv_b,
                 of_ref, ob_ref, st_scr, *, layer, depth, nb):
    c, w = HGRN_CHUNK, C_WIDTH
    sizes = _hgrn_levels()
    n_lvl = len(sizes)

    @pl.when(pl.program_id(1) == 0)
    def _():
        st_scr[...] = jnp.zeros_like(st_scr)

    rows = [lb_ref[i:i + 1, :] for i in range(depth)]
    mx = functools.reduce(jnp.maximum, rows)
    es = [jnp.exp(r - mx) for r in rows]
    tot = functools.reduce(lambda a, b_: a + b_, es)
    ps = [e / tot for e in es]
    cum = functools.reduce(lambda a, b_: a + b_, ps[:layer + 1])
    one_minus_lb = 1.0 - jnp.clip(cum - ps[0], 0.0, 1.0)

    low_half = half_ref[...].astype(BF16)
    low_bool = half_ref[...] > 0.5
    owner = [[level_ref[d] == float(li) for li in range(n_lvl + 1)] for d in range(2)]
    chains = [(j, d) for j in range(nb) for d in range(2)]
    zrefs = {0: (zq_f, zf_f, zv_f), 1: (zq_b, zf_b, zv_b)}

    q, k, v, b, total = {}, {}, {}, {}, {}
    g_split = {}
    for ch in chains:
        j, d = ch
        zq, zf = zrefs[d][0][j], zrefs[d][1][j]
        v[ch] = zrefs[d][2][j]
        q[ch] = (zq * (C_DIM ** -0.5)) / (1.0 + jnp.exp2(zq * -LOG2_E))
        k[ch] = one_minus_lb * (1.0 - 1.0 / (1.0 + jnp.exp2(zf * -LOG2_E)))
        g = jnp.log2(jnp.maximum(1.0 - k[ch], FORGET_FLOOR))
        g_hi = g.astype(BF16)
        g_r = g - g_hi.astype(F32)
        g_mid = g_r.astype(BF16)
        g_lo = (g_r - g_mid.astype(F32)).astype(BF16)
        g_split[ch] = jnp.concatenate([g_hi, g_mid, g_lo], axis=1)
    for ch in chains:
        d = ch[1]
        bb = _dot(tri_ref[d], g_split[ch])
        b[ch] = bb[:, 0:w] + bb[:, w:2 * w] + bb[:, 2 * w:3 * w]
        total[ch] = b[ch][c - 1:c, :] if d == 0 else b[ch][0:1, :]

    parts = {ch: [] for ch in chains}
    for li in range(n_lvl + 1):
        for ch in chains:
            d = ch[1]
            if li < n_lvl:
                qp, kp = _level_operands(q[ch], k[ch], b[ch], sizes[li], d)
            else:
                ref = _ref_rows(b[ch], HGRN_LEAF, HGRN_LEAF // 2 - 1 if d == 0 else HGRN_LEAF // 2)
                arg = b[ch] - ref
                qp = (q[ch] * jnp.exp2(arg)).astype(BF16)
                kp = (k[ch] * jnp.exp2(-arg)).astype(BF16)
            parts[ch].append(_dot(qp, _block_diag_t(kp, low_half)))
    a = {}
    for ch in chains:
        acc = parts[ch][0]
        for li in range(1, n_lvl + 1):
            acc = jnp.where(owner[ch[1]][li], parts[ch][li], acc)
        a[ch] = acc.astype(BF16)

    o_intra, o_inter, upd = {}, {}, {}
    for ch in chains:
        j, d = ch
        o_intra[ch] = _dot(a[ch], _block_diag(v[ch], low_half))
        qe = (q[ch] * jnp.exp2(b[ch])).astype(BF16)
        o_inter[ch] = _dot(qe, _block_diag_t(st_scr[d, j].astype(BF16), low_half))
        kb = (k[ch] * jnp.exp2(total[ch] - b[ch])).astype(BF16)
        tiles = []
        for t in range(w // LANES):
            lanes = slice(t * LANES, (t + 1) * LANES)
            full = _dot_tn(v[ch][:, lanes], kb[:, lanes])
            tiles.append(jnp.where(low_bool, full[0:C_DIM], full[C_DIM:2 * C_DIM]))
        upd[ch] = jnp.concatenate(tiles, axis=1)
    for ch in chains:
        j, d = ch
        st_scr[d, j] = st_scr[d, j] * jnp.exp2(total[ch]) + upd[ch]
        (of_ref if d == 0 else ob_ref)[j] = o_intra[ch] + o_inter[ch]


def _hgrn(cq, cff, cfb, ci, lower_bounds, layer, nb):
    bsz, s, _ = cq.shape
    depth = lower_bounds.shape[0]
    n_chunks = s // HGRN_CHUNK
    c, w = HGRN_CHUNK, C_WIDTH
    tri, level, half = _hgrn_constants()
    consts = (jnp.asarray(tri, BF16), jnp.asarray(level), jnp.asarray(half))
    full = lambda a: pl.BlockSpec(a.shape, lambda g, i: (0,) * a.ndim)
    last = n_chunks - 1
    fwd = pl.BlockSpec((nb, c, w), lambda g, i: (g, i, 0))
    bwd = pl.BlockSpec((nb, c, w), lambda g, i: (g, last - i, 0))
    return pl.pallas_call(
        functools.partial(_hgrn_kernel, layer=layer, depth=depth, nb=nb),
        grid=(bsz // nb, n_chunks),
        in_specs=[full(lower_bounds)] + [full(a) for a in consts] + [fwd, fwd, fwd, bwd, bwd, bwd],
        out_specs=[fwd, bwd],
        out_shape=[jax.ShapeDtypeStruct((bsz, s, w), F32)] * 2,
        scratch_shapes=[pltpu.VMEM((2, nb, C_DIM, w), F32)],
        compiler_params=pltpu.CompilerParams(
            dimension_semantics=("arbitrary", "arbitrary"), vmem_limit_bytes=VMEM_LIMIT),
        name="hgrn2_bidir",
    )(lower_bounds, *consts, cq, cff, ci, cq, cfb, ci)


def _out_ffn_kernel(x_ref, mod_ref, oa_ref, ob_ref, cf_ref, cb_ref, zg_ref, cgain_ref,
                    wo_ref, w1_ref, w2_ref, fg_ref, o_ref, *, final, ff_chunk):
    oc = cf_ref[0] + cb_ref[0]
    ms = _split_dot(oc * oc, _group_ones(C_WIDTH, C_DIM))
    zg = zg_ref[0]
    oc = oc * lax.rsqrt(ms + EPS) * cgain_ref[...] * (zg * jax.nn.sigmoid(zg))
    mix = jnp.concatenate([oa_ref[0], ob_ref[0], oc.astype(BF16)], axis=-1)
    x1 = x_ref[0] + mod_ref[0, 2:3, :] * _dot(mix, wo_ref[0])
    h = (_rms_rows(x1) * (1.0 + mod_ref[0, 4:5, :]) + mod_ref[0, 3:4, :]).astype(BF16)
    d_ff = w1_ref.shape[2]
    acc = jnp.zeros_like(x1)
    for j in range(d_ff // ff_chunk):
        u = _dot(h, w1_ref[0, :, j * ff_chunk:(j + 1) * ff_chunk])
        u = jnp.square(jnp.maximum(u, 0.0)).astype(BF16)
        acc = acc + _dot(u, w2_ref[0, j * ff_chunk:(j + 1) * ff_chunk, :])
    x2 = x1 + mod_ref[0, 5:6, :] * acc
    if final:
        x2 = _rms_rows(x2) * fg_ref[...]
    o_ref[0] = x2


def _out_ffn(x, mod_l, oa, ob, oc_f, oc_b, cg, c_gain, w_out, w_ff1, w_ff2, final_gain, layer, final, tm):
    bsz, s, d = x.shape
    d_ff = w_ff1.shape[2]
    row_spec = lambda width: pl.BlockSpec((1, tm, width), lambda b, i: (b, i, 0))
    weight = lambda a: pl.BlockSpec((1,) + a.shape[1:], lambda b, i: (layer, 0, 0),
                                    pipeline_mode=pl.Buffered(1))
    return pl.pallas_call(
        functools.partial(_out_ffn_kernel, final=final, ff_chunk=1024),
        grid=(bsz, s // tm),
        in_specs=[
            row_spec(d),
            pl.BlockSpec((1, 6, d), lambda b, i: (b, 0, 0)),
            row_spec(A_WIDTH), row_spec(B_WIDTH), row_spec(C_WIDTH), row_spec(C_WIDTH), row_spec(C_WIDTH),
            pl.BlockSpec((1, C_WIDTH), lambda b, i: (0, 0)),
            weight(w_out), weight(w_ff1), weight(w_ff2),
            pl.BlockSpec((1, d), lambda b, i: (0, 0)),
        ],
        out_specs=row_spec(d),
        out_shape=jax.ShapeDtypeStruct((bsz, s, d), F32),
        compiler_params=pltpu.CompilerParams(
            dimension_semantics=("arbitrary", "arbitrary"), vmem_limit_bytes=VMEM_LIMIT),
        name="out_proj_ffn",
    )(x, mod_l, oa, ob, oc_f, oc_b, cg, c_gain, w_out, w_ff1, w_ff2, final_gain)


def _rope_tables(s):
    half = HEAD_DIM // 2
    inv = ROPE_THETA ** (-jnp.arange(0, half, 2, dtype=F32) / half)
    t = jnp.arange(s)
    ang_row = (t // GRID_W).astype(F32)[:, None] * inv[None, :]
    ang_col = (t % GRID_W).astype(F32)[:, None] * inv[None, :]
    ang_1d = t.astype(F32)[:, None] * inv[None, :]

    def pair(ang):
        return (jnp.concatenate([jnp.cos(ang), jnp.cos(ang)], axis=-1),
                jnp.concatenate([-jnp.sin(ang), jnp.sin(ang)], axis=-1))

    cr, sr = pair(ang_row)
    cc, sc = pair(ang_col)
    c1, s1 = pair(ang_1d)
    tile = lambda a: jnp.tile(a, (1, LANES // a.shape[1]))
    return (tile(jnp.concatenate([cr, cc], axis=-1)), tile(jnp.concatenate([sr, sc], axis=-1)),
            tile(c1), tile(s1))


def kernel(x, c, w_mod, b_mod, w_in, a_qk_norm, diff_lambda, diff_subln, hgrn_lower_bounds, hgrn_norm,
           w_out, w_ff1, w_ff2, final_norm):
    bsz, s, d = x.shape
    depth = w_in.shape[0]
    tm = min(512, s)
    tq = min(2 * Q_TILE, s)
    tables = _rope_tables(s)
    mod = _modulation(c, w_mod, b_mod).reshape(depth, bsz, 6, d)
    w_in_b, w_out_b, w_ff1_b, w_ff2_b = (w.astype(BF16) for w in (w_in, w_out, w_ff1, w_ff2))
    final_gain = final_norm.reshape(1, d)
    for l in range(depth):
        qk_gain = jnp.tile(a_qk_norm[l], (1, LANES // HEAD_DIM))
        qat, ka, vat, qbt, kb, vbt, cq, cff, cfb, ci, cg = _in_projection(
            x, mod[l], w_in_b, l, tables, qk_gain, tm)
        oa = _attention_a(qat, ka, vat, min(A_Q_TILES * Q_TILE, s))
        lam_init = 0.8 - 0.6 * math.exp(-0.3 * l)
        subln = jnp.tile(diff_subln[l].reshape(1, HEAD_DIM), (1, B_HEADS))
        ob = _attention_b(qbt, kb, vbt, diff_lambda[l], subln, lam_init, min(B_Q_TILES * Q_TILE, s))
        hg_gain = jnp.tile(hgrn_norm[l].reshape(1, C_DIM), (1, C_HEADS))
        oc_f, oc_b = _hgrn(cq, cff, cfb, ci, hgrn_lower_bounds, l, min(HGRN_BATCH, bsz))
        x = _out_ffn(x, mod[l], oa, ob, oc_f, oc_b, cg, hg_gain, w_out_b, w_ff1_b, w_ff2_b, final_gain,
                     l, l == depth - 1, tm)
    return x
```

```python
import functools
import math

import numpy as np
import jax
import jax.numpy as jnp
from jax import lax
from jax.experimental import pallas as pl
from jax.experimental.pallas import tpu as pltpu

F32 = jnp.float32
BF16 = jnp.bfloat16

HEAD_DIM = 64
GRID_W = 64
ROPE_THETA = 10000.0
EPS = 1e-6
FORGET_FLOOR = 1e-6
A_HEADS, A_KV_HEADS = 6, 2
B_HEADS, B_QK_DIM = 6, 32
C_HEADS, C_DIM = 4, 64
A_WIDTH = A_HEADS * HEAD_DIM
A_KV_WIDTH = A_KV_HEADS * HEAD_DIM
B_WIDTH = B_HEADS * HEAD_DIM
C_WIDTH = C_HEADS * C_DIM
LANES = 128
HGRN_CHUNK = 64
HGRN_LEAF = 8
HGRN_BATCH = 16
ONES_ROWS = 16
VT_ROWS = HEAD_DIM + ONES_ROWS
KV_CHUNK = 256
Q_TILE = 256
A_Q_TILES = 4
B_Q_TILES = 4
QK_AHEAD = 16
LOG2_E = math.log2(math.e)
VMEM_LIMIT = 56 * 1024 * 1024

OFF_AQ, OFF_AK, OFF_AV = 0, 384, 512
OFF_BQ, OFF_BK, OFF_BV = 640, 1024, 1408
OFF_C = 1792
C_DTYPES = (F32, F32, F32, BF16, F32)
IN_TOTAL = 3072


def _dot(a, b):
    return jnp.dot(a, b, preferred_element_type=F32)


def _dot_tn(a, b):
    return lax.dot_general(a, b, (((0,), (0,)), ((), ())), preferred_element_type=F32)


def _split_dot(x, w_bf16):
    hi = x.astype(BF16)
    lo = (x - hi.astype(F32)).astype(BF16)
    return _dot(hi, w_bf16) + _dot(lo, w_bf16)


def _group_ones(n, group):
    r = lax.broadcasted_iota(jnp.int32, (n, n), 0) // group
    c = lax.broadcasted_iota(jnp.int32, (n, n), 1) // group
    return jnp.where(r == c, 1.0 / group, 0.0).astype(BF16)


def _rope(xb, cos, sin_signed):
    lane = lax.broadcasted_iota(jnp.int32, xb.shape, 1)
    low = (lane % 32) < 16
    partner = jnp.where(low, pltpu.roll(xb, LANES - 16, 1), pltpu.roll(xb, 16, 1))
    return xb * cos + partner * sin_signed


def _rms_rows(x):
    return x * lax.rsqrt(jnp.mean(x * x, axis=-1, keepdims=True) + EPS)


def _mod_kernel(c_ref, w_ref, b_ref, o_ref):
    c = c_ref[...]
    cond = (c * jax.nn.sigmoid(c)).astype(BF16)
    o_ref[0] = _dot(cond, w_ref[0].astype(BF16)) + b_ref[0]


def _modulation(c, w_mod, b_mod):
    depth, d, n = w_mod.shape
    bsz = c.shape[0]
    tn = 1024
    return pl.pallas_call(
        _mod_kernel,
        grid=(depth, n // tn),
        in_specs=[
            pl.BlockSpec((bsz, d), lambda l, j: (0, 0)),
            pl.BlockSpec((1, d, tn), lambda l, j: (l, 0, j)),
            pl.BlockSpec((1, 1, tn), lambda l, j: (l, 0, j)),
        ],
        out_specs=pl.BlockSpec((1, bsz, tn), lambda l, j: (l, 0, j)),
        out_shape=jax.ShapeDtypeStruct((depth, bsz, n), F32),
        compiler_params=pltpu.CompilerParams(
            dimension_semantics=("arbitrary", "arbitrary"), vmem_limit_bytes=VMEM_LIMIT),
        name="adaln_mod",
    )(c, w_mod, b_mod.reshape(depth, 1, n))


def _inproj_kernel(x_ref, mod_ref, w_ref, cosa_ref, sina_ref, cosb_ref, sinb_ref, gain_ref,
                   qat_ref, ka_ref, vat_ref, qbt_ref, kb_ref, vbt_ref,
                   cq_ref, cff_ref, cfb_ref, ci_ref, cg_ref, h_scr, z_scr):
    x = x_ref[0]
    h = _rms_rows(x) * (1.0 + mod_ref[0, 1:2, :]) + mod_ref[0, 0:1, :]
    h_scr[...] = h.astype(BF16)

    def project(j):
        z_scr[:, j * 1024:(j + 1) * 1024] = _dot(h_scr[...], w_ref[0, :, j * 1024:(j + 1) * 1024])

    project(0)
    project(1)

    ones = _group_ones(LANES, HEAD_DIM)
    cosa, sina = cosa_ref[...], sina_ref[...]
    cosb, sinb = cosb_ref[...], sinb_ref[...]

    def a_norm_rope(zb, gain):
        ms = _split_dot(zb * zb, ones)
        return _rope(zb * lax.rsqrt(ms + EPS) * gain, cosa, sina)

    def store_vt(vt_ref, first_head, blk_t):
        ones_rows = jnp.ones((ONES_ROWS, blk_t.shape[1]), BF16)
        for j in range(LANES // HEAD_DIM):
            r0 = (first_head + j) * VT_ROWS
            vt_ref[0, r0:r0 + HEAD_DIM, :] = blk_t[j * HEAD_DIM:(j + 1) * HEAD_DIM].astype(BF16)
            vt_ref[0, r0 + HEAD_DIM:r0 + VT_ROWS, :] = ones_rows

    for cblk in range(A_WIDTH // LANES):
        zb = z_scr[:, OFF_AQ + cblk * LANES:OFF_AQ + (cblk + 1) * LANES]
        y = a_norm_rope(zb, gain_ref[0:1, :]) * (HEAD_DIM ** -0.5 * LOG2_E)
        qat_ref[0, cblk * LANES:(cblk + 1) * LANES, :] = y.T.astype(BF16)
    ka_ref[0, 0] = a_norm_rope(z_scr[:, OFF_AK:OFF_AK + LANES], gain_ref[1:2, :]).astype(BF16)
    store_vt(vat_ref, 0, z_scr[:, OFF_AV:OFF_AV + LANES].T)
    project(2)

    for cblk in range(B_WIDTH // LANES):
        lo = cblk * LANES
        zq = z_scr[:, OFF_BQ + lo:OFF_BQ + lo + LANES]
        qbt_ref[0, lo:lo + LANES, :] = (_rope(zq, cosb, sinb) * (B_QK_DIM ** -0.5 * LOG2_E)).T.astype(BF16)
    for cblk in range(B_WIDTH // LANES):
        lo = cblk * LANES
        zk = z_scr[:, OFF_BK + lo:OFF_BK + lo + LANES]
        kb_ref[0, cblk] = _rope(zk, cosb, sinb).astype(BF16)
        store_vt(vbt_ref, cblk * (LANES // HEAD_DIM), z_scr[:, OFF_BV + lo:OFF_BV + lo + LANES].T)
    for j, ref in enumerate((cq_ref, cff_ref, cfb_ref, ci_ref, cg_ref)):
        ref[0] = z_scr[:, OFF_C + j * C_WIDTH:OFF_C + (j + 1) * C_WIDTH].astype(ref.dtype)


def _in_projection(x, mod_l, w_in_bf16, layer, tables, qk_gain, tm):
    bsz, s, d = x.shape
    cosa, sina, cosb, sinb = tables
    row_spec = lambda width: pl.BlockSpec((1, tm, width), lambda b, i: (b, i, 0))
    tab_spec = pl.BlockSpec((tm, LANES), lambda b, i: (i, 0))
    col_spec = lambda height: pl.BlockSpec((1, height, tm), lambda b, i: (b, 0, i))
    key_spec = lambda blocks: pl.BlockSpec((1, blocks, tm, LANES), lambda b, i: (b, 0, i, 0))
    return pl.pallas_call(
        _inproj_kernel,
        grid=(bsz, s // tm),
        in_specs=[
            row_spec(d),
            pl.BlockSpec((1, 6, d), lambda b, i: (b, 0, 0)),
            pl.BlockSpec((1, d, IN_TOTAL), lambda b, i: (layer, 0, 0)),
            tab_spec, tab_spec, tab_spec, tab_spec,
            pl.BlockSpec((2, LANES), lambda b, i: (0, 0)),
        ],
        out_specs=[
            col_spec(A_WIDTH), key_spec(A_KV_WIDTH // LANES), col_spec(A_KV_HEADS * VT_ROWS),
            col_spec(B_WIDTH), key_spec(B_WIDTH // LANES), col_spec(B_HEADS * VT_ROWS),
        ] + [row_spec(C_WIDTH)] * len(C_DTYPES),
        out_shape=[
            jax.ShapeDtypeStruct((bsz, A_WIDTH, s), BF16),
            jax.ShapeDtypeStruct((bsz, A_KV_WIDTH // LANES, s, LANES), BF16),
            jax.ShapeDtypeStruct((bsz, A_KV_HEADS * VT_ROWS, s), BF16),
            jax.ShapeDtypeStruct((bsz, B_WIDTH, s), BF16),
            jax.ShapeDtypeStruct((bsz, B_WIDTH // LANES, s, LANES), BF16),
            jax.ShapeDtypeStruct((bsz, B_HEADS * VT_ROWS, s), BF16),
        ] + [jax.ShapeDtypeStruct((bsz, s, C_WIDTH), dt) for dt in C_DTYPES],
        scratch_shapes=[pltpu.VMEM((tm, d), BF16), pltpu.VMEM((tm, IN_TOTAL), F32)],
        compiler_params=pltpu.CompilerParams(
            dimension_semantics=("arbitrary", "arbitrary"), vmem_limit_bytes=VMEM_LIMIT),
        name="in_projection",
    )(x, mod_l, w_in_bf16, cosa, sina, cosb, sinb, qk_gain)


def _place_rows(block, start, total):
    parts = []
    if start:
        parts.append(jnp.zeros((start, block.shape[1]), block.dtype))
    parts.append(block)
    rest = total - start - block.shape[0]
    if rest:
        parts.append(jnp.zeros((rest, block.shape[1]), block.dtype))
    return jnp.concatenate(parts, axis=0) if len(parts) > 1 else block


def _attention_items(items, k_ref, vt_ref, qz_scr, m_scr, acc_scr, exp_dtype=F32):
    s = k_ref.shape[2]
    tq = acc_scr.shape[2]
    m_scr[...] = jnp.full(m_scr.shape, -1e30, F32)
    acc_scr[...] = jnp.zeros_like(acc_scr)

    units = [(i, c) for c in range(s // KV_CHUNK) for i in range(len(items))]

    def scores(u):
        i, c = units[u]
        return _dot(k_ref[0, items[i][0], c * KV_CHUNK:(c + 1) * KV_CHUNK, :], qz_scr[i])

    pending = [scores(u) for u in range(min(QK_AHEAD, len(units)))]
    for u, (i, c) in enumerate(units):
        v_off = items[i][1]
        st = pending.pop(0)
        m_old = m_scr[i]
        part = jnp.max(st.reshape(KV_CHUNK // 8, 8, tq), axis=0)
        m_new = jnp.maximum(m_old, jnp.max(part, axis=0, keepdims=True))
        m_scr[i] = m_new
        p = jnp.exp2((st - m_new[0:1]).astype(exp_dtype)).astype(BF16)
        d = _dot(vt_ref[0, v_off:v_off + VT_ROWS, c * KV_CHUNK:(c + 1) * KV_CHUNK], p)
        if u + QK_AHEAD < len(units):
            pending.append(scores(u + QK_AHEAD))
        acc_scr[i] = acc_scr[i] * jnp.exp2(m_old[0:1] - m_new[0:1]) + d
    results = []
    for i in range(len(items)):
        acc = acc_scr[i]
        results.append(acc[0:HEAD_DIM] * (1.0 / acc[HEAD_DIM:HEAD_DIM + 1]))
    return results


def _attention_scratch(heads, tq):
    n_items = heads * (tq // Q_TILE)
    return [pltpu.VMEM((n_items, LANES, Q_TILE), BF16), pltpu.VMEM((n_items, 8, Q_TILE), F32),
            pltpu.VMEM((n_items, VT_ROWS, Q_TILE), F32)]


def _attn_a_kernel(qt_ref, k_ref, vt_ref, o_ref, qz_scr, m_scr, acc_scr):
    group = A_HEADS // A_KV_HEADS
    n_tiles = qt_ref.shape[2] // Q_TILE
    items = []
    for t in range(n_tiles):
        cols = slice(t * Q_TILE, (t + 1) * Q_TILE)
        for h in range(A_HEADS):
            g = h // group
            qz_scr[len(items)] = _place_rows(qt_ref[0, h * HEAD_DIM:(h + 1) * HEAD_DIM, cols], g * HEAD_DIM, LANES)
            items.append((0, g * VT_ROWS))
    outs = _attention_items(items, k_ref, vt_ref, qz_scr, m_scr, acc_scr)
    for t in range(n_tiles):
        tile = outs[t * A_HEADS:(t + 1) * A_HEADS]
        o_ref[0, t * Q_TILE:(t + 1) * Q_TILE, :] = jnp.concatenate([o.T for o in tile], axis=-1).astype(BF16)


def _attention_a(qat, ka, vat, tq):
    bsz, _, s, _ = ka.shape
    return pl.pallas_call(
        _attn_a_kernel,
        grid=(bsz, s // tq),
        in_specs=[
            pl.BlockSpec((1, A_WIDTH, tq), lambda b, i: (b, 0, i)),
            pl.BlockSpec((1, A_KV_WIDTH // LANES, s, LANES), lambda b, i: (b, 0, 0, 0)),
            pl.BlockSpec((1, A_KV_HEADS * VT_ROWS, s), lambda b, i: (b, 0, 0)),
        ],
        out_specs=pl.BlockSpec((1, tq, A_WIDTH), lambda b, i: (b, i, 0)),
        out_shape=jax.ShapeDtypeStruct((bsz, s, A_WIDTH), BF16),
        scratch_shapes=_attention_scratch(A_HEADS, tq),
        compiler_params=pltpu.CompilerParams(
            dimension_semantics=("arbitrary", "arbitrary"), vmem_limit_bytes=VMEM_LIMIT),
        name="attn_axial_gqa",
    )(qat, ka, vat)


def _attn_b_kernel(lam_ref, gain_ref, qt_ref, k_ref, vt_ref, o_ref, qz_scr, m_scr, acc_scr, *, lam_init):
    lp = lam_ref[...]
    lam = (jnp.exp(jnp.sum(lp[0:1] * lp[1:2], axis=-1, keepdims=True))
           - jnp.exp(jnp.sum(lp[2:3] * lp[3:4], axis=-1, keepdims=True)) + lam_init)
    n_tiles = qt_ref.shape[2] // Q_TILE
    items = []
    for t in range(n_tiles):
        cols = slice(t * Q_TILE, (t + 1) * Q_TILE)
        for h in range(B_HEADS):
            pair = (h // 2) * LANES
            for comp in range(2):
                r0 = h * HEAD_DIM + comp * B_QK_DIM
                qz_scr[len(items)] = _place_rows(qt_ref[0, r0:r0 + B_QK_DIM, cols], r0 - pair, LANES)
                items.append((h // 2, h * VT_ROWS))
    comps = _attention_items(items, k_ref, vt_ref, qz_scr, m_scr, acc_scr, exp_dtype=BF16)
    for t in range(n_tiles):
        outs = []
        for h in range(B_HEADS):
            i = (t * B_HEADS + h) * 2
            o = comps[i] - lam * comps[i + 1]
            o = o * lax.rsqrt(jnp.mean(o * o, axis=0, keepdims=True) + EPS)
            outs.append(o.T)
        y = jnp.concatenate(outs, axis=-1) * gain_ref[...] * (1.0 - lam_init)
        o_ref[0, t * Q_TILE:(t + 1) * Q_TILE, :] = y.astype(BF16)


def _attention_b(qbt, kb, vbt, lam_params, subln_gain, lam_init, tq):
    bsz, _, s, _ = kb.shape
    return pl.pallas_call(
        functools.partial(_attn_b_kernel, lam_init=lam_init),
        grid=(bsz, s // tq),
        in_specs=[
            pl.BlockSpec((4, B_QK_DIM), lambda b, i: (0, 0)),
            pl.BlockSpec((1, B_WIDTH), lambda b, i: (0, 0)),
            pl.BlockSpec((1, B_WIDTH, tq), lambda b, i: (b, 0, i)),
            pl.BlockSpec((1, B_WIDTH // LANES, s, LANES), lambda b, i: (b, 0, 0, 0)),
            pl.BlockSpec((1, B_HEADS * VT_ROWS, s), lambda b, i: (b, 0, 0)),
        ],
        out_specs=pl.BlockSpec((1, tq, B_WIDTH), lambda b, i: (b, i, 0)),
        out_shape=jax.ShapeDtypeStruct((bsz, s, B_WIDTH), BF16),
        scratch_shapes=_attention_scratch(2 * B_HEADS, tq),
        compiler_params=pltpu.CompilerParams(
            dimension_semantics=("arbitrary", "arbitrary"), vmem_limit_bytes=VMEM_LIMIT),
        name="attn_differential",
    )(lam_params, subln_gain, qbt, kb, vbt)


def _hgrn_levels():
    sizes = []
    bs = HGRN_CHUNK
    while bs > HGRN_LEAF:
        sizes.append(bs)
        bs //= 2
    return sizes


def _hgrn_constants():
    c, w = HGRN_CHUNK, C_WIDTH
    t = np.arange(c)
    tri = np.stack([(t[:, None] >= t[None, :]), (t[:, None] <= t[None, :])]).astype(np.float32)
    s_of_lane = np.arange(w) % C_DIM
    sizes = _hgrn_levels()
    level = np.full((2, c, w), float(len(sizes) + 1), np.float32)
    for d in range(2):
        for li, bs in enumerate(sizes):
            upper = (t % bs) >= bs // 2
            qrow = upper if d == 0 else ~upper
            same = (t[:, None] // bs) == (s_of_lane[None, :] // bs)
            cross = qrow[:, None] & ~qrow[s_of_lane][None, :]
            level[d][same & cross] = li
        same_leaf = (t[:, None] // HGRN_LEAF) == (s_of_lane[None, :] // HGRN_LEAF)
        order = (s_of_lane[None, :] <= t[:, None]) if d == 0 else (s_of_lane[None, :] >= t[:, None])
        level[d][same_leaf & order] = len(sizes)
    half = (np.arange(LANES)[None, :] < C_DIM).astype(np.float32) * np.ones((c, 1), np.float32)
    return tri, level, half


def _block_diag(x, low_half):
    zero = jnp.zeros((x.shape[0], LANES), x.dtype)
    blocks = []
    for h in range(C_HEADS):
        tile = x[:, (h // 2) * LANES:(h // 2 + 1) * LANES]
        keep = tile * (low_half if h % 2 == 0 else 1 - low_half)
        blocks.append(jnp.concatenate([keep, zero] if h // 2 == 0 else [zero, keep], axis=1))
    return jnp.concatenate(blocks, axis=0)


def _block_diag_t(x, low_half):
    zero = jnp.zeros((LANES, LANES), x.dtype)
    rows = []
    for t in range(C_WIDTH // LANES):
        tile = x[:, t * LANES:(t + 1) * LANES]
        pair = jnp.concatenate([tile * low_half, tile * (1 - low_half)], axis=0).T
        rows.append(jnp.concatenate([pair, zero] if t == 0 else [zero, pair], axis=1))
    return jnp.concatenate(rows, axis=0)


def _ref_rows(b, bs, offset):
    parts = []
    for lo in range(0, b.shape[0], bs):
        parts.append(jnp.broadcast_to(b[lo + offset:lo + offset + 1, :], (bs, b.shape[1])))
    return jnp.concatenate(parts, axis=0)


def _level_operands(q, k, b, bs, d):
    half = bs // 2
    ref_off = half - 1 if d == 0 else half
    zeros = jnp.zeros((8, b.shape[1]), F32)
    qparts, kparts = [], []
    for r0 in range(0, b.shape[0], 8):
        lo = (r0 // bs) * bs
        ref = b[lo + ref_off:lo + ref_off + 1, :]
        upper = (r0 % bs) >= half
        rows = slice(r0, r0 + 8)
        if upper == (d == 0):
            qparts.append(q[rows] * jnp.exp2(b[rows] - ref))
            kparts.append(zeros)
        else:
            qparts.append(zeros)
            kparts.append(k[rows] * jnp.exp2(ref - b[rows]))
    return jnp.concatenate(qparts, axis=0).astype(BF16), jnp.concatenate(kparts, axis=0).astype(BF16)


def _hgrn_kernel(lb_ref, tri_ref, level_ref, half_ref, zq_f, zf_f, zv_f, zq_b, zf_b, zv_b,
                 of_ref, ob_ref, st_scr, *, layer, depth, nb):
    c, w = HGRN_CHUNK, C_WIDTH
    sizes = _hgrn_levels()
    n_lvl = len(sizes)

    @pl.when(pl.program_id(1) == 0)
    def _():
        st_scr[...] = jnp.zeros_like(st_scr)

    rows = [lb_ref[i:i + 1, :] for i in range(depth)]
    mx = functools.reduce(jnp.maximum, rows)
    es = [jnp.exp(r - mx) for r in rows]
    tot = functools.reduce(lambda a, b_: a + b_, es)
    ps = [e / tot for e in es]
    cum = functools.reduce(lambda a, b_: a + b_, ps[:layer + 1])
    one_minus_lb = 1.0 - jnp.clip(cum - ps[0], 0.0, 1.0)

    low_half = half_ref[...].astype(BF16)
    low_bool = half_ref[...] > 0.5
    owner = [[level_ref[d] == float(li) for li in range(n_lvl + 1)] for d in range(2)]
    chains = [(j, d) for j in range(nb) for d in range(2)]
    zrefs = {0: (zq_f, zf_f, zv_f), 1: (zq_b, zf_b, zv_b)}

    q, k, v, b, total = {}, {}, {}, {}, {}
    g_split = {}
    for ch in chains:
        j, d = ch
        zq, zf = zrefs[d][0][j], zrefs[d][1][j]
        v[ch] = zrefs[d][2][j]
        q[ch] = (zq * (C_DIM ** -0.5)) / (1.0 + jnp.exp2(zq * -LOG2_E))
        k[ch] = one_minus_lb * (1.0 - 1.0 / (1.0 + jnp.exp2(zf * -LOG2_E)))
        g = jnp.log2(jnp.maximum(1.0 - k[ch], FORGET_FLOOR))
        g_hi = g.astype(BF16)
        g_r = g - g_hi.astype(F32)
        g_mid = g_r.astype(BF16)
        g_lo = (g_r - g_mid.astype(F32)).astype(BF16)
        g_split[ch] = jnp.concatenate([g_hi, g_mid, g_lo], axis=1)
    for ch in chains:
        d = ch[1]
        bb = _dot(tri_ref[d], g_split[ch])
        b[ch] = bb[:, 0:w] + bb[:, w:2 * w] + bb[:, 2 * w:3 * w]
        total[ch] = b[ch][c - 1:c, :] if d == 0 else b[ch][0:1, :]

    parts = {ch: [] for ch in chains}
    for li in range(n_lvl + 1):
        for ch in chains:
            d = ch[1]
            if li < n_lvl:
                qp, kp = _level_operands(q[ch], k[ch], b[ch], sizes[li], d)
            else:
                ref = _ref_rows(b[ch], HGRN_LEAF, HGRN_LEAF // 2 - 1 if d == 0 else HGRN_LEAF // 2)
                arg = b[ch] - ref
                qp = (q[ch] * jnp.exp2(arg)).astype(BF16)
                kp = (k[ch] * jnp.exp2(-arg)).astype(BF16)
            parts[ch].append(_dot(qp, _block_diag_t(kp, low_half)))
    a = {}
    for ch in chains:
        acc = parts[ch][0]
        for li in range(1, n_lvl + 1):
            acc = jnp.where(owner[ch[1]][li], parts[ch][li], acc)
        a[ch] = acc.astype(BF16)

    o_intra, o_inter, upd = {}, {}, {}
    for ch in chains:
        j, d = ch
        o_intra[ch] = _dot(a[ch], _block_diag(v[ch], low_half))
        qe = (q[ch] * jnp.exp2(b[ch])).astype(BF16)
        o_inter[ch] = _dot(qe, _block_diag_t(st_scr[d, j].astype(BF16), low_half))
        kb = (k[ch] * jnp.exp2(total[ch] - b[ch])).astype(BF16)
        tiles = []
        for t in range(w // LANES):
            lanes = slice(t * LANES, (t + 1) * LANES)
            full = _dot_tn(v[ch][:, lanes], kb[:, lanes])
            tiles.append(jnp.where(low_bool, full[0:C_DIM], full[C_DIM:2 * C_DIM]))
        upd[ch] = jnp.concatenate(tiles, axis=1)
    for ch in chains:
        j, d = ch
        st_scr[d, j] = st_scr[d, j] * jnp.exp2(total[ch]) + upd[ch]
        (of_ref if d == 0 else ob_ref)[j] = o_intra[ch] + o_inter[ch]


def _hgrn(cq, cff, cfb, ci, lower_bounds, layer, nb):
    bsz, s, _ = cq.shape
    depth = lower_bounds.shape[0]
    n_chunks = s // HGRN_CHUNK
    c, w = HGRN_CHUNK, C_WIDTH
    tri, level, half = _hgrn_constants()
    consts = (jnp.asarray(tri, BF16), jnp.asarray(level), jnp.asarray(half))
    full = lambda a: pl.BlockSpec(a.shape, lambda g, i: (0,) * a.ndim)
    last = n_chunks - 1
    fwd = pl.BlockSpec((nb, c, w), lambda g, i: (g, i, 0))
    bwd = pl.BlockSpec((nb, c, w), lambda g, i: (g, last - i, 0))
    return pl.pallas_call(
        functools.partial(_hgrn_kernel, layer=layer, depth=depth, nb=nb),
        grid=(bsz // nb, n_chunks),
        in_specs=[full(lower_bounds)] + [full(a) for a in consts] + [fwd, fwd, fwd, bwd, bwd, bwd],
        out_specs=[fwd, bwd],
        out_shape=[jax.ShapeDtypeStruct((bsz, s, w), F32)] * 2,
        scratch_shapes=[pltpu.VMEM((2, nb, C_DIM, w), F32)],
        compiler_params=pltpu.CompilerParams(
            dimension_semantics=("arbitrary", "arbitrary"), vmem_limit_bytes=VMEM_LIMIT),
        name="hgrn2_bidir",
    )(lower_bounds, *consts, cq, cff, ci, cq, cfb, ci)


def _out_ffn_kernel(x_ref, mod_ref, oa_ref, ob_ref, cf_ref, cb_ref, zg_ref, cgain_ref,
                    wo_ref, w1_ref, w2_ref, fg_ref, o_ref, *, final, ff_chunk):
    oc = cf_ref[0] + cb_ref[0]
    ms = _split_dot(oc * oc, _group_ones(C_WIDTH, C_DIM))
    zg = zg_ref[0]
    oc = oc * lax.rsqrt(ms + EPS) * cgain_ref[...] * (zg * jax.nn.sigmoid(zg))
    mix = jnp.concatenate([oa_ref[0], ob_ref[0], oc.astype(BF16)], axis=-1)
    x1 = x_ref[0] + mod_ref[0, 2:3, :] * _dot(mix, wo_ref[0])
    h = (_rms_rows(x1) * (1.0 + mod_ref[0, 4:5, :]) + mod_ref[0, 3:4, :]).astype(BF16)
    d_ff = w1_ref.shape[2]
    acc = jnp.zeros_like(x1)
    for j in range(d_ff // ff_chunk):
        u = _dot(h, w1_ref[0, :, j * ff_chunk:(j + 1) * ff_chunk])
        u = jnp.square(jnp.maximum(u, 0.0)).astype(BF16)
        acc = acc + _dot(u, w2_ref[0, j * ff_chunk:(j + 1) * ff_chunk, :])
    x2 = x1 + mod_ref[0, 5:6, :] * acc
    if final:
        x2 = _rms_rows(x2) * fg_ref[...]
    o_ref[0] = x2


def _out_ffn(x, mod_l, oa, ob, oc_f, oc_b, cg, c_gain, w_out, w_ff1, w_ff2, final_gain, layer, final, tm):
    bsz, s, d = x.shape
    d_ff = w_ff1.shape[2]
    row_spec = lambda width: pl.BlockSpec((1, tm, width), lambda b, i: (b, i, 0))
    weight = lambda a: pl.BlockSpec((1,) + a.shape[1:], lambda b, i: (layer, 0, 0),
                                    pipeline_mode=pl.Buffered(1))
    return pl.pallas_call(
        functools.partial(_out_ffn_kernel, final=final, ff_chunk=1024),
        grid=(bsz, s // tm),
        in_specs=[
            row_spec(d),
            pl.BlockSpec((1, 6, d), lambda b, i: (b, 0, 0)),
            row_spec(A_WIDTH), row_spec(B_WIDTH), row_spec(C_WIDTH), row_spec(C_WIDTH), row_spec(C_WIDTH),
            pl.BlockSpec((1, C_WIDTH), lambda b, i: (0, 0)),
            weight(w_out), weight(w_ff1), weight(w_ff2),
            pl.BlockSpec((1, d), lambda b, i: (0, 0)),
        ],
        out_specs=row_spec(d),
        out_shape=jax.ShapeDtypeStruct((bsz, s, d), F32),
        compiler_params=pltpu.CompilerParams(
            dimension_semantics=("arbitrary", "arbitrary"), vmem_limit_bytes=VMEM_LIMIT),
        name="out_proj_ffn",
    )(x, mod_l, oa, ob, oc_f, oc_b, cg, c_gain, w_out, w_ff1, w_ff2, final_gain)


def _rope_tables(s):
    half = HEAD_DIM // 2
    inv = ROPE_THETA ** (-jnp.arange(0, half, 2, dtype=F32) / half)
    t = jnp.arange(s)
    ang_row = (t // GRID_W).astype(F32)[:, None] * inv[None, :]
    ang_col = (t % GRID_W).astype(F32)[:, None] * inv[None, :]
    ang_1d = t.astype(F32)[:, None] * inv[None, :]

    def pair(ang):
        return (jnp.concatenate([jnp.cos(ang), jnp.cos(ang)], axis=-1),
                jnp.concatenate([-jnp.sin(ang), jnp.sin(ang)], axis=-1))

    cr, sr = pair(ang_row)
    cc, sc = pair(ang_col)
    c1, s1 = pair(ang_1d)
    tile = lambda a: jnp.tile(a, (1, LANES // a.shape[1]))
    return (tile(jnp.concatenate([cr, cc], axis=-1)), tile(jnp.concatenate([sr, sc], axis=-1)),
            tile(c1), tile(s1))


def kernel(x, c, w_mod, b_mod, w_in, a_qk_norm, diff_lambda, diff_subln, hgrn_lower_bounds, hgrn_norm,
           w_out, w_ff1, w_ff2, final_norm):
    bsz, s, d = x.shape
    depth = w_in.shape[0]
    tm = min(512, s)
    tables = _rope_tables(s)
    mod = _modulation(c, w_mod, b_mod).reshape(depth, bsz, 6, d)
    w_in_b, w_out_b, w_ff1_b, w_ff2_b = (w.astype(BF16) for w in (w_in, w_out, w_ff1, w_ff2))
    final_gain = final_norm.reshape(1, d)
    for l in range(depth):
        qk_gain = jnp.tile(a_qk_norm[l], (1, LANES // HEAD_DIM))
        qat, ka, vat, qbt, kb, vbt, cq, cff, cfb, ci, cg = _in_projection(
            x, mod[l], w_in_b, l, tables, qk_gain, tm)
        oa = _attention_a(qat, ka, vat, min(A_Q_TILES * Q_TILE, s))
        lam_init = 0.8 - 0.6 * math.exp(-0.3 * l)
        subln = jnp.tile(diff_subln[l].reshape(1, HEAD_DIM), (1, B_HEADS))
        ob = _attention_b(qbt, kb, vbt, diff_lambda[l], subln, lam_init, min(B_Q_TILES * Q_TILE, s))
        hg_gain = jnp.tile(hgrn_norm[l].reshape(1, C_DIM), (1, C_HEADS))
        oc_f, oc_b = _hgrn(cq, cff, cfb, ci, hgrn_lower_bounds, l, min(HGRN_BATCH, bsz))
        x = _out_ffn(x, mod[l], oa, ob, oc_f, oc_b, cg, hg_gain, w_out_b, w_ff1_b, w_ff2_b, final_gain,
                     l, l == depth - 1, tm)
    return x
```

```python
import functools
import math

import numpy as np
import jax
import jax.numpy as jnp
from jax import lax
from jax.experimental import pallas as pl
from jax.experimental.pallas import tpu as pltpu

F32 = jnp.float32
BF16 = jnp.bfloat16

HEAD_DIM = 64
GRID_W = 64
ROPE_THETA = 10000.0
EPS = 1e-6
FORGET_FLOOR = 1e-6
A_HEADS, A_KV_HEADS = 6, 2
B_HEADS, B_QK_DIM = 6, 32
C_HEADS, C_DIM = 4, 64
A_WIDTH = A_HEADS * HEAD_DIM
A_KV_WIDTH = A_KV_HEADS * HEAD_DIM
B_WIDTH = B_HEADS * HEAD_DIM
C_WIDTH = C_HEADS * C_DIM
LANES = 128
HGRN_CHUNK = 64
HGRN_LEAF = 8
HGRN_BATCH = 32
ONES_ROWS = 16
VT_ROWS = HEAD_DIM + ONES_ROWS
KV_CHUNK = 256
Q_TILE = 256
A_Q_TILES = 4
B_Q_TILES = 4
QK_AHEAD = 16
LOG2_E = math.log2(math.e)
VMEM_LIMIT = 56 * 1024 * 1024

OFF_AQ, OFF_AK, OFF_AV = 0, 384, 512
OFF_BQ, OFF_BK, OFF_BV = 640, 1024, 1408
OFF_C = 1792
C_DTYPES = (F32, F32, F32, BF16, F32)
IN_TOTAL = 3072


def _dot(a, b):
    return jnp.dot(a, b, preferred_element_type=F32)


def _dot_tn(a, b):
    return lax.dot_general(a, b, (((0,), (0,)), ((), ())), preferred_element_type=F32)


def _split_dot(x, w_bf16):
    hi = x.astype(BF16)
    lo = (x - hi.astype(F32)).astype(BF16)
    return _dot(hi, w_bf16) + _dot(lo, w_bf16)


def _group_ones(n, group):
    r = lax.broadcasted_iota(jnp.int32, (n, n), 0) // group
    c = lax.broadcasted_iota(jnp.int32, (n, n), 1) // group
    return jnp.where(r == c, 1.0 / group, 0.0).astype(BF16)


def _rope(xb, cos, sin_signed):
    lane = lax.broadcasted_iota(jnp.int32, xb.shape, 1)
    low = (lane % 32) < 16
    partner = jnp.where(low, pltpu.roll(xb, LANES - 16, 1), pltpu.roll(xb, 16, 1))
    return xb * cos + partner * sin_signed


def _rms_rows(x):
    return x * lax.rsqrt(jnp.mean(x * x, axis=-1, keepdims=True) + EPS)


def _mod_kernel(c_ref, w_ref, b_ref, o_ref):
    c = c_ref[...]
    cond = (c * jax.nn.sigmoid(c)).astype(BF16)
    o_ref[0] = _dot(cond, w_ref[0].astype(BF16)) + b_ref[0]


def _modulation(c, w_mod, b_mod):
    depth, d, n = w_mod.shape
    bsz = c.shape[0]
    tn = 1024
    return pl.pallas_call(
        _mod_kernel,
        grid=(depth, n // tn),
        in_specs=[
            pl.BlockSpec((bsz, d), lambda l, j: (0, 0)),
            pl.BlockSpec((1, d, tn), lambda l, j: (l, 0, j)),
            pl.BlockSpec((1, 1, tn), lambda l, j: (l, 0, j)),
        ],
        out_specs=pl.BlockSpec((1, bsz, tn), lambda l, j: (l, 0, j)),
        out_shape=jax.ShapeDtypeStruct((depth, bsz, n), F32),
        compiler_params=pltpu.CompilerParams(
            dimension_semantics=("arbitrary", "arbitrary"), vmem_limit_bytes=VMEM_LIMIT),
        name="adaln_mod",
    )(c, w_mod, b_mod.reshape(depth, 1, n))


def _inproj_kernel(x_ref, mod_ref, w_ref, cosa_ref, sina_ref, cosb_ref, sinb_ref, gain_ref,
                   qat_ref, ka_ref, vat_ref, qbt_ref, kb_ref, vbt_ref,
                   cq_ref, cff_ref, cfb_ref, ci_ref, cg_ref, h_scr, z_scr):
    x = x_ref[0]
    h = _rms_rows(x) * (1.0 + mod_ref[0, 1:2, :]) + mod_ref[0, 0:1, :]
    h_scr[...] = h.astype(BF16)

    def project(j):
        z_scr[:, j * 1024:(j + 1) * 1024] = _dot(h_scr[...], w_ref[0, :, j * 1024:(j + 1) * 1024])

    project(0)
    project(1)

    ones = _group_ones(LANES, HEAD_DIM)
    cosa, sina = cosa_ref[...], sina_ref[...]
    cosb, sinb = cosb_ref[...], sinb_ref[...]

    def a_norm_rope(zb, gain):
        ms = _split_dot(zb * zb, ones)
        return _rope(zb * lax.rsqrt(ms + EPS) * gain, cosa, sina)

    def store_vt(vt_ref, first_head, blk_t):
        ones_rows = jnp.ones((ONES_ROWS, blk_t.shape[1]), BF16)
        for j in range(LANES // HEAD_DIM):
            r0 = (first_head + j) * VT_ROWS
            vt_ref[0, r0:r0 + HEAD_DIM, :] = blk_t[j * HEAD_DIM:(j + 1) * HEAD_DIM].astype(BF16)
            vt_ref[0, r0 + HEAD_DIM:r0 + VT_ROWS, :] = ones_rows

    for cblk in range(A_WIDTH // LANES):
        zb = z_scr[:, OFF_AQ + cblk * LANES:OFF_AQ + (cblk + 1) * LANES]
        y = a_norm_rope(zb, gain_ref[0:1, :]) * (HEAD_DIM ** -0.5 * LOG2_E)
        qat_ref[0, cblk * LANES:(cblk + 1) * LANES, :] = y.T.astype(BF16)
    ka_ref[0, 0] = a_norm_rope(z_scr[:, OFF_AK:OFF_AK + LANES], gain_ref[1:2, :]).astype(BF16)
    store_vt(vat_ref, 0, z_scr[:, OFF_AV:OFF_AV + LANES].T)
    project(2)

    for cblk in range(B_WIDTH // LANES):
        lo = cblk * LANES
        zq = z_scr[:, OFF_BQ + lo:OFF_BQ + lo + LANES]
        qbt_ref[0, lo:lo + LANES, :] = (_rope(zq, cosb, sinb) * (B_QK_DIM ** -0.5 * LOG2_E)).T.astype(BF16)
    for cblk in range(B_WIDTH // LANES):
        lo = cblk * LANES
        zk = z_scr[:, OFF_BK + lo:OFF_BK + lo + LANES]
        kb_ref[0, cblk] = _rope(zk, cosb, sinb).astype(BF16)
        store_vt(vbt_ref, cblk * (LANES // HEAD_DIM), z_scr[:, OFF_BV + lo:OFF_BV + lo + LANES].T)
    for j, ref in enumerate((cq_ref, cff_ref, cfb_ref, ci_ref, cg_ref)):
        ref[0] = z_scr[:, OFF_C + j * C_WIDTH:OFF_C + (j + 1) * C_WIDTH].astype(ref.dtype)


def _in_projection(x, mod_l, w_in_bf16, layer, tables, qk_gain, tm):
    bsz, s, d = x.shape
    cosa, sina, cosb, sinb = tables
    row_spec = lambda width: pl.BlockSpec((1, tm, width), lambda b, i: (b, i, 0))
    tab_spec = pl.BlockSpec((tm, LANES), lambda b, i: (i, 0))
    col_spec = lambda height: pl.BlockSpec((1, height, tm), lambda b, i: (b, 0, i))
    key_spec = lambda blocks: pl.BlockSpec((1, blocks, tm, LANES), lambda b, i: (b, 0, i, 0))
    return pl.pallas_call(
        _inproj_kernel,
        grid=(bsz, s // tm),
        in_specs=[
            row_spec(d),
            pl.BlockSpec((1, 6, d), lambda b, i: (b, 0, 0)),
            pl.BlockSpec((1, d, IN_TOTAL), lambda b, i: (layer, 0, 0)),
            tab_spec, tab_spec, tab_spec, tab_spec,
            pl.BlockSpec((2, LANES), lambda b, i: (0, 0)),
        ],
        out_specs=[
            col_spec(A_WIDTH), key_spec(A_KV_WIDTH // LANES), col_spec(A_KV_HEADS * VT_ROWS),
            col_spec(B_WIDTH), key_spec(B_WIDTH // LANES), col_spec(B_HEADS * VT_ROWS),
        ] + [row_spec(C_WIDTH)] * len(C_DTYPES),
        out_shape=[
            jax.ShapeDtypeStruct((bsz, A_WIDTH, s), BF16),
            jax.ShapeDtypeStruct((bsz, A_KV_WIDTH // LANES, s, LANES), BF16),
            jax.ShapeDtypeStruct((bsz, A_KV_HEADS * VT_ROWS, s), BF16),
            jax.ShapeDtypeStruct((bsz, B_WIDTH, s), BF16),
            jax.ShapeDtypeStruct((bsz, B_WIDTH // LANES, s, LANES), BF16),
            jax.ShapeDtypeStruct((bsz, B_HEADS * VT_ROWS, s), BF16),
        ] + [jax.ShapeDtypeStruct((bsz, s, C_WIDTH), dt) for dt in C_DTYPES],
        scratch_shapes=[pltpu.VMEM((tm, d), BF16), pltpu.VMEM((tm, IN_TOTAL), F32)],
        compiler_params=pltpu.CompilerParams(
            dimension_semantics=("arbitrary", "arbitrary"), vmem_limit_bytes=VMEM_LIMIT),
        name="in_projection",
    )(x, mod_l, w_in_bf16, cosa, sina, cosb, sinb, qk_gain)


def _place_rows(block, start, total):
    parts = []
    if start:
        parts.append(jnp.zeros((start, block.shape[1]), block.dtype))
    parts.append(block)
    rest = total - start - block.shape[0]
    if rest:
        parts.append(jnp.zeros((rest, block.shape[1]), block.dtype))
    return jnp.concatenate(parts, axis=0) if len(parts) > 1 else block


def _attention_items(items, k_ref, vt_ref, qz_scr, m_scr, acc_scr, exp_dtype=F32):
    s = k_ref.shape[2]
    tq = acc_scr.shape[2]
    m_scr[...] = jnp.full(m_scr.shape, -1e30, F32)
    acc_scr[...] = jnp.zeros_like(acc_scr)

    units = [(i, c) for c in range(s // KV_CHUNK) for i in range(len(items))]

    def scores(u):
        i, c = units[u]
        return _dot(k_ref[0, items[i][0], c * KV_CHUNK:(c + 1) * KV_CHUNK, :], qz_scr[i])

    pending = [scores(u) for u in range(min(QK_AHEAD, len(units)))]
    for u, (i, c) in enumerate(units):
        v_off = items[i][1]
        st = pending.pop(0)
        m_old = m_scr[i]
        part = jnp.max(st.reshape(KV_CHUNK // 8, 8, tq), axis=0)
        m_new = jnp.maximum(m_old, jnp.max(part, axis=0, keepdims=True))
        m_scr[i] = m_new
        p = jnp.exp2((st - m_new[0:1]).astype(exp_dtype)).astype(BF16)
        d = _dot(vt_ref[0, v_off:v_off + VT_ROWS, c * KV_CHUNK:(c + 1) * KV_CHUNK], p)
        if u + QK_AHEAD < len(units):
            pending.append(scores(u + QK_AHEAD))
        acc_scr[i] = acc_scr[i] * jnp.exp2(m_old[0:1] - m_new[0:1]) + d
    results = []
    for i in range(len(items)):
        acc = acc_scr[i]
        results.append(acc[0:HEAD_DIM] * (1.0 / acc[HEAD_DIM:HEAD_DIM + 1]))
    return results


def _attention_scratch(heads, tq):
    n_items = heads * (tq // Q_TILE)
    return [pltpu.VMEM((n_items, LANES, Q_TILE), BF16), pltpu.VMEM((n_items, 8, Q_TILE), F32),
            pltpu.VMEM((n_items, VT_ROWS, Q_TILE), F32)]


def _attn_a_kernel(qt_ref, k_ref, vt_ref, o_ref, qz_scr, m_scr, acc_scr):
    group = A_HEADS // A_KV_HEADS
    n_tiles = qt_ref.shape[2] // Q_TILE
    items = []
    for t in range(n_tiles):
        cols = slice(t * Q_TILE, (t + 1) * Q_TILE)
        for h in range(A_HEADS):
            g = h // group
            qz_scr[len(items)] = _place_rows(qt_ref[0, h * HEAD_DIM:(h + 1) * HEAD_DIM, cols], g * HEAD_DIM, LANES)
            items.append((0, g * VT_ROWS))
    outs = _attention_items(items, k_ref, vt_ref, qz_scr, m_scr, acc_scr)
    for t in range(n_tiles):
        tile = outs[t * A_HEADS:(t + 1) * A_HEADS]
        o_ref[0, t * Q_TILE:(t + 1) * Q_TILE, :] = jnp.concatenate([o.T for o in tile], axis=-1).astype(BF16)


def _attention_a(qat, ka, vat, tq):
    bsz, _, s, _ = ka.shape
    return pl.pallas_call(
        _attn_a_kernel,
        grid=(bsz, s // tq),
        in_specs=[
            pl.BlockSpec((1, A_WIDTH, tq), lambda b, i: (b, 0, i)),
            pl.BlockSpec((1, A_KV_WIDTH // LANES, s, LANES), lambda b, i: (b, 0, 0, 0)),
            pl.BlockSpec((1, A_KV_HEADS * VT_ROWS, s), lambda b, i: (b, 0, 0)),
        ],
        out_specs=pl.BlockSpec((1, tq, A_WIDTH), lambda b, i: (b, i, 0)),
        out_shape=jax.ShapeDtypeStruct((bsz, s, A_WIDTH), BF16),
        scratch_shapes=_attention_scratch(A_HEADS, tq),
        compiler_params=pltpu.CompilerParams(
            dimension_semantics=("arbitrary", "arbitrary"), vmem_limit_bytes=VMEM_LIMIT),
        name="attn_axial_gqa",
    )(qat, ka, vat)


def _attn_b_kernel(lam_ref, gain_ref, qt_ref, k_ref, vt_ref, o_ref, qz_scr, m_scr, acc_scr, *, lam_init):
    lp = lam_ref[...]
    lam = (jnp.exp(jnp.sum(lp[0:1] * lp[1:2], axis=-1, keepdims=True))
           - jnp.exp(jnp.sum(lp[2:3] * lp[3:4], axis=-1, keepdims=True)) + lam_init)
    n_tiles = qt_ref.shape[2] // Q_TILE
    items = []
    for t in range(n_tiles):
        cols = slice(t * Q_TILE, (t + 1) * Q_TILE)
        for h in range(B_HEADS):
            pair = (h // 2) * LANES
            for comp in range(2):
                r0 = h * HEAD_DIM + comp * B_QK_DIM
                qz_scr[len(items)] = _place_rows(qt_ref[0, r0:r0 + B_QK_DIM, cols], r0 - pair, LANES)
                items.append((h // 2, h * VT_ROWS))
    comps = _attention_items(items, k_ref, vt_ref, qz_scr, m_scr, acc_scr, exp_dtype=BF16)
    for t in range(n_tiles):
        outs = []
        for h in range(B_HEADS):
            i = (t * B_HEADS + h) * 2
            o = comps[i] - lam * comps[i + 1]
            o = o * lax.rsqrt(jnp.mean(o * o, axis=0, keepdims=True) + EPS)
            outs.append(o.T)
        y = jnp.concatenate(outs, axis=-1) * gain_ref[...] * (1.0 - lam_init)
        o_ref[0, t * Q_TILE:(t + 1) * Q_TILE, :] = y.astype(BF16)


def _attention_b(qbt, kb, vbt, lam_params, subln_gain, lam_init, tq):
    bsz, _, s, _ = kb.shape
    return pl.pallas_call(
        functools.partial(_attn_b_kernel, lam_init=lam_init),
        grid=(bsz, s // tq),
        in_specs=[
            pl.BlockSpec((4, B_QK_DIM), lambda b, i: (0, 0)),
            pl.BlockSpec((1, B_WIDTH), lambda b, i: (0, 0)),
            pl.BlockSpec((1, B_WIDTH, tq), lambda b, i: (b, 0, i)),
            pl.BlockSpec((1, B_WIDTH // LANES, s, LANES), lambda b, i: (b, 0, 0, 0)),
            pl.BlockSpec((1, B_HEADS * VT_ROWS, s), lambda b, i: (b, 0, 0)),
        ],
        out_specs=pl.BlockSpec((1, tq, B_WIDTH), lambda b, i: (b, i, 0)),
        out_shape=jax.ShapeDtypeStruct((bsz, s, B_WIDTH), BF16),
        scratch_shapes=_attention_scratch(2 * B_HEADS, tq),
        compiler_params=pltpu.CompilerParams(
            dimension_semantics=("arbitrary", "arbitrary"), vmem_limit_bytes=VMEM_LIMIT),
        name="attn_differential",
    )(lam_params, subln_gain, qbt, kb, vbt)


def _hgrn_levels():
    sizes = []
    bs = HGRN_CHUNK
    while bs > HGRN_LEAF:
        sizes.append(bs)
        bs //= 2
    return sizes


def _hgrn_constants():
    c, w = HGRN_CHUNK, C_WIDTH
    t = np.arange(c)
    tri = np.stack([(t[:, None] >= t[None, :]), (t[:, None] <= t[None, :])]).astype(np.float32)
    s_of_lane = np.arange(w) % C_DIM
    sizes = _hgrn_levels()
    level = np.full((2, c, w), float(len(sizes) + 1), np.float32)
    for d in range(2):
        for li, bs in enumerate(sizes):
            upper = (t % bs) >= bs // 2
            qrow = upper if d == 0 else ~upper
            same = (t[:, None] // bs) == (s_of_lane[None, :] // bs)
            cross = qrow[:, None] & ~qrow[s_of_lane][None, :]
            level[d][same & cross] = li
        same_leaf = (t[:, None] // HGRN_LEAF) == (s_of_lane[None, :] // HGRN_LEAF)
        order = (s_of_lane[None, :] <= t[:, None]) if d == 0 else (s_of_lane[None, :] >= t[:, None])
        level[d][same_leaf & order] = len(sizes)
    half = (np.arange(LANES)[None, :] < C_DIM).astype(np.float32) * np.ones((c, 1), np.float32)
    return tri, level, half


def _block_diag(x, low_half):
    zero = jnp.zeros((x.shape[0], LANES), x.dtype)
    blocks = []
    for h in range(C_HEADS):
        tile = x[:, (h // 2) * LANES:(h // 2 + 1) * LANES]
        keep = tile * (low_half if h % 2 == 0 else 1 - low_half)
        blocks.append(jnp.concatenate([keep, zero] if h // 2 == 0 else [zero, keep], axis=1))
    return jnp.concatenate(blocks, axis=0)


def _block_diag_t(x, low_half):
    zero = jnp.zeros((LANES, LANES), x.dtype)
    rows = []
    for t in range(C_WIDTH // LANES):
        tile = x[:, t * LANES:(t + 1) * LANES]
        pair = jnp.concatenate([tile * low_half, tile * (1 - low_half)], axis=0).T
        rows.append(jnp.concatenate([pair, zero] if t == 0 else [zero, pair], axis=1))
    return jnp.concatenate(rows, axis=0)


def _ref_rows(b, bs, offset):
    parts = []
    for lo in range(0, b.shape[0], bs):
        parts.append(jnp.broadcast_to(b[lo + offset:lo + offset + 1, :], (bs, b.shape[1])))
    return jnp.concatenate(parts, axis=0)


def _level_operands(q, k, b, bs, d):
    half = bs // 2
    ref_off = half - 1 if d == 0 else half
    zeros = jnp.zeros((8, b.shape[1]), F32)
    qparts, kparts = [], []
    for r0 in range(0, b.shape[0], 8):
        lo = (r0 // bs) * bs
        ref = b[lo + ref_off:lo + ref_off + 1, :]
        upper = (r0 % bs) >= half
        rows = slice(r0, r0 + 8)
        if upper == (d == 0):
            qparts.append(q[rows] * jnp.exp2(b[rows] - ref))
            kparts.append(zeros)
        else:
            qparts.append(zeros)
            kparts.append(k[rows] * jnp.exp2(ref - b[rows]))
    return jnp.concatenate(qparts, axis=0).astype(BF16), jnp.concatenate(kparts, axis=0).astype(BF16)


def _hgrn_kernel(lb_ref, tri_ref, level_ref, half_ref, zq_f, zf_f, zv_f, zq_b, zf_b, zv_b,
                 of_ref, ob_ref, st_scr, *, layer, depth, nb):
    c, w = HGRN_CHUNK, C_WIDTH
    sizes = _hgrn_levels()
    n_lvl = len(sizes)

    @pl.when(pl.program_id(1) == 0)
    def _():
        st_scr[...] = jnp.zeros_like(st_scr)

    rows = [lb_ref[i:i + 1, :] for i in range(depth)]
    mx = functools.reduce(jnp.maximum, rows)
    es = [jnp.exp(r - mx) for r in rows]
    tot = functools.reduce(lambda a, b_: a + b_, es)
    ps = [e / tot for e in es]
    cum = functools.reduce(lambda a, b_: a + b_, ps[:layer + 1])
    one_minus_lb = 1.0 - jnp.clip(cum - ps[0], 0.0, 1.0)

    low_half = half_ref[...].astype(BF16)
    low_bool = half_ref[...] > 0.5
    owner = [[level_ref[d] == float(li) for li in range(n_lvl + 1)] for d in range(2)]
    chains = [(j, d) for j in range(nb) for d in range(2)]
    zrefs = {0: (zq_f, zf_f, zv_f), 1: (zq_b, zf_b, zv_b)}

    q, k, v, b, total = {}, {}, {}, {}, {}
    g_split = {}
    for ch in chains:
        j, d = ch
        zq, zf = zrefs[d][0][j], zrefs[d][1][j]
        v[ch] = zrefs[d][2][j]
        q[ch] = (zq * (C_DIM ** -0.5)) / (1.0 + jnp.exp2(zq * -LOG2_E))
        k[ch] = one_minus_lb * (1.0 - 1.0 / (1.0 + jnp.exp2(zf * -LOG2_E)))
        g = jnp.log2(jnp.maximum(1.0 - k[ch], FORGET_FLOOR))
        g_hi = g.astype(BF16)
        g_r = g - g_hi.astype(F32)
        g_mid = g_r.astype(BF16)
        g_lo = (g_r - g_mid.astype(F32)).astype(BF16)
        g_split[ch] = jnp.concatenate([g_hi, g_mid, g_lo], axis=1)
    for ch in chains:
        d = ch[1]
        bb = _dot(tri_ref[d], g_split[ch])
        b[ch] = bb[:, 0:w] + bb[:, w:2 * w] + bb[:, 2 * w:3 * w]
        total[ch] = b[ch][c - 1:c, :] if d == 0 else b[ch][0:1, :]

    parts = {ch: [] for ch in chains}
    for li in range(n_lvl + 1):
        for ch in chains:
            d = ch[1]
            if li < n_lvl:
                qp, kp = _level_operands(q[ch], k[ch], b[ch], sizes[li], d)
            else:
                ref = _ref_rows(b[ch], HGRN_LEAF, HGRN_LEAF // 2 - 1 if d == 0 else HGRN_LEAF // 2)
                arg = b[ch] - ref
                qp = (q[ch] * jnp.exp2(arg)).astype(BF16)
                kp = (k[ch] * jnp.exp2(-arg)).astype(BF16)
            parts[ch].append(_dot(qp, _block_diag_t(kp, low_half)))
    a = {}
    for ch in chains:
        acc = parts[ch][0]
        for li in range(1, n_lvl + 1):
            acc = jnp.where(owner[ch[1]][li], parts[ch][li], acc)
        a[ch] = acc.astype(BF16)

    o_intra, o_inter, upd = {}, {}, {}
    for ch in chains:
        j, d = ch
        o_intra[ch] = _dot(a[ch], _block_diag(v[ch], low_half))
        qe = (q[ch] * jnp.exp2(b[ch])).astype(BF16)
        o_inter[ch] = _dot(qe, _block_diag_t(st_scr[d, j].astype(BF16), low_half))
        kb = (k[ch] * jnp.exp2(total[ch] - b[ch])).astype(BF16)
        tiles = []
        for t in range(w // LANES):
            lanes = slice(t * LANES, (t + 1) * LANES)
            full = _dot_tn(v[ch][:, lanes], kb[:, lanes])
            tiles.append(jnp.where(low_bool, full[0:C_DIM], full[C_DIM:2 * C_DIM]))
        upd[ch] = jnp.concatenate(tiles, axis=1)
    for ch in chains:
        j, d = ch
        st_scr[d, j] = st_scr[d, j] * jnp.exp2(total[ch]) + upd[ch]
        (of_ref if d == 0 else ob_ref)[j] = o_intra[ch] + o_inter[ch]


def _hgrn(cq, cff, cfb, ci, lower_bounds, layer, nb):
    bsz, s, _ = cq.shape
    depth = lower_bounds.shape[0]
    n_chunks = s // HGRN_CHUNK
    c, w = HGRN_CHUNK, C_WIDTH
    tri, level, half = _hgrn_constants()
    consts = (jnp.asarray(tri, BF16), jnp.asarray(level), jnp.asarray(half))
    full = lambda a: pl.BlockSpec(a.shape, lambda g, i: (0,) * a.ndim)
    last = n_chunks - 1
    fwd = pl.BlockSpec((nb, c, w), lambda g, i: (g, i, 0))
    bwd = pl.BlockSpec((nb, c, w), lambda g, i: (g, last - i, 0))
    return pl.pallas_call(
        functools.partial(_hgrn_kernel, layer=layer, depth=depth, nb=nb),
        grid=(bsz // nb, n_chunks),
        in_specs=[full(lower_bounds)] + [full(a) for a in consts] + [fwd, fwd, fwd, bwd, bwd, bwd],
        out_specs=[fwd, bwd],
        out_shape=[jax.ShapeDtypeStruct((bsz, s, w), F32)] * 2,
        scratch_shapes=[pltpu.VMEM((2, nb, C_DIM, w), F32)],
        compiler_params=pltpu.CompilerParams(
            dimension_semantics=("arbitrary", "arbitrary"), vmem_limit_bytes=VMEM_LIMIT),
        name="hgrn2_bidir",
    )(lower_bounds, *consts, cq, cff, ci, cq, cfb, ci)


def _out_ffn_kernel(x_ref, mod_ref, oa_ref, ob_ref, cf_ref, cb_ref, zg_ref, cgain_ref,
                    wo_ref, w1_ref, w2_ref, fg_ref, o_ref, *, final, ff_chunk):
    oc = cf_ref[0] + cb_ref[0]
    ms = _split_dot(oc * oc, _group_ones(C_WIDTH, C_DIM))
    zg = zg_ref[0]
    oc = oc * lax.rsqrt(ms + EPS) * cgain_ref[...] * (zg * jax.nn.sigmoid(zg))
    mix = jnp.concatenate([oa_ref[0], ob_ref[0], oc.astype(BF16)], axis=-1)
    x1 = x_ref[0] + mod_ref[0, 2:3, :] * _dot(mix, wo_ref[0])
    h = (_rms_rows(x1) * (1.0 + mod_ref[0, 4:5, :]) + mod_ref[0, 3:4, :]).astype(BF16)
    d_ff = w1_ref.shape[2]
    acc = jnp.zeros_like(x1)
    for j in range(d_ff // ff_chunk):
        u = _dot(h, w1_ref[0, :, j * ff_chunk:(j + 1) * ff_chunk])
        u = jnp.square(jnp.maximum(u, 0.0)).astype(BF16)
        acc = acc + _dot(u, w2_ref[0, j * ff_chunk:(j + 1) * ff_chunk, :])
    x2 = x1 + mod_ref[0, 5:6, :] * acc
    if final:
        x2 = _rms_rows(x2) * fg_ref[...]
    o_ref[0] = x2


def _out_ffn(x, mod_l, oa, ob, oc_f, oc_b, cg, c_gain, w_out, w_ff1, w_ff2, final_gain, layer, final, tm):
    bsz, s, d = x.shape
    d_ff = w_ff1.shape[2]
    row_spec = lambda width: pl.BlockSpec((1, tm, width), lambda b, i: (b, i, 0))
    weight = lambda a: pl.BlockSpec((1,) + a.shape[1:], lambda b, i: (layer, 0, 0),
                                    pipeline_mode=pl.Buffered(1))
    return pl.pallas_call(
        functools.partial(_out_ffn_kernel, final=final, ff_chunk=1024),
        grid=(bsz, s // tm),
        in_specs=[
            row_spec(d),
            pl.BlockSpec((1, 6, d), lambda b, i: (b, 0, 0)),
            row_spec(A_WIDTH), row_spec(B_WIDTH), row_spec(C_WIDTH), row_spec(C_WIDTH), row_spec(C_WIDTH),
            pl.BlockSpec((1, C_WIDTH), lambda b, i: (0, 0)),
            weight(w_out), weight(w_ff1), weight(w_ff2),
            pl.BlockSpec((1, d), lambda b, i: (0, 0)),
        ],
        out_specs=row_spec(d),
        out_shape=jax.ShapeDtypeStruct((bsz, s, d), F32),
        compiler_params=pltpu.CompilerParams(
            dimension_semantics=("arbitrary", "arbitrary"), vmem_limit_bytes=VMEM_LIMIT),
        name="out_proj_ffn",
    )(x, mod_l, oa, ob, oc_f, oc_b, cg, c_gain, w_out, w_ff1, w_ff2, final_gain)


def _rope_tables(s):
    half = HEAD_DIM // 2
    inv = ROPE_THETA ** (-jnp.arange(0, half, 2, dtype=F32) / half)
    t = jnp.arange(s)
    ang_row = (t // GRID_W).astype(F32)[:, None] * inv[None, :]
    ang_col = (t % GRID_W).astype(F32)[:, None] * inv[None, :]
    ang_1d = t.astype(F32)[:, None] * inv[None, :]

    def pair(ang):
        return (jnp.concatenate([jnp.cos(ang), jnp.cos(ang)], axis=-1),
                jnp.concatenate([-jnp.sin(ang), jnp.sin(ang)], axis=-1))

    cr, sr = pair(ang_row)
    cc, sc = pair(ang_col)
    c1, s1 = pair(ang_1d)
    tile = lambda a: jnp.tile(a, (1, LANES // a.shape[1]))
    return (tile(jnp.concatenate([cr, cc], axis=-1)), tile(jnp.concatenate([sr, sc], axis=-1)),
            tile(c1), tile(s1))


def kernel(x, c, w_mod, b_mod, w_in, a_qk_norm, diff_lambda, diff_subln, hgrn_lower_bounds, hgrn_norm,
           w_out, w_ff1, w_ff2, final_norm):
    bsz, s, d = x.shape
    depth = w_in.shape[0]
    tm = min(512, s)
    tables = _rope_tables(s)
    mod = _modulation(c, w_mod, b_mod).reshape(depth, bsz, 6, d)
    w_in_b, w_out_b, w_ff1_b, w_ff2_b = (w.astype(BF16) for w in (w_in, w_out, w_ff1, w_ff2))
    final_gain = final_norm.reshape(1, d)
    for l in range(depth):
        qk_gain = jnp.tile(a_qk_norm[l], (1, LANES // HEAD_DIM))
        qat, ka, vat, qbt, kb, vbt, cq, cff, cfb, ci, cg = _in_projection(
            x, mod[l], w_in_b, l, tables, qk_gain, tm)
        oa = _attention_a(qat, ka, vat, min(A_Q_TILES * Q_TILE, s))
        lam_init = 0.8 - 0.6 * math.exp(-0.3 * l)
        subln = jnp.tile(diff_subln[l].reshape(1, HEAD_DIM), (1, B_HEADS))
        ob = _attention_b(qbt, kb, vbt, diff_lambda[l], subln, lam_init, min(B_Q_TILES * Q_TILE, s))
        hg_gain = jnp.tile(hgrn_norm[l].reshape(1, C_DIM), (1, C_HEADS))
        oc_f, oc_b = _hgrn(cq, cff, cfb, ci, hgrn_lower_bounds, l, min(HGRN_BATCH, bsz))
        x = _out_ffn(x, mod[l], oa, ob, oc_f, oc_b, cg, hg_gain, w_out_b, w_ff1_b, w_ff2_b, final_gain,
                     l, l == depth - 1, tm)
    return x
```

```python
import functools
import math

import numpy as np
import jax
import jax.numpy as jnp
from jax import lax
from jax.experimental import pallas as pl
from jax.experimental.pallas import tpu as pltpu

F32 = jnp.float32
BF16 = jnp.bfloat16

HEAD_DIM = 64
GRID_W = 64
ROPE_THETA = 10000.0
EPS = 1e-6
FORGET_FLOOR = 1e-6
A_HEADS, A_KV_HEADS = 6, 2
B_HEADS, B_QK_DIM = 6, 32
C_HEADS, C_DIM = 4, 64
A_WIDTH = A_HEADS * HEAD_DIM
A_KV_WIDTH = A_KV_HEADS * HEAD_DIM
B_WIDTH = B_HEADS * HEAD_DIM
C_WIDTH = C_HEADS * C_DIM
LANES = 128
HGRN_CHUNK = 64
HGRN_LEAF = 8
HGRN_BATCH = 16
ONES_ROWS = 16
VT_ROWS = HEAD_DIM + ONES_ROWS
KV_CHUNK = 256
Q_TILE = 256
A_Q_TILES = 4
B_Q_TILES = 4
QK_AHEAD = 16
LOG2_E = math.log2(math.e)
ROW_TILE = 512
COL_CHUNK = 1024
VMEM_LIMIT = 56 * 1024 * 1024

C_DTYPES = (F32, F32, F32, BF16, F32)
OFF_AQ = 0
OFF_AK = OFF_AQ + A_WIDTH
OFF_AV = OFF_AK + A_KV_WIDTH
OFF_BQ = OFF_AV + A_KV_WIDTH
OFF_BK = OFF_BQ + B_WIDTH
OFF_BV = OFF_BK + B_WIDTH
OFF_C = OFF_BV + B_WIDTH
IN_TOTAL = OFF_C + len(C_DTYPES) * C_WIDTH


def _dot(a, b):
    return jnp.dot(a, b, preferred_element_type=F32)


def _dot_tn(a, b):
    return lax.dot_general(a, b, (((0,), (0,)), ((), ())), preferred_element_type=F32)


def _split_dot(x, w_bf16):
    hi = x.astype(BF16)
    lo = (x - hi.astype(F32)).astype(BF16)
    return _dot(hi, w_bf16) + _dot(lo, w_bf16)


def _group_ones(n, group):
    r = lax.broadcasted_iota(jnp.int32, (n, n), 0) // group
    c = lax.broadcasted_iota(jnp.int32, (n, n), 1) // group
    return jnp.where(r == c, 1.0 / group, 0.0).astype(BF16)


def _rope(xb, cos, sin_signed):
    lane = lax.broadcasted_iota(jnp.int32, xb.shape, 1)
    low = (lane % 32) < 16
    partner = jnp.where(low, pltpu.roll(xb, LANES - 16, 1), pltpu.roll(xb, 16, 1))
    return xb * cos + partner * sin_signed


def _rms_rows(x):
    return x * lax.rsqrt(jnp.mean(x * x, axis=-1, keepdims=True) + EPS)


def _mod_kernel(c_ref, w_ref, b_ref, o_ref):
    c = c_ref[...]
    cond = (c * jax.nn.sigmoid(c)).astype(BF16)
    o_ref[0] = _dot(cond, w_ref[0].astype(BF16)) + b_ref[0]


def _modulation(c, w_mod, b_mod):
    depth, d, n = w_mod.shape
    bsz = c.shape[0]
    tn = COL_CHUNK
    return pl.pallas_call(
        _mod_kernel,
        grid=(depth, n // tn),
        in_specs=[
            pl.BlockSpec((bsz, d), lambda l, j: (0, 0)),
            pl.BlockSpec((1, d, tn), lambda l, j: (l, 0, j)),
            pl.BlockSpec((1, 1, tn), lambda l, j: (l, 0, j)),
        ],
        out_specs=pl.BlockSpec((1, bsz, tn), lambda l, j: (l, 0, j)),
        out_shape=jax.ShapeDtypeStruct((depth, bsz, n), F32),
        compiler_params=pltpu.CompilerParams(
            dimension_semantics=("arbitrary", "arbitrary"), vmem_limit_bytes=VMEM_LIMIT),
        name="adaln_mod",
    )(c, w_mod, b_mod.reshape(depth, 1, n))


def _inproj_kernel(x_ref, mod_ref, w_ref, cosa_ref, sina_ref, cosb_ref, sinb_ref, gain_ref,
                   qat_ref, ka_ref, vat_ref, qbt_ref, kb_ref, vbt_ref,
                   cq_ref, cff_ref, cfb_ref, ci_ref, cg_ref, h_scr, z_scr):
    x = x_ref[0]
    h = _rms_rows(x) * (1.0 + mod_ref[0, 1:2, :]) + mod_ref[0, 0:1, :]
    h_scr[...] = h.astype(BF16)

    def project(j):
        cols = slice(j * COL_CHUNK, (j + 1) * COL_CHUNK)
        z_scr[:, cols] = _dot(h_scr[...], w_ref[0, :, cols])

    project(0)
    project(1)

    ones = _group_ones(LANES, HEAD_DIM)
    cosa, sina = cosa_ref[...], sina_ref[...]
    cosb, sinb = cosb_ref[...], sinb_ref[...]

    def a_norm_rope(zb, gain):
        ms = _split_dot(zb * zb, ones)
        return _rope(zb * lax.rsqrt(ms + EPS) * gain, cosa, sina)

    def store_vt(vt_ref, first_head, blk_t):
        ones_rows = jnp.ones((ONES_ROWS, blk_t.shape[1]), BF16)
        for j in range(LANES // HEAD_DIM):
            r0 = (first_head + j) * VT_ROWS
            vt_ref[0, r0:r0 + HEAD_DIM, :] = blk_t[j * HEAD_DIM:(j + 1) * HEAD_DIM].astype(BF16)
            vt_ref[0, r0 + HEAD_DIM:r0 + VT_ROWS, :] = ones_rows

    for cblk in range(A_WIDTH // LANES):
        zb = z_scr[:, OFF_AQ + cblk * LANES:OFF_AQ + (cblk + 1) * LANES]
        y = a_norm_rope(zb, gain_ref[0:1, :]) * (HEAD_DIM ** -0.5 * LOG2_E)
        qat_ref[0, cblk * LANES:(cblk + 1) * LANES, :] = y.T.astype(BF16)
    ka_ref[0, 0] = a_norm_rope(z_scr[:, OFF_AK:OFF_AK + LANES], gain_ref[1:2, :]).astype(BF16)
    store_vt(vat_ref, 0, z_scr[:, OFF_AV:OFF_AV + LANES].T)
    project(2)

    for cblk in range(B_WIDTH // LANES):
        lo = cblk * LANES
        zq = z_scr[:, OFF_BQ + lo:OFF_BQ + lo + LANES]
        qbt_ref[0, lo:lo + LANES, :] = (_rope(zq, cosb, sinb) * (B_QK_DIM ** -0.5 * LOG2_E)).T.astype(BF16)
    for cblk in range(B_WIDTH // LANES):
        lo = cblk * LANES
        zk = z_scr[:, OFF_BK + lo:OFF_BK + lo + LANES]
        kb_ref[0, cblk] = _rope(zk, cosb, sinb).astype(BF16)
        store_vt(vbt_ref, cblk * (LANES // HEAD_DIM), z_scr[:, OFF_BV + lo:OFF_BV + lo + LANES].T)
    for j, ref in enumerate((cq_ref, cff_ref, cfb_ref, ci_ref, cg_ref)):
        ref[0] = z_scr[:, OFF_C + j * C_WIDTH:OFF_C + (j + 1) * C_WIDTH].astype(ref.dtype)


def _in_projection(x, mod_l, w_in_bf16, layer, tables, qk_gain, tm):
    bsz, s, d = x.shape
    cosa, sina, cosb, sinb = tables
    row_spec = lambda width: pl.BlockSpec((1, tm, width), lambda b, i: (b, i, 0))
    tab_spec = pl.BlockSpec((tm, LANES), lambda b, i: (i, 0))
    col_spec = lambda height: pl.BlockSpec((1, height, tm), lambda b, i: (b, 0, i))
    key_spec = lambda blocks: pl.BlockSpec((1, blocks, tm, LANES), lambda b, i: (b, 0, i, 0))
    return pl.pallas_call(
        _inproj_kernel,
        grid=(bsz, s // tm),
        in_specs=[
            row_spec(d),
            pl.BlockSpec((1, 6, d), lambda b, i: (b, 0, 0)),
            pl.BlockSpec((1, d, IN_TOTAL), lambda b, i: (layer, 0, 0)),
            tab_spec, tab_spec, tab_spec, tab_spec,
            pl.BlockSpec((2, LANES), lambda b, i: (0, 0)),
        ],
        out_specs=[
            col_spec(A_WIDTH), key_spec(A_KV_WIDTH // LANES), col_spec(A_KV_HEADS * VT_ROWS),
            col_spec(B_WIDTH), key_spec(B_WIDTH // LANES), col_spec(B_HEADS * VT_ROWS),
        ] + [row_spec(C_WIDTH)] * len(C_DTYPES),
        out_shape=[
            jax.ShapeDtypeStruct((bsz, A_WIDTH, s), BF16),
            jax.ShapeDtypeStruct((bsz, A_KV_WIDTH // LANES, s, LANES), BF16),
            jax.ShapeDtypeStruct((bsz, A_KV_HEADS * VT_ROWS, s), BF16),
            jax.ShapeDtypeStruct((bsz, B_WIDTH, s), BF16),
            jax.ShapeDtypeStruct((bsz, B_WIDTH // LANES, s, LANES), BF16),
            jax.ShapeDtypeStruct((bsz, B_HEADS * VT_ROWS, s), BF16),
        ] + [jax.ShapeDtypeStruct((bsz, s, C_WIDTH), dt) for dt in C_DTYPES],
        scratch_shapes=[pltpu.VMEM((tm, d), BF16), pltpu.VMEM((tm, IN_TOTAL), F32)],
        compiler_params=pltpu.CompilerParams(
            dimension_semantics=("arbitrary", "arbitrary"), vmem_limit_bytes=VMEM_LIMIT),
        name="in_projection",
    )(x, mod_l, w_in_bf16, cosa, sina, cosb, sinb, qk_gain)


def _place_rows(block, start, total):
    parts = []
    if start:
        parts.append(jnp.zeros((start, block.shape[1]), block.dtype))
    parts.append(block)
    rest = total - start - block.shape[0]
    if rest:
        parts.append(jnp.zeros((rest, block.shape[1]), block.dtype))
    return jnp.concatenate(parts, axis=0) if len(parts) > 1 else block


def _attention_items(items, k_ref, vt_ref, qz_scr, m_scr, acc_scr, exp_dtype=F32):
    s = k_ref.shape[2]
    tq = acc_scr.shape[2]
    m_scr[...] = jnp.full(m_scr.shape, -1e30, F32)
    acc_scr[...] = jnp.zeros_like(acc_scr)

    units = [(i, c) for c in range(s // KV_CHUNK) for i in range(len(items))]

    def scores(u):
        i, c = units[u]
        return _dot(k_ref[0, items[i][0], c * KV_CHUNK:(c + 1) * KV_CHUNK, :], qz_scr[i])

    pending = [scores(u) for u in range(min(QK_AHEAD, len(units)))]
    for u, (i, c) in enumerate(units):
        v_off = items[i][1]
        st = pending.pop(0)
        m_old = m_scr[i]
        part = jnp.max(st.reshape(KV_CHUNK // 8, 8, tq), axis=0)
        m_new = jnp.maximum(m_old, jnp.max(part, axis=0, keepdims=True))
        m_scr[i] = m_new
        p = jnp.exp2((st - m_new[0:1]).astype(exp_dtype)).astype(BF16)
        d = _dot(vt_ref[0, v_off:v_off + VT_ROWS, c * KV_CHUNK:(c + 1) * KV_CHUNK], p)
        if u + QK_AHEAD < len(units):
            pending.append(scores(u + QK_AHEAD))
        acc_scr[i] = acc_scr[i] * jnp.exp2(m_old[0:1] - m_new[0:1]) + d
    results = []
    for i in range(len(items)):
        acc = acc_scr[i]
        results.append(acc[0:HEAD_DIM] * (1.0 / acc[HEAD_DIM:HEAD_DIM + 1]))
    return results


def _attention_scratch(heads, tq):
    n_items = heads * (tq // Q_TILE)
    return [pltpu.VMEM((n_items, LANES, Q_TILE), BF16), pltpu.VMEM((n_items, 8, Q_TILE), F32),
            pltpu.VMEM((n_items, VT_ROWS, Q_TILE), F32)]


def _attn_a_kernel(qt_ref, k_ref, vt_ref, o_ref, qz_scr, m_scr, acc_scr):
    group = A_HEADS // A_KV_HEADS
    n_tiles = qt_ref.shape[2] // Q_TILE
    items = []
    for t in range(n_tiles):
        cols = slice(t * Q_TILE, (t + 1) * Q_TILE)
        for h in range(A_HEADS):
            g = h // group
            qz_scr[len(items)] = _place_rows(qt_ref[0, h * HEAD_DIM:(h + 1) * HEAD_DIM, cols], g * HEAD_DIM, LANES)
            items.append((0, g * VT_ROWS))
    outs = _attention_items(items, k_ref, vt_ref, qz_scr, m_scr, acc_scr)
    for t in range(n_tiles):
        tile = outs[t * A_HEADS:(t + 1) * A_HEADS]
        o_ref[0, t * Q_TILE:(t + 1) * Q_TILE, :] = jnp.concatenate([o.T for o in tile], axis=-1).astype(BF16)


def _attention_a(qat, ka, vat, tq):
    bsz, _, s, _ = ka.shape
    return pl.pallas_call(
        _attn_a_kernel,
        grid=(bsz, s // tq),
        in_specs=[
            pl.BlockSpec((1, A_WIDTH, tq), lambda b, i: (b, 0, i)),
            pl.BlockSpec((1, A_KV_WIDTH // LANES, s, LANES), lambda b, i: (b, 0, 0, 0)),
            pl.BlockSpec((1, A_KV_HEADS * VT_ROWS, s), lambda b, i: (b, 0, 0)),
        ],
        out_specs=pl.BlockSpec((1, tq, A_WIDTH), lambda b, i: (b, i, 0)),
        out_shape=jax.ShapeDtypeStruct((bsz, s, A_WIDTH), BF16),
        scratch_shapes=_attention_scratch(A_HEADS, tq),
        compiler_params=pltpu.CompilerParams(
            dimension_semantics=("arbitrary", "arbitrary"), vmem_limit_bytes=VMEM_LIMIT),
        name="attn_axial_gqa",
    )(qat, ka, vat)


def _attn_b_kernel(lam_ref, gain_ref, qt_ref, k_ref, vt_ref, o_ref, qz_scr, m_scr, acc_scr, *, lam_init):
    lp = lam_ref[...]
    lam = (jnp.exp(jnp.sum(lp[0:1] * lp[1:2], axis=-1, keepdims=True))
           - jnp.exp(jnp.sum(lp[2:3] * lp[3:4], axis=-1, keepdims=True)) + lam_init)
    n_tiles = qt_ref.shape[2] // Q_TILE
    items = []
    for t in range(n_tiles):
        cols = slice(t * Q_TILE, (t + 1) * Q_TILE)
        for h in range(B_HEADS):
            pair = (h // 2) * LANES
            for comp in range(2):
                r0 = h * HEAD_DIM + comp * B_QK_DIM
                qz_scr[len(items)] = _place_rows(qt_ref[0, r0:r0 + B_QK_DIM, cols], r0 - pair, LANES)
                items.append((h // 2, h * VT_ROWS))
    comps = _attention_items(items, k_ref, vt_ref, qz_scr, m_scr, acc_scr, exp_dtype=BF16)
    for t in range(n_tiles):
        outs = []
        for h in range(B_HEADS):
            i = (t * B_HEADS + h) * 2
            o = comps[i] - lam * comps[i + 1]
            o = o * lax.rsqrt(jnp.mean(o * o, axis=0, keepdims=True) + EPS)
            outs.append(o.T)
        y = jnp.concatenate(outs, axis=-1) * gain_ref[...] * (1.0 - lam_init)
        o_ref[0, t * Q_TILE:(t + 1) * Q_TILE, :] = y.astype(BF16)


def _attention_b(qbt, kb, vbt, lam_params, subln_gain, lam_init, tq):
    bsz, _, s, _ = kb.shape
    return pl.pallas_call(
        functools.partial(_attn_b_kernel, lam_init=lam_init),
        grid=(bsz, s // tq),
        in_specs=[
            pl.BlockSpec((4, B_QK_DIM), lambda b, i: (0, 0)),
            pl.BlockSpec((1, B_WIDTH), lambda b, i: (0, 0)),
            pl.BlockSpec((1, B_WIDTH, tq), lambda b, i: (b, 0, i)),
            pl.BlockSpec((1, B_WIDTH // LANES, s, LANES), lambda b, i: (b, 0, 0, 0)),
            pl.BlockSpec((1, B_HEADS * VT_ROWS, s), lambda b, i: (b, 0, 0)),
        ],
        out_specs=pl.BlockSpec((1, tq, B_WIDTH), lambda b, i: (b, i, 0)),
        out_shape=jax.ShapeDtypeStruct((bsz, s, B_WIDTH), BF16),
        scratch_shapes=_attention_scratch(2 * B_HEADS, tq),
        compiler_params=pltpu.CompilerParams(
            dimension_semantics=("arbitrary", "arbitrary"), vmem_limit_bytes=VMEM_LIMIT),
        name="attn_differential",
    )(lam_params, subln_gain, qbt, kb, vbt)


def _hgrn_levels():
    sizes = []
    bs = HGRN_CHUNK
    while bs > HGRN_LEAF:
        sizes.append(bs)
        bs //= 2
    return sizes


def _hgrn_constants():
    c, w = HGRN_CHUNK, C_WIDTH
    t = np.arange(c)
    tri = np.stack([(t[:, None] >= t[None, :]), (t[:, None] <= t[None, :])]).astype(np.float32)
    s_of_lane = np.arange(w) % C_DIM
    sizes = _hgrn_levels()
    level = np.full((2, c, w), float(len(sizes) + 1), np.float32)
    for d in range(2):
        for li, bs in enumerate(sizes):
            upper = (t % bs) >= bs // 2
            qrow = upper if d == 0 else ~upper
            same = (t[:, None] // bs) == (s_of_lane[None, :] // bs)
            cross = qrow[:, None] & ~qrow[s_of_lane][None, :]
            level[d][same & cross] = li
        same_leaf = (t[:, None] // HGRN_LEAF) == (s_of_lane[None, :] // HGRN_LEAF)
        order = (s_of_lane[None, :] <= t[:, None]) if d == 0 else (s_of_lane[None, :] >= t[:, None])
        level[d][same_leaf & order] = len(sizes)
    half = (np.arange(LANES)[None, :] < C_DIM).astype(np.float32) * np.ones((c, 1), np.float32)
    return tri, level, half


def _block_diag(x, low_half):
    zero = jnp.zeros((x.shape[0], LANES), x.dtype)
    blocks = []
    for h in range(C_HEADS):
        tile = x[:, (h // 2) * LANES:(h // 2 + 1) * LANES]
        keep = tile * (low_half if h % 2 == 0 else 1 - low_half)
        blocks.append(jnp.concatenate([keep, zero] if h // 2 == 0 else [zero, keep], axis=1))
    return jnp.concatenate(blocks, axis=0)


def _block_diag_t(x, low_half):
    zero = jnp.zeros((LANES, LANES), x.dtype)
    rows = []
    for t in range(C_WIDTH // LANES):
        tile = x[:, t * LANES:(t + 1) * LANES]
        pair = jnp.concatenate([tile * low_half, tile * (1 - low_half)], axis=0).T
        rows.append(jnp.concatenate([pair, zero] if t == 0 else [zero, pair], axis=1))
    return jnp.concatenate(rows, axis=0)


def _ref_rows(b, bs, offset):
    parts = []
    for lo in range(0, b.shape[0], bs):
        parts.append(jnp.broadcast_to(b[lo + offset:lo + offset + 1, :], (bs, b.shape[1])))
    return jnp.concatenate(parts, axis=0)


def _level_operands(q, k, b, bs, d):
    half = bs // 2
    ref_off = half - 1 if d == 0 else half
    zeros = jnp.zeros((8, b.shape[1]), F32)
    qparts, kparts = [], []
    for r0 in range(0, b.shape[0], 8):
        lo = (r0 // bs) * bs
        ref = b[lo + ref_off:lo + ref_off + 1, :]
        upper = (r0 % bs) >= half
        rows = slice(r0, r0 + 8)
        if upper == (d == 0):
            qparts.append(q[rows] * jnp.exp2(b[rows] - ref))
            kparts.append(zeros)
        else:
            qparts.append(zeros)
            kparts.append(k[rows] * jnp.exp2(ref - b[rows]))
    return jnp.concatenate(qparts, axis=0).astype(BF16), jnp.concatenate(kparts, axis=0).astype(BF16)


def _hgrn_kernel(lb_ref, tri_ref, level_ref, half_ref, zq_f, zf_f, zv_f, zq_b, zf_b, zv_b,
                 of_ref, ob_ref, st_scr, *, layer, depth, nb):
    c, w = HGRN_CHUNK, C_WIDTH
    sizes = _hgrn_levels()
    n_lvl = len(sizes)

    @pl.when(pl.program_id(1) == 0)
    def _():
        st_scr[...] = jnp.zeros_like(st_scr)

    rows = [lb_ref[i:i + 1, :] for i in range(depth)]
    mx = functools.reduce(jnp.maximum, rows)
    es = [jnp.exp(r - mx) for r in rows]
    tot = functools.reduce(lambda a, b_: a + b_, es)
    ps = [e / tot for e in es]
    cum = functools.reduce(lambda a, b_: a + b_, ps[:layer + 1])
    one_minus_lb = 1.0 - jnp.clip(cum - ps[0], 0.0, 1.0)

    low_half = half_ref[...].astype(BF16)
    low_bool = half_ref[...] > 0.5
    owner = [[level_ref[d] == float(li) for li in range(n_lvl + 1)] for d in range(2)]
    chains = [(j, d) for j in range(nb) for d in range(2)]
    zrefs = {0: (zq_f, zf_f, zv_f), 1: (zq_b, zf_b, zv_b)}

    q, k, v, b, total = {}, {}, {}, {}, {}
    g_split = {}
    for ch in chains:
        j, d = ch
        zq, zf = zrefs[d][0][j], zrefs[d][1][j]
        v[ch] = zrefs[d][2][j]
        q[ch] = (zq * (C_DIM ** -0.5)) / (1.0 + jnp.exp2(zq * -LOG2_E))
        k[ch] = one_minus_lb * (1.0 - 1.0 / (1.0 + jnp.exp2(zf * -LOG2_E)))
        g = jnp.log2(jnp.maximum(1.0 - k[ch], FORGET_FLOOR))
        g_hi = g.astype(BF16)
        g_r = g - g_hi.astype(F32)
        g_mid = g_r.astype(BF16)
        g_lo = (g_r - g_mid.astype(F32)).astype(BF16)
        g_split[ch] = jnp.concatenate([g_hi, g_mid, g_lo], axis=1)
    for ch in chains:
        d = ch[1]
        bb = _dot(tri_ref[d], g_split[ch])
        b[ch] = bb[:, 0:w] + bb[:, w:2 * w] + bb[:, 2 * w:3 * w]
        total[ch] = b[ch][c - 1:c, :] if d == 0 else b[ch][0:1, :]

    parts = {ch: [] for ch in chains}
    for li in range(n_lvl + 1):
        for ch in chains:
            d = ch[1]
            if li < n_lvl:
                qp, kp = _level_operands(q[ch], k[ch], b[ch], sizes[li], d)
            else:
                ref = _ref_rows(b[ch], HGRN_LEAF, HGRN_LEAF // 2 - 1 if d == 0 else HGRN_LEAF // 2)
                arg = b[ch] - ref
                qp = (q[ch] * jnp.exp2(arg)).astype(BF16)
                kp = (k[ch] * jnp.exp2(-arg)).astype(BF16)
            parts[ch].append(_dot(qp, _block_diag_t(kp, low_half)))
    a = {}
    for ch in chains:
        acc = parts[ch][0]
        for li in range(1, n_lvl + 1):
            acc = jnp.where(owner[ch[1]][li], parts[ch][li], acc)
        a[ch] = acc.astype(BF16)

    o_intra, o_inter, upd = {}, {}, {}
    for ch in chains:
        j, d = ch
        o_intra[ch] = _dot(a[ch], _block_diag(v[ch], low_half))
        qe = (q[ch] * jnp.exp2(b[ch])).astype(BF16)
        o_inter[ch] = _dot(qe, _block_diag_t(st_scr[d, j].astype(BF16), low_half))
        kb = (k[ch] * jnp.exp2(total[ch] - b[ch])).astype(BF16)
        tiles = []
        for t in range(w // LANES):
            lanes = slice(t * LANES, (t + 1) * LANES)
            full = _dot_tn(v[ch][:, lanes], kb[:, lanes])
            tiles.append(jnp.where(low_bool, full[0:C_DIM], full[C_DIM:2 * C_DIM]))
        upd[ch] = jnp.concatenate(tiles, axis=1)
    for ch in chains:
        j, d = ch
        st_scr[d, j] = st_scr[d, j] * jnp.exp2(total[ch]) + upd[ch]
        (of_ref if d == 0 else ob_ref)[j] = o_intra[ch] + o_inter[ch]


def _hgrn(cq, cff, cfb, ci, lower_bounds, layer, nb):
    bsz, s, _ = cq.shape
    depth = lower_bounds.shape[0]
    n_chunks = s // HGRN_CHUNK
    c, w = HGRN_CHUNK, C_WIDTH
    tri, level, half = _hgrn_constants()
    consts = (jnp.asarray(tri, BF16), jnp.asarray(level), jnp.asarray(half))
    full = lambda a: pl.BlockSpec(a.shape, lambda g, i: (0,) * a.ndim)
    last = n_chunks - 1
    fwd = pl.BlockSpec((nb, c, w), lambda g, i: (g, i, 0))
    bwd = pl.BlockSpec((nb, c, w), lambda g, i: (g, last - i, 0))
    return pl.pallas_call(
        functools.partial(_hgrn_kernel, layer=layer, depth=depth, nb=nb),
        grid=(bsz // nb, n_chunks),
        in_specs=[full(lower_bounds)] + [full(a) for a in consts] + [fwd, fwd, fwd, bwd, bwd, bwd],
        out_specs=[fwd, bwd],
        out_shape=[jax.ShapeDtypeStruct((bsz, s, w), F32)] * 2,
        scratch_shapes=[pltpu.VMEM((2, nb, C_DIM, w), F32)],
        compiler_params=pltpu.CompilerParams(
            dimension_semantics=("arbitrary", "arbitrary"), vmem_limit_bytes=VMEM_LIMIT),
        name="hgrn2_bidir",
    )(lower_bounds, *consts, cq, cff, ci, cq, cfb, ci)


def _out_ffn_kernel(x_ref, mod_ref, oa_ref, ob_ref, cf_ref, cb_ref, zg_ref, cgain_ref,
                    wo_ref, w1_ref, w2_ref, fg_ref, o_ref, *, final, ff_chunk):
    oc = cf_ref[0] + cb_ref[0]
    ms = _split_dot(oc * oc, _group_ones(C_WIDTH, C_DIM))
    zg = zg_ref[0]
    oc = oc * lax.rsqrt(ms + EPS) * cgain_ref[...] * (zg * jax.nn.sigmoid(zg))
    mix = jnp.concatenate([oa_ref[0], ob_ref[0], oc.astype(BF16)], axis=-1)
    x1 = x_ref[0] + mod_ref[0, 2:3, :] * _dot(mix, wo_ref[0])
    h = (_rms_rows(x1) * (1.0 + mod_ref[0, 4:5, :]) + mod_ref[0, 3:4, :]).astype(BF16)
    d_ff = w1_ref.shape[2]
    acc = jnp.zeros_like(x1)
    for j in range(d_ff // ff_chunk):
        u = _dot(h, w1_ref[0, :, j * ff_chunk:(j + 1) * ff_chunk])
        u = jnp.square(jnp.maximum(u, 0.0)).astype(BF16)
        acc = acc + _dot(u, w2_ref[0, j * ff_chunk:(j + 1) * ff_chunk, :])
    x2 = x1 + mod_ref[0, 5:6, :] * acc
    if final:
        x2 = _rms_rows(x2) * fg_ref[...]
    o_ref[0] = x2


def _out_ffn(x, mod_l, oa, ob, oc_f, oc_b, cg, c_gain, w_out, w_ff1, w_ff2, final_gain, layer, final, tm):
    bsz, s, d = x.shape
    d_ff = w_ff1.shape[2]
    row_spec = lambda width: pl.BlockSpec((1, tm, width), lambda b, i: (b, i, 0))
    weight = lambda a: pl.BlockSpec((1,) + a.shape[1:], lambda b, i: (layer, 0, 0),
                                    pipeline_mode=pl.Buffered(1))
    return pl.pallas_call(
        functools.partial(_out_ffn_kernel, final=final, ff_chunk=COL_CHUNK),
        grid=(bsz, s // tm),
        in_specs=[
            row_spec(d),
            pl.BlockSpec((1, 6, d), lambda b, i: (b, 0, 0)),
            row_spec(A_WIDTH), row_spec(B_WIDTH), row_spec(C_WIDTH), row_spec(C_WIDTH), row_spec(C_WIDTH),
            pl.BlockSpec((1, C_WIDTH), lambda b, i: (0, 0)),
            weight(w_out), weight(w_ff1), weight(w_ff2),
            pl.BlockSpec((1, d), lambda b, i: (0, 0)),
        ],
        out_specs=row_spec(d),
        out_shape=jax.ShapeDtypeStruct((bsz, s, d), F32),
        compiler_params=pltpu.CompilerParams(
            dimension_semantics=("arbitrary", "arbitrary"), vmem_limit_bytes=VMEM_LIMIT),
        name="out_proj_ffn",
    )(x, mod_l, oa, ob, oc_f, oc_b, cg, c_gain, w_out, w_ff1, w_ff2, final_gain)


def _rope_tables(s):
    half = HEAD_DIM // 2
    inv = ROPE_THETA ** (-jnp.arange(0, half, 2, dtype=F32) / half)
    t = jnp.arange(s)
    ang_row = (t // GRID_W).astype(F32)[:, None] * inv[None, :]
    ang_col = (t % GRID_W).astype(F32)[:, None] * inv[None, :]
    ang_1d = t.astype(F32)[:, None] * inv[None, :]

    def pair(ang):
        return (jnp.concatenate([jnp.cos(ang), jnp.cos(ang)], axis=-1),
                jnp.concatenate([-jnp.sin(ang), jnp.sin(ang)], axis=-1))

    cr, sr = pair(ang_row)
    cc, sc = pair(ang_col)
    c1, s1 = pair(ang_1d)
    tile = lambda a: jnp.tile(a, (1, LANES // a.shape[1]))
    return (tile(jnp.concatenate([cr, cc], axis=-1)), tile(jnp.concatenate([sr, sc], axis=-1)),
            tile(c1), tile(s1))


def kernel(x, c, w_mod, b_mod, w_in, a_qk_norm, diff_lambda, diff_subln, hgrn_lower_bounds, hgrn_norm,
           w_out, w_ff1, w_ff2, final_norm):
    bsz, s, d = x.shape
    depth = w_in.shape[0]
    tm = min(ROW_TILE, s)
    tables = _rope_tables(s)
    mod = _modulation(c, w_mod, b_mod).reshape(depth, bsz, 6, d)
    w_in_b, w_out_b, w_ff1_b, w_ff2_b = (w.astype(BF16) for w in (w_in, w_out, w_ff1, w_ff2))
    final_gain = final_norm.reshape(1, d)
    for l in range(depth):
        qk_gain = jnp.tile(a_qk_norm[l], (1, LANES // HEAD_DIM))
        qat, ka, vat, qbt, kb, vbt, cq, cff, cfb, ci, cg = _in_projection(
            x, mod[l], w_in_b, l, tables, qk_gain, tm)
        oa = _attention_a(qat, ka, vat, min(A_Q_TILES * Q_TILE, s))
        lam_init = 0.8 - 0.6 * math.exp(-0.3 * l)
        subln = jnp.tile(diff_subln[l].reshape(1, HEAD_DIM), (1, B_HEADS))
        ob = _attention_b(qbt, kb, vbt, diff_lambda[l], subln, lam_init, min(B_Q_TILES * Q_TILE, s))
        hg_gain = jnp.tile(hgrn_norm[l].reshape(1, C_DIM), (1, C_HEADS))
        oc_f, oc_b = _hgrn(cq, cff, cfb, ci, hgrn_lower_bounds, l, min(HGRN_BATCH, bsz))
        x = _out_ffn(x, mod[l], oa, ob, oc_f, oc_b, cg, hg_gain, w_out_b, w_ff1_b, w_ff2_b, final_gain,
                     l, l == depth - 1, tm)
    return x
```

```python
import functools
import math

import numpy as np
import jax
import jax.numpy as jnp
from jax import lax
from jax.experimental import pallas as pl
from jax.experimental.pallas import tpu as pltpu

F32 = jnp.float32
BF16 = jnp.bfloat16

HEAD_DIM = 64
GRID_W = 64
ROPE_THETA = 10000.0
EPS = 1e-6
FORGET_FLOOR = 1e-6
A_HEADS, A_KV_HEADS = 6, 2
B_HEADS, B_QK_DIM = 6, 32
C_HEADS, C_DIM = 4, 64
A_WIDTH = A_HEADS * HEAD_DIM
A_KV_WIDTH = A_KV_HEADS * HEAD_DIM
B_WIDTH = B_HEADS * HEAD_DIM
C_WIDTH = C_HEADS * C_DIM
LANES = 128
HGRN_CHUNK = 64
HGRN_LEAF = 8
HGRN_BATCH = 16
ONES_ROWS = 16
VT_ROWS = HEAD_DIM + ONES_ROWS
KV_CHUNK = 256
Q_TILE = 256
A_Q_TILES = 4
B_Q_TILES = 4
QK_AHEAD = 16
LOG2_E = math.log2(math.e)
ROW_TILE = 512
COL_CHUNK = 1024
VMEM_LIMIT = 56 * 1024 * 1024

C_DTYPES = (F32, F32, F32, BF16, F32)
OFF_AQ = 0
OFF_AK = OFF_AQ + A_WIDTH
OFF_AV = OFF_AK + A_KV_WIDTH
OFF_BQ = OFF_AV + A_KV_WIDTH
OFF_BK = OFF_BQ + B_WIDTH
OFF_BV = OFF_BK + B_WIDTH
OFF_C = OFF_BV + B_WIDTH
IN_TOTAL = OFF_C + len(C_DTYPES) * C_WIDTH


def _dot(a, b):
    return jnp.dot(a, b, preferred_element_type=F32)


def _dot_tn(a, b):
    return lax.dot_general(a, b, (((0,), (0,)), ((), ())), preferred_element_type=F32)


def _split_dot(x, w_bf16):
    hi = x.astype(BF16)
    lo = (x - hi.astype(F32)).astype(BF16)
    return _dot(hi, w_bf16) + _dot(lo, w_bf16)


def _group_ones(n, group):
    r = lax.broadcasted_iota(jnp.int32, (n, n), 0) // group
    c = lax.broadcasted_iota(jnp.int32, (n, n), 1) // group
    return jnp.where(r == c, 1.0 / group, 0.0).astype(BF16)


def _rope(xb, cos, sin_signed):
    lane = lax.broadcasted_iota(jnp.int32, xb.shape, 1)
    low = (lane % 32) < 16
    partner = jnp.where(low, pltpu.roll(xb, LANES - 16, 1), pltpu.roll(xb, 16, 1))
    return xb * cos + partner * sin_signed


def _rms_rows(x):
    return x * lax.rsqrt(jnp.mean(x * x, axis=-1, keepdims=True) + EPS)


def _mod_kernel(c_ref, w_ref, b_ref, o_ref):
    c = c_ref[...]
    cond = (c * jax.nn.sigmoid(c)).astype(BF16)
    o_ref[0] = _dot(cond, w_ref[0].astype(BF16)) + b_ref[0]


def _modulation(c, w_mod, b_mod):
    depth, d, n = w_mod.shape
    bsz = c.shape[0]
    tn = COL_CHUNK
    return pl.pallas_call(
        _mod_kernel,
        grid=(depth, n // tn),
        in_specs=[
            pl.BlockSpec((bsz, d), lambda l, j: (0, 0)),
            pl.BlockSpec((1, d, tn), lambda l, j: (l, 0, j)),
            pl.BlockSpec((1, 1, tn), lambda l, j: (l, 0, j)),
        ],
        out_specs=pl.BlockSpec((1, bsz, tn), lambda l, j: (l, 0, j)),
        out_shape=jax.ShapeDtypeStruct((depth, bsz, n), F32),
        compiler_params=pltpu.CompilerParams(
            dimension_semantics=("arbitrary", "arbitrary"), vmem_limit_bytes=VMEM_LIMIT),
        name="adaln_mod",
    )(c, w_mod, b_mod.reshape(depth, 1, n))


def _inproj_kernel(x_ref, mod_ref, w_ref, cosa_ref, sina_ref, cosb_ref, sinb_ref, gain_ref,
                   qat_ref, ka_ref, vat_ref, qbt_ref, kb_ref, vbt_ref,
                   cq_ref, cff_ref, cfb_ref, ci_ref, cg_ref, h_scr, z_scr):
    x = x_ref[0]
    h = _rms_rows(x) * (1.0 + mod_ref[0, 1:2, :]) + mod_ref[0, 0:1, :]
    h_scr[...] = h.astype(BF16)

    def project(j):
        cols = slice(j * COL_CHUNK, (j + 1) * COL_CHUNK)
        z_scr[:, cols] = _dot(h_scr[...], w_ref[0, :, cols])

    project(0)
    project(1)

    ones = _group_ones(LANES, HEAD_DIM)
    cosa, sina = cosa_ref[...], sina_ref[...]
    cosb, sinb = cosb_ref[...], sinb_ref[...]

    def a_norm_rope(zb, gain):
        ms = _split_dot(zb * zb, ones)
        return _rope(zb * lax.rsqrt(ms + EPS) * gain, cosa, sina)

    def store_vt(vt_ref, first_head, blk_t):
        ones_rows = jnp.ones((ONES_ROWS, blk_t.shape[1]), BF16)
        for j in range(LANES // HEAD_DIM):
            r0 = (first_head + j) * VT_ROWS
            vt_ref[0, r0:r0 + HEAD_DIM, :] = blk_t[j * HEAD_DIM:(j + 1) * HEAD_DIM].astype(BF16)
            vt_ref[0, r0 + HEAD_DIM:r0 + VT_ROWS, :] = ones_rows

    for cblk in range(A_WIDTH // LANES):
        zb = z_scr[:, OFF_AQ + cblk * LANES:OFF_AQ + (cblk + 1) * LANES]
        y = a_norm_rope(zb, gain_ref[0:1, :]) * (HEAD_DIM ** -0.5 * LOG2_E)
        qat_ref[0, cblk * LANES:(cblk + 1) * LANES, :] = y.T.astype(BF16)
    ka_ref[0, 0] = a_norm_rope(z_scr[:, OFF_AK:OFF_AK + LANES], gain_ref[1:2, :]).astype(BF16)
    store_vt(vat_ref, 0, z_scr[:, OFF_AV:OFF_AV + LANES].T)
    project(2)

    for cblk in range(B_WIDTH // LANES):
        lo = cblk * LANES
        zq = z_scr[:, OFF_BQ + lo:OFF_BQ + lo + LANES]
        qbt_ref[0, lo:lo + LANES, :] = (_rope(zq, cosb, sinb) * (B_QK_DIM ** -0.5 * LOG2_E)).T.astype(BF16)
    for cblk in range(B_WIDTH // LANES):
        lo = cblk * LANES
        zk = z_scr[:, OFF_BK + lo:OFF_BK + lo + LANES]
        kb_ref[0, cblk] = _rope(zk, cosb, sinb).astype(BF16)
        store_vt(vbt_ref, cblk * (LANES // HEAD_DIM), z_scr[:, OFF_BV + lo:OFF_BV + lo + LANES].T)
    for j, ref in enumerate((cq_ref, cff_ref, cfb_ref, ci_ref, cg_ref)):
        ref[0] = z_scr[:, OFF_C + j * C_WIDTH:OFF_C + (j + 1) * C_WIDTH].astype(ref.dtype)


def _in_projection(x, mod_l, w_in_bf16, layer, tables, qk_gain, tm):
    bsz, s, d = x.shape
    cosa, sina, cosb, sinb = tables
    row_spec = lambda width: pl.BlockSpec((1, tm, width), lambda b, i: (b, i, 0))
    tab_spec = pl.BlockSpec((tm, LANES), lambda b, i: (i, 0))
    col_spec = lambda height: pl.BlockSpec((1, height, tm), lambda b, i: (b, 0, i))
    key_spec = lambda blocks: pl.BlockSpec((1, blocks, tm, LANES), lambda b, i: (b, 0, i, 0))
    return pl.pallas_call(
        _inproj_kernel,
        grid=(bsz, s // tm),
        in_specs=[
            row_spec(d),
            pl.BlockSpec((1, 6, d), lambda b, i: (b, 0, 0)),
            pl.BlockSpec((1, d, IN_TOTAL), lambda b, i: (layer, 0, 0)),
            tab_spec, tab_spec, tab_spec, tab_spec,
            pl.BlockSpec((2, LANES), lambda b, i: (0, 0)),
        ],
        out_specs=[
            col_spec(A_WIDTH), key_spec(A_KV_WIDTH // LANES), col_spec(A_KV_HEADS * VT_ROWS),
            col_spec(B_WIDTH), key_spec(B_WIDTH // LANES), col_spec(B_HEADS * VT_ROWS),
        ] + [row_spec(C_WIDTH)] * len(C_DTYPES),
        out_shape=[
            jax.ShapeDtypeStruct((bsz, A_WIDTH, s), BF16),
            jax.ShapeDtypeStruct((bsz, A_KV_WIDTH // LANES, s, LANES), BF16),
            jax.ShapeDtypeStruct((bsz, A_KV_HEADS * VT_ROWS, s), BF16),
            jax.ShapeDtypeStruct((bsz, B_WIDTH, s), BF16),
            jax.ShapeDtypeStruct((bsz, B_WIDTH // LANES, s, LANES), BF16),
            jax.ShapeDtypeStruct((bsz, B_HEADS * VT_ROWS, s), BF16),
        ] + [jax.ShapeDtypeStruct((bsz, s, C_WIDTH), dt) for dt in C_DTYPES],
        scratch_shapes=[pltpu.VMEM((tm, d), BF16), pltpu.VMEM((tm, IN_TOTAL), F32)],
        compiler_params=pltpu.CompilerParams(
            dimension_semantics=("arbitrary", "arbitrary"), vmem_limit_bytes=VMEM_LIMIT),
        name="in_projection",
    )(x, mod_l, w_in_bf16, cosa, sina, cosb, sinb, qk_gain)


def _place_rows(block, start, total):
    parts = []
    if start:
        parts.append(jnp.zeros((start, block.shape[1]), block.dtype))
    parts.append(block)
    rest = total - start - block.shape[0]
    if rest:
        parts.append(jnp.zeros((rest, block.shape[1]), block.dtype))
    return jnp.concatenate(parts, axis=0) if len(parts) > 1 else block


def _attention_items(items, k_ref, vt_ref, qz_scr, m_scr, acc_scr, exp_dtype=F32):
    s = k_ref.shape[2]
    tq = acc_scr.shape[2]
    m_scr[...] = jnp.full(m_scr.shape, -1e30, F32)
    acc_scr[...] = jnp.zeros_like(acc_scr)

    units = [(i, c) for c in range(s // KV_CHUNK) for i in range(len(items))]

    def scores(u):
        i, c = units[u]
        return _dot(k_ref[0, items[i][0], c * KV_CHUNK:(c + 1) * KV_CHUNK, :], qz_scr[i])

    pending = [scores(u) for u in range(min(QK_AHEAD, len(units)))]
    for u, (i, c) in enumerate(units):
        v_off = items[i][1]
        st = pending.pop(0)
        m_old = m_scr[i]
        part = jnp.max(st.reshape(KV_CHUNK // 8, 8, tq), axis=0)
        m_new = jnp.maximum(m_old, jnp.max(part, axis=0, keepdims=True))
        m_scr[i] = m_new
        p = jnp.exp2((st - m_new[0:1]).astype(exp_dtype)).astype(BF16)
        d = _dot(vt_ref[0, v_off:v_off + VT_ROWS, c * KV_CHUNK:(c + 1) * KV_CHUNK], p)
        if u + QK_AHEAD < len(units):
            pending.append(scores(u + QK_AHEAD))
        acc_scr[i] = acc_scr[i] * jnp.exp2(m_old[0:1] - m_new[0:1]) + d
    results = []
    for i in range(len(items)):
        acc = acc_scr[i]
        results.append(acc[0:HEAD_DIM] * (1.0 / acc[HEAD_DIM:HEAD_DIM + 1]))
    return results


def _attention_scratch(heads, tq):
    n_items = heads * (tq // Q_TILE)
    return [pltpu.VMEM((n_items, LANES, Q_TILE), BF16), pltpu.VMEM((n_items, 8, Q_TILE), F32),
            pltpu.VMEM((n_items, VT_ROWS, Q_TILE), F32)]


def _attn_a_kernel(qt_ref, k_ref, vt_ref, o_ref, qz_scr, m_scr, acc_scr):
    group = A_HEADS // A_KV_HEADS
    n_tiles = qt_ref.shape[2] // Q_TILE
    items = []
    for t in range(n_tiles):
        cols = slice(t * Q_TILE, (t + 1) * Q_TILE)
        for h in range(A_HEADS):
            g = h // group
            qz_scr[len(items)] = _place_rows(qt_ref[0, h * HEAD_DIM:(h + 1) * HEAD_DIM, cols], g * HEAD_DIM, LANES)
            items.append((0, g * VT_ROWS))
    outs = _attention_items(items, k_ref, vt_ref, qz_scr, m_scr, acc_scr, exp_dtype=BF16)
    for t in range(n_tiles):
        tile = outs[t * A_HEADS:(t + 1) * A_HEADS]
        o_ref[0, t * Q_TILE:(t + 1) * Q_TILE, :] = jnp.concatenate([o.T for o in tile], axis=-1).astype(BF16)


def _attention_a(qat, ka, vat, tq):
    bsz, _, s, _ = ka.shape
    return pl.pallas_call(
        _attn_a_kernel,
        grid=(bsz, s // tq),
        in_specs=[
            pl.BlockSpec((1, A_WIDTH, tq), lambda b, i: (b, 0, i)),
            pl.BlockSpec((1, A_KV_WIDTH // LANES, s, LANES), lambda b, i: (b, 0, 0, 0)),
            pl.BlockSpec((1, A_KV_HEADS * VT_ROWS, s), lambda b, i: (b, 0, 0)),
        ],
        out_specs=pl.BlockSpec((1, tq, A_WIDTH), lambda b, i: (b, i, 0)),
        out_shape=jax.ShapeDtypeStruct((bsz, s, A_WIDTH), BF16),
        scratch_shapes=_attention_scratch(A_HEADS, tq),
        compiler_params=pltpu.CompilerParams(
            dimension_semantics=("arbitrary", "arbitrary"), vmem_limit_bytes=VMEM_LIMIT),
        name="attn_axial_gqa",
    )(qat, ka, vat)


def _attn_b_kernel(lam_ref, gain_ref, qt_ref, k_ref, vt_ref, o_ref, qz_scr, m_scr, acc_scr, *, lam_init):
    lp = lam_ref[...]
    lam = (jnp.exp(jnp.sum(lp[0:1] * lp[1:2], axis=-1, keepdims=True))
           - jnp.exp(jnp.sum(lp[2:3] * lp[3:4], axis=-1, keepdims=True)) + lam_init)
    n_tiles = qt_ref.shape[2] // Q_TILE
    items = []
    for t in range(n_tiles):
        cols = slice(t * Q_TILE, (t + 1) * Q_TILE)
        for h in range(B_HEADS):
            pair = (h // 2) * LANES
            for comp in range(2):
                r0 = h * HEAD_DIM + comp * B_QK_DIM
                qz_scr[len(items)] = _place_rows(qt_ref[0, r0:r0 + B_QK_DIM, cols], r0 - pair, LANES)
                items.append((h // 2, h * VT_ROWS))
    comps = _attention_items(items, k_ref, vt_ref, qz_scr, m_scr, acc_scr, exp_dtype=BF16)
    for t in range(n_tiles):
        outs = []
        for h in range(B_HEADS):
            i = (t * B_HEADS + h) * 2
            o = comps[i] - lam * comps[i + 1]
            o = o * lax.rsqrt(jnp.mean(o * o, axis=0, keepdims=True) + EPS)
            outs.append(o.T)
        y = jnp.concatenate(outs, axis=-1) * gain_ref[...] * (1.0 - lam_init)
        o_ref[0, t * Q_TILE:(t + 1) * Q_TILE, :] = y.astype(BF16)


def _attention_b(qbt, kb, vbt, lam_params, subln_gain, lam_init, tq):
    bsz, _, s, _ = kb.shape
    return pl.pallas_call(
        functools.partial(_attn_b_kernel, lam_init=lam_init),
        grid=(bsz, s // tq),
        in_specs=[
            pl.BlockSpec((4, B_QK_DIM), lambda b, i: (0, 0)),
            pl.BlockSpec((1, B_WIDTH), lambda b, i: (0, 0)),
            pl.BlockSpec((1, B_WIDTH, tq), lambda b, i: (b, 0, i)),
            pl.BlockSpec((1, B_WIDTH // LANES, s, LANES), lambda b, i: (b, 0, 0, 0)),
            pl.BlockSpec((1, B_HEADS * VT_ROWS, s), lambda b, i: (b, 0, 0)),
        ],
        out_specs=pl.BlockSpec((1, tq, B_WIDTH), lambda b, i: (b, i, 0)),
        out_shape=jax.ShapeDtypeStruct((bsz, s, B_WIDTH), BF16),
        scratch_shapes=_attention_scratch(2 * B_HEADS, tq),
        compiler_params=pltpu.CompilerParams(
            dimension_semantics=("arbitrary", "arbitrary"), vmem_limit_bytes=VMEM_LIMIT),
        name="attn_differential",
    )(lam_params, subln_gain, qbt, kb, vbt)


def _hgrn_levels():
    sizes = []
    bs = HGRN_CHUNK
    while bs > HGRN_LEAF:
        sizes.append(bs)
        bs //= 2
    return sizes


def _hgrn_constants():
    c, w = HGRN_CHUNK, C_WIDTH
    t = np.arange(c)
    tri = np.stack([(t[:, None] >= t[None, :]), (t[:, None] <= t[None, :])]).astype(np.float32)
    s_of_lane = np.arange(w) % C_DIM
    sizes = _hgrn_levels()
    level = np.full((2, c, w), float(len(sizes) + 1), np.float32)
    for d in range(2):
        for li, bs in enumerate(sizes):
            upper = (t % bs) >= bs // 2
            qrow = upper if d == 0 else ~upper
            same = (t[:, None] // bs) == (s_of_lane[None, :] // bs)
            cross = qrow[:, None] & ~qrow[s_of_lane][None, :]
            level[d][same & cross] = li
        same_leaf = (t[:, None] // HGRN_LEAF) == (s_of_lane[None, :] // HGRN_LEAF)
        order = (s_of_lane[None, :] <= t[:, None]) if d == 0 else (s_of_lane[None, :] >= t[:, None])
        level[d][same_leaf & order] = len(sizes)
    half = (np.arange(LANES)[None, :] < C_DIM).astype(np.float32) * np.ones((c, 1), np.float32)
    return tri, level, half


def _block_diag(x, low_half):
    zero = jnp.zeros((x.shape[0], LANES), x.dtype)
    blocks = []
    for h in range(C_HEADS):
        tile = x[:, (h // 2) * LANES:(h // 2 + 1) * LANES]
        keep = tile * (low_half if h % 2 == 0 else 1 - low_half)
        blocks.append(jnp.concatenate([keep, zero] if h // 2 == 0 else [zero, keep], axis=1))
    return jnp.concatenate(blocks, axis=0)


def _block_diag_t(x, low_half):
    zero = jnp.zeros((LANES, LANES), x.dtype)
    rows = []
    for t in range(C_WIDTH // LANES):
        tile = x[:, t * LANES:(t + 1) * LANES]
        pair = jnp.concatenate([tile * low_half, tile * (1 - low_half)], axis=0).T
        rows.append(jnp.concatenate([pair, zero] if t == 0 else [zero, pair], axis=1))
    return jnp.concatenate(rows, axis=0)


def _ref_rows(b, bs, offset):
    parts = []
    for lo in range(0, b.shape[0], bs):
        parts.append(jnp.broadcast_to(b[lo + offset:lo + offset + 1, :], (bs, b.shape[1])))
    return jnp.concatenate(parts, axis=0)


def _level_operands(q, k, b, bs, d):
    half = bs // 2
    ref_off = half - 1 if d == 0 else half
    zeros = jnp.zeros((8, b.shape[1]), F32)
    qparts, kparts = [], []
    for r0 in range(0, b.shape[0], 8):
        lo = (r0 // bs) * bs
        ref = b[lo + ref_off:lo + ref_off + 1, :]
        upper = (r0 % bs) >= half
        rows = slice(r0, r0 + 8)
        if upper == (d == 0):
            qparts.append(q[rows] * jnp.exp2(b[rows] - ref))
            kparts.append(zeros)
        else:
            qparts.append(zeros)
            kparts.append(k[rows] * jnp.exp2(ref - b[rows]))
    return jnp.concatenate(qparts, axis=0).astype(BF16), jnp.concatenate(kparts, axis=0).astype(BF16)


def _hgrn_kernel(lb_ref, tri_ref, level_ref, half_ref, zq_f, zf_f, zv_f, zq_b, zf_b, zv_b,
                 of_ref, ob_ref, st_scr, *, layer, depth, nb):
    c, w = HGRN_CHUNK, C_WIDTH
    sizes = _hgrn_levels()
    n_lvl = len(sizes)

    @pl.when(pl.program_id(1) == 0)
    def _():
        st_scr[...] = jnp.zeros_like(st_scr)

    rows = [lb_ref[i:i + 1, :] for i in range(depth)]
    mx = functools.reduce(jnp.maximum, rows)
    es = [jnp.exp(r - mx) for r in rows]
    tot = functools.reduce(lambda a, b_: a + b_, es)
    ps = [e / tot for e in es]
    cum = functools.reduce(lambda a, b_: a + b_, ps[:layer + 1])
    one_minus_lb = 1.0 - jnp.clip(cum - ps[0], 0.0, 1.0)

    low_half = half_ref[...].astype(BF16)
    low_bool = half_ref[...] > 0.5
    owner = [[level_ref[d] == float(li) for li in range(n_lvl + 1)] for d in range(2)]
    chains = [(j, d) for j in range(nb) for d in range(2)]
    zrefs = {0: (zq_f, zf_f, zv_f), 1: (zq_b, zf_b, zv_b)}

    q, k, v, b, total = {}, {}, {}, {}, {}
    g_split = {}
    for ch in chains:
        j, d = ch
        zq, zf = zrefs[d][0][j], zrefs[d][1][j]
        v[ch] = zrefs[d][2][j]
        q[ch] = (zq * (C_DIM ** -0.5)) / (1.0 + jnp.exp2(zq * -LOG2_E))
        k[ch] = one_minus_lb * (1.0 - 1.0 / (1.0 + jnp.exp2(zf * -LOG2_E)))
        g = jnp.log2(jnp.maximum(1.0 - k[ch], FORGET_FLOOR))
        g_hi = g.astype(BF16)
        g_r = g - g_hi.astype(F32)
        g_mid = g_r.astype(BF16)
        g_lo = (g_r - g_mid.astype(F32)).astype(BF16)
        g_split[ch] = jnp.concatenate([g_hi, g_mid, g_lo], axis=1)
    for ch in chains:
        d = ch[1]
        bb = _dot(tri_ref[d], g_split[ch])
        b[ch] = bb[:, 0:w] + bb[:, w:2 * w] + bb[:, 2 * w:3 * w]
        total[ch] = b[ch][c - 1:c, :] if d == 0 else b[ch][0:1, :]

    parts = {ch: [] for ch in chains}
    for li in range(n_lvl + 1):
        for ch in chains:
            d = ch[1]
            if li < n_lvl:
                qp, kp = _level_operands(q[ch], k[ch], b[ch], sizes[li], d)
            else:
                ref = _ref_rows(b[ch], HGRN_LEAF, HGRN_LEAF // 2 - 1 if d == 0 else HGRN_LEAF // 2)
                arg = b[ch] - ref
                qp = (q[ch] * jnp.exp2(arg)).astype(BF16)
                kp = (k[ch] * jnp.exp2(-arg)).astype(BF16)
            parts[ch].append(_dot(qp, _block_diag_t(kp, low_half)))
    a = {}
    for ch in chains:
        acc = parts[ch][0]
        for li in range(1, n_lvl + 1):
            acc = jnp.where(owner[ch[1]][li], parts[ch][li], acc)
        a[ch] = acc.astype(BF16)

    o_intra, o_inter, upd = {}, {}, {}
    for ch in chains:
        j, d = ch
        o_intra[ch] = _dot(a[ch], _block_diag(v[ch], low_half))
        qe = (q[ch] * jnp.exp2(b[ch])).astype(BF16)
        o_inter[ch] = _dot(qe, _block_diag_t(st_scr[d, j].astype(BF16), low_half))
        kb = (k[ch] * jnp.exp2(total[ch] - b[ch])).astype(BF16)
        tiles = []
        for t in range(w // LANES):
            lanes = slice(t * LANES, (t + 1) * LANES)
            full = _dot_tn(v[ch][:, lanes], kb[:, lanes])
            tiles.append(jnp.where(low_bool, full[0:C_DIM], full[C_DIM:2 * C_DIM]))
        upd[ch] = jnp.concatenate(tiles, axis=1)
    for ch in chains:
        j, d = ch
        st_scr[d, j] = st_scr[d, j] * jnp.exp2(total[ch]) + upd[ch]
        (of_ref if d == 0 else ob_ref)[j] = o_intra[ch] + o_inter[ch]


def _hgrn(cq, cff, cfb, ci, lower_bounds, layer, nb):
    bsz, s, _ = cq.shape
    depth = lower_bounds.shape[0]
    n_chunks = s // HGRN_CHUNK
    c, w = HGRN_CHUNK, C_WIDTH
    tri, level, half = _hgrn_constants()
    consts = (jnp.asarray(tri, BF16), jnp.asarray(level), jnp.asarray(half))
    full = lambda a: pl.BlockSpec(a.shape, lambda g, i: (0,) * a.ndim)
    last = n_chunks - 1
    fwd = pl.BlockSpec((nb, c, w), lambda g, i: (g, i, 0))
    bwd = pl.BlockSpec((nb, c, w), lambda g, i: (g, last - i, 0))
    return pl.pallas_call(
        functools.partial(_hgrn_kernel, layer=layer, depth=depth, nb=nb),
        grid=(bsz // nb, n_chunks),
        in_specs=[full(lower_bounds)] + [full(a) for a in consts] + [fwd, fwd, fwd, bwd, bwd, bwd],
        out_specs=[fwd, bwd],
        out_shape=[jax.ShapeDtypeStruct((bsz, s, w), F32)] * 2,
        scratch_shapes=[pltpu.VMEM((2, nb, C_DIM, w), F32)],
        compiler_params=pltpu.CompilerParams(
            dimension_semantics=("arbitrary", "arbitrary"), vmem_limit_bytes=VMEM_LIMIT),
        name="hgrn2_bidir",
    )(lower_bounds, *consts, cq, cff, ci, cq, cfb, ci)


def _out_ffn_kernel(x_ref, mod_ref, oa_ref, ob_ref, cf_ref, cb_ref, zg_ref, cgain_ref,
                    wo_ref, w1_ref, w2_ref, fg_ref, o_ref, *, final, ff_chunk):
    oc = cf_ref[0] + cb_ref[0]
    ms = _split_dot(oc * oc, _group_ones(C_WIDTH, C_DIM))
    zg = zg_ref[0]
    oc = oc * lax.rsqrt(ms + EPS) * cgain_ref[...] * (zg * jax.nn.sigmoid(zg))
    mix = jnp.concatenate([oa_ref[0], ob_ref[0], oc.astype(BF16)], axis=-1)
    x1 = x_ref[0] + mod_ref[0, 2:3, :] * _dot(mix, wo_ref[0])
    h = (_rms_rows(x1) * (1.0 + mod_ref[0, 4:5, :]) + mod_ref[0, 3:4, :]).astype(BF16)
    d_ff = w1_ref.shape[2]
    acc = jnp.zeros_like(x1)
    for j in range(d_ff // ff_chunk):
        u = _dot(h, w1_ref[0, :, j * ff_chunk:(j + 1) * ff_chunk])
        u = jnp.square(jnp.maximum(u, 0.0)).astype(BF16)
        acc = acc + _dot(u, w2_ref[0, j * ff_chunk:(j + 1) * ff_chunk, :])
    x2 = x1 + mod_ref[0, 5:6, :] * acc
    if final:
        x2 = _rms_rows(x2) * fg_ref[...]
    o_ref[0] = x2


def _out_ffn(x, mod_l, oa, ob, oc_f, oc_b, cg, c_gain, w_out, w_ff1, w_ff2, final_gain, layer, final, tm):
    bsz, s, d = x.shape
    d_ff = w_ff1.shape[2]
    row_spec = lambda width: pl.BlockSpec((1, tm, width), lambda b, i: (b, i, 0))
    weight = lambda a: pl.BlockSpec((1,) + a.shape[1:], lambda b, i: (layer, 0, 0),
                                    pipeline_mode=pl.Buffered(1))
    return pl.pallas_call(
        functools.partial(_out_ffn_kernel, final=final, ff_chunk=COL_CHUNK),
        grid=(bsz, s // tm),
        in_specs=[
            row_spec(d),
            pl.BlockSpec((1, 6, d), lambda b, i: (b, 0, 0)),
            row_spec(A_WIDTH), row_spec(B_WIDTH), row_spec(C_WIDTH), row_spec(C_WIDTH), row_spec(C_WIDTH),
            pl.BlockSpec((1, C_WIDTH), lambda b, i: (0, 0)),
            weight(w_out), weight(w_ff1), weight(w_ff2),
            pl.BlockSpec((1, d), lambda b, i: (0, 0)),
        ],
        out_specs=row_spec(d),
        out_shape=jax.ShapeDtypeStruct((bsz, s, d), F32),
        compiler_params=pltpu.CompilerParams(
            dimension_semantics=("arbitrary", "arbitrary"), vmem_limit_bytes=VMEM_LIMIT),
        name="out_proj_ffn",
    )(x, mod_l, oa, ob, oc_f, oc_b, cg, c_gain, w_out, w_ff1, w_ff2, final_gain)


def _rope_tables(s):
    half = HEAD_DIM // 2
    inv = ROPE_THETA ** (-jnp.arange(0, half, 2, dtype=F32) / half)
    t = jnp.arange(s)
    ang_row = (t // GRID_W).astype(F32)[:, None] * inv[None, :]
    ang_col = (t % GRID_W).astype(F32)[:, None] * inv[None, :]
    ang_1d = t.astype(F32)[:, None] * inv[None, :]

    def pair(ang):
        return (jnp.concatenate([jnp.cos(ang), jnp.cos(ang)], axis=-1),
                jnp.concatenate([-jnp.sin(ang), jnp.sin(ang)], axis=-1))

    cr, sr = pair(ang_row)
    cc, sc = pair(ang_col)
    c1, s1 = pair(ang_1d)
    tile = lambda a: jnp.tile(a, (1, LANES // a.shape[1]))
    return (tile(jnp.concatenate([cr, cc], axis=-1)), tile(jnp.concatenate([sr, sc], axis=-1)),
            tile(c1), tile(s1))


def kernel(x, c, w_mod, b_mod, w_in, a_qk_norm, diff_lambda, diff_subln, hgrn_lower_bounds, hgrn_norm,
           w_out, w_ff1, w_ff2, final_norm):
    bsz, s, d = x.shape
    depth = w_in.shape[0]
    tm = min(ROW_TILE, s)
    tables = _rope_tables(s)
    mod = _modulation(c, w_mod, b_mod).reshape(depth, bsz, 6, d)
    w_in_b, w_out_b, w_ff1_b, w_ff2_b = (w.astype(BF16) for w in (w_in, w_out, w_ff1, w_ff2))
    final_gain = final_norm.reshape(1, d)
    for l in range(depth):
        qk_gain = jnp.tile(a_qk_norm[l], (1, LANES // HEAD_DIM))
        qat, ka, vat, qbt, kb, vbt, cq, cff, cfb, ci, cg = _in_projection(
            x, mod[l], w_in_b, l, tables, qk_gain, tm)
        oa = _attention_a(qat, ka, vat, min(A_Q_TILES * Q_TILE, s))
        lam_init = 0.8 - 0.6 * math.exp(-0.3 * l)
        subln = jnp.tile(diff_subln[l].reshape(1, HEAD_DIM), (1, B_HEADS))
        ob = _attention_b(qbt, kb, vbt, diff_lambda[l], subln, lam_init, min(B_Q_TILES * Q_TILE, s))
        hg_gain = jnp.tile(hgrn_norm[l].reshape(1, C_DIM), (1, C_HEADS))
        oc_f, oc_b = _hgrn(cq, cff, cfb, ci, hgrn_lower_bounds, l, min(HGRN_BATCH, bsz))
        x = _out_ffn(x, mod[l], oa, ob, oc_f, oc_b, cg, hg_gain, w_out_b, w_ff1_b, w_ff2_b, final_gain,
                     l, l == depth - 1, tm)
    return x
```

```python
import functools
import math

import numpy as np
import jax
import jax.numpy as jnp
from jax import lax
from jax.experimental import pallas as pl
from jax.experimental.pallas import tpu as pltpu

F32 = jnp.float32
BF16 = jnp.bfloat16

HEAD_DIM = 64
GRID_W = 64
ROPE_THETA = 10000.0
EPS = 1e-6
FORGET_FLOOR = 1e-6
A_HEADS, A_KV_HEADS = 6, 2
B_HEADS, B_QK_DIM = 6, 32
C_HEADS, C_DIM = 4, 64
A_WIDTH = A_HEADS * HEAD_DIM
A_KV_WIDTH = A_KV_HEADS * HEAD_DIM
B_WIDTH = B_HEADS * HEAD_DIM
C_WIDTH = C_HEADS * C_DIM
LANES = 128
HGRN_CHUNK = 64
HGRN_LEAF = 8
HGRN_BATCH = 16
ONES_ROWS = 16
VT_ROWS = HEAD_DIM + ONES_ROWS
KV_CHUNK = 256
Q_TILE = 256
A_Q_TILES = 4
B_Q_TILES = 4
QK_AHEAD = 16
LOG2_E = math.log2(math.e)
ROW_TILE = 512
COL_CHUNK = 1024
VMEM_LIMIT = 56 * 1024 * 1024

C_DTYPES = (F32, F32, F32, BF16, F32)
OFF_AQ = 0
OFF_AK = OFF_AQ + A_WIDTH
OFF_AV = OFF_AK + A_KV_WIDTH
OFF_BQ = OFF_AV + A_KV_WIDTH
OFF_BK = OFF_BQ + B_WIDTH
OFF_BV = OFF_BK + B_WIDTH
OFF_C = OFF_BV + B_WIDTH
IN_TOTAL = OFF_C + len(C_DTYPES) * C_WIDTH


def _dot(a, b):
    return jnp.dot(a, b, preferred_element_type=F32)


def _dot_tn(a, b):
    return lax.dot_general(a, b, (((0,), (0,)), ((), ())), preferred_element_type=F32)


def _split_dot(x, w_bf16):
    hi = x.astype(BF16)
    lo = (x - hi.astype(F32)).astype(BF16)
    return _dot(hi, w_bf16) + _dot(lo, w_bf16)


def _group_ones(n, group):
    r = lax.broadcasted_iota(jnp.int32, (n, n), 0) // group
    c = lax.broadcasted_iota(jnp.int32, (n, n), 1) // group
    return jnp.where(r == c, 1.0 / group, 0.0).astype(BF16)


def _rope(xb, cos, sin_signed):
    lane = lax.broadcasted_iota(jnp.int32, xb.shape, 1)
    low = (lane % 32) < 16
    partner = jnp.where(low, pltpu.roll(xb, LANES - 16, 1), pltpu.roll(xb, 16, 1))
    return xb * cos + partner * sin_signed


def _rms_rows(x):
    return x * lax.rsqrt(jnp.mean(x * x, axis=-1, keepdims=True) + EPS)


def _mod_kernel(c_ref, w_ref, b_ref, o_ref):
    c = c_ref[...]
    cond = (c * jax.nn.sigmoid(c)).astype(BF16)
    o_ref[0] = _dot(cond, w_ref[0].astype(BF16)) + b_ref[0]


def _modulation(c, w_mod, b_mod):
    depth, d, n = w_mod.shape
    bsz = c.shape[0]
    tn = COL_CHUNK
    return pl.pallas_call(
        _mod_kernel,
        grid=(depth, n // tn),
        in_specs=[
            pl.BlockSpec((bsz, d), lambda l, j: (0, 0)),
            pl.BlockSpec((1, d, tn), lambda l, j: (l, 0, j)),
            pl.BlockSpec((1, 1, tn), lambda l, j: (l, 0, j)),
        ],
        out_specs=pl.BlockSpec((1, bsz, tn), lambda l, j: (l, 0, j)),
        out_shape=jax.ShapeDtypeStruct((depth, bsz, n), F32),
        compiler_params=pltpu.CompilerParams(
            dimension_semantics=("arbitrary", "arbitrary"), vmem_limit_bytes=VMEM_LIMIT),
        name="adaln_mod",
    )(c, w_mod, b_mod.reshape(depth, 1, n))


def _inproj_kernel(x_ref, mod_ref, w_ref, cosa_ref, sina_ref, cosb_ref, sinb_ref, gain_ref,
                   qat_ref, ka_ref, vat_ref, qbt_ref, kb_ref, vbt_ref,
                   cq_ref, cff_ref, cfb_ref, ci_ref, cg_ref, h_scr, z_scr):
    x = x_ref[0]
    h = _rms_rows(x) * (1.0 + mod_ref[0, 1:2, :]) + mod_ref[0, 0:1, :]
    h_scr[...] = h.astype(BF16)

    def project(j):
        cols = slice(j * COL_CHUNK, (j + 1) * COL_CHUNK)
        z_scr[:, cols] = _dot(h_scr[...], w_ref[0, :, cols])

    project(0)
    project(1)

    ones = _group_ones(LANES, HEAD_DIM)
    cosa, sina = cosa_ref[...], sina_ref[...]
    cosb, sinb = cosb_ref[...], sinb_ref[...]

    def a_norm_rope(zb, gain):
        ms = _split_dot(zb * zb, ones)
        return _rope(zb * lax.rsqrt(ms + EPS) * gain, cosa, sina)

    def store_vt(vt_ref, first_head, blk_t):
        ones_rows = jnp.ones((ONES_ROWS, blk_t.shape[1]), BF16)
        for j in range(LANES // HEAD_DIM):
            r0 = (first_head + j) * VT_ROWS
            vt_ref[0, r0:r0 + HEAD_DIM, :] = blk_t[j * HEAD_DIM:(j + 1) * HEAD_DIM].astype(BF16)
            vt_ref[0, r0 + HEAD_DIM:r0 + VT_ROWS, :] = ones_rows

    for cblk in range(A_WIDTH // LANES):
        zb = z_scr[:, OFF_AQ + cblk * LANES:OFF_AQ + (cblk + 1) * LANES]
        y = a_norm_rope(zb, gain_ref[0:1, :]) * (HEAD_DIM ** -0.5 * LOG2_E)
        qat_ref[0, cblk * LANES:(cblk + 1) * LANES, :] = y.T.astype(BF16)
    ka_ref[0, 0] = a_norm_rope(z_scr[:, OFF_AK:OFF_AK + LANES], gain_ref[1:2, :]).astype(BF16)
    store_vt(vat_ref, 0, z_scr[:, OFF_AV:OFF_AV + LANES].T)
    project(2)

    for cblk in range(B_WIDTH // LANES):
        lo = cblk * LANES
        zq = z_scr[:, OFF_BQ + lo:OFF_BQ + lo + LANES]
        qbt_ref[0, lo:lo + LANES, :] = (_rope(zq, cosb, sinb) * (B_QK_DIM ** -0.5 * LOG2_E)).T.astype(BF16)
    for cblk in range(B_WIDTH // LANES):
        lo = cblk * LANES
        zk = z_scr[:, OFF_BK + lo:OFF_BK + lo + LANES]
        kb_ref[0, cblk] = _rope(zk, cosb, sinb).astype(BF16)
        store_vt(vbt_ref, cblk * (LANES // HEAD_DIM), z_scr[:, OFF_BV + lo:OFF_BV + lo + LANES].T)
    for j, ref in enumerate((cq_ref, cff_ref, cfb_ref, ci_ref, cg_ref)):
        ref[0] = z_scr[:, OFF_C + j * C_WIDTH:OFF_C + (j + 1) * C_WIDTH].astype(ref.dtype)


def _in_projection(x, mod_l, w_in_bf16, layer, tables, qk_gain, tm):
    bsz, s, d = x.shape
    cosa, sina, cosb, sinb = tables
    row_spec = lambda width: pl.BlockSpec((1, tm, width), lambda b, i: (b, i, 0))
    tab_spec = pl.BlockSpec((tm, LANES), lambda b, i: (i, 0))
    col_spec = lambda height: pl.BlockSpec((1, height, tm), lambda b, i: (b, 0, i))
    key_spec = lambda blocks: pl.BlockSpec((1, blocks, tm, LANES), lambda b, i: (b, 0, i, 0))
    return pl.pallas_call(
        _inproj_kernel,
        grid=(bsz, s // tm),
        in_specs=[
            row_spec(d),
            pl.BlockSpec((1, 6, d), lambda b, i: (b, 0, 0)),
            pl.BlockSpec((1, d, IN_TOTAL), lambda b, i: (layer, 0, 0)),
            tab_spec, tab_spec, tab_spec, tab_spec,
            pl.BlockSpec((2, LANES), lambda b, i: (0, 0)),
        ],
        out_specs=[
            col_spec(A_WIDTH), key_spec(A_KV_WIDTH // LANES), col_spec(A_KV_HEADS * VT_ROWS),
            col_spec(B_WIDTH), key_spec(B_WIDTH // LANES), col_spec(B_HEADS * VT_ROWS),
        ] + [row_spec(C_WIDTH)] * len(C_DTYPES),
        out_shape=[
            jax.ShapeDtypeStruct((bsz, A_WIDTH, s), BF16),
            jax.ShapeDtypeStruct((bsz, A_KV_WIDTH // LANES, s, LANES), BF16),
            jax.ShapeDtypeStruct((bsz, A_KV_HEADS * VT_ROWS, s), BF16),
            jax.ShapeDtypeStruct((bsz, B_WIDTH, s), BF16),
            jax.ShapeDtypeStruct((bsz, B_WIDTH // LANES, s, LANES), BF16),
            jax.ShapeDtypeStruct((bsz, B_HEADS * VT_ROWS, s), BF16),
        ] + [jax.ShapeDtypeStruct((bsz, s, C_WIDTH), dt) for dt in C_DTYPES],
        scratch_shapes=[pltpu.VMEM((tm, d), BF16), pltpu.VMEM((tm, IN_TOTAL), F32)],
        compiler_params=pltpu.CompilerParams(
            dimension_semantics=("arbitrary", "arbitrary"), vmem_limit_bytes=VMEM_LIMIT),
        name="in_projection",
    )(x, mod_l, w_in_bf16, cosa, sina, cosb, sinb, qk_gain)


def _place_rows(block, start, total):
    parts = []
    if start:
        parts.append(jnp.zeros((start, block.shape[1]), block.dtype))
    parts.append(block)
    rest = total - start - block.shape[0]
    if rest:
        parts.append(jnp.zeros((rest, block.shape[1]), block.dtype))
    return jnp.concatenate(parts, axis=0) if len(parts) > 1 else block


def _attention_items(items, k_ref, vt_ref, qz_scr, m_scr, acc_scr, exp_dtype=F32):
    s = k_ref.shape[2]
    tq = acc_scr.shape[2]
    m_scr[...] = jnp.full(m_scr.shape, -1e30, F32)
    acc_scr[...] = jnp.zeros_like(acc_scr)

    units = [(i, c) for c in range(s // KV_CHUNK) for i in range(len(items))]

    def scores(u):
        i, c = units[u]
        return _dot(k_ref[0, items[i][0], c * KV_CHUNK:(c + 1) * KV_CHUNK, :], qz_scr[i])

    pending = [scores(u) for u in range(min(QK_AHEAD, len(units)))]
    for u, (i, c) in enumerate(units):
        v_off = items[i][1]
        st = pending.pop(0)
        m_old = m_scr[i]
        part = jnp.max(st.reshape(KV_CHUNK // 8, 8, tq), axis=0)
        m_new = jnp.maximum(m_old, jnp.max(part, axis=0, keepdims=True))
        m_scr[i] = m_new
        p = jnp.exp2((st - m_new[0:1]).astype(exp_dtype)).astype(BF16)
        d = _dot(vt_ref[0, v_off:v_off + VT_ROWS, c * KV_CHUNK:(c + 1) * KV_CHUNK], p)
        if u + QK_AHEAD < len(units):
            pending.append(scores(u + QK_AHEAD))
        acc_scr[i] = acc_scr[i] * jnp.exp2(m_old[0:1] - m_new[0:1]) + d
    results = []
    for i in range(len(items)):
        acc = acc_scr[i]
        results.append(acc[0:HEAD_DIM] * (1.0 / acc[HEAD_DIM:HEAD_DIM + 1]))
    return results


def _attention_scratch(heads, tq):
    n_items = heads * (tq // Q_TILE)
    return [pltpu.VMEM((n_items, LANES, Q_TILE), BF16), pltpu.VMEM((n_items, 8, Q_TILE), F32),
            pltpu.VMEM((n_items, VT_ROWS, Q_TILE), F32)]


def _attn_a_kernel(qt_ref, k_ref, vt_ref, o_ref, qz_scr, m_scr, acc_scr):
    group = A_HEADS // A_KV_HEADS
    n_tiles = qt_ref.shape[2] // Q_TILE
    items = []
    for t in range(n_tiles):
        cols = slice(t * Q_TILE, (t + 1) * Q_TILE)
        for h in range(A_HEADS):
            g = h // group
            qz_scr[len(items)] = _place_rows(qt_ref[0, h * HEAD_DIM:(h + 1) * HEAD_DIM, cols], g * HEAD_DIM, LANES)
            items.append((0, g * VT_ROWS))
    outs = _attention_items(items, k_ref, vt_ref, qz_scr, m_scr, acc_scr)
    for t in range(n_tiles):
        tile = outs[t * A_HEADS:(t + 1) * A_HEADS]
        o_ref[0, t * Q_TILE:(t + 1) * Q_TILE, :] = jnp.concatenate([o.T for o in tile], axis=-1).astype(BF16)


def _attention_a(qat, ka, vat, tq):
    bsz, _, s, _ = ka.shape
    return pl.pallas_call(
        _attn_a_kernel,
        grid=(bsz, s // tq),
        in_specs=[
            pl.BlockSpec((1, A_WIDTH, tq), lambda b, i: (b, 0, i)),
            pl.BlockSpec((1, A_KV_WIDTH // LANES, s, LANES), lambda b, i: (b, 0, 0, 0)),
            pl.BlockSpec((1, A_KV_HEADS * VT_ROWS, s), lambda b, i: (b, 0, 0)),
        ],
        out_specs=pl.BlockSpec((1, tq, A_WIDTH), lambda b, i: (b, i, 0)),
        out_shape=jax.ShapeDtypeStruct((bsz, s, A_WIDTH), BF16),
        scratch_shapes=_attention_scratch(A_HEADS, tq),
        compiler_params=pltpu.CompilerParams(
            dimension_semantics=("arbitrary", "arbitrary"), vmem_limit_bytes=VMEM_LIMIT),
        name="attn_axial_gqa",
    )(qat, ka, vat)


def _attn_b_kernel(lam_ref, gain_ref, qt_ref, k_ref, vt_ref, o_ref, qz_scr, m_scr, acc_scr, *, lam_init):
    lp = lam_ref[...]
    lam = (jnp.exp(jnp.sum(lp[0:1] * lp[1:2], axis=-1, keepdims=True))
           - jnp.exp(jnp.sum(lp[2:3] * lp[3:4], axis=-1, keepdims=True)) + lam_init)
    n_tiles = qt_ref.shape[2] // Q_TILE
    items = []
    for t in range(n_tiles):
        cols = slice(t * Q_TILE, (t + 1) * Q_TILE)
        for h in range(B_HEADS):
            pair = (h // 2) * LANES
            for comp in range(2):
                r0 = h * HEAD_DIM + comp * B_QK_DIM
                qz_scr[len(items)] = _place_rows(qt_ref[0, r0:r0 + B_QK_DIM, cols], r0 - pair, LANES)
                items.append((h // 2, h * VT_ROWS))
    comps = _attention_items(items, k_ref, vt_ref, qz_scr, m_scr, acc_scr)
    for t in range(n_tiles):
        outs = []
        for h in range(B_HEADS):
            i = (t * B_HEADS + h) * 2
            o = comps[i] - lam * comps[i + 1]
            o = o * lax.rsqrt(jnp.mean(o * o, axis=0, keepdims=True) + EPS)
            outs.append(o.T)
        y = jnp.concatenate(outs, axis=-1) * gain_ref[...] * (1.0 - lam_init)
        o_ref[0, t * Q_TILE:(t + 1) * Q_TILE, :] = y.astype(BF16)


def _attention_b(qbt, kb, vbt, lam_params, subln_gain, lam_init, tq):
    bsz, _, s, _ = kb.shape
    return pl.pallas_call(
        functools.partial(_attn_b_kernel, lam_init=lam_init),
        grid=(bsz, s // tq),
        in_specs=[
            pl.BlockSpec((4, B_QK_DIM), lambda b, i: (0, 0)),
            pl.BlockSpec((1, B_WIDTH), lambda b, i: (0, 0)),
            pl.BlockSpec((1, B_WIDTH, tq), lambda b, i: (b, 0, i)),
            pl.BlockSpec((1, B_WIDTH // LANES, s, LANES), lambda b, i: (b, 0, 0, 0)),
            pl.BlockSpec((1, B_HEADS * VT_ROWS, s), lambda b, i: (b, 0, 0)),
        ],
        out_specs=pl.BlockSpec((1, tq, B_WIDTH), lambda b, i: (b, i, 0)),
        out_shape=jax.ShapeDtypeStruct((bsz, s, B_WIDTH), BF16),
        scratch_shapes=_attention_scratch(2 * B_HEADS, tq),
        compiler_params=pltpu.CompilerParams(
            dimension_semantics=("arbitrary", "arbitrary"), vmem_limit_bytes=VMEM_LIMIT),
        name="attn_differential",
    )(lam_params, subln_gain, qbt, kb, vbt)


def _hgrn_levels():
    sizes = []
    bs = HGRN_CHUNK
    while bs > HGRN_LEAF:
        sizes.append(bs)
        bs //= 2
    return sizes


def _hgrn_constants():
    c, w = HGRN_CHUNK, C_WIDTH
    t = np.arange(c)
    tri = np.stack([(t[:, None] >= t[None, :]), (t[:, None] <= t[None, :])]).astype(np.float32)
    s_of_lane = np.arange(w) % C_DIM
    sizes = _hgrn_levels()
    level = np.full((2, c, w), float(len(sizes) + 1), np.float32)
    for d in range(2):
        for li, bs in enumerate(sizes):
            upper = (t % bs) >= bs // 2
            qrow = upper if d == 0 else ~upper
            same = (t[:, None] // bs) == (s_of_lane[None, :] // bs)
            cross = qrow[:, None] & ~qrow[s_of_lane][None, :]
            level[d][same & cross] = li
        same_leaf = (t[:, None] // HGRN_LEAF) == (s_of_lane[None, :] // HGRN_LEAF)
        order = (s_of_lane[None, :] <= t[:, None]) if d == 0 else (s_of_lane[None, :] >= t[:, None])
        level[d][same_leaf & order] = len(sizes)
    half = (np.arange(LANES)[None, :] < C_DIM).astype(np.float32) * np.ones((c, 1), np.float32)
    return tri, level, half


def _block_diag(x, low_half):
    zero = jnp.zeros((x.shape[0], LANES), x.dtype)
    blocks = []
    for h in range(C_HEADS):
        tile = x[:, (h // 2) * LANES:(h // 2 + 1) * LANES]
        keep = tile * (low_half if h % 2 == 0 else 1 - low_half)
        blocks.append(jnp.concatenate([keep, zero] if h // 2 == 0 else [zero, keep], axis=1))
    return jnp.concatenate(blocks, axis=0)


def _block_diag_t(x, low_half):
    zero = jnp.zeros((LANES, LANES), x.dtype)
    rows = []
    for t in range(C_WIDTH // LANES):
        tile = x[:, t * LANES:(t + 1) * LANES]
        pair = jnp.concatenate([tile * low_half, tile * (1 - low_half)], axis=0).T
        rows.append(jnp.concatenate([pair, zero] if t == 0 else [zero, pair], axis=1))
    return jnp.concatenate(rows, axis=0)


def _ref_rows(b, bs, offset):
    parts = []
    for lo in range(0, b.shape[0], bs):
        parts.append(jnp.broadcast_to(b[lo + offset:lo + offset + 1, :], (bs, b.shape[1])))
    return jnp.concatenate(parts, axis=0)


def _level_operands(q, k, b, bs, d):
    half = bs // 2
    ref_off = half - 1 if d == 0 else half
    zeros = jnp.zeros((8, b.shape[1]), F32)
    qparts, kparts = [], []
    for r0 in range(0, b.shape[0], 8):
        lo = (r0 // bs) * bs
        ref = b[lo + ref_off:lo + ref_off + 1, :]
        upper = (r0 % bs) >= half
        rows = slice(r0, r0 + 8)
        if upper == (d == 0):
            qparts.append(q[rows] * jnp.exp2(b[rows] - ref))
            kparts.append(zeros)
        else:
            qparts.append(zeros)
            kparts.append(k[rows] * jnp.exp2(ref - b[rows]))
    return jnp.concatenate(qparts, axis=0).astype(BF16), jnp.concatenate(kparts, axis=0).astype(BF16)


def _hgrn_kernel(lb_ref, tri_ref, level_ref, half_ref, zq_f, zf_f, zv_f, zq_b, zf_b, zv_b,
                 of_ref, ob_ref, st_scr, *, layer, depth, nb):
    c, w = HGRN_CHUNK, C_WIDTH
    sizes = _hgrn_levels()
    n_lvl = len(sizes)

    @pl.when(pl.program_id(1) == 0)
    def _():
        st_scr[...] = jnp.zeros_like(st_scr)

    rows = [lb_ref[i:i + 1, :] for i in range(depth)]
    mx = functools.reduce(jnp.maximum, rows)
    es = [jnp.exp(r - mx) for r in rows]
    tot = functools.reduce(lambda a, b_: a + b_, es)
    ps = [e / tot for e in es]
    cum = functools.reduce(lambda a, b_: a + b_, ps[:layer + 1])
    one_minus_lb = 1.0 - jnp.clip(cum - ps[0], 0.0, 1.0)

    low_half = half_ref[...].astype(BF16)
    low_bool = half_ref[...] > 0.5
    owner = [[level_ref[d] == float(li) for li in range(n_lvl + 1)] for d in range(2)]
    chains = [(j, d) for j in range(nb) for d in range(2)]
    zrefs = {0: (zq_f, zf_f, zv_f), 1: (zq_b, zf_b, zv_b)}

    q, k, v, b, total = {}, {}, {}, {}, {}
    g_split = {}
    for ch in chains:
        j, d = ch
        zq, zf = zrefs[d][0][j], zrefs[d][1][j]
        v[ch] = zrefs[d][2][j]
        q[ch] = (zq * (C_DIM ** -0.5)) / (1.0 + jnp.exp2(zq * -LOG2_E))
        k[ch] = one_minus_lb * (1.0 - 1.0 / (1.0 + jnp.exp2(zf * -LOG2_E)))
        g = jnp.log2(jnp.maximum(1.0 - k[ch], FORGET_FLOOR))
        g_hi = g.astype(BF16)
        g_r = g - g_hi.astype(F32)
        g_mid = g_r.astype(BF16)
        g_lo = (g_r - g_mid.astype(F32)).astype(BF16)
        g_split[ch] = jnp.concatenate([g_hi, g_mid, g_lo], axis=1)
    for ch in chains:
        d = ch[1]
        bb = _dot(tri_ref[d], g_split[ch])
        b[ch] = bb[:, 0:w] + bb[:, w:2 * w] + bb[:, 2 * w:3 * w]
        total[ch] = b[ch][c - 1:c, :] if d == 0 else b[ch][0:1, :]

    parts = {ch: [] for ch in chains}
    for li in range(n_lvl + 1):
        for ch in chains:
            d = ch[1]
            if li < n_lvl:
                qp, kp = _level_operands(q[ch], k[ch], b[ch], sizes[li], d)
            else:
                ref = _ref_rows(b[ch], HGRN_LEAF, HGRN_LEAF // 2 - 1 if d == 0 else HGRN_LEAF // 2)
                arg = b[ch] - ref
                qp = (q[ch] * jnp.exp2(arg)).astype(BF16)
                kp = (k[ch] * jnp.exp2(-arg)).astype(BF16)
            parts[ch].append(_dot(qp, _block_diag_t(kp, low_half)))
    a = {}
    for ch in chains:
        acc = parts[ch][0]
        for li in range(1, n_lvl + 1):
            acc = jnp.where(owner[ch[1]][li], parts[ch][li], acc)
        a[ch] = acc.astype(BF16)

    o_intra, o_inter, upd = {}, {}, {}
    for ch in chains:
        j, d = ch
        o_intra[ch] = _dot(a[ch], _block_diag(v[ch], low_half))
        qe = (q[ch] * jnp.exp2(b[ch])).astype(BF16)
        o_inter[ch] = _dot(qe, _block_diag_t(st_scr[d, j].astype(BF16), low_half))
        kb = (k[ch] * jnp.exp2(total[ch] - b[ch])).astype(BF16)
        tiles = []
        for t in range(w // LANES):
            lanes = slice(t * LANES, (t + 1) * LANES)
            full = _dot_tn(v[ch][:, lanes], kb[:, lanes])
            tiles.append(jnp.where(low_bool, full[0:C_DIM], full[C_DIM:2 * C_DIM]))
        upd[ch] = jnp.concatenate(tiles, axis=1)
    for ch in chains:
        j, d = ch
        st_scr[d, j] = st_scr[d, j] * jnp.exp2(total[ch]) + upd[ch]
        (of_ref if d == 0 else ob_ref)[j] = o_intra[ch] + o_inter[ch]


def _hgrn(cq, cff, cfb, ci, lower_bounds, layer, nb):
    bsz, s, _ = cq.shape
    depth = lower_bounds.shape[0]
    n_chunks = s // HGRN_CHUNK
    c, w = HGRN_CHUNK, C_WIDTH
    tri, level, half = _hgrn_constants()
    consts = (jnp.asarray(tri, BF16), jnp.asarray(level), jnp.asarray(half))
    full = lambda a: pl.BlockSpec(a.shape, lambda g, i: (0,) * a.ndim)
    last = n_chunks - 1
    fwd = pl.BlockSpec((nb, c, w), lambda g, i: (g, i, 0))
    bwd = pl.BlockSpec((nb, c, w), lambda g, i: (g, last - i, 0))
    return pl.pallas_call(
        functools.partial(_hgrn_kernel, layer=layer, depth=depth, nb=nb),
        grid=(bsz // nb, n_chunks),
        in_specs=[full(lower_bounds)] + [full(a) for a in consts] + [fwd, fwd, fwd, bwd, bwd, bwd],
        out_specs=[fwd, bwd],
        out_shape=[jax.ShapeDtypeStruct((bsz, s, w), F32)] * 2,
        scratch_shapes=[pltpu.VMEM((2, nb, C_DIM, w), F32)],
        compiler_params=pltpu.CompilerParams(
            dimension_semantics=("arbitrary", "arbitrary"), vmem_limit_bytes=VMEM_LIMIT),
        name="hgrn2_bidir",
    )(lower_bounds, *consts, cq, cff, ci, cq, cfb, ci)


def _out_ffn_kernel(x_ref, mod_ref, oa_ref, ob_ref, cf_ref, cb_ref, zg_ref, cgain_ref,
                    wo_ref, w1_ref, w2_ref, fg_ref, o_ref, *, final, ff_chunk):
    oc = cf_ref[0] + cb_ref[0]
    ms = _split_dot(oc * oc, _group_ones(C_WIDTH, C_DIM))
    zg = zg_ref[0]
    oc = oc * lax.rsqrt(ms + EPS) * cgain_ref[...] * (zg * jax.nn.sigmoid(zg))
    mix = jnp.concatenate([oa_ref[0], ob_ref[0], oc.astype(BF16)], axis=-1)
    x1 = x_ref[0] + mod_ref[0, 2:3, :] * _dot(mix, wo_ref[0])
    h = (_rms_rows(x1) * (1.0 + mod_ref[0, 4:5, :]) + mod_ref[0, 3:4, :]).astype(BF16)
    d_ff = w1_ref.shape[2]
    acc = jnp.zeros_like(x1)
    for j in range(d_ff // ff_chunk):
        u = _dot(h, w1_ref[0, :, j * ff_chunk:(j + 1) * ff_chunk])
        u = jnp.square(jnp.maximum(u, 0.0)).astype(BF16)
        acc = acc + _dot(u, w2_ref[0, j * ff_chunk:(j + 1) * ff_chunk, :])
    x2 = x1 + mod_ref[0, 5:6, :] * acc
    if final:
        x2 = _rms_rows(x2) * fg_ref[...]
    o_ref[0] = x2


def _out_ffn(x, mod_l, oa, ob, oc_f, oc_b, cg, c_gain, w_out, w_ff1, w_ff2, final_gain, layer, final, tm):
    bsz, s, d = x.shape
    d_ff = w_ff1.shape[2]
    row_spec = lambda width: pl.BlockSpec((1, tm, width), lambda b, i: (b, i, 0))
    weight = lambda a: pl.BlockSpec((1,) + a.shape[1:], lambda b, i: (layer, 0, 0),
                                    pipeline_mode=pl.Buffered(1))
    return pl.pallas_call(
        functools.partial(_out_ffn_kernel, final=final, ff_chunk=COL_CHUNK),
        grid=(bsz, s // tm),
        in_specs=[
            row_spec(d),
            pl.BlockSpec((1, 6, d), lambda b, i: (b, 0, 0)),
            row_spec(A_WIDTH), row_spec(B_WIDTH), row_spec(C_WIDTH), row_spec(C_WIDTH), row_spec(C_WIDTH),
            pl.BlockSpec((1, C_WIDTH), lambda b, i: (0, 0)),
            weight(w_out), weight(w_ff1), weight(w_ff2),
            pl.BlockSpec((1, d), lambda b, i: (0, 0)),
        ],
        out_specs=row_spec(d),
        out_shape=jax.ShapeDtypeStruct((bsz, s, d), F32),
        compiler_params=pltpu.CompilerParams(
            dimension_semantics=("arbitrary", "arbitrary"), vmem_limit_bytes=VMEM_LIMIT),
        name="out_proj_ffn",
    )(x, mod_l, oa, ob, oc_f, oc_b, cg, c_gain, w_out, w_ff1, w_ff2, final_gain)


def _rope_tables(s):
    half = HEAD_DIM // 2
    inv = ROPE_THETA ** (-jnp.arange(0, half, 2, dtype=F32) / half)
    t = jnp.arange(s)
    ang_row = (t // GRID_W).astype(F32)[:, None] * inv[None, :]
    ang_col = (t % GRID_W).astype(F32)[:, None] * inv[None, :]
    ang_1d = t.astype(F32)[:, None] * inv[None, :]

    def pair(ang):
        return (jnp.concatenate([jnp.cos(ang), jnp.cos(ang)], axis=-1),
                jnp.concatenate([-jnp.sin(ang), jnp.sin(ang)], axis=-1))

    cr, sr = pair(ang_row)
    cc, sc = pair(ang_col)
    c1, s1 = pair(ang_1d)
    tile = lambda a: jnp.tile(a, (1, LANES // a.shape[1]))
    return (tile(jnp.concatenate([cr, cc], axis=-1)), tile(jnp.concatenate([sr, sc], axis=-1)),
            tile(c1), tile(s1))


def kernel(x, c, w_mod, b_mod, w_in, a_qk_norm, diff_lambda, diff_subln, hgrn_lower_bounds, hgrn_norm,
           w_out, w_ff1, w_ff2, final_norm):
    bsz, s, d = x.shape
    depth = w_in.shape[0]
    tm = min(ROW_TILE, s)
    tables = _rope_tables(s)
    mod = _modulation(c, w_mod, b_mod).reshape(depth, bsz, 6, d)
    w_in_b, w_out_b, w_ff1_b, w_ff2_b = (w.astype(BF16) for w in (w_in, w_out, w_ff1, w_ff2))
    final_gain = final_norm.reshape(1, d)
    for l in range(depth):
        qk_gain = jnp.tile(a_qk_norm[l], (1, LANES // HEAD_DIM))
        qat, ka, vat, qbt, kb, vbt, cq, cff, cfb, ci, cg = _in_projection(
            x, mod[l], w_in_b, l, tables, qk_gain, tm)
        oa = _attention_a(qat, ka, vat, min(A_Q_TILES * Q_TILE, s))
        lam_init = 0.8 - 0.6 * math.exp(-0.3 * l)
        subln = jnp.tile(diff_subln[l].reshape(1, HEAD_DIM), (1, B_HEADS))
        ob = _attention_b(qbt, kb, vbt, diff_lambda[l], subln, lam_init, min(B_Q_TILES * Q_TILE, s))
        hg_gain = jnp.tile(hgrn_norm[l].reshape(1, C_DIM), (1, C_HEADS))
        oc_f, oc_b = _hgrn(cq, cff, cfb, ci, hgrn_lower_bounds, l, min(HGRN_BATCH, bsz))
        x = _out_ffn(x, mod[l], oa, ob, oc_f, oc_b, cg, hg_gain, w_out_b, w_ff1_b, w_ff2_b, final_gain,
                     l, l == depth - 1, tm)
    return x
```

```python
import functools
import math

import numpy as np
import jax
import jax.numpy as jnp
from jax import lax
from jax.experimental import pallas as pl
from jax.experimental.pallas import tpu as pltpu

F32 = jnp.float32
BF16 = jnp.bfloat16

HEAD_DIM = 64
GRID_W = 64
ROPE_THETA = 10000.0
EPS = 1e-6
FORGET_FLOOR = 1e-6
A_HEADS, A_KV_HEADS = 6, 2
B_HEADS, B_QK_DIM = 6, 32
C_HEADS, C_DIM = 4, 64
A_WIDTH = A_HEADS * HEAD_DIM
A_KV_WIDTH = A_KV_HEADS * HEAD_DIM
B_WIDTH = B_HEADS * HEAD_DIM
C_WIDTH = C_HEADS * C_DIM
LANES = 128
HGRN_CHUNK = 64
HGRN_LEAF = 8
HGRN_BATCH = 16
ONES_ROWS = 16
VT_ROWS = HEAD_DIM + ONES_ROWS
KV_CHUNK = 256
Q_TILE = 256
A_Q_TILES = 4
B_Q_TILES = 4
QK_AHEAD = 16
LOG2_E = math.log2(math.e)
ROW_TILE = 512
COL_CHUNK = 1024
VMEM_LIMIT = 56 * 1024 * 1024

C_DTYPES = (F32, F32, F32, BF16, F32)
OFF_AQ = 0
OFF_AK = OFF_AQ + A_WIDTH
OFF_AV = OFF_AK + A_KV_WIDTH
OFF_BQ = OFF_AV + A_KV_WIDTH
OFF_BK = OFF_BQ + B_WIDTH
OFF_BV = OFF_BK + B_WIDTH
OFF_C = OFF_BV + B_WIDTH
IN_TOTAL = OFF_C + len(C_DTYPES) * C_WIDTH


def _dot(a, b):
    return jnp.dot(a, b, preferred_element_type=F32)


def _dot_tn(a, b):
    return lax.dot_general(a, b, (((0,), (0,)), ((), ())), preferred_element_type=F32)


def _split_dot(x, w_bf16):
    hi = x.astype(BF16)
    lo = (x - hi.astype(F32)).astype(BF16)
    return _dot(hi, w_bf16) + _dot(lo, w_bf16)


def _group_ones(n, group):
    r = lax.broadcasted_iota(jnp.int32, (n, n), 0) // group
    c = lax.broadcasted_iota(jnp.int32, (n, n), 1) // group
    return jnp.where(r == c, 1.0 / group, 0.0).astype(BF16)


def _rope(xb, cos, sin_signed):
    lane = lax.broadcasted_iota(jnp.int32, xb.shape, 1)
    low = (lane % 32) < 16
    partner = jnp.where(low, pltpu.roll(xb, LANES - 16, 1), pltpu.roll(xb, 16, 1))
    return xb * cos + partner * sin_signed


def _rms_rows(x):
    return x * lax.rsqrt(jnp.mean(x * x, axis=-1, keepdims=True) + EPS)


def _mod_kernel(c_ref, w_ref, b_ref, o_ref):
    c = c_ref[...]
    cond = (c * jax.nn.sigmoid(c)).astype(BF16)
    o_ref[0] = _dot(cond, w_ref[0].astype(BF16)) + b_ref[0]


def _modulation(c, w_mod, b_mod):
    depth, d, n = w_mod.shape
    bsz = c.shape[0]
    tn = COL_CHUNK
    return pl.pallas_call(
        _mod_kernel,
        grid=(depth, n // tn),
        in_specs=[
            pl.BlockSpec((bsz, d), lambda l, j: (0, 0)),
            pl.BlockSpec((1, d, tn), lambda l, j: (l, 0, j)),
            pl.BlockSpec((1, 1, tn), lambda l, j: (l, 0, j)),
        ],
        out_specs=pl.BlockSpec((1, bsz, tn), lambda l, j: (l, 0, j)),
        out_shape=jax.ShapeDtypeStruct((depth, bsz, n), F32),
        compiler_params=pltpu.CompilerParams(
            dimension_semantics=("arbitrary", "arbitrary"), vmem_limit_bytes=VMEM_LIMIT),
        name="adaln_mod",
    )(c, w_mod, b_mod.reshape(depth, 1, n))


def _inproj_kernel(x_ref, mod_ref, w_ref, cosa_ref, sina_ref, cosb_ref, sinb_ref, gain_ref,
                   qat_ref, ka_ref, vat_ref, qbt_ref, kb_ref, vbt_ref,
                   cq_ref, cff_ref, cfb_ref, ci_ref, cg_ref, h_scr, z_scr):
    x = x_ref[0]
    h = _rms_rows(x) * (1.0 + mod_ref[0, 1:2, :]) + mod_ref[0, 0:1, :]
    h_scr[...] = h.astype(BF16)

    def project(j):
        cols = slice(j * COL_CHUNK, (j + 1) * COL_CHUNK)
        z_scr[:, cols] = _dot(h_scr[...], w_ref[0, :, cols])

    project(0)
    project(1)

    ones = _group_ones(LANES, HEAD_DIM)
    cosa, sina = cosa_ref[...], sina_ref[...]
    cosb, sinb = cosb_ref[...], sinb_ref[...]

    def a_norm_rope(zb, gain):
        ms = _split_dot(zb * zb, ones)
        return _rope(zb * lax.rsqrt(ms + EPS) * gain, cosa, sina)

    def store_vt(vt_ref, first_head, blk_t):
        ones_rows = jnp.ones((ONES_ROWS, blk_t.shape[1]), BF16)
        for j in range(LANES // HEAD_DIM):
            r0 = (first_head + j) * VT_ROWS
            vt_ref[0, r0:r0 + HEAD_DIM, :] = blk_t[j * HEAD_DIM:(j + 1) * HEAD_DIM].astype(BF16)
            vt_ref[0, r0 + HEAD_DIM:r0 + VT_ROWS, :] = ones_rows

    for cblk in range(A_WIDTH // LANES):
        zb = z_scr[:, OFF_AQ + cblk * LANES:OFF_AQ + (cblk + 1) * LANES]
        y = a_norm_rope(zb, gain_ref[0:1, :]) * (HEAD_DIM ** -0.5 * LOG2_E)
        qat_ref[0, cblk * LANES:(cblk + 1) * LANES, :] = y.T.astype(BF16)
    ka_ref[0, 0] = a_norm_rope(z_scr[:, OFF_AK:OFF_AK + LANES], gain_ref[1:2, :]).astype(BF16)
    store_vt(vat_ref, 0, z_scr[:, OFF_AV:OFF_AV + LANES].T)
    project(2)

    for cblk in range(B_WIDTH // LANES):
        lo = cblk * LANES
        zq = z_scr[:, OFF_BQ + lo:OFF_BQ + lo + LANES]
        qbt_ref[0, lo:lo + LANES, :] = (_rope(zq, cosb, sinb) * (B_QK_DIM ** -0.5 * LOG2_E)).T.astype(BF16)
    for cblk in range(B_WIDTH // LANES):
        lo = cblk * LANES
        zk = z_scr[:, OFF_BK + lo:OFF_BK + lo + LANES]
        kb_ref[0, cblk] = _rope(zk, cosb, sinb).astype(BF16)
        store_vt(vbt_ref, cblk * (LANES // HEAD_DIM), z_scr[:, OFF_BV + lo:OFF_BV + lo + LANES].T)
    for j, ref in enumerate((cq_ref, cff_ref, cfb_ref, ci_ref, cg_ref)):
        ref[0] = z_scr[:, OFF_C + j * C_WIDTH:OFF_C + (j + 1) * C_WIDTH].astype(ref.dtype)


def _in_projection(x, mod_l, w_in_bf16, layer, tables, qk_gain, tm):
    bsz, s, d = x.shape
    cosa, sina, cosb, sinb = tables
    row_spec = lambda width: pl.BlockSpec((1, tm, width), lambda b, i: (b, i, 0))
    tab_spec = pl.BlockSpec((tm, LANES), lambda b, i: (i, 0))
    col_spec = lambda height: pl.BlockSpec((1, height, tm), lambda b, i: (b, 0, i))
    key_spec = lambda blocks: pl.BlockSpec((1, blocks, tm, LANES), lambda b, i: (b, 0, i, 0))
    return pl.pallas_call(
        _inproj_kernel,
        grid=(bsz, s // tm),
        in_specs=[
            row_spec(d),
            pl.BlockSpec((1, 6, d), lambda b, i: (b, 0, 0)),
            pl.BlockSpec((1, d, IN_TOTAL), lambda b, i: (layer, 0, 0)),
            tab_spec, tab_spec, tab_spec, tab_spec,
            pl.BlockSpec((2, LANES), lambda b, i: (0, 0)),
        ],
        out_specs=[
            col_spec(A_WIDTH), key_spec(A_KV_WIDTH // LANES), col_spec(A_KV_HEADS * VT_ROWS),
            col_spec(B_WIDTH), key_spec(B_WIDTH // LANES), col_spec(B_HEADS * VT_ROWS),
        ] + [row_spec(C_WIDTH)] * len(C_DTYPES),
        out_shape=[
            jax.ShapeDtypeStruct((bsz, A_WIDTH, s), BF16),
            jax.ShapeDtypeStruct((bsz, A_KV_WIDTH // LANES, s, LANES), BF16),
            jax.ShapeDtypeStruct((bsz, A_KV_HEADS * VT_ROWS, s), BF16),
            jax.ShapeDtypeStruct((bsz, B_WIDTH, s), BF16),
            jax.ShapeDtypeStruct((bsz, B_WIDTH // LANES, s, LANES), BF16),
            jax.ShapeDtypeStruct((bsz, B_HEADS * VT_ROWS, s), BF16),
        ] + [jax.ShapeDtypeStruct((bsz, s, C_WIDTH), dt) for dt in C_DTYPES],
        scratch_shapes=[pltpu.VMEM((tm, d), BF16), pltpu.VMEM((tm, IN_TOTAL), F32)],
        compiler_params=pltpu.CompilerParams(
            dimension_semantics=("arbitrary", "arbitrary"), vmem_limit_bytes=VMEM_LIMIT),
        name="in_projection",
    )(x, mod_l, w_in_bf16, cosa, sina, cosb, sinb, qk_gain)


def _place_rows(block, start, total):
    parts = []
    if start:
        parts.append(jnp.zeros((start, block.shape[1]), block.dtype))
    parts.append(block)
    rest = total - start - block.shape[0]
    if rest:
        parts.append(jnp.zeros((rest, block.shape[1]), block.dtype))
    return jnp.concatenate(parts, axis=0) if len(parts) > 1 else block


def _attention_items(items, k_ref, vt_ref, qz_scr, m_scr, acc_scr):
    s = k_ref.shape[2]
    tq = acc_scr.shape[2]
    m_scr[...] = jnp.full(m_scr.shape, -1e30, F32)
    acc_scr[...] = jnp.zeros_like(acc_scr)

    units = [(i, c) for c in range(s // KV_CHUNK) for i in range(len(items))]

    def scores(u):
        i, c = units[u]
        return _dot(k_ref[0, items[i][0], c * KV_CHUNK:(c + 1) * KV_CHUNK, :], qz_scr[i])

    pending = [scores(u) for u in range(min(QK_AHEAD, len(units)))]
    for u, (i, c) in enumerate(units):
        v_off = items[i][1]
        st = pending.pop(0)
        m_old = m_scr[i]
        part = jnp.max(st.reshape(KV_CHUNK // 8, 8, tq), axis=0)
        m_new = jnp.maximum(m_old, jnp.max(part, axis=0, keepdims=True))
        m_scr[i] = m_new
        p = jnp.exp2(st - m_new[0:1]).astype(BF16)
        d = _dot(vt_ref[0, v_off:v_off + VT_ROWS, c * KV_CHUNK:(c + 1) * KV_CHUNK], p)
        if u + QK_AHEAD < len(units):
            pending.append(scores(u + QK_AHEAD))
        acc_scr[i] = acc_scr[i] * jnp.exp2(m_old[0:1] - m_new[0:1]) + d
    results = []
    for i in range(len(items)):
        acc = acc_scr[i]
        results.append(acc[0:HEAD_DIM] * (1.0 / acc[HEAD_DIM:HEAD_DIM + 1]))
    return results


def _attention_scratch(heads, tq):
    n_items = heads * (tq // Q_TILE)
    return [pltpu.VMEM((n_items, LANES, Q_TILE), BF16), pltpu.VMEM((n_items, 8, Q_TILE), F32),
            pltpu.VMEM((n_items, VT_ROWS, Q_TILE), F32)]


def _attn_a_kernel(qt_ref, k_ref, vt_ref, o_ref, qz_scr, m_scr, acc_scr):
    group = A_HEADS // A_KV_HEADS
    n_tiles = qt_ref.shape[2] // Q_TILE
    items = []
    for t in range(n_tiles):
        cols = slice(t * Q_TILE, (t + 1) * Q_TILE)
        for h in range(A_HEADS):
            g = h // group
            qz_scr[len(items)] = _place_rows(qt_ref[0, h * HEAD_DIM:(h + 1) * HEAD_DIM, cols], g * HEAD_DIM, LANES)
            items.append((0, g * VT_ROWS))
    outs = _attention_items(items, k_ref, vt_ref, qz_scr, m_scr, acc_scr)
    for t in range(n_tiles):
        tile = outs[t * A_HEADS:(t + 1) * A_HEADS]
        o_ref[0, t * Q_TILE:(t + 1) * Q_TILE, :] = jnp.concatenate([o.T for o in tile], axis=-1).astype(BF16)


def _attention_a(qat, ka, vat, tq):
    bsz, _, s, _ = ka.shape
    return pl.pallas_call(
        _attn_a_kernel,
        grid=(bsz, s // tq),
        in_specs=[
            pl.BlockSpec((1, A_WIDTH, tq), lambda b, i: (b, 0, i)),
            pl.BlockSpec((1, A_KV_WIDTH // LANES, s, LANES), lambda b, i: (b, 0, 0, 0)),
            pl.BlockSpec((1, A_KV_HEADS * VT_ROWS, s), lambda b, i: (b, 0, 0)),
        ],
        out_specs=pl.BlockSpec((1, tq, A_WIDTH), lambda b, i: (b, i, 0)),
        out_shape=jax.ShapeDtypeStruct((bsz, s, A_WIDTH), BF16),
        scratch_shapes=_attention_scratch(A_HEADS, tq),
        compiler_params=pltpu.CompilerParams(
            dimension_semantics=("arbitrary", "arbitrary"), vmem_limit_bytes=VMEM_LIMIT),
        name="attn_axial_gqa",
    )(qat, ka, vat)


def _attn_b_kernel(lam_ref, gain_ref, qt_ref, k_ref, vt_ref, o_ref, qz_scr, m_scr, acc_scr, *, lam_init):
    lp = lam_ref[...]
    lam = (jnp.exp(jnp.sum(lp[0:1] * lp[1:2], axis=-1, keepdims=True))
           - jnp.exp(jnp.sum(lp[2:3] * lp[3:4], axis=-1, keepdims=True)) + lam_init)
    n_tiles = qt_ref.shape[2] // Q_TILE
    items = []
    for t in range(n_tiles):
        cols = slice(t * Q_TILE, (t + 1) * Q_TILE)
        for h in range(B_HEADS):
            pair = (h // 2) * LANES
            for comp in range(2):
                r0 = h * HEAD_DIM + comp * B_QK_DIM
                qz_scr[len(items)] = _place_rows(qt_ref[0, r0:r0 + B_QK_DIM, cols], r0 - pair, LANES)
                items.append((h // 2, h * VT_ROWS))
    comps = _attention_items(items, k_ref, vt_ref, qz_scr, m_scr, acc_scr)
    for t in range(n_tiles):
        outs = []
        for h in range(B_HEADS):
            i = (t * B_HEADS + h) * 2
            o = comps[i] - lam * comps[i + 1]
            o = o * lax.rsqrt(jnp.mean(o * o, axis=0, keepdims=True) + EPS)
            outs.append(o.T)
        y = jnp.concatenate(outs, axis=-1) * gain_ref[...] * (1.0 - lam_init)
        o_ref[0, t * Q_TILE:(t + 1) * Q_TILE, :] = y.astype(BF16)


def _attention_b(qbt, kb, vbt, lam_params, subln_gain, lam_init, tq):
    bsz, _, s, _ = kb.shape
    return pl.pallas_call(
        functools.partial(_attn_b_kernel, lam_init=lam_init),
        grid=(bsz, s // tq),
        in_specs=[
            pl.BlockSpec((4, B_QK_DIM), lambda b, i: (0, 0)),
            pl.BlockSpec((1, B_WIDTH), lambda b, i: (0, 0)),
            pl.BlockSpec((1, B_WIDTH, tq), lambda b, i: (b, 0, i)),
            pl.BlockSpec((1, B_WIDTH // LANES, s, LANES), lambda b, i: (b, 0, 0, 0)),
            pl.BlockSpec((1, B_HEADS * VT_ROWS, s), lambda b, i: (b, 0, 0)),
        ],
        out_specs=pl.BlockSpec((1, tq, B_WIDTH), lambda b, i: (b, i, 0)),
        out_shape=jax.ShapeDtypeStruct((bsz, s, B_WIDTH), BF16),
        scratch_shapes=_attention_scratch(2 * B_HEADS, tq),
        compiler_params=pltpu.CompilerParams(
            dimension_semantics=("arbitrary", "arbitrary"), vmem_limit_bytes=VMEM_LIMIT),
        name="attn_differential",
    )(lam_params, subln_gain, qbt, kb, vbt)


def _hgrn_levels():
    sizes = []
    bs = HGRN_CHUNK
    while bs > HGRN_LEAF:
        sizes.append(bs)
        bs //= 2
    return sizes


def _hgrn_constants():
    c, w = HGRN_CHUNK, C_WIDTH
    t = np.arange(c)
    tri = np.stack([(t[:, None] >= t[None, :]), (t[:, None] <= t[None, :])]).astype(np.float32)
    s_of_lane = np.arange(w) % C_DIM
    sizes = _hgrn_levels()
    level = np.full((2, c, w), float(len(sizes) + 1), np.float32)
    for d in range(2):
        for li, bs in enumerate(sizes):
            upper = (t % bs) >= bs // 2
            qrow = upper if d == 0 else ~upper
            same = (t[:, None] // bs) == (s_of_lane[None, :] // bs)
            cross = qrow[:, None] & ~qrow[s_of_lane][None, :]
            level[d][same & cross] = li
        same_leaf = (t[:, None] // HGRN_LEAF) == (s_of_lane[None, :] // HGRN_LEAF)
        order = (s_of_lane[None, :] <= t[:, None]) if d == 0 else (s_of_lane[None, :] >= t[:, None])
        level[d][same_leaf & order] = len(sizes)
    half = (np.arange(LANES)[None, :] < C_DIM).astype(np.float32) * np.ones((c, 1), np.float32)
    return tri, level, half


def _block_diag(x, low_half):
    zero = jnp.zeros((x.shape[0], LANES), x.dtype)
    blocks = []
    for h in range(C_HEADS):
        tile = x[:, (h // 2) * LANES:(h // 2 + 1) * LANES]
        keep = tile * (low_half if h % 2 == 0 else 1 - low_half)
        blocks.append(jnp.concatenate([keep, zero] if h // 2 == 0 else [zero, keep], axis=1))
    return jnp.concatenate(blocks, axis=0)


def _block_diag_t(x, low_half):
    zero = jnp.zeros((LANES, LANES), x.dtype)
    rows = []
    for t in range(C_WIDTH // LANES):
        tile = x[:, t * LANES:(t + 1) * LANES]
        pair = jnp.concatenate([tile * low_half, tile * (1 - low_half)], axis=0).T
        rows.append(jnp.concatenate([pair, zero] if t == 0 else [zero, pair], axis=1))
    return jnp.concatenate(rows, axis=0)


def _ref_rows(b, bs, offset):
    parts = []
    for lo in range(0, b.shape[0], bs):
        parts.append(jnp.broadcast_to(b[lo + offset:lo + offset + 1, :], (bs, b.shape[1])))
    return jnp.concatenate(parts, axis=0)


def _level_operands(q, k, b, bs, d):
    half = bs // 2
    ref_off = half - 1 if d == 0 else half
    zeros = jnp.zeros((8, b.shape[1]), F32)
    qparts, kparts = [], []
    for r0 in range(0, b.shape[0], 8):
        lo = (r0 // bs) * bs
        ref = b[lo + ref_off:lo + ref_off + 1, :]
        upper = (r0 % bs) >= half
        rows = slice(r0, r0 + 8)
        if upper == (d == 0):
            qparts.append(q[rows] * jnp.exp2(b[rows] - ref))
            kparts.append(zeros)
        else:
            qparts.append(zeros)
            kparts.append(k[rows] * jnp.exp2(ref - b[rows]))
    return jnp.concatenate(qparts, axis=0).astype(BF16), jnp.concatenate(kparts, axis=0).astype(BF16)


def _hgrn_kernel(lb_ref, tri_ref, level_ref, half_ref, zq_f, zf_f, zv_f, zq_b, zf_b, zv_b,
                 of_ref, ob_ref, st_scr, *, layer, depth, nb):
    c, w = HGRN_CHUNK, C_WIDTH
    sizes = _hgrn_levels()
    n_lvl = len(sizes)

    @pl.when(pl.program_id(1) == 0)
    def _():
        st_scr[...] = jnp.zeros_like(st_scr)

    rows = [lb_ref[i:i + 1, :] for i in range(depth)]
    mx = functools.reduce(jnp.maximum, rows)
    es = [jnp.exp(r - mx) for r in rows]
    tot = functools.reduce(lambda a, b_: a + b_, es)
    ps = [e / tot for e in es]
    cum = functools.reduce(lambda a, b_: a + b_, ps[:layer + 1])
    one_minus_lb = 1.0 - jnp.clip(cum - ps[0], 0.0, 1.0)

    low_half = half_ref[...].astype(BF16)
    low_bool = half_ref[...] > 0.5
    owner = [[level_ref[d] == float(li) for li in range(n_lvl + 1)] for d in range(2)]
    chains = [(j, d) for j in range(nb) for d in range(2)]
    zrefs = {0: (zq_f, zf_f, zv_f), 1: (zq_b, zf_b, zv_b)}

    q, k, v, b, total = {}, {}, {}, {}, {}
    g_split = {}
    for ch in chains:
        j, d = ch
        zq, zf = zrefs[d][0][j], zrefs[d][1][j]
        v[ch] = zrefs[d][2][j]
        q[ch] = (zq * (C_DIM ** -0.5)) / (1.0 + jnp.exp2(zq * -LOG2_E))
        k[ch] = one_minus_lb * (1.0 - 1.0 / (1.0 + jnp.exp2(zf * -LOG2_E)))
        g = jnp.log2(jnp.maximum(1.0 - k[ch], FORGET_FLOOR))
        g_hi = g.astype(BF16)
        g_r = g - g_hi.astype(F32)
        g_mid = g_r.astype(BF16)
        g_lo = (g_r - g_mid.astype(F32)).astype(BF16)
        g_split[ch] = jnp.concatenate([g_hi, g_mid, g_lo], axis=1)
    for ch in chains:
        d = ch[1]
        bb = _dot(tri_ref[d], g_split[ch])
        b[ch] = bb[:, 0:w] + bb[:, w:2 * w] + bb[:, 2 * w:3 * w]
        total[ch] = b[ch][c - 1:c, :] if d == 0 else b[ch][0:1, :]

    parts = {ch: [] for ch in chains}
    for li in range(n_lvl + 1):
        for ch in chains:
            d = ch[1]
            if li < n_lvl:
                qp, kp = _level_operands(q[ch], k[ch], b[ch], sizes[li], d)
            else:
                ref = _ref_rows(b[ch], HGRN_LEAF, HGRN_LEAF // 2 - 1 if d == 0 else HGRN_LEAF // 2)
                arg = b[ch] - ref
                qp = (q[ch] * jnp.exp2(arg)).astype(BF16)
                kp = (k[ch] * jnp.exp2(-arg)).astype(BF16)
            parts[ch].append(_dot(qp, _block_diag_t(kp, low_half)))
    a = {}
    for ch in chains:
        acc = parts[ch][0]
        for li in range(1, n_lvl + 1):
            acc = jnp.where(owner[ch[1]][li], parts[ch][li], acc)
        a[ch] = acc.astype(BF16)

    o_intra, o_inter, upd = {}, {}, {}
    for ch in chains:
        j, d = ch
        o_intra[ch] = _dot(a[ch], _block_diag(v[ch], low_half))
        qe = (q[ch] * jnp.exp2(b[ch])).astype(BF16)
        o_inter[ch] = _dot(qe, _block_diag_t(st_scr[d, j].astype(BF16), low_half))
        kb = (k[ch] * jnp.exp2(total[ch] - b[ch])).astype(BF16)
        tiles = []
        for t in range(w // LANES):
            lanes = slice(t * LANES, (t + 1) * LANES)
            full = _dot_tn(v[ch][:, lanes], kb[:, lanes])
            tiles.append(jnp.where(low_bool, full[0:C_DIM], full[C_DIM:2 * C_DIM]))
        upd[ch] = jnp.concatenate(tiles, axis=1)
    for ch in chains:
        j, d = ch
        st_scr[d, j] = st_scr[d, j] * jnp.exp2(total[ch]) + upd[ch]
        (of_ref if d == 0 else ob_ref)[j] = o_intra[ch] + o_inter[ch]


def _hgrn(cq, cff, cfb, ci, lower_bounds, layer, nb):
    bsz, s, _ = cq.shape
    depth = lower_bounds.shape[0]
    n_chunks = s // HGRN_CHUNK
    c, w = HGRN_CHUNK, C_WIDTH
    tri, level, half = _hgrn_constants()
    consts = (jnp.asarray(tri, BF16), jnp.asarray(level), jnp.asarray(half))
    full = lambda a: pl.BlockSpec(a.shape, lambda g, i: (0,) * a.ndim)
    last = n_chunks - 1
    fwd = pl.BlockSpec((nb, c, w), lambda g, i: (g, i, 0))
    bwd = pl.BlockSpec((nb, c, w), lambda g, i: (g, last - i, 0))
    return pl.pallas_call(
        functools.partial(_hgrn_kernel, layer=layer, depth=depth, nb=nb),
        grid=(bsz // nb, n_chunks),
        in_specs=[full(lower_bounds)] + [full(a) for a in consts] + [fwd, fwd, fwd, bwd, bwd, bwd],
        out_specs=[fwd, bwd],
        out_shape=[jax.ShapeDtypeStruct((bsz, s, w), F32)] * 2,
        scratch_shapes=[pltpu.VMEM((2, nb, C_DIM, w), F32)],
        compiler_params=pltpu.CompilerParams(
            dimension_semantics=("arbitrary", "arbitrary"), vmem_limit_bytes=VMEM_LIMIT),
        name="hgrn2_bidir",
    )(lower_bounds, *consts, cq, cff, ci, cq, cfb, ci)


def _out_ffn_kernel(x_ref, mod_ref, oa_ref, ob_ref, cf_ref, cb_ref, zg_ref, cgain_ref,
                    wo_ref, w1_ref, w2_ref, fg_ref, o_ref, *, final, ff_chunk):
    oc = cf_ref[0] + cb_ref[0]
    ms = _split_dot(oc * oc, _group_ones(C_WIDTH, C_DIM))
    zg = zg_ref[0]
    oc = oc * lax.rsqrt(ms + EPS) * cgain_ref[...] * (zg * jax.nn.sigmoid(zg))
    mix = jnp.concatenate([oa_ref[0], ob_ref[0], oc.astype(BF16)], axis=-1)
    x1 = x_ref[0] + mod_ref[0, 2:3, :] * _dot(mix, wo_ref[0])
    h = (_rms_rows(x1) * (1.0 + mod_ref[0, 4:5, :]) + mod_ref[0, 3:4, :]).astype(BF16)
    d_ff = w1_ref.shape[2]
    acc = jnp.zeros_like(x1)
    for j in range(d_ff // ff_chunk):
        u = _dot(h, w1_ref[0, :, j * ff_chunk:(j + 1) * ff_chunk])
        u = jnp.square(jnp.maximum(u, 0.0)).astype(BF16)
        acc = acc + _dot(u, w2_ref[0, j * ff_chunk:(j + 1) * ff_chunk, :])
    x2 = x1 + mod_ref[0, 5:6, :] * acc
    if final:
        x2 = _rms_rows(x2) * fg_ref[...]
    o_ref[0] = x2


def _out_ffn(x, mod_l, oa, ob, oc_f, oc_b, cg, c_gain, w_out, w_ff1, w_ff2, final_gain, layer, final, tm):
    bsz, s, d = x.shape
    d_ff = w_ff1.shape[2]
    row_spec = lambda width: pl.BlockSpec((1, tm, width), lambda b, i: (b, i, 0))
    weight = lambda a: pl.BlockSpec((1,) + a.shape[1:], lambda b, i: (layer, 0, 0),
                                    pipeline_mode=pl.Buffered(1))
    return pl.pallas_call(
        functools.partial(_out_ffn_kernel, final=final, ff_chunk=COL_CHUNK),
        grid=(bsz, s // tm),
        in_specs=[
            row_spec(d),
            pl.BlockSpec((1, 6, d), lambda b, i: (b, 0, 0)),
            row_spec(A_WIDTH), row_spec(B_WIDTH), row_spec(C_WIDTH), row_spec(C_WIDTH), row_spec(C_WIDTH),
            pl.BlockSpec((1, C_WIDTH), lambda b, i: (0, 0)),
            weight(w_out), weight(w_ff1), weight(w_ff2),
            pl.BlockSpec((1, d), lambda b, i: (0, 0)),
        ],
        out_specs=row_spec(d),
        out_shape=jax.ShapeDtypeStruct((bsz, s, d), F32),
        compiler_params=pltpu.CompilerParams(
            dimension_semantics=("arbitrary", "arbitrary"), vmem_limit_bytes=VMEM_LIMIT),
        name="out_proj_ffn",
    )(x, mod_l, oa, ob, oc_f, oc_b, cg, c_gain, w_out, w_ff1, w_ff2, final_gain)


def _rope_tables(s):
    half = HEAD_DIM // 2
    inv = ROPE_THETA ** (-jnp.arange(0, half, 2, dtype=F32) / half)
    t = jnp.arange(s)
    ang_row = (t // GRID_W).astype(F32)[:, None] * inv[None, :]
    ang_col = (t % GRID_W).astype(F32)[:, None] * inv[None, :]
    ang_1d = t.astype(F32)[:, None] * inv[None, :]

    def pair(ang):
        return (jnp.concatenate([jnp.cos(ang), jnp.cos(ang)], axis=-1),
                jnp.concatenate([-jnp.sin(ang), jnp.sin(ang)], axis=-1))

    cr, sr = pair(ang_row)
    cc, sc = pair(ang_col)
    c1, s1 = pair(ang_1d)
    tile = lambda a: jnp.tile(a, (1, LANES // a.shape[1]))
    return (tile(jnp.concatenate([cr, cc], axis=-1)), tile(jnp.concatenate([sr, sc], axis=-1)),
            tile(c1), tile(s1))


def kernel(x, c, w_mod, b_mod, w_in, a_qk_norm, diff_lambda, diff_subln, hgrn_lower_bounds, hgrn_norm,
           w_out, w_ff1, w_ff2, final_norm):
    bsz, s, d = x.shape
    depth = w_in.shape[0]
    tm = min(ROW_TILE, s)
    tables = _rope_tables(s)
    mod = _modulation(c, w_mod, b_mod).reshape(depth, bsz, 6, d)
    w_in_b, w_out_b, w_ff1_b, w_ff2_b = (w.astype(BF16) for w in (w_in, w_out, w_ff1, w_ff2))
    final_gain = final_norm.reshape(1, d)
    for l in range(depth):
        qk_gain = jnp.tile(a_qk_norm[l], (1, LANES // HEAD_DIM))
        qat, ka, vat, qbt, kb, vbt, cq, cff, cfb, ci, cg = _in_projection(
            x, mod[l], w_in_b, l, tables, qk_gain, tm)
        oa = _attention_a(qat, ka, vat, min(A_Q_TILES * Q_TILE, s))
        lam_init = 0.8 - 0.6 * math.exp(-0.3 * l)
        subln = jnp.tile(diff_subln[l].reshape(1, HEAD_DIM), (1, B_HEADS))
        ob = _attention_b(qbt, kb, vbt, diff_lambda[l], subln, lam_init, min(B_Q_TILES * Q_TILE, s))
        hg_gain = jnp.tile(hgrn_norm[l].reshape(1, C_DIM), (1, C_HEADS))
        oc_f, oc_b = _hgrn(cq, cff, cfb, ci, hgrn_lower_bounds, l, min(HGRN_BATCH, bsz))
        x = _out_ffn(x, mod[l], oa, ob, oc_f, oc_b, cg, hg_gain, w_out_b, w_ff1_b, w_ff2_b, final_gain,
                     l, l == depth - 1, tm)
    return x
```

```python
import functools
import math

import numpy as np
import jax
import jax.numpy as jnp
from jax import lax
from jax.experimental import pallas as pl
from jax.experimental.pallas import tpu as pltpu

F32 = jnp.float32
BF16 = jnp.bfloat16

HEAD_DIM = 64
GRID_W = 64
ROPE_THETA = 10000.0
EPS = 1e-6
FORGET_FLOOR = 1e-6
A_HEADS, A_KV_HEADS = 6, 2
B_HEADS, B_QK_DIM = 6, 32
C_HEADS, C_DIM = 4, 64
A_WIDTH = A_HEADS * HEAD_DIM
A_KV_WIDTH = A_KV_HEADS * HEAD_DIM
B_WIDTH = B_HEADS * HEAD_DIM
C_WIDTH = C_HEADS * C_DIM
LANES = 128
HGRN_CHUNK = 64
HGRN_LEAF = 8
HGRN_BATCH = 16
ONES_ROWS = 16
VT_ROWS = HEAD_DIM + ONES_ROWS
KV_CHUNK = 256
Q_TILE = 256
A_Q_TILES = 4
B_Q_TILES = 4
QK_AHEAD = 12
LOG2_E = math.log2(math.e)
ROW_TILE = 512
COL_CHUNK = 1024
VMEM_LIMIT = 56 * 1024 * 1024

C_DTYPES = (F32, F32, F32, BF16, F32)
OFF_AQ = 0
OFF_AK = OFF_AQ + A_WIDTH
OFF_AV = OFF_AK + A_KV_WIDTH
OFF_BQ = OFF_AV + A_KV_WIDTH
OFF_BK = OFF_BQ + B_WIDTH
OFF_BV = OFF_BK + B_WIDTH
OFF_C = OFF_BV + B_WIDTH
IN_TOTAL = OFF_C + len(C_DTYPES) * C_WIDTH


def _dot(a, b):
    return jnp.dot(a, b, preferred_element_type=F32)


def _dot_tn(a, b):
    return lax.dot_general(a, b, (((0,), (0,)), ((), ())), preferred_element_type=F32)


def _split_dot(x, w_bf16):
    hi = x.astype(BF16)
    lo = (x - hi.astype(F32)).astype(BF16)
    return _dot(hi, w_bf16) + _dot(lo, w_bf16)


def _group_ones(n, group):
    r = lax.broadcasted_iota(jnp.int32, (n, n), 0) // group
    c = lax.broadcasted_iota(jnp.int32, (n, n), 1) // group
    return jnp.where(r == c, 1.0 / group, 0.0).astype(BF16)


def _rope(xb, cos, sin_signed):
    lane = lax.broadcasted_iota(jnp.int32, xb.shape, 1)
    low = (lane % 32) < 16
    partner = jnp.where(low, pltpu.roll(xb, LANES - 16, 1), pltpu.roll(xb, 16, 1))
    return xb * cos + partner * sin_signed


def _rms_rows(x):
    return x * lax.rsqrt(jnp.mean(x * x, axis=-1, keepdims=True) + EPS)


def _mod_kernel(c_ref, w_ref, b_ref, o_ref):
    c = c_ref[...]
    cond = (c * jax.nn.sigmoid(c)).astype(BF16)
    o_ref[0] = _dot(cond, w_ref[0].astype(BF16)) + b_ref[0]


def _modulation(c, w_mod, b_mod):
    depth, d, n = w_mod.shape
    bsz = c.shape[0]
    tn = COL_CHUNK
    return pl.pallas_call(
        _mod_kernel,
        grid=(depth, n // tn),
        in_specs=[
            pl.BlockSpec((bsz, d), lambda l, j: (0, 0)),
            pl.BlockSpec((1, d, tn), lambda l, j: (l, 0, j)),
            pl.BlockSpec((1, 1, tn), lambda l, j: (l, 0, j)),
        ],
        out_specs=pl.BlockSpec((1, bsz, tn), lambda l, j: (l, 0, j)),
        out_shape=jax.ShapeDtypeStruct((depth, bsz, n), F32),
        compiler_params=pltpu.CompilerParams(
            dimension_semantics=("arbitrary", "arbitrary"), vmem_limit_bytes=VMEM_LIMIT),
        name="adaln_mod",
    )(c, w_mod, b_mod.reshape(depth, 1, n))


def _inproj_kernel(x_ref, mod_ref, w_ref, cosa_ref, sina_ref, cosb_ref, sinb_ref, gain_ref,
                   qat_ref, ka_ref, vat_ref, qbt_ref, kb_ref, vbt_ref,
                   cq_ref, cff_ref, cfb_ref, ci_ref, cg_ref, h_scr, z_scr):
    x = x_ref[0]
    h = _rms_rows(x) * (1.0 + mod_ref[0, 1:2, :]) + mod_ref[0, 0:1, :]
    h_scr[...] = h.astype(BF16)

    def project(j):
        cols = slice(j * COL_CHUNK, (j + 1) * COL_CHUNK)
        z_scr[:, cols] = _dot(h_scr[...], w_ref[0, :, cols])

    project(0)
    project(1)

    ones = _group_ones(LANES, HEAD_DIM)
    cosa, sina = cosa_ref[...], sina_ref[...]
    cosb, sinb = cosb_ref[...], sinb_ref[...]

    def a_norm_rope(zb, gain):
        ms = _split_dot(zb * zb, ones)
        return _rope(zb * lax.rsqrt(ms + EPS) * gain, cosa, sina)

    def store_vt(vt_ref, first_head, blk_t):
        ones_rows = jnp.ones((ONES_ROWS, blk_t.shape[1]), BF16)
        for j in range(LANES // HEAD_DIM):
            r0 = (first_head + j) * VT_ROWS
            vt_ref[0, r0:r0 + HEAD_DIM, :] = blk_t[j * HEAD_DIM:(j + 1) * HEAD_DIM].astype(BF16)
            vt_ref[0, r0 + HEAD_DIM:r0 + VT_ROWS, :] = ones_rows

    for cblk in range(A_WIDTH // LANES):
        zb = z_scr[:, OFF_AQ + cblk * LANES:OFF_AQ + (cblk + 1) * LANES]
        y = a_norm_rope(zb, gain_ref[0:1, :]) * (HEAD_DIM ** -0.5 * LOG2_E)
        qat_ref[0, cblk * LANES:(cblk + 1) * LANES, :] = y.T.astype(BF16)
    ka_ref[0, 0] = a_norm_rope(z_scr[:, OFF_AK:OFF_AK + LANES], gain_ref[1:2, :]).astype(BF16)
    store_vt(vat_ref, 0, z_scr[:, OFF_AV:OFF_AV + LANES].T)
    project(2)

    for cblk in range(B_WIDTH // LANES):
        lo = cblk * LANES
        zq = z_scr[:, OFF_BQ + lo:OFF_BQ + lo + LANES]
        qbt_ref[0, lo:lo + LANES, :] = (_rope(zq, cosb, sinb) * (B_QK_DIM ** -0.5 * LOG2_E)).T.astype(BF16)
    for cblk in range(B_WIDTH // LANES):
        lo = cblk * LANES
        zk = z_scr[:, OFF_BK + lo:OFF_BK + lo + LANES]
        kb_ref[0, cblk] = _rope(zk, cosb, sinb).astype(BF16)
        store_vt(vbt_ref, cblk * (LANES // HEAD_DIM), z_scr[:, OFF_BV + lo:OFF_BV + lo + LANES].T)
    for j, ref in enumerate((cq_ref, cff_ref, cfb_ref, ci_ref, cg_ref)):
        ref[0] = z_scr[:, OFF_C + j * C_WIDTH:OFF_C + (j + 1) * C_WIDTH].astype(ref.dtype)


def _in_projection(x, mod_l, w_in_bf16, layer, tables, qk_gain, tm):
    bsz, s, d = x.shape
    cosa, sina, cosb, sinb = tables
    row_spec = lambda width: pl.BlockSpec((1, tm, width), lambda b, i: (b, i, 0))
    tab_spec = pl.BlockSpec((tm, LANES), lambda b, i: (i, 0))
    col_spec = lambda height: pl.BlockSpec((1, height, tm), lambda b, i: (b, 0, i))
    key_spec = lambda blocks: pl.BlockSpec((1, blocks, tm, LANES), lambda b, i: (b, 0, i, 0))
    return pl.pallas_call(
        _inproj_kernel,
        grid=(bsz, s // tm),
        in_specs=[
            row_spec(d),
            pl.BlockSpec((1, 6, d), lambda b, i: (b, 0, 0)),
            pl.BlockSpec((1, d, IN_TOTAL), lambda b, i: (layer, 0, 0)),
            tab_spec, tab_spec, tab_spec, tab_spec,
            pl.BlockSpec((2, LANES), lambda b, i: (0, 0)),
        ],
        out_specs=[
            col_spec(A_WIDTH), key_spec(A_KV_WIDTH // LANES), col_spec(A_KV_HEADS * VT_ROWS),
            col_spec(B_WIDTH), key_spec(B_WIDTH // LANES), col_spec(B_HEADS * VT_ROWS),
        ] + [row_spec(C_WIDTH)] * len(C_DTYPES),
        out_shape=[
            jax.ShapeDtypeStruct((bsz, A_WIDTH, s), BF16),
            jax.ShapeDtypeStruct((bsz, A_KV_WIDTH // LANES, s, LANES), BF16),
            jax.ShapeDtypeStruct((bsz, A_KV_HEADS * VT_ROWS, s), BF16),
            jax.ShapeDtypeStruct((bsz, B_WIDTH, s), BF16),
            jax.ShapeDtypeStruct((bsz, B_WIDTH // LANES, s, LANES), BF16),
            jax.ShapeDtypeStruct((bsz, B_HEADS * VT_ROWS, s), BF16),
        ] + [jax.ShapeDtypeStruct((bsz, s, C_WIDTH), dt) for dt in C_DTYPES],
        scratch_shapes=[pltpu.VMEM((tm, d), BF16), pltpu.VMEM((tm, IN_TOTAL), F32)],
        compiler_params=pltpu.CompilerParams(
            dimension_semantics=("arbitrary", "arbitrary"), vmem_limit_bytes=VMEM_LIMIT),
        name="in_projection",
    )(x, mod_l, w_in_bf16, cosa, sina, cosb, sinb, qk_gain)


def _place_rows(block, start, total):
    parts = []
    if start:
        parts.append(jnp.zeros((start, block.shape[1]), block.dtype))
    parts.append(block)
    rest = total - start - block.shape[0]
    if rest:
        parts.append(jnp.zeros((rest, block.shape[1]), block.dtype))
    return jnp.concatenate(parts, axis=0) if len(parts) > 1 else block


def _attention_items(items, k_ref, vt_ref, qz_scr, m_scr, acc_scr):
    s = k_ref.shape[2]
    tq = acc_scr.shape[2]
    m_scr[...] = jnp.full(m_scr.shape, -1e30, F32)
    acc_scr[...] = jnp.zeros_like(acc_scr)

    units = [(i, c) for c in range(s // KV_CHUNK) for i in range(len(items))]

    def scores(u):
        i, c = units[u]
        return _dot(k_ref[0, items[i][0], c * KV_CHUNK:(c + 1) * KV_CHUNK, :], qz_scr[i])

    pending = [scores(u) for u in range(min(QK_AHEAD, len(units)))]
    for u, (i, c) in enumerate(units):
        v_off = items[i][1]
        st = pending.pop(0)
        m_old = m_scr[i]
        part = jnp.max(st.reshape(KV_CHUNK // 8, 8, tq), axis=0)
        m_new = jnp.maximum(m_old, jnp.max(part, axis=0, keepdims=True))
        m_scr[i] = m_new
        p = jnp.exp2(st - m_new[0:1]).astype(BF16)
        d = _dot(vt_ref[0, v_off:v_off + VT_ROWS, c * KV_CHUNK:(c + 1) * KV_CHUNK], p)
        if u + QK_AHEAD < len(units):
            pending.append(scores(u + QK_AHEAD))
        acc_scr[i] = acc_scr[i] * jnp.exp2(m_old[0:1] - m_new[0:1]) + d
    results = []
    for i in range(len(items)):
        acc = acc_scr[i]
        results.append(acc[0:HEAD_DIM] * (1.0 / acc[HEAD_DIM:HEAD_DIM + 1]))
    return results


def _attention_scratch(heads, tq):
    n_items = heads * (tq // Q_TILE)
    return [pltpu.VMEM((n_items, LANES, Q_TILE), BF16), pltpu.VMEM((n_items, 8, Q_TILE), F32),
            pltpu.VMEM((n_items, VT_ROWS, Q_TILE), F32)]


def _attn_a_kernel(qt_ref, k_ref, vt_ref, o_ref, qz_scr, m_scr, acc_scr):
    group = A_HEADS // A_KV_HEADS
    n_tiles = qt_ref.shape[2] // Q_TILE
    items = []
    for t in range(n_tiles):
        cols = slice(t * Q_TILE, (t + 1) * Q_TILE)
        for h in range(A_HEADS):
            g = h // group
            qz_scr[len(items)] = _place_rows(qt_ref[0, h * HEAD_DIM:(h + 1) * HEAD_DIM, cols], g * HEAD_DIM, LANES)
            items.append((0, g * VT_ROWS))
    outs = _attention_items(items, k_ref, vt_ref, qz_scr, m_scr, acc_scr)
    for t in range(n_tiles):
        tile = outs[t * A_HEADS:(t + 1) * A_HEADS]
        o_ref[0, t * Q_TILE:(t + 1) * Q_TILE, :] = jnp.concatenate([o.T for o in tile], axis=-1).astype(BF16)


def _attention_a(qat, ka, vat, tq):
    bsz, _, s, _ = ka.shape
    return pl.pallas_call(
        _attn_a_kernel,
        grid=(bsz, s // tq),
        in_specs=[
            pl.BlockSpec((1, A_WIDTH, tq), lambda b, i: (b, 0, i)),
            pl.BlockSpec((1, A_KV_WIDTH // LANES, s, LANES), lambda b, i: (b, 0, 0, 0)),
            pl.BlockSpec((1, A_KV_HEADS * VT_ROWS, s), lambda b, i: (b, 0, 0)),
        ],
        out_specs=pl.BlockSpec((1, tq, A_WIDTH), lambda b, i: (b, i, 0)),
        out_shape=jax.ShapeDtypeStruct((bsz, s, A_WIDTH), BF16),
        scratch_shapes=_attention_scratch(A_HEADS, tq),
        compiler_params=pltpu.CompilerParams(
            dimension_semantics=("arbitrary", "arbitrary"), vmem_limit_bytes=VMEM_LIMIT),
        name="attn_axial_gqa",
    )(qat, ka, vat)


def _attn_b_kernel(lam_ref, gain_ref, qt_ref, k_ref, vt_ref, o_ref, qz_scr, m_scr, acc_scr, *, lam_init):
    lp = lam_ref[...]
    lam = (jnp.exp(jnp.sum(lp[0:1] * lp[1:2], axis=-1, keepdims=True))
           - jnp.exp(jnp.sum(lp[2:3] * lp[3:4], axis=-1, keepdims=True)) + lam_init)
    n_tiles = qt_ref.shape[2] // Q_TILE
    items = []
    for t in range(n_tiles):
        cols = slice(t * Q_TILE, (t + 1) * Q_TILE)
        for h in range(B_HEADS):
            pair = (h // 2) * LANES
            for comp in range(2):
                r0 = h * HEAD_DIM + comp * B_QK_DIM
                qz_scr[len(items)] = _place_rows(qt_ref[0, r0:r0 + B_QK_DIM, cols], r0 - pair, LANES)
                items.append((h // 2, h * VT_ROWS))
    comps = _attention_items(items, k_ref, vt_ref, qz_scr, m_scr, acc_scr)
    for t in range(n_tiles):
        outs = []
        for h in range(B_HEADS):
            i = (t * B_HEADS + h) * 2
            o = comps[i] - lam * comps[i + 1]
            o = o * lax.rsqrt(jnp.mean(o * o, axis=0, keepdims=True) + EPS)
            outs.append(o.T)
        y = jnp.concatenate(outs, axis=-1) * gain_ref[...] * (1.0 - lam_init)
        o_ref[0, t * Q_TILE:(t + 1) * Q_TILE, :] = y.astype(BF16)


def _attention_b(qbt, kb, vbt, lam_params, subln_gain, lam_init, tq):
    bsz, _, s, _ = kb.shape
    return pl.pallas_call(
        functools.partial(_attn_b_kernel, lam_init=lam_init),
        grid=(bsz, s // tq),
        in_specs=[
            pl.BlockSpec((4, B_QK_DIM), lambda b, i: (0, 0)),
            pl.BlockSpec((1, B_WIDTH), lambda b, i: (0, 0)),
            pl.BlockSpec((1, B_WIDTH, tq), lambda b, i: (b, 0, i)),
            pl.BlockSpec((1, B_WIDTH // LANES, s, LANES), lambda b, i: (b, 0, 0, 0)),
            pl.BlockSpec((1, B_HEADS * VT_ROWS, s), lambda b, i: (b, 0, 0)),
        ],
        out_specs=pl.BlockSpec((1, tq, B_WIDTH), lambda b, i: (b, i, 0)),
        out_shape=jax.ShapeDtypeStruct((bsz, s, B_WIDTH), BF16),
        scratch_shapes=_attention_scratch(2 * B_HEADS, tq),
        compiler_params=pltpu.CompilerParams(
            dimension_semantics=("arbitrary", "arbitrary"), vmem_limit_bytes=VMEM_LIMIT),
        name="attn_differential",
    )(lam_params, subln_gain, qbt, kb, vbt)


def _hgrn_levels():
    sizes = []
    bs = HGRN_CHUNK
    while bs > HGRN_LEAF:
        sizes.append(bs)
        bs //= 2
    return sizes


def _hgrn_constants():
    c, w = HGRN_CHUNK, C_WIDTH
    t = np.arange(c)
    tri = np.stack([(t[:, None] >= t[None, :]), (t[:, None] <= t[None, :])]).astype(np.float32)
    s_of_lane = np.arange(w) % C_DIM
    sizes = _hgrn_levels()
    level = np.full((2, c, w), float(len(sizes) + 1), np.float32)
    for d in range(2):
        for li, bs in enumerate(sizes):
            upper = (t % bs) >= bs // 2
            qrow = upper if d == 0 else ~upper
            same = (t[:, None] // bs) == (s_of_lane[None, :] // bs)
            cross = qrow[:, None] & ~qrow[s_of_lane][None, :]
            level[d][same & cross] = li
        same_leaf = (t[:, None] // HGRN_LEAF) == (s_of_lane[None, :] // HGRN_LEAF)
        order = (s_of_lane[None, :] <= t[:, None]) if d == 0 else (s_of_lane[None, :] >= t[:, None])
        level[d][same_leaf & order] = len(sizes)
    half = (np.arange(LANES)[None, :] < C_DIM).astype(np.float32) * np.ones((c, 1), np.float32)
    return tri, level, half


def _block_diag(x, low_half):
    zero = jnp.zeros((x.shape[0], LANES), x.dtype)
    blocks = []
    for h in range(C_HEADS):
        tile = x[:, (h // 2) * LANES:(h // 2 + 1) * LANES]
        keep = tile * (low_half if h % 2 == 0 else 1 - low_half)
        blocks.append(jnp.concatenate([keep, zero] if h // 2 == 0 else [zero, keep], axis=1))
    return jnp.concatenate(blocks, axis=0)


def _block_diag_t(x, low_half):
    zero = jnp.zeros((LANES, LANES), x.dtype)
    rows = []
    for t in range(C_WIDTH // LANES):
        tile = x[:, t * LANES:(t + 1) * LANES]
        pair = jnp.concatenate([tile * low_half, tile * (1 - low_half)], axis=0).T
        rows.append(jnp.concatenate([pair, zero] if t == 0 else [zero, pair], axis=1))
    return jnp.concatenate(rows, axis=0)


def _ref_rows(b, bs, offset):
    parts = []
    for lo in range(0, b.shape[0], bs):
        parts.append(jnp.broadcast_to(b[lo + offset:lo + offset + 1, :], (bs, b.shape[1])))
    return jnp.concatenate(parts, axis=0)


def _level_operands(q, k, b, bs, d):
    half = bs // 2
    ref_off = half - 1 if d == 0 else half
    zeros = jnp.zeros((8, b.shape[1]), F32)
    qparts, kparts = [], []
    for r0 in range(0, b.shape[0], 8):
        lo = (r0 // bs) * bs
        ref = b[lo + ref_off:lo + ref_off + 1, :]
        upper = (r0 % bs) >= half
        rows = slice(r0, r0 + 8)
        if upper == (d == 0):
            qparts.append(q[rows] * jnp.exp2(b[rows] - ref))
            kparts.append(zeros)
        else:
            qparts.append(zeros)
            kparts.append(k[rows] * jnp.exp2(ref - b[rows]))
    return jnp.concatenate(qparts, axis=0).astype(BF16), jnp.concatenate(kparts, axis=0).astype(BF16)


def _hgrn_kernel(lb_ref, tri_ref, level_ref, half_ref, zq_f, zf_f, zv_f, zq_b, zf_b, zv_b,
                 of_ref, ob_ref, st_scr, *, layer, depth, nb):
    c, w = HGRN_CHUNK, C_WIDTH
    sizes = _hgrn_levels()
    n_lvl = len(sizes)

    @pl.when(pl.program_id(1) == 0)
    def _():
        st_scr[...] = jnp.zeros_like(st_scr)

    rows = [lb_ref[i:i + 1, :] for i in range(depth)]
    mx = functools.reduce(jnp.maximum, rows)
    es = [jnp.exp(r - mx) for r in rows]
    tot = functools.reduce(lambda a, b_: a + b_, es)
    ps = [e / tot for e in es]
    cum = functools.reduce(lambda a, b_: a + b_, ps[:layer + 1])
    one_minus_lb = 1.0 - jnp.clip(cum - ps[0], 0.0, 1.0)

    low_half = half_ref[...].astype(BF16)
    low_bool = half_ref[...] > 0.5
    owner = [[level_ref[d] == float(li) for li in range(n_lvl + 1)] for d in range(2)]
    chains = [(j, d) for j in range(nb) for d in range(2)]
    zrefs = {0: (zq_f, zf_f, zv_f), 1: (zq_b, zf_b, zv_b)}

    q, k, v, b, total = {}, {}, {}, {}, {}
    g_split = {}
    for ch in chains:
        j, d = ch
        zq, zf = zrefs[d][0][j], zrefs[d][1][j]
        v[ch] = zrefs[d][2][j]
        q[ch] = (zq * (C_DIM ** -0.5)) / (1.0 + jnp.exp2(zq * -LOG2_E))
        k[ch] = one_minus_lb * (1.0 - 1.0 / (1.0 + jnp.exp2(zf * -LOG2_E)))
        g = jnp.log2(jnp.maximum(1.0 - k[ch], FORGET_FLOOR))
        g_hi = g.astype(BF16)
        g_r = g - g_hi.astype(F32)
        g_mid = g_r.astype(BF16)
        g_lo = (g_r - g_mid.astype(F32)).astype(BF16)
        g_split[ch] = jnp.concatenate([g_hi, g_mid, g_lo], axis=1)
    for ch in chains:
        d = ch[1]
        bb = _dot(tri_ref[d], g_split[ch])
        b[ch] = bb[:, 0:w] + bb[:, w:2 * w] + bb[:, 2 * w:3 * w]
        total[ch] = b[ch][c - 1:c, :] if d == 0 else b[ch][0:1, :]

    parts = {ch: [] for ch in chains}
    for li in range(n_lvl + 1):
        for ch in chains:
            d = ch[1]
            if li < n_lvl:
                qp, kp = _level_operands(q[ch], k[ch], b[ch], sizes[li], d)
            else:
                ref = _ref_rows(b[ch], HGRN_LEAF, HGRN_LEAF // 2 - 1 if d == 0 else HGRN_LEAF // 2)
                arg = b[ch] - ref
                qp = (q[ch] * jnp.exp2(arg)).astype(BF16)
                kp = (k[ch] * jnp.exp2(-arg)).astype(BF16)
            parts[ch].append(_dot(qp, _block_diag_t(kp, low_half)))
    a = {}
    for ch in chains:
        acc = parts[ch][0]
        for li in range(1, n_lvl + 1):
            acc = jnp.where(owner[ch[1]][li], parts[ch][li], acc)
        a[ch] = acc.astype(BF16)

    o_intra, o_inter, upd = {}, {}, {}
    for ch in chains:
        j, d = ch
        o_intra[ch] = _dot(a[ch], _block_diag(v[ch], low_half))
        qe = (q[ch] * jnp.exp2(b[ch])).astype(BF16)
        o_inter[ch] = _dot(qe, _block_diag_t(st_scr[d, j].astype(BF16), low_half))
        kb = (k[ch] * jnp.exp2(total[ch] - b[ch])).astype(BF16)
        tiles = []
        for t in range(w // LANES):
            lanes = slice(t * LANES, (t + 1) * LANES)
            full = _dot_tn(v[ch][:, lanes], kb[:, lanes])
            tiles.append(jnp.where(low_bool, full[0:C_DIM], full[C_DIM:2 * C_DIM]))
        upd[ch] = jnp.concatenate(tiles, axis=1)
    for ch in chains:
        j, d = ch
        st_scr[d, j] = st_scr[d, j] * jnp.exp2(total[ch]) + upd[ch]
        (of_ref if d == 0 else ob_ref)[j] = o_intra[ch] + o_inter[ch]


def _hgrn(cq, cff, cfb, ci, lower_bounds, layer, nb):
    bsz, s, _ = cq.shape
    depth = lower_bounds.shape[0]
    n_chunks = s // HGRN_CHUNK
    c, w = HGRN_CHUNK, C_WIDTH
    tri, level, half = _hgrn_constants()
    consts = (jnp.asarray(tri, BF16), jnp.asarray(level), jnp.asarray(half))
    full = lambda a: pl.BlockSpec(a.shape, lambda g, i: (0,) * a.ndim)
    last = n_chunks - 1
    fwd = pl.BlockSpec((nb, c, w), lambda g, i: (g, i, 0))
    bwd = pl.BlockSpec((nb, c, w), lambda g, i: (g, last - i, 0))
    return pl.pallas_call(
        functools.partial(_hgrn_kernel, layer=layer, depth=depth, nb=nb),
        grid=(bsz // nb, n_chunks),
        in_specs=[full(lower_bounds)] + [full(a) for a in consts] + [fwd, fwd, fwd, bwd, bwd, bwd],
        out_specs=[fwd, bwd],
        out_shape=[jax.ShapeDtypeStruct((bsz, s, w), F32)] * 2,
        scratch_shapes=[pltpu.VMEM((2, nb, C_DIM, w), F32)],
        compiler_params=pltpu.CompilerParams(
            dimension_semantics=("arbitrary", "arbitrary"), vmem_limit_bytes=VMEM_LIMIT),
        name="hgrn2_bidir",
    )(lower_bounds, *consts, cq, cff, ci, cq, cfb, ci)


def _out_ffn_kernel(x_ref, mod_ref, oa_ref, ob_ref, cf_ref, cb_ref, zg_ref, cgain_ref,
                    wo_ref, w1_ref, w2_ref, fg_ref, o_ref, *, final, ff_chunk):
    oc = cf_ref[0] + cb_ref[0]
    ms = _split_dot(oc * oc, _group_ones(C_WIDTH, C_DIM))
    zg = zg_ref[0]
    oc = oc * lax.rsqrt(ms + EPS) * cgain_ref[...] * (zg * jax.nn.sigmoid(zg))
    mix = jnp.concatenate([oa_ref[0], ob_ref[0], oc.astype(BF16)], axis=-1)
    x1 = x_ref[0] + mod_ref[0, 2:3, :] * _dot(mix, wo_ref[0])
    h = (_rms_rows(x1) * (1.0 + mod_ref[0, 4:5, :]) + mod_ref[0, 3:4, :]).astype(BF16)
    d_ff = w1_ref.shape[2]
    acc = jnp.zeros_like(x1)
    for j in range(d_ff // ff_chunk):
        u = _dot(h, w1_ref[0, :, j * ff_chunk:(j + 1) * ff_chunk])
        u = jnp.square(jnp.maximum(u, 0.0)).astype(BF16)
        acc = acc + _dot(u, w2_ref[0, j * ff_chunk:(j + 1) * ff_chunk, :])
    x2 = x1 + mod_ref[0, 5:6, :] * acc
    if final:
        x2 = _rms_rows(x2) * fg_ref[...]
    o_ref[0] = x2


def _out_ffn(x, mod_l, oa, ob, oc_f, oc_b, cg, c_gain, w_out, w_ff1, w_ff2, final_gain, layer, final, tm):
    bsz, s, d = x.shape
    d_ff = w_ff1.shape[2]
    row_spec = lambda width: pl.BlockSpec((1, tm, width), lambda b, i: (b, i, 0))
    weight = lambda a: pl.BlockSpec((1,) + a.shape[1:], lambda b, i: (layer, 0, 0),
                                    pipeline_mode=pl.Buffered(1))
    return pl.pallas_call(
        functools.partial(_out_ffn_kernel, final=final, ff_chunk=COL_CHUNK),
        grid=(bsz, s // tm),
        in_specs=[
            row_spec(d),
            pl.BlockSpec((1, 6, d), lambda b, i: (b, 0, 0)),
            row_spec(A_WIDTH), row_spec(B_WIDTH), row_spec(C_WIDTH), row_spec(C_WIDTH), row_spec(C_WIDTH),
            pl.BlockSpec((1, C_WIDTH), lambda b, i: (0, 0)),
            weight(w_out), weight(w_ff1), weight(w_ff2),
            pl.BlockSpec((1, d), lambda b, i: (0, 0)),
        ],
        out_specs=row_spec(d),
        out_shape=jax.ShapeDtypeStruct((bsz, s, d), F32),
        compiler_params=pltpu.CompilerParams(
            dimension_semantics=("arbitrary", "arbitrary"), vmem_limit_bytes=VMEM_LIMIT),
        name="out_proj_ffn",
    )(x, mod_l, oa, ob, oc_f, oc_b, cg, c_gain, w_out, w_ff1, w_ff2, final_gain)


def _rope_tables(s):
    half = HEAD_DIM // 2
    inv = ROPE_THETA ** (-jnp.arange(0, half, 2, dtype=F32) / half)
    t = jnp.arange(s)
    ang_row = (t // GRID_W).astype(F32)[:, None] * inv[None, :]
    ang_col = (t % GRID_W).astype(F32)[:, None] * inv[None, :]
    ang_1d = t.astype(F32)[:, None] * inv[None, :]

    def pair(ang):
        return (jnp.concatenate([jnp.cos(ang), jnp.cos(ang)], axis=-1),
                jnp.concatenate([-jnp.sin(ang), jnp.sin(ang)], axis=-1))

    cr, sr = pair(ang_row)
    cc, sc = pair(ang_col)
    c1, s1 = pair(ang_1d)
    tile = lambda a: jnp.tile(a, (1, LANES // a.shape[1]))
    return (tile(jnp.concatenate([cr, cc], axis=-1)), tile(jnp.concatenate([sr, sc], axis=-1)),
            tile(c1), tile(s1))


def kernel(x, c, w_mod, b_mod, w_in, a_qk_norm, diff_lambda, diff_subln, hgrn_lower_bounds, hgrn_norm,
           w_out, w_ff1, w_ff2, final_norm):
    bsz, s, d = x.shape
    depth = w_in.shape[0]
    tm = min(ROW_TILE, s)
    tables = _rope_tables(s)
    mod = _modulation(c, w_mod, b_mod).reshape(depth, bsz, 6, d)
    w_in_b, w_out_b, w_ff1_b, w_ff2_b = (w.astype(BF16) for w in (w_in, w_out, w_ff1, w_ff2))
    final_gain = final_norm.reshape(1, d)
    for l in range(depth):
        qk_gain = jnp.tile(a_qk_norm[l], (1, LANES // HEAD_DIM))
        qat, ka, vat, qbt, kb, vbt, cq, cff, cfb, ci, cg = _in_projection(
            x, mod[l], w_in_b, l, tables, qk_gain, tm)
        oa = _attention_a(qat, ka, vat, min(A_Q_TILES * Q_TILE, s))
        lam_init = 0.8 - 0.6 * math.exp(-0.3 * l)
        subln = jnp.tile(diff_subln[l].reshape(1, HEAD_DIM), (1, B_HEADS))
        ob = _attention_b(qbt, kb, vbt, diff_lambda[l], subln, lam_init, min(B_Q_TILES * Q_TILE, s))
        hg_gain = jnp.tile(hgrn_norm[l].reshape(1, C_DIM), (1, C_HEADS))
        oc_f, oc_b = _hgrn(cq, cff, cfb, ci, hgrn_lower_bounds, l, min(HGRN_BATCH, bsz))
        x = _out_ffn(x, mod[l], oa, ob, oc_f, oc_b, cg, hg_gain, w_out_b, w_ff1_b, w_ff2_b, final_gain,
                     l, l == depth - 1, tm)
    return x
```

```python
import functools
import math

import numpy as np
import jax
import jax.numpy as jnp
from jax import lax
from jax.experimental import pallas as pl
from jax.experimental.pallas import tpu as pltpu

F32 = jnp.float32
BF16 = jnp.bfloat16

HEAD_DIM = 64
GRID_W = 64
ROPE_THETA = 10000.0
EPS = 1e-6
FORGET_FLOOR = 1e-6
A_HEADS, A_KV_HEADS = 6, 2
B_HEADS, B_QK_DIM = 6, 32
C_HEADS, C_DIM = 4, 64
A_WIDTH = A_HEADS * HEAD_DIM
A_KV_WIDTH = A_KV_HEADS * HEAD_DIM
B_WIDTH = B_HEADS * HEAD_DIM
C_WIDTH = C_HEADS * C_DIM
LANES = 128
HGRN_CHUNK = 64
HGRN_LEAF = 8
HGRN_BATCH = 16
ONES_ROWS = 16
VT_ROWS = HEAD_DIM + ONES_ROWS
KV_CHUNK = 256
Q_TILE = 256
A_Q_TILES = 4
B_Q_TILES = 4
QK_AHEAD = 16
LOG2_E = math.log2(math.e)
ROW_TILE = 512
COL_CHUNK = 1024
VMEM_LIMIT = 56 * 1024 * 1024

C_DTYPES = (F32, F32, F32, BF16, F32)
OFF_AQ = 0
OFF_AK = OFF_AQ + A_WIDTH
OFF_AV = OFF_AK + A_KV_WIDTH
OFF_BQ = OFF_AV + A_KV_WIDTH
OFF_BK = OFF_BQ + B_WIDTH
OFF_BV = OFF_BK + B_WIDTH
OFF_C = OFF_BV + B_WIDTH
IN_TOTAL = OFF_C + len(C_DTYPES) * C_WIDTH


def _dot(a, b):
    return jnp.dot(a, b, preferred_element_type=F32)


def _dot_tn(a, b):
    return lax.dot_general(a, b, (((0,), (0,)), ((), ())), preferred_element_type=F32)


def _split_dot(x, w_bf16):
    hi = x.astype(BF16)
    lo = (x - hi.astype(F32)).astype(BF16)
    return _dot(hi, w_bf16) + _dot(lo, w_bf16)


def _group_ones(n, group):
    r = lax.broadcasted_iota(jnp.int32, (n, n), 0) // group
    c = lax.broadcasted_iota(jnp.int32, (n, n), 1) // group
    return jnp.where(r == c, 1.0 / group, 0.0).astype(BF16)


def _rope(xb, cos, sin_signed):
    lane = lax.broadcasted_iota(jnp.int32, xb.shape, 1)
    low = (lane % 32) < 16
    partner = jnp.where(low, pltpu.roll(xb, LANES - 16, 1), pltpu.roll(xb, 16, 1))
    return xb * cos + partner * sin_signed


def _rms_rows(x):
    return x * lax.rsqrt(jnp.mean(x * x, axis=-1, keepdims=True) + EPS)


def _mod_kernel(c_ref, w_ref, b_ref, o_ref):
    c = c_ref[...]
    cond = (c * jax.nn.sigmoid(c)).astype(BF16)
    o_ref[0] = _dot(cond, w_ref[0].astype(BF16)) + b_ref[0]


def _modulation(c, w_mod, b_mod):
    depth, d, n = w_mod.shape
    bsz = c.shape[0]
    tn = COL_CHUNK
    return pl.pallas_call(
        _mod_kernel,
        grid=(depth, n // tn),
        in_specs=[
            pl.BlockSpec((bsz, d), lambda l, j: (0, 0)),
            pl.BlockSpec((1, d, tn), lambda l, j: (l, 0, j)),
            pl.BlockSpec((1, 1, tn), lambda l, j: (l, 0, j)),
        ],
        out_specs=pl.BlockSpec((1, bsz, tn), lambda l, j: (l, 0, j)),
        out_shape=jax.ShapeDtypeStruct((depth, bsz, n), F32),
        compiler_params=pltpu.CompilerParams(
            dimension_semantics=("parallel", "parallel"), vmem_limit_bytes=VMEM_LIMIT),
        name="adaln_mod",
    )(c, w_mod, b_mod.reshape(depth, 1, n))


def _inproj_kernel(x_ref, mod_ref, w_ref, cosa_ref, sina_ref, cosb_ref, sinb_ref, gain_ref,
                   qat_ref, ka_ref, vat_ref, qbt_ref, kb_ref, vbt_ref,
                   cq_ref, cff_ref, cfb_ref, ci_ref, cg_ref, h_scr, z_scr):
    x = x_ref[0]
    h = _rms_rows(x) * (1.0 + mod_ref[0, 1:2, :]) + mod_ref[0, 0:1, :]
    h_scr[...] = h.astype(BF16)

    def project(j):
        cols = slice(j * COL_CHUNK, (j + 1) * COL_CHUNK)
        z_scr[:, cols] = _dot(h_scr[...], w_ref[0, :, cols])

    project(0)
    project(1)

    ones = _group_ones(LANES, HEAD_DIM)
    cosa, sina = cosa_ref[...], sina_ref[...]
    cosb, sinb = cosb_ref[...], sinb_ref[...]

    def a_norm_rope(zb, gain):
        ms = _split_dot(zb * zb, ones)
        return _rope(zb * lax.rsqrt(ms + EPS) * gain, cosa, sina)

    def store_vt(vt_ref, first_head, blk_t):
        ones_rows = jnp.ones((ONES_ROWS, blk_t.shape[1]), BF16)
        for j in range(LANES // HEAD_DIM):
            r0 = (first_head + j) * VT_ROWS
            vt_ref[0, r0:r0 + HEAD_DIM, :] = blk_t[j * HEAD_DIM:(j + 1) * HEAD_DIM].astype(BF16)
            vt_ref[0, r0 + HEAD_DIM:r0 + VT_ROWS, :] = ones_rows

    for cblk in range(A_WIDTH // LANES):
        zb = z_scr[:, OFF_AQ + cblk * LANES:OFF_AQ + (cblk + 1) * LANES]
        y = a_norm_rope(zb, gain_ref[0:1, :]) * (HEAD_DIM ** -0.5 * LOG2_E)
        qat_ref[0, cblk * LANES:(cblk + 1) * LANES, :] = y.T.astype(BF16)
    ka_ref[0, 0] = a_norm_rope(z_scr[:, OFF_AK:OFF_AK + LANES], gain_ref[1:2, :]).astype(BF16)
    store_vt(vat_ref, 0, z_scr[:, OFF_AV:OFF_AV + LANES].T)
    project(2)

    for cblk in range(B_WIDTH // LANES):
        lo = cblk * LANES
        zq = z_scr[:, OFF_BQ + lo:OFF_BQ + lo + LANES]
        qbt_ref[0, lo:lo + LANES, :] = (_rope(zq, cosb, sinb) * (B_QK_DIM ** -0.5 * LOG2_E)).T.astype(BF16)
    for cblk in range(B_WIDTH // LANES):
        lo = cblk * LANES
        zk = z_scr[:, OFF_BK + lo:OFF_BK + lo + LANES]
        kb_ref[0, cblk] = _rope(zk, cosb, sinb).astype(BF16)
        store_vt(vbt_ref, cblk * (LANES // HEAD_DIM), z_scr[:, OFF_BV + lo:OFF_BV + lo + LANES].T)
    for j, ref in enumerate((cq_ref, cff_ref, cfb_ref, ci_ref, cg_ref)):
        ref[0] = z_scr[:, OFF_C + j * C_WIDTH:OFF_C + (j + 1) * C_WIDTH].astype(ref.dtype)


def _in_projection(x, mod_l, w_in_bf16, layer, tables, qk_gain, tm):
    bsz, s, d = x.shape
    cosa, sina, cosb, sinb = tables
    row_spec = lambda width: pl.BlockSpec((1, tm, width), lambda b, i: (b, i, 0))
    tab_spec = pl.BlockSpec((tm, LANES), lambda b, i: (i, 0))
    col_spec = lambda height: pl.BlockSpec((1, height, tm), lambda b, i: (b, 0, i))
    key_spec = lambda blocks: pl.BlockSpec((1, blocks, tm, LANES), lambda b, i: (b, 0, i, 0))
    return pl.pallas_call(
        _inproj_kernel,
        grid=(bsz, s // tm),
        in_specs=[
            row_spec(d),
            pl.BlockSpec((1, 6, d), lambda b, i: (b, 0, 0)),
            pl.BlockSpec((1, d, IN_TOTAL), lambda b, i: (layer, 0, 0)),
            tab_spec, tab_spec, tab_spec, tab_spec,
            pl.BlockSpec((2, LANES), lambda b, i: (0, 0)),
        ],
        out_specs=[
            col_spec(A_WIDTH), key_spec(A_KV_WIDTH // LANES), col_spec(A_KV_HEADS * VT_ROWS),
            col_spec(B_WIDTH), key_spec(B_WIDTH // LANES), col_spec(B_HEADS * VT_ROWS),
        ] + [row_spec(C_WIDTH)] * len(C_DTYPES),
        out_shape=[
            jax.ShapeDtypeStruct((bsz, A_WIDTH, s), BF16),
            jax.ShapeDtypeStruct((bsz, A_KV_WIDTH // LANES, s, LANES), BF16),
            jax.ShapeDtypeStruct((bsz, A_KV_HEADS * VT_ROWS, s), BF16),
            jax.ShapeDtypeStruct((bsz, B_WIDTH, s), BF16),
            jax.ShapeDtypeStruct((bsz, B_WIDTH // LANES, s, LANES), BF16),
            jax.ShapeDtypeStruct((bsz, B_HEADS * VT_ROWS, s), BF16),
        ] + [jax.ShapeDtypeStruct((bsz, s, C_WIDTH), dt) for dt in C_DTYPES],
        scratch_shapes=[pltpu.VMEM((tm, d), BF16), pltpu.VMEM((tm, IN_TOTAL), F32)],
        compiler_params=pltpu.CompilerParams(
            dimension_semantics=("parallel", "parallel"), vmem_limit_bytes=VMEM_LIMIT),
        name="in_projection",
    )(x, mod_l, w_in_bf16, cosa, sina, cosb, sinb, qk_gain)


def _place_rows(block, start, total):
    parts = []
    if start:
        parts.append(jnp.zeros((start, block.shape[1]), block.dtype))
    parts.append(block)
    rest = total - start - block.shape[0]
    if rest:
        parts.append(jnp.zeros((rest, block.shape[1]), block.dtype))
    return jnp.concatenate(parts, axis=0) if len(parts) > 1 else block


def _attention_items(items, k_ref, vt_ref, qz_scr, m_scr, acc_scr):
    s = k_ref.shape[2]
    tq = acc_scr.shape[2]
    m_scr[...] = jnp.full(m_scr.shape, -1e30, F32)
    acc_scr[...] = jnp.zeros_like(acc_scr)

    units = [(i, c) for c in range(s // KV_CHUNK) for i in range(len(items))]

    def scores(u):
        i, c = units[u]
        return _dot(k_ref[0, items[i][0], c * KV_CHUNK:(c + 1) * KV_CHUNK, :], qz_scr[i])

    pending = [scores(u) for u in range(min(QK_AHEAD, len(units)))]
    for u, (i, c) in enumerate(units):
        v_off = items[i][1]
        st = pending.pop(0)
        m_old = m_scr[i]
        part = jnp.max(st.reshape(KV_CHUNK // 8, 8, tq), axis=0)
        m_new = jnp.maximum(m_old, jnp.max(part, axis=0, keepdims=True))
        m_scr[i] = m_new
        p = jnp.exp2(st - m_new[0:1]).astype(BF16)
        d = _dot(vt_ref[0, v_off:v_off + VT_ROWS, c * KV_CHUNK:(c + 1) * KV_CHUNK], p)
        if u + QK_AHEAD < len(units):
            pending.append(scores(u + QK_AHEAD))
        acc_scr[i] = acc_scr[i] * jnp.exp2(m_old[0:1] - m_new[0:1]) + d
    results = []
    for i in range(len(items)):
        acc = acc_scr[i]
        results.append(acc[0:HEAD_DIM] * (1.0 / acc[HEAD_DIM:HEAD_DIM + 1]))
    return results


def _attention_scratch(heads, tq):
    n_items = heads * (tq // Q_TILE)
    return [pltpu.VMEM((n_items, LANES, Q_TILE), BF16), pltpu.VMEM((n_items, 8, Q_TILE), F32),
            pltpu.VMEM((n_items, VT_ROWS, Q_TILE), F32)]


def _attn_a_kernel(qt_ref, k_ref, vt_ref, o_ref, qz_scr, m_scr, acc_scr):
    group = A_HEADS // A_KV_HEADS
    n_tiles = qt_ref.shape[2] // Q_TILE
    items = []
    for t in range(n_tiles):
        cols = slice(t * Q_TILE, (t + 1) * Q_TILE)
        for h in range(A_HEADS):
            g = h // group
            qz_scr[len(items)] = _place_rows(qt_ref[0, h * HEAD_DIM:(h + 1) * HEAD_DIM, cols], g * HEAD_DIM, LANES)
            items.append((0, g * VT_ROWS))
    outs = _attention_items(items, k_ref, vt_ref, qz_scr, m_scr, acc_scr)
    for t in range(n_tiles):
        tile = outs[t * A_HEADS:(t + 1) * A_HEADS]
        o_ref[0, t * Q_TILE:(t + 1) * Q_TILE, :] = jnp.concatenate([o.T for o in tile], axis=-1).astype(BF16)


def _attention_a(qat, ka, vat, tq):
    bsz, _, s, _ = ka.shape
    return pl.pallas_call(
        _attn_a_kernel,
        grid=(bsz, s // tq),
        in_specs=[
            pl.BlockSpec((1, A_WIDTH, tq), lambda b, i: (b, 0, i)),
            pl.BlockSpec((1, A_KV_WIDTH // LANES, s, LANES), lambda b, i: (b, 0, 0, 0)),
            pl.BlockSpec((1, A_KV_HEADS * VT_ROWS, s), lambda b, i: (b, 0, 0)),
        ],
        out_specs=pl.BlockSpec((1, tq, A_WIDTH), lambda b, i: (b, i, 0)),
        out_shape=jax.ShapeDtypeStruct((bsz, s, A_WIDTH), BF16),
        scratch_shapes=_attention_scratch(A_HEADS, tq),
        compiler_params=pltpu.CompilerParams(
            dimension_semantics=("parallel", "parallel"), vmem_limit_bytes=VMEM_LIMIT),
        name="attn_axial_gqa",
    )(qat, ka, vat)


def _attn_b_kernel(lam_ref, gain_ref, qt_ref, k_ref, vt_ref, o_ref, qz_scr, m_scr, acc_scr, *, lam_init):
    lp = lam_ref[...]
    lam = (jnp.exp(jnp.sum(lp[0:1] * lp[1:2], axis=-1, keepdims=True))
           - jnp.exp(jnp.sum(lp[2:3] * lp[3:4], axis=-1, keepdims=True)) + lam_init)
    n_tiles = qt_ref.shape[2] // Q_TILE
    items = []
    for t in range(n_tiles):
        cols = slice(t * Q_TILE, (t + 1) * Q_TILE)
        for h in range(B_HEADS):
            pair = (h // 2) * LANES
            for comp in range(2):
                r0 = h * HEAD_DIM + comp * B_QK_DIM
                qz_scr[len(items)] = _place_rows(qt_ref[0, r0:r0 + B_QK_DIM, cols], r0 - pair, LANES)
                items.append((h // 2, h * VT_ROWS))
    comps = _attention_items(items, k_ref, vt_ref, qz_scr, m_scr, acc_scr)
    for t in range(n_tiles):
        outs = []
        for h in range(B_HEADS):
            i = (t * B_HEADS + h) * 2
            o = comps[i] - lam * comps[i + 1]
            o = o * lax.rsqrt(jnp.mean(o * o, axis=0, keepdims=True) + EPS)
            outs.append(o.T)
        y = jnp.concatenate(outs, axis=-1) * gain_ref[...] * (1.0 - lam_init)
        o_ref[0, t * Q_TILE:(t + 1) * Q_TILE, :] = y.astype(BF16)


def _attention_b(qbt, kb, vbt, lam_params, subln_gain, lam_init, tq):
    bsz, _, s, _ = kb.shape
    return pl.pallas_call(
        functools.partial(_attn_b_kernel, lam_init=lam_init),
        grid=(bsz, s // tq),
        in_specs=[
            pl.BlockSpec((4, B_QK_DIM), lambda b, i: (0, 0)),
            pl.BlockSpec((1, B_WIDTH), lambda b, i: (0, 0)),
            pl.BlockSpec((1, B_WIDTH, tq), lambda b, i: (b, 0, i)),
            pl.BlockSpec((1, B_WIDTH // LANES, s, LANES), lambda b, i: (b, 0, 0, 0)),
            pl.BlockSpec((1, B_HEADS * VT_ROWS, s), lambda b, i: (b, 0, 0)),
        ],
        out_specs=pl.BlockSpec((1, tq, B_WIDTH), lambda b, i: (b, i, 0)),
        out_shape=jax.ShapeDtypeStruct((bsz, s, B_WIDTH), BF16),
        scratch_shapes=_attention_scratch(2 * B_HEADS, tq),
        compiler_params=pltpu.CompilerParams(
            dimension_semantics=("parallel", "parallel"), vmem_limit_bytes=VMEM_LIMIT),
        name="attn_differential",
    )(lam_params, subln_gain, qbt, kb, vbt)


def _hgrn_levels():
    sizes = []
    bs = HGRN_CHUNK
    while bs > HGRN_LEAF:
        sizes.append(bs)
        bs //= 2
    return sizes


def _hgrn_constants():
    c, w = HGRN_CHUNK, C_WIDTH
    t = np.arange(c)
    tri = np.stack([(t[:, None] >= t[None, :]), (t[:, None] <= t[None, :])]).astype(np.float32)
    s_of_lane = np.arange(w) % C_DIM
    sizes = _hgrn_levels()
    level = np.full((2, c, w), float(len(sizes) + 1), np.float32)
    for d in range(2):
        for li, bs in enumerate(sizes):
            upper = (t % bs) >= bs // 2
            qrow = upper if d == 0 else ~upper
            same = (t[:, None] // bs) == (s_of_lane[None, :] // bs)
            cross = qrow[:, None] & ~qrow[s_of_lane][None, :]
            level[d][same & cross] = li
        same_leaf = (t[:, None] // HGRN_LEAF) == (s_of_lane[None, :] // HGRN_LEAF)
        order = (s_of_lane[None, :] <= t[:, None]) if d == 0 else (s_of_lane[None, :] >= t[:, None])
        level[d][same_leaf & order] = len(sizes)
    half = (np.arange(LANES)[None, :] < C_DIM).astype(np.float32) * np.ones((c, 1), np.float32)
    return tri, level, half


def _block_diag(x, low_half):
    zero = jnp.zeros((x.shape[0], LANES), x.dtype)
    blocks = []
    for h in range(C_HEADS):
        tile = x[:, (h // 2) * LANES:(h // 2 + 1) * LANES]
        keep = tile * (low_half if h % 2 == 0 else 1 - low_half)
        blocks.append(jnp.concatenate([keep, zero] if h // 2 == 0 else [zero, keep], axis=1))
    return jnp.concatenate(blocks, axis=0)


def _block_diag_t(x, low_half):
    zero = jnp.zeros((LANES, LANES), x.dtype)
    rows = []
    for t in range(C_WIDTH // LANES):
        tile = x[:, t * LANES:(t + 1) * LANES]
        pair = jnp.concatenate([tile * low_half, tile * (1 - low_half)], axis=0).T
        rows.append(jnp.concatenate([pair, zero] if t == 0 else [zero, pair], axis=1))
    return jnp.concatenate(rows, axis=0)


def _ref_rows(b, bs, offset):
    parts = []
    for lo in range(0, b.shape[0], bs):
        parts.append(jnp.broadcast_to(b[lo + offset:lo + offset + 1, :], (bs, b.shape[1])))
    return jnp.concatenate(parts, axis=0)


def _level_operands(q, k, b, bs, d):
    half = bs // 2
    ref_off = half - 1 if d == 0 else half
    zeros = jnp.zeros((8, b.shape[1]), F32)
    qparts, kparts = [], []
    for r0 in range(0, b.shape[0], 8):
        lo = (r0 // bs) * bs
        ref = b[lo + ref_off:lo + ref_off + 1, :]
        upper = (r0 % bs) >= half
        rows = slice(r0, r0 + 8)
        if upper == (d == 0):
            qparts.append(q[rows] * jnp.exp2(b[rows] - ref))
            kparts.append(zeros)
        else:
            qparts.append(zeros)
            kparts.append(k[rows] * jnp.exp2(ref - b[rows]))
    return jnp.concatenate(qparts, axis=0).astype(BF16), jnp.concatenate(kparts, axis=0).astype(BF16)


def _hgrn_kernel(lb_ref, tri_ref, level_ref, half_ref, zq_f, zf_f, zv_f, zq_b, zf_b, zv_b,
                 of_ref, ob_ref, st_scr, *, layer, depth, nb):
    c, w = HGRN_CHUNK, C_WIDTH
    sizes = _hgrn_levels()
    n_lvl = len(sizes)

    @pl.when(pl.program_id(1) == 0)
    def _():
        st_scr[...] = jnp.zeros_like(st_scr)

    rows = [lb_ref[i:i + 1, :] for i in range(depth)]
    mx = functools.reduce(jnp.maximum, rows)
    es = [jnp.exp(r - mx) for r in rows]
    tot = functools.reduce(lambda a, b_: a + b_, es)
    ps = [e / tot for e in es]
    cum = functools.reduce(lambda a, b_: a + b_, ps[:layer + 1])
    one_minus_lb = 1.0 - jnp.clip(cum - ps[0], 0.0, 1.0)

    low_half = half_ref[...].astype(BF16)
    low_bool = half_ref[...] > 0.5
    owner = [[level_ref[d] == float(li) for li in range(n_lvl + 1)] for d in range(2)]
    chains = [(j, d) for j in range(nb) for d in range(2)]
    zrefs = {0: (zq_f, zf_f, zv_f), 1: (zq_b, zf_b, zv_b)}

    q, k, v, b, total = {}, {}, {}, {}, {}
    g_split = {}
    for ch in chains:
        j, d = ch
        zq, zf = zrefs[d][0][j], zrefs[d][1][j]
        v[ch] = zrefs[d][2][j]
        q[ch] = (zq * (C_DIM ** -0.5)) / (1.0 + jnp.exp2(zq * -LOG2_E))
        k[ch] = one_minus_lb * (1.0 - 1.0 / (1.0 + jnp.exp2(zf * -LOG2_E)))
        g = jnp.log2(jnp.maximum(1.0 - k[ch], FORGET_FLOOR))
        g_hi = g.astype(BF16)
        g_r = g - g_hi.astype(F32)
        g_mid = g_r.astype(BF16)
        g_lo = (g_r - g_mid.astype(F32)).astype(BF16)
        g_split[ch] = jnp.concatenate([g_hi, g_mid, g_lo], axis=1)
    for ch in chains:
        d = ch[1]
        bb = _dot(tri_ref[d], g_split[ch])
        b[ch] = bb[:, 0:w] + bb[:, w:2 * w] + bb[:, 2 * w:3 * w]
        total[ch] = b[ch][c - 1:c, :] if d == 0 else b[ch][0:1, :]

    parts = {ch: [] for ch in chains}
    for li in range(n_lvl + 1):
        for ch in chains:
            d = ch[1]
            if li < n_lvl:
                qp, kp = _level_operands(q[ch], k[ch], b[ch], sizes[li], d)
            else:
                ref = _ref_rows(b[ch], HGRN_LEAF, HGRN_LEAF // 2 - 1 if d == 0 else HGRN_LEAF // 2)
                arg = b[ch] - ref
                qp = (q[ch] * jnp.exp2(arg)).astype(BF16)
                kp = (k[ch] * jnp.exp2(-arg)).astype(BF16)
            parts[ch].append(_dot(qp, _block_diag_t(kp, low_half)))
    a = {}
    for ch in chains:
        acc = parts[ch][0]
        for li in range(1, n_lvl + 1):
            acc = jnp.where(owner[ch[1]][li], parts[ch][li], acc)
        a[ch] = acc.astype(BF16)

    o_intra, o_inter, upd = {}, {}, {}
    for ch in chains:
        j, d = ch
        o_intra[ch] = _dot(a[ch], _block_diag(v[ch], low_half))
        qe = (q[ch] * jnp.exp2(b[ch])).astype(BF16)
        o_inter[ch] = _dot(qe, _block_diag_t(st_scr[d, j].astype(BF16), low_half))
        kb = (k[ch] * jnp.exp2(total[ch] - b[ch])).astype(BF16)
        tiles = []
        for t in range(w // LANES):
            lanes = slice(t * LANES, (t + 1) * LANES)
            full = _dot_tn(v[ch][:, lanes], kb[:, lanes])
            tiles.append(jnp.where(low_bool, full[0:C_DIM], full[C_DIM:2 * C_DIM]))
        upd[ch] = jnp.concatenate(tiles, axis=1)
    for ch in chains:
        j, d = ch
        st_scr[d, j] = st_scr[d, j] * jnp.exp2(total[ch]) + upd[ch]
        (of_ref if d == 0 else ob_ref)[j] = o_intra[ch] + o_inter[ch]


def _hgrn(cq, cff, cfb, ci, lower_bounds, layer, nb):
    bsz, s, _ = cq.shape
    depth = lower_bounds.shape[0]
    n_chunks = s // HGRN_CHUNK
    c, w = HGRN_CHUNK, C_WIDTH
    tri, level, half = _hgrn_constants()
    consts = (jnp.asarray(tri, BF16), jnp.asarray(level), jnp.asarray(half))
    full = lambda a: pl.BlockSpec(a.shape, lambda g, i: (0,) * a.ndim)
    last = n_chunks - 1
    fwd = pl.BlockSpec((nb, c, w), lambda g, i: (g, i, 0))
    bwd = pl.BlockSpec((nb, c, w), lambda g, i: (g, last - i, 0))
    return pl.pallas_call(
        functools.partial(_hgrn_kernel, layer=layer, depth=depth, nb=nb),
        grid=(bsz // nb, n_chunks),
        in_specs=[full(lower_bounds)] + [full(a) for a in consts] + [fwd, fwd, fwd, bwd, bwd, bwd],
        out_specs=[fwd, bwd],
        out_shape=[jax.ShapeDtypeStruct((bsz, s, w), F32)] * 2,
        scratch_shapes=[pltpu.VMEM((2, nb, C_DIM, w), F32)],
        compiler_params=pltpu.CompilerParams(
            dimension_semantics=("parallel", "arbitrary"), vmem_limit_bytes=VMEM_LIMIT),
        name="hgrn2_bidir",
    )(lower_bounds, *consts, cq, cff, ci, cq, cfb, ci)


def _out_ffn_kernel(x_ref, mod_ref, oa_ref, ob_ref, cf_ref, cb_ref, zg_ref, cgain_ref,
                    wo_ref, w1_ref, w2_ref, fg_ref, o_ref, *, final, ff_chunk):
    oc = cf_ref[0] + cb_ref[0]
    ms = _split_dot(oc * oc, _group_ones(C_WIDTH, C_DIM))
    zg = zg_ref[0]
    oc = oc * lax.rsqrt(ms + EPS) * cgain_ref[...] * (zg * jax.nn.sigmoid(zg))
    mix = jnp.concatenate([oa_ref[0], ob_ref[0], oc.astype(BF16)], axis=-1)
    x1 = x_ref[0] + mod_ref[0, 2:3, :] * _dot(mix, wo_ref[0])
    h = (_rms_rows(x1) * (1.0 + mod_ref[0, 4:5, :]) + mod_ref[0, 3:4, :]).astype(BF16)
    d_ff = w1_ref.shape[2]
    acc = jnp.zeros_like(x1)
    for j in range(d_ff // ff_chunk):
        u = _dot(h, w1_ref[0, :, j * ff_chunk:(j + 1) * ff_chunk])
        u = jnp.square(jnp.maximum(u, 0.0)).astype(BF16)
        acc = acc + _dot(u, w2_ref[0, j * ff_chunk:(j + 1) * ff_chunk, :])
    x2 = x1 + mod_ref[0, 5:6, :] * acc
    if final:
        x2 = _rms_rows(x2) * fg_ref[...]
    o_ref[0] = x2


def _out_ffn(x, mod_l, oa, ob, oc_f, oc_b, cg, c_gain, w_out, w_ff1, w_ff2, final_gain, layer, final, tm):
    bsz, s, d = x.shape
    d_ff = w_ff1.shape[2]
    row_spec = lambda width: pl.BlockSpec((1, tm, width), lambda b, i: (b, i, 0))
    weight = lambda a: pl.BlockSpec((1,) + a.shape[1:], lambda b, i: (layer, 0, 0),
                                    pipeline_mode=pl.Buffered(1))
    return pl.pallas_call(
        functools.partial(_out_ffn_kernel, final=final, ff_chunk=COL_CHUNK),
        grid=(bsz, s // tm),
        in_specs=[
            row_spec(d),
            pl.BlockSpec((1, 6, d), lambda b, i: (b, 0, 0)),
            row_spec(A_WIDTH), row_spec(B_WIDTH), row_spec(C_WIDTH), row_spec(C_WIDTH), row_spec(C_WIDTH),
            pl.BlockSpec((1, C_WIDTH), lambda b, i: (0, 0)),
            weight(w_out), weight(w_ff1), weight(w_ff2),
            pl.BlockSpec((1, d), lambda b, i: (0, 0)),
        ],
        out_specs=row_spec(d),
        out_shape=jax.ShapeDtypeStruct((bsz, s, d), F32),
        compiler_params=pltpu.CompilerParams(
            dimension_semantics=("parallel", "parallel"), vmem_limit_bytes=VMEM_LIMIT),
        name="out_proj_ffn",
    )(x, mod_l, oa, ob, oc_f, oc_b, cg, c_gain, w_out, w_ff1, w_ff2, final_gain)


def _rope_tables(s):
    half = HEAD_DIM // 2
    inv = ROPE_THETA ** (-jnp.arange(0, half, 2, dtype=F32) / half)
    t = jnp.arange(s)
    ang_row = (t // GRID_W).astype(F32)[:, None] * inv[None, :]
    ang_col = (t % GRID_W).astype(F32)[:, None] * inv[None, :]
    ang_1d = t.astype(F32)[:, None] * inv[None, :]

    def pair(ang):
        return (jnp.concatenate([jnp.cos(ang), jnp.cos(ang)], axis=-1),
                jnp.concatenate([-jnp.sin(ang), jnp.sin(ang)], axis=-1))

    cr, sr = pair(ang_row)
    cc, sc = pair(ang_col)
    c1, s1 = pair(ang_1d)
    tile = lambda a: jnp.tile(a, (1, LANES // a.shape[1]))
    return (tile(jnp.concatenate([cr, cc], axis=-1)), tile(jnp.concatenate([sr, sc], axis=-1)),
            tile(c1), tile(s1))


def kernel(x, c, w_mod, b_mod, w_in, a_qk_norm, diff_lambda, diff_subln, hgrn_lower_bounds, hgrn_norm,
           w_out, w_ff1, w_ff2, final_norm):
    bsz, s, d = x.shape
    depth = w_in.shape[0]
    tm = min(ROW_TILE, s)
    tables = _rope_tables(s)
    mod = _modulation(c, w_mod, b_mod).reshape(depth, bsz, 6, d)
    w_in_b, w_out_b, w_ff1_b, w_ff2_b = (w.astype(BF16) for w in (w_in, w_out, w_ff1, w_ff2))
    final_gain = final_norm.reshape(1, d)
    for l in range(depth):
        qk_gain = jnp.tile(a_qk_norm[l], (1, LANES // HEAD_DIM))
        qat, ka, vat, qbt, kb, vbt, cq, cff, cfb, ci, cg = _in_projection(
            x, mod[l], w_in_b, l, tables, qk_gain, tm)
        oa = _attention_a(qat, ka, vat, min(A_Q_TILES * Q_TILE, s))
        lam_init = 0.8 - 0.6 * math.exp(-0.3 * l)
        subln = jnp.tile(diff_subln[l].reshape(1, HEAD_DIM), (1, B_HEADS))
        ob = _attention_b(qbt, kb, vbt, diff_lambda[l], subln, lam_init, min(B_Q_TILES * Q_TILE, s))
        hg_gain = jnp.tile(hgrn_norm[l].reshape(1, C_DIM), (1, C_HEADS))
        oc_f, oc_b = _hgrn(cq, cff, cfb, ci, hgrn_lower_bounds, l, min(HGRN_BATCH, bsz))
        x = _out_ffn(x, mod[l], oa, ob, oc_f, oc_b, cg, hg_gain, w_out_b, w_ff1_b, w_ff2_b, final_gain,
                     l, l == depth - 1, tm)
    return x
```
